```python
import math
import jax, jax.numpy as jnp
from jax import lax
import numpy as np

D_MODEL = 1024
BATCH = 8
SEQ = 2048
DEPTH = 2
DEC_BATCH = 32
DEC_SEQ = 4
PAST_LEN = 16384
PAGE_SIZE = 128

D_A = D_MODEL
NA_BLOCKS = 16
CONV_W = 4
LRU_C = 8.0
H_B = 8
HD_B = D_MODEL // H_B
D_B = H_B * HD_B
ROT_DIM = HD_B // 4
ROPE_THETA = 500000.0
DILATIONS = ((128, 1), (512, 4), (2048, 16))
MAX_WINDOW = 2048
QBLOCK = 128
D_C = 2 * D_MODEL
NH_C = 4
DH_C = D_C // NH_C
QKV_BLOCK = 4
MLSTM_CHUNK = 64
D_FF = 4 * D_MODEL
N_EVEN = (DEPTH + 1) // 2
N_ODD = DEPTH // 2
EPS = 1e-6
F32 = jnp.float32

kernel_name = 'hawk_longnet_mlstm_hybrid_step'


def rmsnorm(x, g):
    xf = x.astype(F32)
    y = xf * lax.rsqrt(jnp.mean(xf * xf, axis=-1, keepdims=True) + EPS)
    return (y * g.astype(F32)).astype(x.dtype)


def causal_dwconv(x, buf, w, b):
    L = x.shape[1]
    xp = jnp.concatenate([buf.astype(x.dtype), x], axis=1)
    y = b
    for j in range(CONV_W):
        y = y + w[j] * xp[:, j:j + L]
    return y, xp[:, -(CONV_W - 1):]


def block_diag(x, w):
    nb, bi, bo = w.shape
    xr = x.reshape(x.shape[:-1] + (nb, bi))
    return jnp.einsum('...ni,nio->...no', xr, w).reshape(x.shape[:-1] + (nb * bo,))


def rglru(x, h0, wa, ba, wx, bx, lam):
    xf = x.astype(F32)
    r = jax.nn.sigmoid(block_diag(xf, wa.astype(F32)) + ba.astype(F32))
    i = jax.nn.sigmoid(block_diag(xf, wx.astype(F32)) + bx.astype(F32))
    log_a = -LRU_C * r * jax.nn.softplus(-lam.astype(F32))
    a = jnp.exp(log_a)
    b = jnp.sqrt(-jnp.expm1(2.0 * log_a)) * (i * xf)
    b = b.at[:, 0].add(a[:, 0] * h0.astype(F32))

    def comb(left, right):
        a1, b1 = left
        a2, b2 = right
        return a1 * a2, a2 * b1 + b2

    _, h = lax.associative_scan(comb, (a, b), axis=1)
    return h, h[:, -1]


def rope(x, pos):
    half = ROT_DIM // 2
    inv = ROPE_THETA ** (-jnp.arange(0, ROT_DIM, 2, dtype=F32) / ROT_DIM)
    ang = pos.astype(F32)[:, None] * inv[None, :]
    cos = jnp.cos(ang)[None, :, None, :]
    sin = jnp.sin(ang)[None, :, None, :]
    x1, x2, rest = x[..., :half], x[..., half:ROT_DIM], x[..., ROT_DIM:]
    return jnp.concatenate([x1 * cos - x2 * sin, x1 * sin + x2 * cos, rest], axis=-1)


def dilated_group_prompt(q, k, v, window, dilation):
    B, S, H, E = q.shape
    d = dilation
    back = window // dilation
    L = S // d
    nb = -(-L // QBLOCK)
    Lp = nb * QBLOCK

    def split(t):
        return jnp.swapaxes(t.reshape(B, L, d, H, E), 1, 2)

    qb = jnp.pad(split(q), ((0, 0), (0, 0), (0, Lp - L), (0, 0), (0, 0))).reshape(B, d, nb, QBLOCK, H, E)

    def kblocks(t):
        tp = jnp.pad(split(t), ((0, 0), (0, 0), (QBLOCK, Lp - L), (0, 0), (0, 0)))
        tp = tp.reshape(B, d, nb + 1, QBLOCK, H, E)
        return jnp.concatenate([tp[:, :, :-1], tp[:, :, 1:]], axis=3)

    kk, vv = kblocks(k), kblocks(v)
    s = jnp.einsum('bdnqhe,bdnkhe->bdnhqk', qb, kk)
    qi = jnp.arange(QBLOCK)[:, None]
    ki = jnp.arange(2 * QBLOCK)[None, :]
    dist = qi + QBLOCK - ki
    blk = jnp.arange(nb)[:, None, None]
    valid = (dist >= 0) & (dist <= back) & (blk * QBLOCK - QBLOCK + ki >= 0)
    s = jnp.where(valid[None, None, :, None], s, -jnp.inf)
    m = jnp.max(s, axis=-1)
    p = jnp.exp(s - m[..., None])
    den = jnp.sum(p, axis=-1)
    acc = jnp.einsum('bdnhqk,bdnkhe->bdnqhe', p, vv)

    def merge(t):
        rest = t.shape[4:]
        t = t.reshape((B, d, Lp) + rest)[:, :, :L]
        return jnp.swapaxes(t, 1, 2).reshape((B, S) + rest)

    return merge(jnp.swapaxes(m, -1, -2)), merge(jnp.swapaxes(den, -1, -2)), merge(acc)


def dilated_group_sample(q, k_all, v_all, window, dilation, wb):
    T = q.shape[1]
    back = window // dilation
    idx = wb + jnp.arange(T)[:, None] - dilation * jnp.arange(back + 1)[None, :]
    valid = idx >= 0
    idxc = jnp.maximum(idx, 0)
    kg = k_all[:, idxc]
    vg = v_all[:, idxc]
    s = jnp.einsum('bthe,btjhe->bthj', q, kg)
    s = jnp.where(valid[None, :, None, :], s, -jnp.inf)
    m = jnp.max(s, axis=-1)
    p = jnp.exp(s - m[..., None])
    den = jnp.sum(p, axis=-1)
    acc = jnp.einsum('bthj,btjhe->bthe', p, vg)
    return m, den, acc


def merge_dilation_groups(parts):
    ms = jnp.stack([pt[0] for pt in parts])
    dens = jnp.stack([pt[1] for pt in parts])
    accs = jnp.stack([pt[2] for pt in parts])
    w = jnp.exp(ms - jnp.max(ms, axis=0))
    return jnp.sum(w[..., None] * accs, axis=0) / jnp.sum(w * dens, axis=0)[..., None]


def hybrid_ab_mixer(xn, pos, conv_buf, h0, k_buf, v_buf, w_in, conv_w, conv_b, wa, ba, wx, bx, lam, w_out):
    B, L, _ = xn.shape
    xa, ga, q, k, v = jnp.split(xn @ w_in, [D_A, 2 * D_A, 2 * D_A + D_B, 2 * D_A + 2 * D_B], axis=-1)
    xc, new_conv = causal_dwconv(xa, conv_buf, conv_w, conv_b)
    h, h_last = rglru(xc, h0, wa, ba, wx, bx, lam)
    ya = h * jax.nn.gelu(ga.astype(F32))
    q = rope(q.reshape(B, L, H_B, HD_B).astype(F32), pos) * (HD_B ** -0.5)
    k = rope(k.reshape(B, L, H_B, HD_B).astype(F32), pos)
    v = v.reshape(B, L, H_B, HD_B).astype(F32)
    if k_buf is None:
        parts = [dilated_group_prompt(q, k, v, w, d) for (w, d) in DILATIONS]
        wb = min(MAX_WINDOW, L)
        new_k, new_v = k[:, L - wb:], v[:, L - wb:]
    else:
        wb = k_buf.shape[1]
        k_all = jnp.concatenate([k_buf.astype(F32), k], axis=1)
        v_all = jnp.concatenate([v_buf.astype(F32), v], axis=1)
        parts = [dilated_group_sample(q, k_all, v_all, w, d, wb) for (w, d) in DILATIONS]
        new_k, new_v = k, v
    yb = merge_dilation_groups(parts).reshape(B, L, D_B)
    y = jnp.concatenate([ya, yb], axis=-1).astype(xn.dtype) @ w_out
    return y, new_conv, h_last, new_k, new_v


def mlstm_chunkwise(q, k, v, i_pre, f_pre, C0, n0, m0):
    B, L, NH, DH = q.shape
    c = math.gcd(L, MLSTM_CHUNK)
    nc = L // c

    def to_chunks(t):
        return jnp.moveaxis(t.reshape((B, nc, c) + t.shape[2:]), 1, 0)

    xs = tuple(to_chunks(t) for t in (q, k, v, i_pre, f_pre))
    causal = jnp.tril(jnp.ones((c, c), dtype=bool))

    def step(carry, inp):
        C, n, m = carry
        qc, kc, vc, ic, fc = inp
        bcum = jnp.cumsum(jax.nn.log_sigmoid(fc), axis=1)
        log_intra = bcum[:, :, None, :] - bcum[:, None, :, :] + ic[:, None, :, :]
        log_intra = jnp.where(causal[None, :, :, None], log_intra, -jnp.inf)
        log_inter = bcum + m[:, None, :]
        m_t = jnp.maximum(log_inter, jnp.max(log_intra, axis=2))
        w_intra = jnp.exp(log_intra - m_t[:, :, None, :]) * jnp.einsum('bthd,bshd->btsh', qc, kc)
        w_inter = jnp.exp(log_inter - m_t)
        num = jnp.einsum('btsh,bshe->bthe', w_intra, vc) + w_inter[..., None] * jnp.einsum('bthd,bhde->bthe', qc, C)
        den = jnp.sum(w_intra, axis=2) + w_inter * jnp.einsum('bthd,bhd->bth', qc, n)
        h = num / jnp.maximum(jnp.abs(den), jnp.exp(-m_t))[..., None]
        m_new = m_t[:, -1]
        w_state = jnp.exp(bcum[:, -1:, :] - bcum + ic - m_new[:, None, :])
        decay = jnp.exp(bcum[:, -1] + m - m_new)
        kw = kc * w_state[..., None]
        C_new = decay[..., None, None] * C + jnp.einsum('bshd,bshe->bhde', kw, vc)
        n_new = decay[..., None] * n + jnp.sum(kw, axis=1)
        return (C_new, n_new, m_new), h

    (C, n, m), hs = lax.scan(step, (C0, n0, m0), xs)
    h = jnp.moveaxis(hs, 0, 1).reshape(B, L, NH, DH)
    return h, C, n, m


def mlstm_mixer(xn, conv_buf, C0, n0, m0, w_in, conv_w, conv_b, wq, wk, wv, w_gate, b_gate, norm_w, skip, w_out):
    B, L, _ = xn.shape
    xm, z = jnp.split(xn @ w_in, 2, axis=-1)
    xc, new_conv = causal_dwconv(xm, conv_buf, conv_w, conv_b)
    xc = jax.nn.silu(xc)
    q = block_diag(xc, wq)
    k = block_diag(xc, wk)
    v = block_diag(xm, wv)
    gates = (jnp.concatenate([q, k, v], axis=-1) @ w_gate + b_gate).astype(F32)
    i_pre, f_pre = gates[..., :NH_C], gates[..., NH_C:]

    def heads(t):
        return t.reshape(B, L, NH_C, DH_C).astype(F32)

    h, C, n, m = mlstm_chunkwise(heads(q), heads(k) * (DH_C ** -0.5), heads(v), i_pre, f_pre,
                                 C0.astype(F32), n0.astype(F32), m0.astype(F32))
    h = h * lax.rsqrt(jnp.mean(h * h, axis=-1, keepdims=True) + EPS)
    h = h.reshape(B, L, D_C) * norm_w.astype(F32)
    y = (h + skip * xc) * jax.nn.silu(z)
    return y.astype(xn.dtype) @ w_out, new_conv, C, n, m


def run_trunk(x, pos, conv_a0, h_a0, k_buf, v_buf, conv_c0, C0, n0, m0, p):
    new = [[] for _ in range(8)]
    for layer in range(DEPTH):
        xn = rmsnorm(x, p['norm_mix_pre'][layer])
        if layer % 2 == 0:
            e = layer // 2
            y, *st = hybrid_ab_mixer(
                xn, pos, conv_a0[e], h_a0[e],
                None if k_buf is None else k_buf[e], None if v_buf is None else v_buf[e],
                p['w_in_ab'][e], p['conv_a_w'][e], p['conv_a_b'][e], p['lru_wa'][e], p['lru_ba'][e],
                p['lru_wx'][e], p['lru_bx'][e], p['lru_lambda'][e], p['w_out_ab'][e])
            for j, s in enumerate(st):
                new[j].append(s)
        else:
            o = layer // 2
            y, *st = mlstm_mixer(
                xn, conv_c0[o], C0[o], n0[o], m0[o],
                p['w_in_c'][o], p['conv_c_w'][o], p['conv_c_b'][o], p['mlstm_wq'][o], p['mlstm_wk'][o],
                p['mlstm_wv'][o], p['mlstm_w_gate'][o], p['mlstm_b_gate'][o], p['mlstm_norm'][o],
                p['mlstm_skip'][o], p['w_out_c'][o])
            for j, s in enumerate(st):
                new[4 + j].append(s)
        x = x + rmsnorm(y, p['norm_mix_post'][layer])
        xn = rmsnorm(x, p['norm_ffn_pre'][layer])
        hid = jnp.square(jax.nn.relu(xn @ p['w_ffn_up'][layer]))
        x = x + rmsnorm(hid @ p['w_ffn_down'][layer], p['norm_ffn_post'][layer])
    return x, [jnp.stack(s) for s in new]


def setup_inputs(seed: int = 0) -> dict:
    key = jax.random.key(seed)
    ks = iter(jax.random.split(key, 48))

    def nrm(shape, scale):
        return scale * jax.random.normal(next(ks), shape, F32)

    def gain(shape):
        return 1.0 + nrm(shape, 0.05)

    wb = min(MAX_WINDOW, PAST_LEN)
    u = jax.random.uniform(next(ks), (N_EVEN, D_A), F32, 0.9, 0.999)
    lam = jnp.log(u) - jnp.log1p(-u)
    b_gate = jnp.concatenate([
        nrm((N_ODD, NH_C), 0.1),
        jnp.broadcast_to(jnp.linspace(3.0, 6.0, NH_C, dtype=F32), (N_ODD, NH_C)) + nrm((N_ODD, NH_C), 0.01)], axis=-1)
    d_in_ab = 2 * D_A + 3 * D_B
    return {
        'x_prompt': nrm((BATCH, SEQ, D_MODEL), 1.0),
        'x_sample': nrm((DEC_BATCH, DEC_SEQ, D_MODEL), 1.0),
        'state_rglru_conv': nrm((N_EVEN, DEC_BATCH, CONV_W - 1, D_A), 1.0),
        'state_rglru_h': nrm((N_EVEN, DEC_BATCH, D_A), 0.5),
        'cache_swa_k': nrm((N_EVEN, DEC_BATCH, wb, H_B, HD_B), 1.0),
        'cache_swa_v': nrm((N_EVEN, DEC_BATCH, wb, H_B, HD_B), 1.0),
        'state_mlstm_conv': nrm((N_ODD, DEC_BATCH, CONV_W - 1, D_C), 1.0),
        'state_mlstm_C': nrm((N_ODD, DEC_BATCH, NH_C, DH_C, DH_C), 0.02),
        'state_mlstm_n': nrm((N_ODD, DEC_BATCH, NH_C, DH_C), 0.1),
        'state_mlstm_m': nrm((N_ODD, DEC_BATCH, NH_C), 1.0),
        'norm_mix_pre': gain((DEPTH, D_MODEL)),
        'norm_mix_post': gain((DEPTH, D_MODEL)),
        'norm_ffn_pre': gain((DEPTH, D_MODEL)),
        'norm_ffn_post': gain((DEPTH, D_MODEL)),
        'w_ffn_up': nrm((DEPTH, D_MODEL, D_FF), D_MODEL ** -0.5),
        'w_ffn_down': nrm((DEPTH, D_FF, D_MODEL), D_FF ** -0.5),
        'w_in_ab': nrm((N_EVEN, D_MODEL, d_in_ab), D_MODEL ** -0.5),
        'conv_a_w': nrm((N_EVEN, CONV_W, D_A), CONV_W ** -0.5),
        'conv_a_b': nrm((N_EVEN, D_A), 0.01),
        'lru_wa': nrm((N_EVEN, NA_BLOCKS, D_A // NA_BLOCKS, D_A // NA_BLOCKS), (D_A // NA_BLOCKS) ** -0.5),
        'lru_ba': nrm((N_EVEN, D_A), 0.01),
        'lru_wx': nrm((N_EVEN, NA_BLOCKS, D_A // NA_BLOCKS, D_A // NA_BLOCKS), (D_A // NA_BLOCKS) ** -0.5),
        'lru_bx': nrm((N_EVEN, D_A), 0.01),
        'lru_lambda': lam,
        'w_out_ab': nrm((N_EVEN, D_A + D_B, D_MODEL), (D_A + D_B) ** -0.5),
        'w_in_c': nrm((N_ODD, D_MODEL, 2 * D_C), D_MODEL ** -0.5),
        'conv_c_w': nrm((N_ODD, CONV_W, D_C), CONV_W ** -0.5),
        'conv_c_b': nrm((N_ODD, D_C), 0.01),
        'mlstm_wq': nrm((N_ODD, D_C // QKV_BLOCK, QKV_BLOCK, QKV_BLOCK), QKV_BLOCK ** -0.5),
        'mlstm_wk': nrm((N_ODD, D_C // QKV_BLOCK, QKV_BLOCK, QKV_BLOCK), QKV_BLOCK ** -0.5),
        'mlstm_wv': nrm((N_ODD, D_C // QKV_BLOCK, QKV_BLOCK, QKV_BLOCK), QKV_BLOCK ** -0.5),
        'mlstm_w_gate': nrm((N_ODD, 3 * D_C, 2 * NH_C), 0.1 * (3 * D_C) ** -0.5),
        'mlstm_b_gate': b_gate,
        'mlstm_norm': gain((N_ODD, D_C)),
        'mlstm_skip': gain((N_ODD, D_C)),
        'w_out_c': nrm((N_ODD, D_C, D_MODEL), D_C ** -0.5),
    }


def reference(x_prompt, x_sample, state_rglru_conv, state_rglru_h, cache_swa_k, cache_swa_v,
              state_mlstm_conv, state_mlstm_C, state_mlstm_n, state_mlstm_m,
              norm_mix_pre, norm_mix_post, norm_ffn_pre, norm_ffn_post, w_ffn_up, w_ffn_down,
              w_in_ab, conv_a_w, conv_a_b, lru_wa, lru_ba, lru_wx, lru_bx, lru_lambda, w_out_ab,
              w_in_c, conv_c_w, conv_c_b, mlstm_wq, mlstm_wk, mlstm_wv, mlstm_w_gate, mlstm_b_gate,
              mlstm_norm, mlstm_skip, w_out_c):
    p = {
        'norm_mix_pre': norm_mix_pre, 'norm_mix_post': norm_mix_post,
        'norm_ffn_pre': norm_ffn_pre, 'norm_ffn_post': norm_ffn_post,
        'w_ffn_up': w_ffn_up, 'w_ffn_down': w_ffn_down,
        'w_in_ab': w_in_ab, 'conv_a_w': conv_a_w, 'conv_a_b': conv_a_b,
        'lru_wa': lru_wa, 'lru_ba': lru_ba, 'lru_wx': lru_wx, 'lru_bx': lru_bx,
        'lru_lambda': lru_lambda, 'w_out_ab': w_out_ab,
        'w_in_c': w_in_c, 'conv_c_w': conv_c_w, 'conv_c_b': conv_c_b,
        'mlstm_wq': mlstm_wq, 'mlstm_wk': mlstm_wk, 'mlstm_wv': mlstm_wv,
        'mlstm_w_gate': mlstm_w_gate, 'mlstm_b_gate': mlstm_b_gate,
        'mlstm_norm': mlstm_norm, 'mlstm_skip': mlstm_skip, 'w_out_c': w_out_c,
    }
    B, S, _ = x_prompt.shape
    dt = x_prompt.dtype
    y_prompt, new_p = run_trunk(
        x_prompt, jnp.arange(S),
        jnp.zeros((N_EVEN, B, CONV_W - 1, D_A), dt), jnp.zeros((N_EVEN, B, D_A), dt), None, None,
        jnp.zeros((N_ODD, B, CONV_W - 1, D_C), dt), jnp.zeros((N_ODD, B, NH_C, DH_C, DH_C), dt),
        jnp.zeros((N_ODD, B, NH_C, DH_C), dt), jnp.zeros((N_ODD, B, NH_C), dt), p)
    y_sample, new_s = run_trunk(
        x_sample, PAST_LEN + jnp.arange(x_sample.shape[1]),
        state_rglru_conv, state_rglru_h, cache_swa_k, cache_swa_v,
        state_mlstm_conv, state_mlstm_C, state_mlstm_n, state_mlstm_m, p)
    refs = (state_rglru_conv, state_rglru_h, cache_swa_k, cache_swa_v,
            state_mlstm_conv, state_mlstm_C, state_mlstm_n, state_mlstm_m)
    p_rglru_conv, p_rglru_h, p_swa_k, p_swa_v, p_mlstm_conv, p_mlstm_C, p_mlstm_n, p_mlstm_m = [
        t.astype(r.dtype) for t, r in zip(new_p, refs)]
    s_rglru_conv, s_rglru_h, s_swa_k, s_swa_v, s_mlstm_conv, s_mlstm_C, s_mlstm_n, s_mlstm_m = [
        t.astype(r.dtype) for t, r in zip(new_s, refs)]
    return (y_prompt, y_sample,
            p_rglru_conv, s_rglru_conv, p_rglru_h, s_rglru_h,
            p_swa_k, s_swa_k, p_swa_v, s_swa_v,
            p_mlstm_conv, s_mlstm_conv, p_mlstm_C, s_mlstm_C,
            p_mlstm_n, s_mlstm_n, p_mlstm_m, s_mlstm_m)
```

```python
import functools
import math

import jax
import jax.numpy as jnp
from jax import lax
from jax.experimental import pallas as pl
from jax.experimental.pallas import tpu as pltpu

F32 = jnp.float32
BF16 = jnp.bfloat16

NA_BLOCKS = 16
CONV_W = 4
LRU_C = 8.0
H_B = 8
ROPE_THETA = 500000.0
DILATIONS = ((128, 1), (512, 4), (2048, 16))
MAX_WINDOW = 2048
QBLOCK = 128
NH_C = 4
QKV_BLOCK = 4
PAST_LEN = 16384
EPS = 1e-6

LANES = 128
SUBLANES = 8
MXU_DIM = 256
VMEM_LIMIT = 56 * 1024 * 1024

NEG_INF = float("-inf")


def _cparams(sem):
    return pltpu.CompilerParams(dimension_semantics=sem, vmem_limit_bytes=VMEM_LIMIT)


def _tile(n, pref):
    t = min(n, pref)
    while n % t:
        t -= 1
    return t


def _rms(x, g):
    return x * lax.rsqrt(jnp.mean(x * x, axis=-1, keepdims=True) + EPS) * g


def _softplus(z):
    return jnp.maximum(z, 0.0) + jnp.log1p(jnp.exp(-jnp.abs(z)))


def _gelu_tanh(x):
    c = math.sqrt(2.0 / math.pi)
    return x * (0.5 * (1.0 + jnp.tanh(c * (x + 0.044715 * (x * x * x)))))


def _silu(x):
    return x * jax.nn.sigmoid(x)


def _rope_tile(y, c, sa, sb):
    outs = []
    for h in range(y.shape[1] // LANES):
        yh = y[:, h * LANES:(h + 1) * LANES]
        outs.append(yh * c + pltpu.roll(yh, LANES - 16, 1) * sa + pltpu.roll(yh, 16, 1) * sb)
    return jnp.concatenate(outs, axis=1)


def _norm_proj_kernel(*refs, groups, rope_q, rope_k, q_scale):
    n_out = len(groups)
    x_ref, g_ref, w_ref = refs[:3]
    pos = 3
    if rope_q is not None:
        c_ref, sa_ref, sb_ref = refs[3:6]
        pos = 6
    out_refs = refs[pos:pos + n_out]
    xn_ref = refs[pos + n_out]
    j = pl.program_id(1)

    @pl.when(j == 0)
    def _():
        xn_ref[...] = _rms(x_ref[...], g_ref[...]).astype(BF16)

    y = jnp.dot(xn_ref[...], w_ref[...], preferred_element_type=F32)
    for (start, count), o_ref in zip(groups, out_refs):
        for jj in range(start, start + count):
            @pl.when(j == jj)
            def _(o_ref=o_ref, jj=jj):
                if rope_q is not None and jj == rope_q:
                    o_ref[...] = (_rope_tile(y, c_ref[...], sa_ref[...], sb_ref[...]) * q_scale).astype(o_ref.dtype)
                elif rope_q is not None and jj == rope_k:
                    o_ref[...] = _rope_tile(y, c_ref[...], sa_ref[...], sb_ref[...]).astype(o_ref.dtype)
                else:
                    o_ref[...] = y.astype(o_ref.dtype)


def _norm_proj(x, g, w, groups, tm, tn, rope=None, rope_q=None, rope_k=None, q_scale=1.0):
    M, K = x.shape
    N = w.shape[1]
    nj = N // tn
    in_specs = [
        pl.BlockSpec((tm, K), lambda i, j: (i, 0)),
        pl.BlockSpec((1, K), lambda i, j: (0, 0)),
        pl.BlockSpec((K, tn), lambda i, j: (0, j)),
    ]
    args = [x, g, w]
    if rope is not None:
        pos_blocks = rope[0].shape[0] // tm
        for t in rope:
            in_specs.append(pl.BlockSpec((tm, LANES), lambda i, j, pb=pos_blocks: (i % pb, 0)))
            args.append(t)
    out_specs, out_shapes = [], []
    for (start, count) in groups:
        out_specs.append(pl.BlockSpec(
            (tm, tn), lambda i, j, s=start, c=count: (i, jnp.clip(j - s, 0, c - 1))))
        out_shapes.append(jax.ShapeDtypeStruct((M, count * tn), F32))
    kern = functools.partial(_norm_proj_kernel, groups=tuple(groups),
                             rope_q=rope_q if rope is not None else None, rope_k=rope_k, q_scale=q_scale)
    return pl.pallas_call(
        kern,
        grid=(M // tm, nj),
        in_specs=in_specs,
        out_specs=out_specs,
        out_shape=out_shapes,
        scratch_shapes=[pltpu.VMEM((tm, K), BF16)],
        compiler_params=_cparams(("parallel", "arbitrary")),
        name="norm_proj",
    )(*args)


def _out_proj_kernel(*refs, n_in):
    a_refs = refs[:n_in]
    w_refs = refs[n_in:2 * n_in]
    x_ref, g_ref, o_ref = refs[2 * n_in:]
    y = None
    for a_ref, w_ref in zip(a_refs, w_refs):
        d = jnp.dot(a_ref[...].astype(BF16), w_ref[...], preferred_element_type=F32)
        y = d if y is None else y + d
    o_ref[...] = x_ref[...] + _rms(y, g_ref[...])


def _out_proj(acts, ws, x, g, tm):
    M, D = x.shape
    n_in = len(acts)
    in_specs = [pl.BlockSpec((tm, a.shape[1]), lambda i: (i, 0)) for a in acts]
    in_specs += [pl.BlockSpec(w.shape, lambda i: (0, 0)) for w in ws]
    in_specs += [pl.BlockSpec((tm, D), lambda i: (i, 0)), pl.BlockSpec((1, D), lambda i: (0, 0))]
    return pl.pallas_call(
        functools.partial(_out_proj_kernel, n_in=n_in),
        grid=(M // tm,),
        in_specs=in_specs,
        out_specs=pl.BlockSpec((tm, D), lambda i: (i, 0)),
        out_shape=jax.ShapeDtypeStruct((M, D), F32),
        compiler_params=_cparams(("parallel",)),
        name="out_proj",
    )(*acts, *ws, x, g)


def _ffn_kernel(x_ref, g1_ref, wu_ref, wd_ref, g2_ref, o_ref, xn_ref, acc_ref):
    f = pl.program_id(1)

    @pl.when(f == 0)
    def _():
        xn_ref[...] = _rms(x_ref[...], g1_ref[...]).astype(BF16)
        acc_ref[...] = jnp.zeros_like(acc_ref)

    hid = jnp.dot(xn_ref[...], wu_ref[...], preferred_element_type=F32)
    hid = jnp.square(jnp.maximum(hid, 0.0))
    acc_ref[...] += jnp.dot(hid.astype(BF16), wd_ref[...], preferred_element_type=F32)

    @pl.when(f == pl.num_programs(1) - 1)
    def _():
        o_ref[...] = x_ref[...] + _rms(acc_ref[...], g2_ref[...])


def _ffn(x, g1, wu, wd, g2, tm, tf):
    M, D = x.shape
    FF = wu.shape[1]
    return pl.pallas_call(
        _ffn_kernel,
        grid=(M // tm, FF // tf),
        in_specs=[
            pl.BlockSpec((tm, D), lambda i, f: (i, 0)),
            pl.BlockSpec((1, D), lambda i, f: (0, 0)),
            pl.BlockSpec((D, tf), lambda i, f: (0, f)),
            pl.BlockSpec((tf, D), lambda i, f: (f, 0)),
            pl.BlockSpec((1, D), lambda i, f: (0, 0)),
        ],
        out_specs=pl.BlockSpec((tm, D), lambda i, f: (i, 0)),
        out_shape=jax.ShapeDtypeStruct((M, D), F32),
        scratch_shapes=[pltpu.VMEM((tm, D), BF16), pltpu.VMEM((tm, D), F32)],
        compiler_params=_cparams(("parallel", "arbitrary")),
        name="ffn",
    )(x, g1, wu, wd, g2)


def _lru_gates(xc, wa_ref, ba, wx_ref, bx, lam):
    xcb = xc.astype(BF16)
    ng = wa_ref.shape[0]
    gw = wa_ref.shape[1]
    ra, ri = [], []
    for g in range(ng):
        xs = xcb[:, g * gw:(g + 1) * gw]
        ra.append(jnp.dot(xs, wa_ref[g], preferred_element_type=F32))
        ri.append(jnp.dot(xs, wx_ref[g], preferred_element_type=F32))
    r = jax.nn.sigmoid(jnp.concatenate(ra, axis=1) + ba)
    i = jax.nn.sigmoid(jnp.concatenate(ri, axis=1) + bx)
    log_a = -LRU_C * r * _softplus(-lam)
    a = jnp.exp(log_a)
    th = jnp.tanh(log_a)
    b = jnp.sqrt(-2.0 * th / (1.0 - th)) * (i * xc)
    return a, b


def _conv4(xe, off, T, w_ref, b):
    y = b
    for j in range(CONV_W):
        s = off - (CONV_W - 1) + j
        y = y + w_ref[j:j + 1, :] * xe[s:s + T]
    return y


def _rglru_prompt_kernel(xa_ref, ga_ref, cw_ref, cb_ref, wa_ref, ba_ref, wx_ref, bx_ref, lam_ref,
                         ya_ref, hl_ref, tail_ref, hc_ref, a_s, b_s, h_s):
    t = pl.program_id(1)
    T = xa_ref.shape[0]

    @pl.when(t == 0)
    def _():
        tail_ref[...] = jnp.zeros_like(tail_ref)
        hc_ref[...] = jnp.zeros_like(hc_ref)

    x = xa_ref[...]
    xe = jnp.concatenate([tail_ref[...], x], axis=0)
    xc = _conv4(xe, SUBLANES, T, cw_ref, cb_ref[...])
    tail_ref[...] = x[T - SUBLANES:, :]
    a, b = _lru_gates(xc, wa_ref, ba_ref[...], wx_ref, bx_ref[...], lam_ref[...])
    a_s[...] = a
    b_s[...] = b

    def body(s, h):
        h = a_s[pl.ds(s, 1), :] * h + b_s[pl.ds(s, 1), :]
        h_s[pl.ds(s, 1), :] = h
        return h

    h = lax.fori_loop(0, T, body, hc_ref[...], unroll=8)
    hc_ref[...] = h
    ya_ref[...] = (h_s[...] * _gelu_tanh(ga_ref[...])).astype(ya_ref.dtype)

    @pl.when(t == pl.num_programs(1) - 1)
    def _():
        hl_ref[...] = h


def _rglru_prompt(xaga, cw, cb, wa_bd, ba, wx_bd, bx, lam, ts):
    B, S, C2 = xaga.shape
    C = C2 // 2
    wspec = lambda a: pl.BlockSpec(a.shape, lambda b, t: (0,) * a.ndim)
    return pl.pallas_call(
        _rglru_prompt_kernel,
        grid=(B, S // ts),
        in_specs=[
            pl.BlockSpec((None, ts, C), lambda b, t: (b, t, 0)),
            pl.BlockSpec((None, ts, C), lambda b, t: (b, t, 1)),
            wspec(cw), wspec(cb), wspec(wa_bd), wspec(ba), wspec(wx_bd), wspec(bx), wspec(lam),
        ],
        out_specs=[
            pl.BlockSpec((None, ts, C), lambda b, t: (b, t, 0)),
            pl.BlockSpec((None, 1, C), lambda b, t: (b, 0, 0)),
        ],
        out_shape=[jax.ShapeDtypeStruct((B, S, C), BF16), jax.ShapeDtypeStruct((B, 1, C), F32)],
        scratch_shapes=[pltpu.VMEM((SUBLANES, C), F32), pltpu.VMEM((1, C), F32),
                        pltpu.VMEM((ts, C), F32), pltpu.VMEM((ts, C), F32), pltpu.VMEM((ts, C), F32)],
        compiler_params=_cparams(("parallel", "arbitrary")),
        name="rglru_prompt",
    )(xaga, xaga, cw, cb, wa_bd, ba, wx_bd, bx, lam)


def _rglru_sample_kernel(xa_ref, ga_ref, st_ref, h0_ref, cw_ref, cb_ref, wa_ref, ba_ref, wx_ref, bx_ref,
                         lam_ref, ya_ref, hl_ref):
    T = xa_ref.shape[0]
    rows = [st_ref[j] for j in range(CONV_W - 1)] + [xa_ref[s] for s in range(T)]
    h = h0_ref[...]
    for s in range(T):
        y = cb_ref[...]
        for j in range(CONV_W):
            y = y + cw_ref[j:j + 1, :] * rows[s + j]
        a, b = _lru_gates(y, wa_ref, ba_ref[...], wx_ref, bx_ref[...], lam_ref[...])
        h = a * h + b
        ya_ref[s] = (h * _gelu_tanh(ga_ref[s])).astype(ya_ref.dtype)
    hl_ref[...] = h


def _rglru_sample(xa_t, ga_t, st_t, h0, cw, cb, wa_bd, ba, wx_bd, bx, lam):
    T, B, C = xa_t.shape
    args = (xa_t, ga_t, st_t, h0, cw, cb, wa_bd, ba, wx_bd, bx, lam)
    return pl.pallas_call(
        _rglru_sample_kernel,
        grid=(1,),
        in_specs=[pl.BlockSpec(a.shape, lambda i, n=a.ndim: (0,) * n) for a in args],
        out_specs=[pl.BlockSpec((T, B, C), lambda i: (0, 0, 0)), pl.BlockSpec((B, C), lambda i: (0, 0))],
        out_shape=[jax.ShapeDtypeStruct((T, B, C), F32), jax.ShapeDtypeStruct((B, C), F32)],
        compiler_params=_cparams(("arbitrary",)),
        name="rglru_sample",
    )(*args)


def _attn_prompt_kernel(q_ref, k_ref, v_ref, o_ref, m_s, l_s, acc_s, *, groups, qb):
    S = q_ref.shape[0]
    E = q_ref.shape[1]
    dist2 = (lax.broadcasted_iota(jnp.int32, (qb, 2 * qb), 0) + qb
             - lax.broadcasted_iota(jnp.int32, (qb, 2 * qb), 1))
    dist1 = lax.broadcasted_iota(jnp.int32, (qb, qb), 0) - lax.broadcasted_iota(jnp.int32, (qb, qb), 1)

    def rows(start, d):
        return pl.ds(start, qb) if d == 1 else pl.ds(start, qb, stride=d)

    def block(start, d, back, has_prev, first):
        qs = rows(start, d)
        q = q_ref[qs, :].astype(BF16)
        if has_prev:
            ps = rows(start - qb * d, d)
            k = jnp.concatenate([k_ref[ps, :], k_ref[qs, :]], axis=0).astype(BF16)
            v = jnp.concatenate([v_ref[ps, :], v_ref[qs, :]], axis=0).astype(BF16)
            dist = dist2
        else:
            k = k_ref[qs, :].astype(BF16)
            v = v_ref[qs, :].astype(BF16)
            dist = dist1
        valid = (dist >= 0) & (dist <= back)
        s = lax.dot_general(q, k, (((1,), (1,)), ((), ())), preferred_element_type=F32)
        s = jnp.where(valid, s, NEG_INF)
        m_b = jnp.max(s, axis=-1, keepdims=True)
        p = jnp.exp(s - m_b)
        l_b = jnp.sum(p, axis=-1, keepdims=True)
        acc_b = jnp.dot(p.astype(BF16), v, preferred_element_type=F32)
        m_b = jnp.broadcast_to(m_b, (qb, E))
        l_b = jnp.broadcast_to(l_b, (qb, E))
        if first:
            m_s[qs, :] = m_b
            l_s[qs, :] = l_b
            acc_s[qs, :] = acc_b
        else:
            m_o = m_s[qs, :]
            m_n = jnp.maximum(m_o, m_b)
            a_o = jnp.exp(m_o - m_n)
            a_b = jnp.exp(m_b - m_n)
            l_s[qs, :] = l_s[qs, :] * a_o + l_b * a_b
            acc_s[qs, :] = acc_s[qs, :] * a_o + acc_b * a_b
            m_s[qs, :] = m_n

    for gi, (w, d) in enumerate(groups):
        back = w // d
        L = S // d
        nb = L // qb
        first = gi == 0

        def first_blocks(r, c, d=d, back=back, first=first):
            block(r, d, back, False, first)
            return c

        if d == 1:
            block(0, d, back, False, first)
        else:
            lax.fori_loop(0, d, first_blocks, 0)
        if nb > 1:
            def later_blocks(idx, c, d=d, back=back, first=first):
                n = idx // d + 1
                r = idx % d
                start = n * (qb * d) + r
                if d == 1:
                    start = pl.multiple_of(start, qb)
                block(start, d, back, True, first)
                return c

            lax.fori_loop(0, (nb - 1) * d, later_blocks, 0)

    o_ref[...] = (acc_s[...] / l_s[...]).astype(o_ref.dtype)


def _attn_prompt(q, k, v, groups, qb):
    B, S, D = q.shape
    nh = D // LANES
    for (w, d) in groups:
        assert S % (d * qb) == 0 and w // d <= qb
    spec = pl.BlockSpec((None, S, LANES), lambda b, h: (b, 0, h))
    return pl.pallas_call(
        functools.partial(_attn_prompt_kernel, groups=tuple(groups), qb=qb),
        grid=(B, nh),
        in_specs=[spec, spec, spec],
        out_specs=spec,
        out_shape=jax.ShapeDtypeStruct((B, S, D), BF16),
        scratch_shapes=[pltpu.VMEM((S, LANES), F32)] * 3,
        compiler_params=_cparams(("parallel", "parallel")),
        name="attn_prompt",
    )(q, k, v)


def _attn_sample_kernel(q_ref, kn_ref, vn_ref, kc_ref, vc_ref, mc_ref, mn_ref, o_ref, *, T, nh):
    D = q_ref.shape[1]
    PADR = LANES
    lane_head = lax.broadcasted_iota(jnp.int32, (SUBLANES, D), 1) // (D // nh)
    sub = lax.broadcasted_iota(jnp.int32, (SUBLANES, D), 0)
    head_mask = lane_head == sub
    qrows = []
    for t in range(T):
        qt = jnp.broadcast_to(q_ref[t:t + 1, :], (SUBLANES, D))
        qrows.append(jnp.where(head_mask, qt, 0.0))
    qrows.append(jnp.zeros((PADR - T * SUBLANES, D), F32))
    qbd = jnp.concatenate(qrows, axis=0).astype(BF16)
    npad = kn_ref.shape[0]
    zpad = jnp.zeros((PADR - npad, D), BF16)
    kn = jnp.concatenate([kn_ref[...].astype(BF16), zpad], axis=0)
    vn = jnp.concatenate([vn_ref[...].astype(BF16), zpad], axis=0)
    dn = (((1,), (1,)), ((), ()))
    s_c = lax.dot_general(kc_ref[...].astype(BF16), qbd, dn, preferred_element_type=F32)
    s_n = lax.dot_general(kn, qbd, dn, preferred_element_type=F32)
    mult_c = mc_ref[...]
    mult_n = mn_ref[...]
    s_c = jnp.where(mult_c > 0.0, s_c, NEG_INF)
    s_n = jnp.where(mult_n > 0.0, s_n, NEG_INF)
    m = jnp.maximum(jnp.max(s_c, axis=0, keepdims=True), jnp.max(s_n, axis=0, keepdims=True))
    m = jnp.where(m == NEG_INF, 0.0, m)
    p_c = (mult_c * jnp.exp(s_c - m)).T
    p_n = (mult_n * jnp.exp(s_n - m)).T
    den = jnp.sum(p_c, axis=1, keepdims=True) + jnp.sum(p_n, axis=1, keepdims=True)
    acc = jnp.dot(p_c.astype(BF16), vc_ref[...].astype(BF16), preferred_element_type=F32)
    acc = acc + jnp.dot(p_n.astype(BF16), vn, preferred_element_type=F32)
    out = acc / jnp.where(den > 0.0, den, 1.0)
    o_ref[...] = jnp.zeros_like(o_ref)
    for t in range(T):
        blk = out[t * SUBLANES:(t + 1) * SUBLANES, :]
        o_ref[t:t + 1, :] = jnp.sum(jnp.where(head_mask, blk, 0.0), axis=0, keepdims=True)


def _attn_sample(q, kn, vn, kc, vc, mult_c, mult_n, T, nh):
    B, Tp, D = q.shape
    wb = kc.shape[1]
    assert nh == SUBLANES and T * SUBLANES <= LANES
    small = pl.BlockSpec((None, Tp, D), lambda b: (b, 0, 0))
    cache = pl.BlockSpec((None, wb, D), lambda b: (b, 0, 0))
    return pl.pallas_call(
        functools.partial(_attn_sample_kernel, T=T, nh=nh),
        grid=(B,),
        in_specs=[small, small, small, cache, cache,
                  pl.BlockSpec(mult_c.shape, lambda b: (0, 0)), pl.BlockSpec(mult_n.shape, lambda b: (0, 0))],
        out_specs=small,
        out_shape=jax.ShapeDtypeStruct((B, Tp, D), F32),
        compiler_params=_cparams(("parallel",)),
        name="attn_sample",
    )(q, kn, vn, kc, vc, mult_c, mult_n)


def _mlstm_pre_kernel(xm_ref, tail0_ref, cw_ref, cb_ref, wq_ref, wk_ref, wv_ref, wg_ref, bg_ref,
                      q_ref, k_ref, v_ref, xc_ref, g_ref, tail_ref, *, k_scale):
    t = pl.program_id(1)
    T = xm_ref.shape[0]

    @pl.when(t == 0)
    def _():
        tail_ref[...] = tail0_ref[...]

    xm = xm_ref[...]
    xe = jnp.concatenate([tail_ref[...], xm], axis=0)
    xc = _silu(_conv4(xe, SUBLANES, T, cw_ref, cb_ref[...]))
    if T >= SUBLANES:
        tail_ref[...] = xm[T - SUBLANES:, :]
    xcb = xc.astype(BF16)
    xmb = xm.astype(BF16)
    ng, gw = wq_ref.shape[0], wq_ref.shape[1]
    qs, ks, vs = [], [], []
    for g in range(ng):
        sl = slice(g * gw, (g + 1) * gw)
        qs.append(jnp.dot(xcb[:, sl], wq_ref[g], preferred_element_type=F32))
        ks.append(jnp.dot(xcb[:, sl], wk_ref[g], preferred_element_type=F32))
        vs.append(jnp.dot(xmb[:, sl], wv_ref[g], preferred_element_type=F32))
    q = jnp.concatenate(qs, axis=1)
    k = jnp.concatenate(ks, axis=1)
    v = jnp.concatenate(vs, axis=1)
    qkv = jnp.concatenate([q, k, v], axis=1).astype(BF16)
    g_ref[...] = jnp.dot(qkv, wg_ref[...], preferred_element_type=F32) + bg_ref[...]
    q_ref[...] = q.astype(q_ref.dtype)
    k_ref[...] = (k * k_scale).astype(k_ref.dtype)
    v_ref[...] = v.astype(v_ref.dtype)
    xc_ref[...] = xc.astype(xc_ref.dtype)


def _mlstm_pre(xmz, tail0, cw, cb, wq_bd, wk_bd, wv_bd, wg, bg, ts, k_scale):
    B, S, C2 = xmz.shape
    C = C2 // 2
    wspec = lambda a: pl.BlockSpec(a.shape, lambda b, t: (0,) * a.ndim)
    act = pl.BlockSpec((None, ts, C), lambda b, t: (b, t, 0))
    return pl.pallas_call(
        functools.partial(_mlstm_pre_kernel, k_scale=k_scale),
        grid=(B, S // ts),
        in_specs=[act, pl.BlockSpec((None, SUBLANES, C), lambda b, t: (b, 0, 0)),
                  wspec(cw), wspec(cb), wspec(wq_bd), wspec(wk_bd), wspec(wv_bd), wspec(wg), wspec(bg)],
        out_specs=[act, act, act, act, pl.BlockSpec((None, ts, 2 * LANES), lambda b, t: (b, t, 0))],
        out_shape=[jax.ShapeDtypeStruct((B, S, C), BF16)] * 4 + [jax.ShapeDtypeStruct((B, S, 2 * LANES), F32)],
        scratch_shapes=[pltpu.VMEM((SUBLANES, C), F32)],
        compiler_params=_cparams(("parallel", "arbitrary")),
        name="mlstm_pre",
    )(xmz, tail0, cw, cb, wq_bd, wk_bd, wv_bd, wg, bg)


def _mlstm_core_kernel(*refs, c, n_valid, nh, zero_init):
    q_ref, k_ref, v_ref, xc_ref, z_ref, g_ref, nw_ref, sk_ref = refs[:8]
    pos = 8
    if not zero_init:
        c0_ref, n0_ref, m0_ref = refs[8:11]
        pos = 11
    y_ref, C_ref, n_ref, m_ref = refs[pos:pos + 4]
    ch = pl.program_id(1)
    c_in = q_ref.shape[0]
    DH = q_ref.shape[1] // nh

    @pl.when(ch == 0)
    def _():
        if zero_init:
            C_ref[...] = jnp.zeros_like(C_ref)
            n_ref[...] = jnp.zeros_like(n_ref)
            m_ref[...] = jnp.zeros_like(m_ref)
        else:
            C_ref[...] = c0_ref[...]
            n_ref[...] = n0_ref[...]
            m_ref[...] = m0_ref[...]

    def padded(x):
        if c_in == c:
            return x
        return jnp.concatenate([x, jnp.zeros((c - c_in, x.shape[1]), x.dtype)], axis=0)

    g = padded(g_ref[...])
    row = lax.broadcasted_iota(jnp.int32, (c, LANES), 0)
    valid = row < n_valid
    gi = jnp.where(valid, g[:, :LANES], NEG_INF)
    lf = jnp.where(valid, -_softplus(-g[:, LANES:]), 0.0)
    tri_f = (lax.broadcasted_iota(jnp.int32, (c, c), 1) <= lax.broadcasted_iota(jnp.int32, (c, c), 0))
    bcum = jnp.dot(tri_f.astype(F32), lf, preferred_element_type=F32, precision=lax.Precision.HIGHEST)
    bcum_t = bcum.T
    gi_t = gi.T
    causal = tri_f

    for h in range(nh):
        sl = slice(h * DH, (h + 1) * DH)
        qh = padded(q_ref[:, sl])
        kh = padded(k_ref[:, sl])
        vh = padded(v_ref[:, sl])
        bc_col = bcum[:, h:h + 1]
        bc_row = bcum_t[h:h + 1, :]
        i_row = gi_t[h:h + 1, :]
        i_col = gi[:, h:h + 1]
        m_prev = m_ref[h:h + 1, 0:1]
        log_intra = jnp.where(causal, bc_col - bc_row + i_row, NEG_INF)
        log_inter = bc_col + m_prev
        m_t = jnp.maximum(log_inter, jnp.max(log_intra, axis=1, keepdims=True))
        qk = lax.dot_general(qh, kh, (((1,), (1,)), ((), ())), preferred_element_type=F32)
        w_intra = jnp.exp(log_intra - m_t) * qk
        w_inter = jnp.exp(log_inter - m_t)
        C_old = C_ref[h]
        n_old = n_ref[h:h + 1, :]
        num = jnp.dot(w_intra.astype(BF16), vh, preferred_element_type=F32)
        num = num + w_inter * jnp.dot(qh, C_old.astype(BF16), preferred_element_type=F32)
        qn = jnp.sum(qh.astype(F32) * n_old, axis=1, keepdims=True)
        den = jnp.sum(w_intra, axis=1, keepdims=True) + w_inter * qn
        hh = num / jnp.maximum(jnp.abs(den), jnp.exp(-m_t))
        m_new = m_t[c - 1:c, :]
        bc_last = bc_col[c - 1:c, :]
        w_state = jnp.exp(bc_last - bc_col + i_col - m_new)
        decay = jnp.exp(bc_last + m_prev - m_new)
        kw = kh.astype(F32) * w_state
        C_ref[h] = decay * C_old + lax.dot_general(
            kw.astype(BF16), vh, (((0,), (0,)), ((), ())), preferred_element_type=F32)
        n_ref[h:h + 1, :] = decay * n_old + jnp.sum(kw, axis=0, keepdims=True)
        m_ref[h:h + 1, :] = jnp.broadcast_to(m_new, (1, LANES))
        hn = hh * lax.rsqrt(jnp.mean(hh * hh, axis=-1, keepdims=True) + EPS)
        hn = hn[:c_in] * nw_ref[:, sl]
        y = (hn + sk_ref[:, sl] * xc_ref[:, sl].astype(F32)) * _silu(z_ref[:, sl])
        y_ref[:, sl] = y.astype(y_ref.dtype)


def _mlstm_core(q, k, v, xc, xmz, gates, nw, sk, state, c, n_valid, nh):
    B, S, C = q.shape
    c_in = min(c, S)
    DH = C // nh
    zero_init = state is None
    act = pl.BlockSpec((None, c_in, C), lambda b, t: (b, t, 0))
    in_specs = [act, act, act, act,
                pl.BlockSpec((None, c_in, C), lambda b, t: (b, t, 1)),
                pl.BlockSpec((None, c_in, 2 * LANES), lambda b, t: (b, t, 0)),
                pl.BlockSpec((1, C), lambda b, t: (0, 0)), pl.BlockSpec((1, C), lambda b, t: (0, 0))]
    args = [q, k, v, xc, xmz, gates, nw, sk]
    C_spec = pl.BlockSpec((None, nh, DH, DH), lambda b, t: (b, 0, 0, 0))
    n_spec = pl.BlockSpec((None, nh, DH), lambda b, t: (b, 0, 0))
    m_spec = pl.BlockSpec((None, nh, LANES), lambda b, t: (b, 0, 0))
    if not zero_init:
        in_specs += [C_spec, n_spec, m_spec]
        args += list(state)
    return pl.pallas_call(
        functools.partial(_mlstm_core_kernel, c=c, n_valid=n_valid, nh=nh, zero_init=zero_init),
        grid=(B, S // c_in),
        in_specs=in_specs,
        out_specs=[act, C_spec, n_spec, m_spec],
        out_shape=[jax.ShapeDtypeStruct((B, S, C), BF16),
                   jax.ShapeDtypeStruct((B, nh, DH, DH), F32),
                   jax.ShapeDtypeStruct((B, nh, DH), F32),
                   jax.ShapeDtypeStruct((B, nh, LANES), F32)],
        compiler_params=_cparams(("parallel", "arbitrary")),
        name="mlstm_core",
    )(*args)


def _rope_tables(pos, hd):
    rot = hd // 4
    half = rot // 2
    inv = ROPE_THETA ** (-jnp.arange(0, rot, 2, dtype=F32) / rot)
    ang = pos.astype(F32)[:, None] * inv[None, :]
    cos, sin = jnp.cos(ang), jnp.sin(ang)
    n = pos.shape[0]
    c = jnp.concatenate([cos, cos, jnp.ones((n, hd - rot), F32)], axis=1)
    sa = jnp.concatenate([-sin, jnp.zeros((n, hd - half), F32)], axis=1)
    sb = jnp.concatenate([jnp.zeros((n, half), F32), sin, jnp.zeros((n, hd - rot), F32)], axis=1)
    return c, sa, sb


def _regroup_block_diag(w, group):
    nb, bi, bo = w.shape
    per = group // bi
    wg = w.reshape(nb // per, per, bi, bo)
    eye = jnp.eye(per, dtype=w.dtype)
    dense = jnp.einsum("gpio,pq->gpiqo", wg, eye)
    return dense.reshape(nb // per, per * bi, per * bo)


def _sample_multiplicity(wb, T, nh):
    col_t = jnp.arange(LANES) // nh
    col_ok = col_t < T

    def mult(delta):
        m = jnp.zeros(delta.shape, F32)
        for (w, d) in DILATIONS:
            m = m + ((delta >= 0) & (delta % d == 0) & (delta <= (w // d) * d)).astype(F32)
        return m

    idx = jnp.arange(wb)[:, None]
    mc = jnp.where(col_ok[None, :], mult(wb + col_t[None, :] - idx), 0.0)
    tn = jnp.arange(LANES)[:, None]
    mn = jnp.where(col_ok[None, :] & (tn < T), mult(col_t[None, :] - tn), 0.0)
    return mc, mn


def _layer0(x, rope, p, prompt, state):
    Bg, L, D = x.shape
    M = Bg * L
    hd = D // H_B
    tm = _tile(M, 512)
    x2 = x.reshape(M, D)
    xaga, q, k, v = _norm_proj(x2, p["g_mix_pre0"], p["w_in_ab"], ((0, 2), (2, 1), (3, 1), (4, 1)),
                               tm, D, rope=rope, rope_q=2, rope_k=3, q_scale=hd ** -0.5)
    lru_w = (p["conv_a_w"], p["conv_a_b"], p["wa_bd"], p["lru_ba"], p["wx_bd"], p["lru_bx"], p["lam"])
    if prompt:
        xaga3 = xaga.reshape(Bg, L, 2 * D)
        ya, h_last = _rglru_prompt(xaga3, *lru_w, ts=_tile(L, 256))
        ya = ya.reshape(M, D)
        h_last = h_last.reshape(Bg, D)
        new_conv = xaga3[:, L - (CONV_W - 1):, :D]
        yb = _attn_prompt(q.reshape(Bg, L, D), k.reshape(Bg, L, D), v.reshape(Bg, L, D),
                          sorted(DILATIONS, key=lambda wd: -wd[1]), QBLOCK).reshape(M, D)
        wb = min(MAX_WINDOW, L)
        new_k = k.reshape(Bg, L, H_B, hd)[:, L - wb:]
        new_v = v.reshape(Bg, L, H_B, hd)[:, L - wb:]
    else:
        conv0, h0, k_buf, v_buf = state
        xaga_t = xaga.reshape(Bg, L, 2 * D).transpose(1, 0, 2)
        xa_t, ga_t = xaga_t[:, :, :D], xaga_t[:, :, D:]
        st_t = conv0.transpose(1, 0, 2)
        ya_t, h_last = _rglru_sample(xa_t, ga_t, st_t, h0, *lru_w)
        ya = ya_t.transpose(1, 0, 2).reshape(M, D)
        new_conv = jnp.concatenate([conv0, xaga.reshape(Bg, L, 2 * D)[:, :, :D]], axis=1)[:, -(CONV_W - 1):]
        wb = k_buf.shape[1]
        tp = 16
        pad = lambda a: jnp.pad(a.reshape(Bg, L, D), ((0, 0), (0, tp - L), (0, 0)))
        mc, mn = _sample_multiplicity(wb, L, H_B)
        yb = _attn_sample(pad(q), pad(k), pad(v), k_buf.reshape(Bg, wb, D), v_buf.reshape(Bg, wb, D),
                          mc, mn, L, H_B)[:, :L].reshape(M, D)
        new_k = k.reshape(Bg, L, H_B, hd)
        new_v = v.reshape(Bg, L, H_B, hd)
    x1 = _out_proj([ya, yb], [p["w_out_a"], p["w_out_b"]], x2, p["g_mix_post0"], tm)
    x2o = _ffn(x1, p["g_ffn_pre0"], p["w_up0"], p["w_dn0"], p["g_ffn_post0"], _tile(M, 1024), 1024)
    return x2o.reshape(Bg, L, D), (new_conv, h_last, new_k, new_v)


def _layer1(x, p, prompt, state):
    Bg, L, D = x.shape
    M = Bg * L
    x2 = x.reshape(M, D)
    tm = _tile(M, 512)
    DC = p["w_in_c"].shape[1] // 2
    DH = DC // NH_C
    (xmz,) = _norm_proj(x2, p["g_mix_pre1"], p["w_in_c"], ((0, 2 * DC // 1024),), tm, 1024)
    xmz3 = xmz.reshape(Bg, L, 2 * DC)
    pre_w = (p["conv_c_w"], p["conv_c_b"], p["wq_bd"], p["wk_bd"], p["wv_bd"], p["wg"], p["bg"])
    if prompt:
        tail0 = jnp.zeros((Bg, SUBLANES, DC), F32)
        c = _tile(L, 256)
        q, k, v, xc, gates = _mlstm_pre(xmz3, tail0, *pre_w, ts=c, k_scale=DH ** -0.5)
        y, C, n, m = _mlstm_core(q, k, v, xc, xmz3, gates, p["mlstm_norm"], p["mlstm_skip"], None, c, c, NH_C)
        new_conv = xmz3[:, L - (CONV_W - 1):, :DC]
        y = y.reshape(M, DC)
    else:
        conv0, C0, n0, m0 = state
        lp = 16
        xmz_p = jnp.pad(xmz3, ((0, 0), (0, lp - L), (0, 0)))
        tail0 = jnp.pad(conv0, ((0, 0), (SUBLANES - (CONV_W - 1), 0), (0, 0)))
        q, k, v, xc, gates = _mlstm_pre(xmz_p, tail0, *pre_w, ts=lp, k_scale=DH ** -0.5)
        m0b = jnp.broadcast_to(m0[:, :, None], m0.shape + (LANES,))
        y, C, n, m = _mlstm_core(q, k, v, xc, xmz_p, gates, p["mlstm_norm"], p["mlstm_skip"],
                                 (C0, n0, m0b), 128, L, NH_C)
        new_conv = jnp.concatenate([conv0, xmz3[:, :, :DC]], axis=1)[:, -(CONV_W - 1):]
        y = y[:, :L].reshape(M, DC)
    x1 = _out_proj([y], [p["w_out_c"]], x2, p["g_mix_post1"], tm)
    x2o = _ffn(x1, p["g_ffn_pre1"], p["w_up1"], p["w_dn1"], p["g_ffn_post1"], _tile(M, 1024), 1024)
    return x2o.reshape(Bg, L, D), (new_conv, C, n, m[:, :, 0])


def kernel(x_prompt, x_sample, state_rglru_conv, state_rglru_h, cache_swa_k, cache_swa_v, state_mlstm_conv, state_mlstm_C, state_mlstm_n, state_mlstm_m, norm_mix_pre, norm_mix_post, norm_ffn_pre, norm_ffn_post, w_ffn_up, w_ffn_down, w_in_ab, conv_a_w, conv_a_b, lru_wa, lru_ba, lru_wx, lru_bx, lru_lambda, w_out_ab, w_in_c, conv_c_w, conv_c_b, mlstm_wq, mlstm_wk, mlstm_wv, mlstm_w_gate, mlstm_b_gate, mlstm_norm, mlstm_skip, w_out_c):
    B, S, D = x_prompt.shape
    Bs, Ts, _ = x_sample.shape
    DA = conv_a_w.shape[-1]
    DC = conv_c_w.shape[-1]
    hd = D // H_B
    row = lambda a: a.reshape(1, -1)
    wg = jnp.zeros((3 * DC, 2 * LANES), F32)
    wg = wg.at[:, :NH_C].set(mlstm_w_gate[0][:, :NH_C]).at[:, LANES:LANES + NH_C].set(mlstm_w_gate[0][:, NH_C:])
    bg = jnp.zeros((1, 2 * LANES), F32)
    bg = bg.at[0, :NH_C].set(mlstm_b_gate[0][:NH_C]).at[0, LANES:LANES + NH_C].set(mlstm_b_gate[0][NH_C:])
    p = {
        "g_mix_pre0": row(norm_mix_pre[0]), "g_mix_post0": row(norm_mix_post[0]),
        "g_ffn_pre0": row(norm_ffn_pre[0]), "g_ffn_post0": row(norm_ffn_post[0]),
        "g_mix_pre1": row(norm_mix_pre[1]), "g_mix_post1": row(norm_mix_post[1]),
        "g_ffn_pre1": row(norm_ffn_pre[1]), "g_ffn_post1": row(norm_ffn_post[1]),
        "w_up0": w_ffn_up[0].astype(BF16), "w_dn0": w_ffn_down[0].astype(BF16),
        "w_up1": w_ffn_up[1].astype(BF16), "w_dn1": w_ffn_down[1].astype(BF16),
        "w_in_ab": w_in_ab[0].astype(BF16),
        "conv_a_w": conv_a_w[0], "conv_a_b": row(conv_a_b[0]),
        "wa_bd": _regroup_block_diag(lru_wa[0], MXU_DIM).astype(BF16), "lru_ba": row(lru_ba[0]),
        "wx_bd": _regroup_block_diag(lru_wx[0], MXU_DIM).astype(BF16), "lru_bx": row(lru_bx[0]),
        "lam": row(lru_lambda[0]),
        "w_out_a": w_out_ab[0][:DA].astype(BF16), "w_out_b": w_out_ab[0][DA:].astype(BF16),
        "w_in_c": w_in_c[0].astype(BF16),
        "conv_c_w": conv_c_w[0], "conv_c_b": row(conv_c_b[0]),
        "wq_bd": _regroup_block_diag(mlstm_wq[0], MXU_DIM).astype(BF16),
        "wk_bd": _regroup_block_diag(mlstm_wk[0], MXU_DIM).astype(BF16),
        "wv_bd": _regroup_block_diag(mlstm_wv[0], MXU_DIM).astype(BF16),
        "wg": wg.astype(BF16), "bg": bg,
        "mlstm_norm": row(mlstm_norm[0]), "mlstm_skip": row(mlstm_skip[0]),
        "w_out_c": w_out_c[0].astype(BF16),
    }
    dt = state_rglru_conv.dtype

    xp, st0 = _layer0(x_prompt, _rope_tables(jnp.arange(S), hd), p, True, None)
    xp, st1 = _layer1(xp, p, True, None)
    pos_s = PAST_LEN + (jnp.arange(Bs * Ts) % Ts)
    xs, ss0 = _layer0(x_sample, _rope_tables(pos_s, hd), p, False,
                      (state_rglru_conv[0], state_rglru_h[0], cache_swa_k[0], cache_swa_v[0]))
    xs, ss1 = _layer1(xs, p, False, (state_mlstm_conv[0], state_mlstm_C[0], state_mlstm_n[0], state_mlstm_m[0]))

    lead = lambda a: a[None].astype(dt)
    return (xp, xs,
            lead(st0[0]), lead(ss0[0]), lead(st0[1]), lead(ss0[1]),
            lead(st0[2]), lead(ss0[2]), lead(st0[3]), lead(ss0[3]),
            lead(st1[0]), lead(ss1[0]), lead(st1[1]), lead(ss1[1]),
            lead(st1[2]), lead(ss1[2]), lead(st1[3]), lead(ss1[3]))
```

```python
import functools
import math

import jax
import jax.numpy as jnp
from jax import lax
from jax.experimental import pallas as pl
from jax.experimental.pallas import tpu as pltpu

F32 = jnp.float32
BF16 = jnp.bfloat16

NA_BLOCKS = 16
CONV_W = 4
LRU_C = 8.0
H_B = 8
ROPE_THETA = 500000.0
DILATIONS = ((128, 1), (512, 4), (2048, 16))
MAX_WINDOW = 2048
QBLOCK = 128
NH_C = 4
QKV_BLOCK = 4
PAST_LEN = 16384
EPS = 1e-6

LANES = 128
SUBLANES = 8
MXU_DIM = 256
VMEM_LIMIT = 56 * 1024 * 1024

NEG_INF = float("-inf")


def _cparams(sem):
    return pltpu.CompilerParams(dimension_semantics=sem, vmem_limit_bytes=VMEM_LIMIT)


def _tile(n, pref):
    t = min(n, pref)
    while n % t:
        t -= 1
    return t


def _rms(x, g):
    return x * lax.rsqrt(jnp.mean(x * x, axis=-1, keepdims=True) + EPS) * g


def _softplus(z):
    return jnp.maximum(z, 0.0) + jnp.log1p(jnp.exp(-jnp.abs(z)))


def _gelu_tanh(x):
    c = math.sqrt(2.0 / math.pi)
    return x * (0.5 * (1.0 + jnp.tanh(c * (x + 0.044715 * (x * x * x)))))


def _silu(x):
    return x * jax.nn.sigmoid(x)


def _rope_tile(y, c, sa, sb):
    outs = []
    for h in range(y.shape[1] // LANES):
        yh = y[:, h * LANES:(h + 1) * LANES]
        outs.append(yh * c + pltpu.roll(yh, LANES - 16, 1) * sa + pltpu.roll(yh, 16, 1) * sb)
    return jnp.concatenate(outs, axis=1)


def _norm_proj_kernel(*refs, groups, rope_q, rope_k, q_scale):
    n_out = len(groups)
    x_ref, g_ref, w_ref = refs[:3]
    pos = 3
    if rope_q is not None:
        c_ref, sa_ref, sb_ref = refs[3:6]
        pos = 6
    out_refs = refs[pos:pos + n_out]
    xn_ref = refs[pos + n_out]
    j = pl.program_id(1)

    @pl.when(j == 0)
    def _():
        xn_ref[...] = _rms(x_ref[...], g_ref[...]).astype(BF16)

    y = jnp.dot(xn_ref[...], w_ref[...], preferred_element_type=F32)
    for (start, count), o_ref in zip(groups, out_refs):
        for jj in range(start, start + count):
            @pl.when(j == jj)
            def _(o_ref=o_ref, jj=jj):
                if rope_q is not None and jj == rope_q:
                    o_ref[...] = (_rope_tile(y, c_ref[...], sa_ref[...], sb_ref[...]) * q_scale).astype(o_ref.dtype)
                elif rope_q is not None and jj == rope_k:
                    o_ref[...] = _rope_tile(y, c_ref[...], sa_ref[...], sb_ref[...]).astype(o_ref.dtype)
                else:
                    o_ref[...] = y.astype(o_ref.dtype)


def _norm_proj(x, g, w, groups, tm, tn, rope=None, rope_q=None, rope_k=None, q_scale=1.0):
    M, K = x.shape
    N = w.shape[1]
    nj = N // tn
    in_specs = [
        pl.BlockSpec((tm, K), lambda i, j: (i, 0)),
        pl.BlockSpec((1, K), lambda i, j: (0, 0)),
        pl.BlockSpec((K, tn), lambda i, j: (0, j)),
    ]
    args = [x, g, w]
    if rope is not None:
        assert rope[0].shape[0] % tm == 0 and M % tm == 0
        pos_blocks = rope[0].shape[0] // tm
        for t in rope:
            in_specs.append(pl.BlockSpec((tm, LANES), lambda i, j, pb=pos_blocks: (i % pb, 0)))
            args.append(t)
    out_specs, out_shapes = [], []
    for (start, count) in groups:
        out_specs.append(pl.BlockSpec(
            (tm, tn), lambda i, j, s=start, c=count: (i, jnp.clip(j - s, 0, c - 1))))
        out_shapes.append(jax.ShapeDtypeStruct((M, count * tn), F32))
    kern = functools.partial(_norm_proj_kernel, groups=tuple(groups),
                             rope_q=rope_q if rope is not None else None, rope_k=rope_k, q_scale=q_scale)
    return pl.pallas_call(
        kern,
        grid=(M // tm, nj),
        in_specs=in_specs,
        out_specs=out_specs,
        out_shape=out_shapes,
        scratch_shapes=[pltpu.VMEM((tm, K), BF16)],
        compiler_params=_cparams(("parallel", "arbitrary")),
        name="norm_proj",
    )(*args)


def _out_proj_kernel(*refs, n_in):
    a_refs = refs[:n_in]
    w_refs = refs[n_in:2 * n_in]
    x_ref, g_ref, o_ref = refs[2 * n_in:]
    y = None
    for a_ref, w_ref in zip(a_refs, w_refs):
        d = jnp.dot(a_ref[...].astype(BF16), w_ref[...], preferred_element_type=F32)
        y = d if y is None else y + d
    o_ref[...] = x_ref[...] + _rms(y, g_ref[...])


def _out_proj(acts, ws, x, g, tm):
    M, D = x.shape
    n_in = len(acts)
    in_specs = [pl.BlockSpec((tm, a.shape[1]), lambda i: (i, 0)) for a in acts]
    in_specs += [pl.BlockSpec(w.shape, lambda i: (0, 0)) for w in ws]
    in_specs += [pl.BlockSpec((tm, D), lambda i: (i, 0)), pl.BlockSpec((1, D), lambda i: (0, 0))]
    return pl.pallas_call(
        functools.partial(_out_proj_kernel, n_in=n_in),
        grid=(M // tm,),
        in_specs=in_specs,
        out_specs=pl.BlockSpec((tm, D), lambda i: (i, 0)),
        out_shape=jax.ShapeDtypeStruct((M, D), F32),
        compiler_params=_cparams(("parallel",)),
        name="out_proj",
    )(*acts, *ws, x, g)


def _ffn_kernel(x_ref, g1_ref, wu_ref, wd_ref, g2_ref, o_ref, xn_ref, acc_ref):
    f = pl.program_id(1)

    @pl.when(f == 0)
    def _():
        xn_ref[...] = _rms(x_ref[...], g1_ref[...]).astype(BF16)
        acc_ref[...] = jnp.zeros_like(acc_ref)

    hid = jnp.dot(xn_ref[...], wu_ref[...], preferred_element_type=F32)
    hid = jnp.square(jnp.maximum(hid, 0.0))
    acc_ref[...] += jnp.dot(hid.astype(BF16), wd_ref[...], preferred_element_type=F32)

    @pl.when(f == pl.num_programs(1) - 1)
    def _():
        o_ref[...] = x_ref[...] + _rms(acc_ref[...], g2_ref[...])


def _ffn(x, g1, wu, wd, g2, tm, tf):
    M, D = x.shape
    FF = wu.shape[1]
    return pl.pallas_call(
        _ffn_kernel,
        grid=(M // tm, FF // tf),
        in_specs=[
            pl.BlockSpec((tm, D), lambda i, f: (i, 0)),
            pl.BlockSpec((1, D), lambda i, f: (0, 0)),
            pl.BlockSpec((D, tf), lambda i, f: (0, f)),
            pl.BlockSpec((tf, D), lambda i, f: (f, 0)),
            pl.BlockSpec((1, D), lambda i, f: (0, 0)),
        ],
        out_specs=pl.BlockSpec((tm, D), lambda i, f: (i, 0)),
        out_shape=jax.ShapeDtypeStruct((M, D), F32),
        scratch_shapes=[pltpu.VMEM((tm, D), BF16), pltpu.VMEM((tm, D), F32)],
        compiler_params=_cparams(("parallel", "arbitrary")),
        name="ffn",
    )(x, g1, wu, wd, g2)


def _lru_gates(xc, wa_ref, ba, wx_ref, bx, lam):
    xcb = xc.astype(BF16)
    ng = wa_ref.shape[0]
    gw = wa_ref.shape[1]
    ra, ri = [], []
    for g in range(ng):
        xs = xcb[:, g * gw:(g + 1) * gw]
        ra.append(jnp.dot(xs, wa_ref[g], preferred_element_type=F32))
        ri.append(jnp.dot(xs, wx_ref[g], preferred_element_type=F32))
    r = jax.nn.sigmoid(jnp.concatenate(ra, axis=1) + ba)
    i = jax.nn.sigmoid(jnp.concatenate(ri, axis=1) + bx)
    log_a = -LRU_C * r * _softplus(-lam)
    a = jnp.exp(log_a)
    om = 1.0 - a * a
    b = jnp.where(om > 0.0, om * lax.rsqrt(om), 0.0) * (i * xc)
    return a, b


def _conv4(xe, off, T, w_ref, b):
    y = b
    for j in range(CONV_W):
        s = off - (CONV_W - 1) + j
        y = y + w_ref[j:j + 1, :] * xe[s:s + T]
    return y


def _rglru_prompt_kernel(xa_ref, ga_ref, cw_ref, cb_ref, wa_ref, ba_ref, wx_ref, bx_ref, lam_ref,
                         ya_ref, hl_ref, tail_ref, hc_ref, a_s, b_s, h_s):
    t = pl.program_id(1)
    T = xa_ref.shape[0]

    @pl.when(t == 0)
    def _():
        tail_ref[...] = jnp.zeros_like(tail_ref)
        hc_ref[...] = jnp.zeros_like(hc_ref)

    x = xa_ref[...]
    xe = jnp.concatenate([tail_ref[...], x], axis=0)
    xc = _conv4(xe, SUBLANES, T, cw_ref, cb_ref[...])
    tail_ref[...] = x[T - SUBLANES:, :]
    a, b = _lru_gates(xc, wa_ref, ba_ref[...], wx_ref, bx_ref[...], lam_ref[...])
    a_s[...] = a
    b_s[...] = b

    def body(s, h):
        h = a_s[pl.ds(s, 1), :] * h + b_s[pl.ds(s, 1), :]
        h_s[pl.ds(s, 1), :] = h
        return h

    h = lax.fori_loop(0, T, body, hc_ref[...], unroll=8)
    hc_ref[...] = h
    ya_ref[...] = (h_s[...] * _gelu_tanh(ga_ref[...])).astype(ya_ref.dtype)

    @pl.when(t == pl.num_programs(1) - 1)
    def _():
        hl_ref[...] = h


def _rglru_prompt(xaga, cw, cb, wa_bd, ba, wx_bd, bx, lam, ts):
    B, S, _ = xaga.shape
    C = cw.shape[-1]
    wspec = lambda a: pl.BlockSpec(a.shape, lambda b, t: (0,) * a.ndim)
    return pl.pallas_call(
        _rglru_prompt_kernel,
        grid=(B, S // ts),
        in_specs=[
            pl.BlockSpec((None, ts, C), lambda b, t: (b, t, 0)),
            pl.BlockSpec((None, ts, C), lambda b, t: (b, t, 1)),
            wspec(cw), wspec(cb), wspec(wa_bd), wspec(ba), wspec(wx_bd), wspec(bx), wspec(lam),
        ],
        out_specs=[
            pl.BlockSpec((None, ts, C), lambda b, t: (b, t, 0)),
            pl.BlockSpec((None, 1, C), lambda b, t: (b, 0, 0)),
        ],
        out_shape=[jax.ShapeDtypeStruct((B, S, C), BF16), jax.ShapeDtypeStruct((B, 1, C), F32)],
        scratch_shapes=[pltpu.VMEM((SUBLANES, C), F32), pltpu.VMEM((1, C), F32),
                        pltpu.VMEM((ts, C), F32), pltpu.VMEM((ts, C), F32), pltpu.VMEM((ts, C), F32)],
        compiler_params=_cparams(("parallel", "arbitrary")),
        name="rglru_prompt",
    )(xaga, xaga, cw, cb, wa_bd, ba, wx_bd, bx, lam)


def _rglru_sample_kernel(xa_ref, ga_ref, st_ref, h0_ref, cw_ref, cb_ref, wa_ref, ba_ref, wx_ref, bx_ref,
                         lam_ref, ya_ref, hl_ref):
    T = xa_ref.shape[0]
    rows = [st_ref[j] for j in range(CONV_W - 1)] + [xa_ref[s] for s in range(T)]
    h = h0_ref[...]
    for s in range(T):
        y = cb_ref[...]
        for j in range(CONV_W):
            y = y + cw_ref[j:j + 1, :] * rows[s + j]
        a, b = _lru_gates(y, wa_ref, ba_ref[...], wx_ref, bx_ref[...], lam_ref[...])
        h = a * h + b
        ya_ref[s] = (h * _gelu_tanh(ga_ref[s])).astype(ya_ref.dtype)
    hl_ref[...] = h


def _rglru_sample(xa_t, ga_t, st_t, h0, cw, cb, wa_bd, ba, wx_bd, bx, lam):
    T, B, C = xa_t.shape
    args = (xa_t, ga_t, st_t, h0, cw, cb, wa_bd, ba, wx_bd, bx, lam)
    return pl.pallas_call(
        _rglru_sample_kernel,
        grid=(1,),
        in_specs=[pl.BlockSpec(a.shape, lambda i, n=a.ndim: (0,) * n) for a in args],
        out_specs=[pl.BlockSpec((T, B, C), lambda i: (0, 0, 0)), pl.BlockSpec((B, C), lambda i: (0, 0))],
        out_shape=[jax.ShapeDtypeStruct((T, B, C), F32), jax.ShapeDtypeStruct((B, C), F32)],
        compiler_params=_cparams(("arbitrary",)),
        name="rglru_sample",
    )(*args)


def _attn_prompt_kernel(q_ref, k_ref, v_ref, o_ref, *scratch, groups, qb, unroll):
    S = q_ref.shape[0]
    E = q_ref.shape[1]
    dist2 = (lax.broadcasted_iota(jnp.int32, (qb, 2 * qb), 0) + qb
             - lax.broadcasted_iota(jnp.int32, (qb, 2 * qb), 1))
    dist1 = lax.broadcasted_iota(jnp.int32, (qb, qb), 0) - lax.broadcasted_iota(jnp.int32, (qb, qb), 1)

    def rows(start, d):
        return pl.ds(start, qb) if d == 1 else pl.ds(start, qb, stride=d)

    ng = len(groups)
    s_scr, p_scr = scratch[3 * ng], scratch[3 * ng + 1]
    dn_t = (((1,), (1,)), ((), ()))

    for gi, (w, d) in enumerate(groups):
        back = w // d
        nb = (S // d) // qb
        nblk = nb * d
        two = nb > 1
        kw = 2 * qb if two else qb
        dist = dist2 if two else dist1
        lim_all = jnp.full((qb, kw), back, jnp.int32)
        lim_first = jnp.minimum(lax.broadcasted_iota(jnp.int32, (qb, kw), 0), back)
        m_s, l_s, acc_s = scratch[3 * gi:3 * gi + 3]
        un = math.gcd(nblk, unroll)

        def locate(idx, d=d):
            if d == 1:
                n = idx
                start = pl.multiple_of(idx * qb, qb)
                prev = pl.multiple_of(jnp.maximum(idx - 1, 0) * qb, qb)
            else:
                n = idx // d
                start = n * (qb * d) + idx % d
                prev = jnp.where(n > 0, start - qb * d, start)
            return start, prev, n

        def gather(ref, start, prev, d=d, two=two):
            if two:
                return jnp.concatenate([ref[rows(prev, d), :], ref[rows(start, d), :]], axis=0).astype(BF16)
            return ref[rows(start, d), :].astype(BF16)

        def scores(idx, c, kw=kw, locate=locate, gather=gather, d=d):
            start, prev, _ = locate(idx)
            q = q_ref[rows(start, d), :].astype(BF16)
            s_scr[pl.ds(pl.multiple_of(idx * qb, qb), qb), 0:kw] = lax.dot_general(
                q, gather(k_ref, start, prev), dn_t, preferred_element_type=F32)
            return c

        def softmax(idx, c, kw=kw, locate=locate, d=d, two=two, dist=dist, lim_all=lim_all,
                    lim_first=lim_first, m_s=m_s, l_s=l_s):
            start, _, n = locate(idx)
            blk = pl.ds(pl.multiple_of(idx * qb, qb), qb)
            lim = jnp.where(n > 0, lim_all, lim_first) if two else lim_all
            valid = (dist >= 0) & (dist <= lim)
            s = jnp.where(valid, s_scr[blk, 0:kw], NEG_INF)
            m_b = jnp.max(s, axis=-1, keepdims=True)
            p_scr[blk, 0:kw] = jnp.exp(s - m_b).astype(BF16)
            m_s[rows(start, d), :] = jnp.broadcast_to(m_b, (qb, E))
            return c

        def values(idx, c, kw=kw, locate=locate, gather=gather, d=d, acc_s=acc_s, l_s=l_s):
            start, prev, _ = locate(idx)
            p = p_scr[pl.ds(pl.multiple_of(idx * qb, qb), qb), 0:kw]
            v1 = jnp.concatenate([gather(v_ref, start, prev), jnp.ones((kw, E), BF16)], axis=1)
            r = jnp.dot(p, v1, preferred_element_type=F32)
            acc_s[rows(start, d), :] = r[:, :E]
            l_s[rows(start, d), :] = r[:, E:]
            return c

        lax.fori_loop(0, nblk, scores, 0, unroll=un)
        lax.fori_loop(0, nblk, softmax, 0, unroll=un)
        lax.fori_loop(0, nblk, values, 0, unroll=un)

    ng = len(groups)

    def merge(i, c):
        rs = pl.ds(pl.multiple_of(i * qb, qb), qb)
        ms = [scratch[3 * g][rs, :] for g in range(ng)]
        m = functools.reduce(jnp.maximum, ms)
        num = den = None
        for g in range(ng):
            wg = jnp.exp(ms[g] - m)
            n_g = wg * scratch[3 * g + 2][rs, :]
            d_g = wg * scratch[3 * g + 1][rs, :]
            num = n_g if num is None else num + n_g
            den = d_g if den is None else den + d_g
        o_ref[rs, :] = (num / den).astype(o_ref.dtype)
        return c

    lax.fori_loop(0, S // qb, merge, 0, unroll=2)


def _attn_prompt(q, q_off, k, v, groups, qb):
    B, S, D = k.shape
    nh = D // LANES
    for (w, d) in groups:
        assert S % (d * qb) == 0 and w // d <= qb
    spec = pl.BlockSpec((None, S, LANES), lambda b, h: (b, 0, h))
    q_spec = pl.BlockSpec((None, S, LANES), lambda b, h: (b, 0, q_off + h))
    return pl.pallas_call(
        functools.partial(_attn_prompt_kernel, groups=tuple(groups), qb=qb, unroll=16),
        grid=(B, nh),
        in_specs=[q_spec, spec, spec],
        out_specs=spec,
        out_shape=jax.ShapeDtypeStruct((B, S, D), BF16),
        scratch_shapes=[pltpu.VMEM((S, LANES), F32)] * (3 * len(groups))
        + [pltpu.VMEM((S, 2 * qb), F32), pltpu.VMEM((S, 2 * qb), BF16)],
        compiler_params=_cparams(("parallel", "parallel")),
        name="attn_prompt",
    )(q, k, v)


def _attn_sample_kernel(q_ref, kn_ref, vn_ref, kc_ref, vc_ref, mc_ref, mn_ref, o_ref, kc_s, vc_s, *, T, nh):
    D = q_ref.shape[1]
    hd = D // nh
    wb = kc_s.shape[0]
    PADR = LANES
    for h in range(nh):
        kc_s[:, h * hd:(h + 1) * hd] = kc_ref[pl.ds(h, wb, stride=nh), :].astype(BF16)
        vc_s[:, h * hd:(h + 1) * hd] = vc_ref[pl.ds(h, wb, stride=nh), :].astype(BF16)
    lane_head = lax.broadcasted_iota(jnp.int32, (SUBLANES, D), 1) // (D // nh)
    sub = lax.broadcasted_iota(jnp.int32, (SUBLANES, D), 0)
    head_mask = lane_head == sub
    qrows = []
    for t in range(T):
        qt = jnp.broadcast_to(q_ref[t:t + 1, :], (SUBLANES, D))
        qrows.append(jnp.where(head_mask, qt, 0.0))
    qrows.append(jnp.zeros((PADR - T * SUBLANES, D), F32))
    qbd = jnp.concatenate(qrows, axis=0).astype(BF16)
    npad = kn_ref.shape[0]
    zpad = jnp.zeros((PADR - npad, D), BF16)
    kn = jnp.concatenate([kn_ref[...].astype(BF16), zpad], axis=0)
    vn = jnp.concatenate([vn_ref[...].astype(BF16), zpad], axis=0)
    dn = (((1,), (1,)), ((), ()))
    s_c = lax.dot_general(kc_s[...], qbd, dn, preferred_element_type=F32)
    s_n = lax.dot_general(kn, qbd, dn, preferred_element_type=F32)
    mult_c = mc_ref[...]
    mult_n = mn_ref[...]
    s_c = jnp.where(mult_c > 0.0, s_c, NEG_INF)
    s_n = jnp.where(mult_n > 0.0, s_n, NEG_INF)
    m = jnp.maximum(jnp.max(s_c, axis=0, keepdims=True), jnp.max(s_n, axis=0, keepdims=True))
    m = jnp.where(m == NEG_INF, 0.0, m)
    p_c = (mult_c * jnp.exp(s_c - m)).T
    p_n = (mult_n * jnp.exp(s_n - m)).T
    den = jnp.sum(p_c, axis=1, keepdims=True) + jnp.sum(p_n, axis=1, keepdims=True)
    acc = jnp.dot(p_c.astype(BF16), vc_s[...], preferred_element_type=F32)
    acc = acc + jnp.dot(p_n.astype(BF16), vn, preferred_element_type=F32)
    out = acc / jnp.where(den > 0.0, den, 1.0)
    o_ref[...] = jnp.zeros_like(o_ref)
    for t in range(T):
        blk = out[t * SUBLANES:(t + 1) * SUBLANES, :]
        o_ref[t:t + 1, :] = jnp.sum(jnp.where(head_mask, blk, 0.0), axis=0, keepdims=True)


def _attn_sample(q, kn, vn, kc, vc, mult_c, mult_n, T, nh):
    B, Tp, D = q.shape
    wb = kc.shape[1] // nh
    assert nh == SUBLANES and T * SUBLANES <= LANES
    small = pl.BlockSpec((None, Tp, D), lambda b: (b, 0, 0))
    cache = pl.BlockSpec((None, wb * nh, D // nh), lambda b: (b, 0, 0))
    return pl.pallas_call(
        functools.partial(_attn_sample_kernel, T=T, nh=nh),
        grid=(B,),
        in_specs=[small, small, small, cache, cache,
                  pl.BlockSpec(mult_c.shape, lambda b: (0, 0)), pl.BlockSpec(mult_n.shape, lambda b: (0, 0))],
        out_specs=small,
        out_shape=jax.ShapeDtypeStruct((B, Tp, D), F32),
        scratch_shapes=[pltpu.VMEM((wb, D), BF16), pltpu.VMEM((wb, D), BF16)],
        compiler_params=_cparams(("parallel",)),
        name="attn_sample",
    )(q, kn, vn, kc, vc, mult_c, mult_n)


def _mlstm_pre_kernel(xm_ref, tail0_ref, cw_ref, cb_ref, wq_ref, wk_ref, wv_ref, wg_ref, bg_ref,
                      q_ref, k_ref, v_ref, xc_ref, g_ref, tail_ref, *, k_scale):
    t = pl.program_id(1)
    T = xm_ref.shape[0]

    @pl.when(t == 0)
    def _():
        tail_ref[...] = tail0_ref[...]

    xm = xm_ref[...]
    xe = jnp.concatenate([tail_ref[...], xm], axis=0)
    xc = _silu(_conv4(xe, SUBLANES, T, cw_ref, cb_ref[...]))
    if T >= SUBLANES:
        tail_ref[...] = xm[T - SUBLANES:, :]
    xcb = xc.astype(BF16)
    xmb = xm.astype(BF16)
    ng, gw = wq_ref.shape[0], wq_ref.shape[1]
    qs, ks, vs = [], [], []
    for g in range(ng):
        sl = slice(g * gw, (g + 1) * gw)
        qs.append(jnp.dot(xcb[:, sl], wq_ref[g], preferred_element_type=F32))
        ks.append(jnp.dot(xcb[:, sl], wk_ref[g], preferred_element_type=F32))
        vs.append(jnp.dot(xmb[:, sl], wv_ref[g], preferred_element_type=F32))
    q = jnp.concatenate(qs, axis=1)
    k = jnp.concatenate(ks, axis=1)
    v = jnp.concatenate(vs, axis=1)
    qkv = jnp.concatenate([q, k, v], axis=1).astype(BF16)
    g_ref[...] = jnp.dot(qkv, wg_ref[...], preferred_element_type=F32) + bg_ref[...]
    q_ref[...] = q.astype(q_ref.dtype)
    k_ref[...] = (k * k_scale).astype(k_ref.dtype)
    v_ref[...] = v.astype(v_ref.dtype)
    xc_ref[...] = xc.astype(xc_ref.dtype)


def _mlstm_pre(xmz, tail0, cw, cb, wq_bd, wk_bd, wv_bd, wg, bg, ts, k_scale):
    B, S, C2 = xmz.shape
    C = C2 // 2
    wspec = lambda a: pl.BlockSpec(a.shape, lambda b, t: (0,) * a.ndim)
    act = pl.BlockSpec((None, ts, C), lambda b, t: (b, t, 0))
    return pl.pallas_call(
        functools.partial(_mlstm_pre_kernel, k_scale=k_scale),
        grid=(B, S // ts),
        in_specs=[act, pl.BlockSpec((None, SUBLANES, C), lambda b, t: (b, 0, 0)),
                  wspec(cw), wspec(cb), wspec(wq_bd), wspec(wk_bd), wspec(wv_bd), wspec(wg), wspec(bg)],
        out_specs=[act, act, act, act, pl.BlockSpec((None, ts, 2 * LANES), lambda b, t: (b, t, 0))],
        out_shape=[jax.ShapeDtypeStruct((B, S, C), BF16)] * 4 + [jax.ShapeDtypeStruct((B, S, 2 * LANES), F32)],
        scratch_shapes=[pltpu.VMEM((SUBLANES, C), F32)],
        compiler_params=_cparams(("parallel", "arbitrary")),
        name="mlstm_pre",
    )(xmz, tail0, cw, cb, wq_bd, wk_bd, wv_bd, wg, bg)


def _mlstm_core_kernel(*refs, c, n_valid, nh, zero_init):
    q_ref, k_ref, v_ref, xc_ref, z_ref, g_ref, nw_ref, sk_ref = refs[:8]
    pos = 8
    if not zero_init:
        c0_ref, n0_ref, m0_ref = refs[8:11]
        pos = 11
    y_ref, C_ref, n_ref, m_ref = refs[pos:pos + 4]
    ch = pl.program_id(1)
    c_in = q_ref.shape[0]
    DH = q_ref.shape[1] // nh

    @pl.when(ch == 0)
    def _():
        if zero_init:
            C_ref[...] = jnp.zeros_like(C_ref)
            n_ref[...] = jnp.zeros_like(n_ref)
            m_ref[...] = jnp.zeros_like(m_ref)
        else:
            C_ref[...] = c0_ref[...]
            n_ref[...] = n0_ref[...]
            m_ref[...] = m0_ref[...]

    def padded(x):
        if c_in == c:
            return x
        return jnp.concatenate([x, jnp.zeros((c - c_in, x.shape[1]), x.dtype)], axis=0)

    g = padded(g_ref[...])
    row = lax.broadcasted_iota(jnp.int32, (c, LANES), 0)
    valid = row < n_valid
    gi = jnp.where(valid, g[:, :LANES], NEG_INF)
    lf = jnp.where(valid, -_softplus(-g[:, LANES:]), 0.0)
    tri_f = (lax.broadcasted_iota(jnp.int32, (c, c), 1) <= lax.broadcasted_iota(jnp.int32, (c, c), 0))
    bcum = jnp.dot(tri_f.astype(F32), lf, preferred_element_type=F32, precision=lax.Precision.HIGHEST)
    bcum_t = bcum.T
    gi_t = gi.T
    causal = tri_f

    for h in range(nh):
        sl = slice(h * DH, (h + 1) * DH)
        qh = padded(q_ref[:, sl])
        kh = padded(k_ref[:, sl])
        vh = padded(v_ref[:, sl])
        bc_col = bcum[:, h:h + 1]
        bc_row = bcum_t[h:h + 1, :]
        i_row = gi_t[h:h + 1, :]
        i_col = gi[:, h:h + 1]
        m_prev = m_ref[h:h + 1, 0:1]
        log_intra = jnp.where(causal, bc_col - bc_row + i_row, NEG_INF)
        log_inter = bc_col + m_prev
        m_t = jnp.maximum(log_inter, jnp.max(log_intra, axis=1, keepdims=True))
        qk = lax.dot_general(qh, kh, (((1,), (1,)), ((), ())), preferred_element_type=F32)
        w_intra = jnp.exp(log_intra - m_t) * qk
        w_inter = jnp.exp(log_inter - m_t)
        C_old = C_ref[h]
        n_old = n_ref[h:h + 1, :]
        num = jnp.dot(w_intra.astype(BF16), vh, preferred_element_type=F32)
        num = num + w_inter * jnp.dot(qh, C_old.astype(BF16), preferred_element_type=F32)
        n8 = jnp.broadcast_to(n_old, (SUBLANES, DH)).astype(BF16)
        qn = lax.dot_general(qh, n8, (((1,), (1,)), ((), ())), preferred_element_type=F32)[:, 0:1]
        den = jnp.sum(w_intra, axis=1, keepdims=True) + w_inter * qn
        hh = num / jnp.maximum(jnp.abs(den), jnp.exp(-m_t))
        m_new = m_t[c - 1:c, :]
        bc_last = bc_col[c - 1:c, :]
        w_state = jnp.exp(bc_last - bc_col + i_col - m_new)
        decay = jnp.exp(bc_last + m_prev - m_new)
        kw = kh.astype(F32) * w_state
        C_ref[h] = decay * C_old + lax.dot_general(
            kw.astype(BF16), vh, (((0,), (0,)), ((), ())), preferred_element_type=F32)
        n_ref[h:h + 1, :] = decay * n_old + jnp.sum(kw, axis=0, keepdims=True)
        m_ref[h:h + 1, :] = jnp.broadcast_to(m_new, (1, LANES))
        hn = hh * lax.rsqrt(jnp.mean(hh * hh, axis=-1, keepdims=True) + EPS)
        hn = hn[:c_in] * nw_ref[:, sl]
        y = (hn + sk_ref[:, sl] * xc_ref[:, sl].astype(F32)) * _silu(z_ref[:, sl])
        y_ref[:, sl] = y.astype(y_ref.dtype)


def _mlstm_core(q, k, v, xc, xmz, gates, nw, sk, state, c, n_valid, nh):
    B, S, C = q.shape
    c_in = min(c, S)
    DH = C // nh
    zero_init = state is None
    act = pl.BlockSpec((None, c_in, C), lambda b, t: (b, t, 0))
    in_specs = [act, act, act, act,
                pl.BlockSpec((None, c_in, C), lambda b, t: (b, t, 1)),
                pl.BlockSpec((None, c_in, 2 * LANES), lambda b, t: (b, t, 0)),
                pl.BlockSpec((1, C), lambda b, t: (0, 0)), pl.BlockSpec((1, C), lambda b, t: (0, 0))]
    args = [q, k, v, xc, xmz, gates, nw, sk]
    C_spec = pl.BlockSpec((None, nh, DH, DH), lambda b, t: (b, 0, 0, 0))
    n_spec = pl.BlockSpec((None, nh, DH), lambda b, t: (b, 0, 0))
    m_spec = pl.BlockSpec((None, nh, LANES), lambda b, t: (b, 0, 0))
    if not zero_init:
        in_specs += [C_spec, n_spec, m_spec]
        args += list(state)
    return pl.pallas_call(
        functools.partial(_mlstm_core_kernel, c=c, n_valid=n_valid, nh=nh, zero_init=zero_init),
        grid=(B, S // c_in),
        in_specs=in_specs,
        out_specs=[act, C_spec, n_spec, m_spec],
        out_shape=[jax.ShapeDtypeStruct((B, S, C), BF16),
                   jax.ShapeDtypeStruct((B, nh, DH, DH), F32),
                   jax.ShapeDtypeStruct((B, nh, DH), F32),
                   jax.ShapeDtypeStruct((B, nh, LANES), F32)],
        compiler_params=_cparams(("parallel", "arbitrary")),
        name="mlstm_core",
    )(*args)


def _rope_tables(pos, hd):
    rot = hd // 4
    half = rot // 2
    inv = ROPE_THETA ** (-jnp.arange(0, rot, 2, dtype=F32) / rot)
    ang = pos.astype(F32)[:, None] * inv[None, :]
    cos, sin = jnp.cos(ang), jnp.sin(ang)
    n = pos.shape[0]
    c = jnp.concatenate([cos, cos, jnp.ones((n, hd - rot), F32)], axis=1)
    sa = jnp.concatenate([-sin, jnp.zeros((n, hd - half), F32)], axis=1)
    sb = jnp.concatenate([jnp.zeros((n, half), F32), sin, jnp.zeros((n, hd - rot), F32)], axis=1)
    return c, sa, sb


def _regroup_block_diag(w, group):
    nb, bi, bo = w.shape
    per = group // bi
    wg = w.reshape(nb // per, per, bi, bo)
    eye = jnp.eye(per, dtype=w.dtype)
    dense = jnp.einsum("gpio,pq->gpiqo", wg, eye)
    return dense.reshape(nb // per, per * bi, per * bo)


def _sample_multiplicity(wb, T, nh):
    col_t = jnp.arange(LANES) // nh
    col_ok = col_t < T

    def mult(delta):
        m = jnp.zeros(delta.shape, F32)
        for (w, d) in DILATIONS:
            m = m + ((delta >= 0) & (delta % d == 0) & (delta <= (w // d) * d)).astype(F32)
        return m

    idx = jnp.arange(wb)[:, None]
    mc = jnp.where(col_ok[None, :], mult(wb + col_t[None, :] - idx), 0.0)
    tn = jnp.arange(LANES)[:, None]
    mn = jnp.where(col_ok[None, :] & (tn < T), mult(col_t[None, :] - tn), 0.0)
    return mc, mn


def _layer0(x, rope, p, prompt, state):
    Bg, L, D = x.shape
    M = Bg * L
    hd = D // H_B
    tm = _tile(rope[0].shape[0], 1024)
    x2 = x.reshape(M, D)
    xagq, k, v = _norm_proj(x2, p["g_mix_pre0"], p["w_in_ab"], ((0, 3), (3, 1), (4, 1)),
                            tm, D, rope=rope, rope_q=2, rope_k=3, q_scale=hd ** -0.5)
    xagq3 = xagq.reshape(Bg, L, 3 * D)
    lru_w = (p["conv_a_w"], p["conv_a_b"], p["wa_bd"], p["lru_ba"], p["wx_bd"], p["lru_bx"], p["lam"])
    if prompt:
        ya, h_last = _rglru_prompt(xagq3, *lru_w, ts=_tile(L, 256))
        ya = ya.reshape(M, D)
        h_last = h_last.reshape(Bg, D)
        new_conv = xagq3[:, L - (CONV_W - 1):, :D]
        yb = _attn_prompt(xagq3, 2 * D // LANES, k.reshape(Bg, L, D), v.reshape(Bg, L, D),
                          sorted(DILATIONS, key=lambda wd: -wd[1]), QBLOCK).reshape(M, D)
        wb = min(MAX_WINDOW, L)
        new_k = k.reshape(Bg, L, H_B, hd)[:, L - wb:]
        new_v = v.reshape(Bg, L, H_B, hd)[:, L - wb:]
    else:
        conv0, h0, k_buf, v_buf = state
        xagq_t = xagq3.transpose(1, 0, 2)
        xa_t, ga_t = xagq_t[:, :, :D], xagq_t[:, :, D:2 * D]
        st_t = conv0.transpose(1, 0, 2)
        ya_t, h_last = _rglru_sample(xa_t, ga_t, st_t, h0, *lru_w)
        ya = ya_t.transpose(1, 0, 2).reshape(M, D)
        new_conv = jnp.concatenate([conv0, xagq3[:, :, :D]], axis=1)[:, -(CONV_W - 1):]
        wb = k_buf.shape[1]
        tp = 16
        pad = lambda a: jnp.pad(a.reshape(Bg, L, D), ((0, 0), (0, tp - L), (0, 0)))
        mc, mn = _sample_multiplicity(wb, L, H_B)
        q = xagq3[:, :, 2 * D:]
        yb = _attn_sample(pad(q), pad(k), pad(v), k_buf.reshape(Bg, wb * H_B, hd), v_buf.reshape(Bg, wb * H_B, hd),
                          mc, mn, L, H_B)[:, :L].reshape(M, D)
        new_k = k.reshape(Bg, L, H_B, hd)
        new_v = v.reshape(Bg, L, H_B, hd)
    x1 = _out_proj([ya, yb], [p["w_out_a"], p["w_out_b"]], x2, p["g_mix_post0"], tm)
    x2o = _ffn(x1, p["g_ffn_pre0"], p["w_up0"], p["w_dn0"], p["g_ffn_post0"], _tile(M, 1024), 1024)
    return x2o.reshape(Bg, L, D), (new_conv, h_last, new_k, new_v)


def _layer1(x, p, prompt, state):
    Bg, L, D = x.shape
    M = Bg * L
    x2 = x.reshape(M, D)
    tm = _tile(M, 1024)
    DC = p["w_in_c"].shape[1] // 2
    DH = DC // NH_C
    (xmz,) = _norm_proj(x2, p["g_mix_pre1"], p["w_in_c"], ((0, 2 * DC // 1024),), tm, 1024)
    xmz3 = xmz.reshape(Bg, L, 2 * DC)
    pre_w = (p["conv_c_w"], p["conv_c_b"], p["wq_bd"], p["wk_bd"], p["wv_bd"], p["wg"], p["bg"])
    if prompt:
        tail0 = jnp.zeros((Bg, SUBLANES, DC), F32)
        c = _tile(L, 256)
        q, k, v, xc, gates = _mlstm_pre(xmz3, tail0, *pre_w, ts=c, k_scale=DH ** -0.5)
        y, C, n, m = _mlstm_core(q, k, v, xc, xmz3, gates, p["mlstm_norm"], p["mlstm_skip"], None, c, c, NH_C)
        new_conv = xmz3[:, L - (CONV_W - 1):, :DC]
        y = y.reshape(M, DC)
    else:
        conv0, C0, n0, m0 = state
        lp = 16
        xmz_p = jnp.pad(xmz3, ((0, 0), (0, lp - L), (0, 0)))
        tail0 = jnp.pad(conv0, ((0, 0), (SUBLANES - (CONV_W - 1), 0), (0, 0)))
        q, k, v, xc, gates = _mlstm_pre(xmz_p, tail0, *pre_w, ts=lp, k_scale=DH ** -0.5)
        m0b = jnp.broadcast_to(m0[:, :, None], m0.shape + (LANES,))
        y, C, n, m = _mlstm_core(q, k, v, xc, xmz_p, gates, p["mlstm_norm"], p["mlstm_skip"],
                                 (C0, n0, m0b), 128, L, NH_C)
        new_conv = jnp.concatenate([conv0, xmz3[:, :, :DC]], axis=1)[:, -(CONV_W - 1):]
        y = y[:, :L].reshape(M, DC)
    x1 = _out_proj([y], [p["w_out_c"]], x2, p["g_mix_post1"], tm)
    x2o = _ffn(x1, p["g_ffn_pre1"], p["w_up1"], p["w_dn1"], p["g_ffn_post1"], _tile(M, 1024), 1024)
    return x2o.reshape(Bg, L, D), (new_conv, C, n, m[:, :, 0])


def kernel(x_prompt, x_sample, state_rglru_conv, state_rglru_h, cache_swa_k, cache_swa_v, state_mlstm_conv, state_mlstm_C, state_mlstm_n, state_mlstm_m, norm_mix_pre, norm_mix_post, norm_ffn_pre, norm_ffn_post, w_ffn_up, w_ffn_down, w_in_ab, conv_a_w, conv_a_b, lru_wa, lru_ba, lru_wx, lru_bx, lru_lambda, w_out_ab, w_in_c, conv_c_w, conv_c_b, mlstm_wq, mlstm_wk, mlstm_wv, mlstm_w_gate, mlstm_b_gate, mlstm_norm, mlstm_skip, w_out_c):
    B, S, D = x_prompt.shape
    Bs, Ts, _ = x_sample.shape
    DA = conv_a_w.shape[-1]
    DC = conv_c_w.shape[-1]
    hd = D // H_B
    row = lambda a: a.reshape(1, -1)
    wg = jnp.zeros((3 * DC, 2 * LANES), F32)
    wg = wg.at[:, :NH_C].set(mlstm_w_gate[0][:, :NH_C]).at[:, LANES:LANES + NH_C].set(mlstm_w_gate[0][:, NH_C:])
    bg = jnp.zeros((1, 2 * LANES), F32)
    bg = bg.at[0, :NH_C].set(mlstm_b_gate[0][:NH_C]).at[0, LANES:LANES + NH_C].set(mlstm_b_gate[0][NH_C:])
    p = {
        "g_mix_pre0": row(norm_mix_pre[0]), "g_mix_post0": row(norm_mix_post[0]),
        "g_ffn_pre0": row(norm_ffn_pre[0]), "g_ffn_post0": row(norm_ffn_post[0]),
        "g_mix_pre1": row(norm_mix_pre[1]), "g_mix_post1": row(norm_mix_post[1]),
        "g_ffn_pre1": row(norm_ffn_pre[1]), "g_ffn_post1": row(norm_ffn_post[1]),
        "w_up0": w_ffn_up[0].astype(BF16), "w_dn0": w_ffn_down[0].astype(BF16),
        "w_up1": w_ffn_up[1].astype(BF16), "w_dn1": w_ffn_down[1].astype(BF16),
        "w_in_ab": w_in_ab[0].astype(BF16),
        "conv_a_w": conv_a_w[0], "conv_a_b": row(conv_a_b[0]),
        "wa_bd": _regroup_block_diag(lru_wa[0], MXU_DIM).astype(BF16), "lru_ba": row(lru_ba[0]),
        "wx_bd": _regroup_block_diag(lru_wx[0], MXU_DIM).astype(BF16), "lru_bx": row(lru_bx[0]),
        "lam": row(lru_lambda[0]),
        "w_out_a": w_out_ab[0][:DA].astype(BF16), "w_out_b": w_out_ab[0][DA:].astype(BF16),
        "w_in_c": w_in_c[0].astype(BF16),
        "conv_c_w": conv_c_w[0], "conv_c_b": row(conv_c_b[0]),
        "wq_bd": _regroup_block_diag(mlstm_wq[0], MXU_DIM).astype(BF16),
        "wk_bd": _regroup_block_diag(mlstm_wk[0], MXU_DIM).astype(BF16),
        "wv_bd": _regroup_block_diag(mlstm_wv[0], MXU_DIM).astype(BF16),
        "wg": wg.astype(BF16), "bg": bg,
        "mlstm_norm": row(mlstm_norm[0]), "mlstm_skip": row(mlstm_skip[0]),
        "w_out_c": w_out_c[0].astype(BF16),
    }
    dt = state_rglru_conv.dtype

    xp, st0 = _layer0(x_prompt, _rope_tables(jnp.arange(S), hd), p, True, None)
    xp, st1 = _layer1(xp, p, True, None)
    pos_s = PAST_LEN + (jnp.arange(Bs * Ts) % Ts)
    xs, ss0 = _layer0(x_sample, _rope_tables(pos_s, hd), p, False,
                      (state_rglru_conv[0], state_rglru_h[0], cache_swa_k[0], cache_swa_v[0]))
    xs, ss1 = _layer1(xs, p, False, (state_mlstm_conv[0], state_mlstm_C[0], state_mlstm_n[0], state_mlstm_m[0]))

    lead = lambda a: a[None].astype(dt)
    return (xp, xs,
            lead(st0[0]), lead(ss0[0]), lead(st0[1]), lead(ss0[1]),
            lead(st0[2]), lead(ss0[2]), lead(st0[3]), lead(ss0[3]),
            lead(st1[0]), lead(ss1[0]), lead(st1[1]), lead(ss1[1]),
            lead(st1[2]), lead(ss1[2]), lead(st1[3]), lead(ss1[3]))
```

```python
import functools
import math

import numpy as np
import jax
import jax.numpy as jnp
from jax import lax
from jax.experimental import pallas as pl
from jax.experimental.pallas import tpu as pltpu

F32 = jnp.float32
BF16 = jnp.bfloat16

NA_BLOCKS = 16
CONV_W = 4
LRU_C = 8.0
H_B = 8
ROPE_THETA = 500000.0
DILATIONS = ((128, 1), (512, 4), (2048, 16))
MAX_WINDOW = 2048
QBLOCK = 128
NH_C = 4
QKV_BLOCK = 4
PAST_LEN = 16384
EPS = 1e-6

LANES = 128
SUBLANES = 8
MXU_DIM = 256
VMEM_LIMIT = 56 * 1024 * 1024

NEG_INF = float("-inf")


def _cparams(sem):
    return pltpu.CompilerParams(dimension_semantics=sem, vmem_limit_bytes=VMEM_LIMIT)


def _tile(n, pref):
    t = min(n, pref)
    while n % t:
        t -= 1
    return t


def _rms(x, g):
    return x * lax.rsqrt(jnp.mean(x * x, axis=-1, keepdims=True) + EPS) * g


def _softplus(z):
    return jnp.maximum(z, 0.0) + jnp.log1p(jnp.exp(-jnp.abs(z)))


def _gelu_tanh(x):
    c = math.sqrt(2.0 / math.pi)
    return x * (0.5 * (1.0 + jnp.tanh(c * (x + 0.044715 * (x * x * x)))))


def _silu(x):
    return x * jax.nn.sigmoid(x)


def _rope_tile(y, c, sa, sb):
    outs = []
    for h in range(y.shape[1] // LANES):
        yh = y[:, h * LANES:(h + 1) * LANES]
        outs.append(yh * c + pltpu.roll(yh, LANES - 16, 1) * sa + pltpu.roll(yh, 16, 1) * sb)
    return jnp.concatenate(outs, axis=1)


def _norm_proj_kernel(*refs, groups, tn, rope_q, rope_k, q_scale):
    n_out = len(groups)
    x_ref, g_ref, w_ref = refs[:3]
    pos = 3
    if rope_q is not None:
        c_ref, sa_ref, sb_ref = refs[3:6]
        pos = 6
    out_refs = refs[pos:pos + n_out]
    xn = _rms(x_ref[...], g_ref[...]).astype(BF16)
    for (start, count), o_ref in zip(groups, out_refs):
        for jj in range(start, start + count):
            y = jnp.dot(xn, w_ref[:, jj * tn:(jj + 1) * tn], preferred_element_type=F32)
            if rope_q is not None and jj == rope_q:
                y = _rope_tile(y, c_ref[...], sa_ref[...], sb_ref[...]) * q_scale
            elif rope_q is not None and jj == rope_k:
                y = _rope_tile(y, c_ref[...], sa_ref[...], sb_ref[...])
            o_ref[:, (jj - start) * tn:(jj - start + 1) * tn] = y.astype(o_ref.dtype)


def _norm_proj(x, g, w, groups, tm, tn, rope=None, rope_q=None, rope_k=None, q_scale=1.0):
    M, K = x.shape
    in_specs = [
        pl.BlockSpec((tm, K), lambda i: (i, 0)),
        pl.BlockSpec((1, K), lambda i: (0, 0)),
        pl.BlockSpec(w.shape, lambda i: (0, 0)),
    ]
    args = [x, g, w]
    if rope is not None:
        assert rope[0].shape[0] % tm == 0 and M % tm == 0
        pos_blocks = rope[0].shape[0] // tm
        for t in rope:
            in_specs.append(pl.BlockSpec((tm, LANES), lambda i, pb=pos_blocks: (i % pb, 0)))
            args.append(t)
    out_specs, out_shapes = [], []
    for (start, count) in groups:
        out_specs.append(pl.BlockSpec((tm, count * tn), lambda i: (i, 0)))
        out_shapes.append(jax.ShapeDtypeStruct((M, count * tn), F32))
    kern = functools.partial(_norm_proj_kernel, groups=tuple(groups), tn=tn,
                             rope_q=rope_q if rope is not None else None, rope_k=rope_k, q_scale=q_scale)
    return pl.pallas_call(
        kern,
        grid=(M // tm,),
        in_specs=in_specs,
        out_specs=out_specs,
        out_shape=out_shapes,
        compiler_params=_cparams(("parallel",)),
        name="norm_proj",
    )(*args)


def _out_proj_kernel(*refs, n_in):
    a_refs = refs[:n_in]
    w_refs = refs[n_in:2 * n_in]
    x_ref, g_ref, o_ref = refs[2 * n_in:]
    y = None
    for a_ref, w_ref in zip(a_refs, w_refs):
        d = jnp.dot(a_ref[...].astype(BF16), w_ref[...], preferred_element_type=F32)
        y = d if y is None else y + d
    o_ref[...] = x_ref[...] + _rms(y, g_ref[...])


def _out_proj(acts, ws, x, g, tm):
    M, D = x.shape
    n_in = len(acts)
    in_specs = [pl.BlockSpec((tm, a.shape[1]), lambda i: (i, 0)) for a in acts]
    in_specs += [pl.BlockSpec(w.shape, lambda i: (0, 0)) for w in ws]
    in_specs += [pl.BlockSpec((tm, D), lambda i: (i, 0)), pl.BlockSpec((1, D), lambda i: (0, 0))]
    return pl.pallas_call(
        functools.partial(_out_proj_kernel, n_in=n_in),
        grid=(M // tm,),
        in_specs=in_specs,
        out_specs=pl.BlockSpec((tm, D), lambda i: (i, 0)),
        out_shape=jax.ShapeDtypeStruct((M, D), F32),
        compiler_params=_cparams(("parallel",)),
        name="out_proj",
    )(*acts, *ws, x, g)


def _ffn_kernel(x_ref, g1_ref, wu_ref, wd_ref, g2_ref, o_ref, xn_ref, acc_ref):
    f = pl.program_id(1)

    @pl.when(f == 0)
    def _():
        xn_ref[...] = _rms(x_ref[...], g1_ref[...]).astype(BF16)
        acc_ref[...] = jnp.zeros_like(acc_ref)

    hid = jnp.dot(xn_ref[...], wu_ref[...], preferred_element_type=F32)
    hid = jnp.square(jnp.maximum(hid, 0.0))
    acc_ref[...] += jnp.dot(hid.astype(BF16), wd_ref[...], preferred_element_type=F32)

    @pl.when(f == pl.num_programs(1) - 1)
    def _():
        o_ref[...] = x_ref[...] + _rms(acc_ref[...], g2_ref[...])


def _ffn(x, g1, wu, wd, g2, tm, tf):
    M, D = x.shape
    FF = wu.shape[1]
    return pl.pallas_call(
        _ffn_kernel,
        grid=(M // tm, FF // tf),
        in_specs=[
            pl.BlockSpec((tm, D), lambda i, f: (i, 0)),
            pl.BlockSpec((1, D), lambda i, f: (0, 0)),
            pl.BlockSpec((D, tf), lambda i, f: (0, f)),
            pl.BlockSpec((tf, D), lambda i, f: (f, 0)),
            pl.BlockSpec((1, D), lambda i, f: (0, 0)),
        ],
        out_specs=pl.BlockSpec((tm, D), lambda i, f: (i, 0)),
        out_shape=jax.ShapeDtypeStruct((M, D), F32),
        scratch_shapes=[pltpu.VMEM((tm, D), BF16), pltpu.VMEM((tm, D), F32)],
        compiler_params=_cparams(("parallel", "arbitrary")),
        name="ffn",
    )(x, g1, wu, wd, g2)


def _lru_gates(xc, wa_ref, ba, wx_ref, bx, lam):
    xcb = xc.astype(BF16)
    ng = wa_ref.shape[0]
    gw = wa_ref.shape[1]
    ra, ri = [], []
    for g in range(ng):
        xs = xcb[:, g * gw:(g + 1) * gw]
        ra.append(jnp.dot(xs, wa_ref[g], preferred_element_type=F32))
        ri.append(jnp.dot(xs, wx_ref[g], preferred_element_type=F32))
    r = jax.nn.sigmoid(jnp.concatenate(ra, axis=1) + ba)
    i = jax.nn.sigmoid(jnp.concatenate(ri, axis=1) + bx)
    log_a = -LRU_C * r * _softplus(-lam)
    a = jnp.exp(log_a)
    om = 1.0 - a * a
    b = jnp.where(om > 0.0, om * lax.rsqrt(om), 0.0) * (i * xc)
    return a, b


def _conv4(xe_ref, x, first, w_ref, b):
    T = x.shape[0]
    if first is not None:
        @pl.when(first)
        def _():
            xe_ref[0:SUBLANES, :] = jnp.zeros((SUBLANES, x.shape[1]), x.dtype)
    xe_ref[SUBLANES:, :] = x
    y = b
    for j in range(CONV_W):
        y = y + w_ref[j:j + 1, :] * xe_ref[pl.ds(SUBLANES - (CONV_W - 1) + j, T), :]
    if T >= SUBLANES:
        xe_ref[0:SUBLANES, :] = x[T - SUBLANES:, :]
    return y


def _rglru_prompt_kernel(xa_ref, ga_ref, cw_ref, cb_ref, wa_ref, ba_ref, wx_ref, bx_ref, lam_ref,
                         ya_ref, hl_ref, xe_ref, hc_ref, a_s, b_s, h_s):
    t = pl.program_id(1)
    T = xa_ref.shape[0]

    @pl.when(t == 0)
    def _():
        hc_ref[...] = jnp.zeros_like(hc_ref)

    xc = _conv4(xe_ref, xa_ref[...], t == 0, cw_ref, cb_ref[...])
    a, b = _lru_gates(xc, wa_ref, ba_ref[...], wx_ref, bx_ref[...], lam_ref[...])
    a_s[...] = a
    b_s[...] = b

    def body(s, h):
        h = a_s[pl.ds(s, 1), :] * h + b_s[pl.ds(s, 1), :]
        h_s[pl.ds(s, 1), :] = h
        return h

    h = lax.fori_loop(0, T, body, hc_ref[...], unroll=8)
    hc_ref[...] = h
    ya_ref[...] = (h_s[...] * _gelu_tanh(ga_ref[...])).astype(ya_ref.dtype)

    @pl.when(t == pl.num_programs(1) - 1)
    def _():
        hl_ref[...] = h


def _rglru_prompt(xaga, cw, cb, wa_bd, ba, wx_bd, bx, lam, ts):
    B, S, _ = xaga.shape
    C = cw.shape[-1]
    wspec = lambda a: pl.BlockSpec(a.shape, lambda b, t: (0,) * a.ndim)
    return pl.pallas_call(
        _rglru_prompt_kernel,
        grid=(B, S // ts),
        in_specs=[
            pl.BlockSpec((None, ts, C), lambda b, t: (b, t, 0)),
            pl.BlockSpec((None, ts, C), lambda b, t: (b, t, 1)),
            wspec(cw), wspec(cb), wspec(wa_bd), wspec(ba), wspec(wx_bd), wspec(bx), wspec(lam),
        ],
        out_specs=[
            pl.BlockSpec((None, ts, C), lambda b, t: (b, t, 0)),
            pl.BlockSpec((None, 1, C), lambda b, t: (b, 0, 0)),
        ],
        out_shape=[jax.ShapeDtypeStruct((B, S, C), BF16), jax.ShapeDtypeStruct((B, 1, C), F32)],
        scratch_shapes=[pltpu.VMEM((SUBLANES + ts, C), F32), pltpu.VMEM((1, C), F32),
                        pltpu.VMEM((ts, C), F32), pltpu.VMEM((ts, C), F32), pltpu.VMEM((ts, C), F32)],
        compiler_params=_cparams(("parallel", "arbitrary")),
        name="rglru_prompt",
    )(xaga, xaga, cw, cb, wa_bd, ba, wx_bd, bx, lam)


def _rglru_sample_kernel(xa_ref, ga_ref, st_ref, h0_ref, cw_ref, cb_ref, wa_ref, ba_ref, wx_ref, bx_ref,
                         lam_ref, ya_ref, hl_ref):
    T = xa_ref.shape[0]
    rows = [st_ref[j] for j in range(CONV_W - 1)] + [xa_ref[s] for s in range(T)]
    h = h0_ref[...]
    for s in range(T):
        y = cb_ref[...]
        for j in range(CONV_W):
            y = y + cw_ref[j:j + 1, :] * rows[s + j]
        a, b = _lru_gates(y, wa_ref, ba_ref[...], wx_ref, bx_ref[...], lam_ref[...])
        h = a * h + b
        ya_ref[s] = (h * _gelu_tanh(ga_ref[s])).astype(ya_ref.dtype)
    hl_ref[...] = h


def _rglru_sample(xa_t, ga_t, st_t, h0, cw, cb, wa_bd, ba, wx_bd, bx, lam):
    T, B, C = xa_t.shape
    args = (xa_t, ga_t, st_t, h0, cw, cb, wa_bd, ba, wx_bd, bx, lam)
    return pl.pallas_call(
        _rglru_sample_kernel,
        grid=(1,),
        in_specs=[pl.BlockSpec(a.shape, lambda i, n=a.ndim: (0,) * n) for a in args],
        out_specs=[pl.BlockSpec((T, B, C), lambda i: (0, 0, 0)), pl.BlockSpec((B, C), lambda i: (0, 0))],
        out_shape=[jax.ShapeDtypeStruct((T, B, C), F32), jax.ShapeDtypeStruct((B, C), F32)],
        compiler_params=_cparams(("arbitrary",)),
        name="rglru_sample",
    )(*args)


def _attn_prompt_kernel(q_ref, k_ref, v_ref, o_ref, *scratch, groups, qb, unroll):
    S = q_ref.shape[0]
    E = q_ref.shape[1]
    dist2 = (lax.broadcasted_iota(jnp.int32, (qb, 2 * qb), 0) + qb
             - lax.broadcasted_iota(jnp.int32, (qb, 2 * qb), 1))
    dist1 = lax.broadcasted_iota(jnp.int32, (qb, qb), 0) - lax.broadcasted_iota(jnp.int32, (qb, qb), 1)

    def rows(start, d):
        return pl.ds(start, qb) if d == 1 else pl.ds(start, qb, stride=d)

    ng = len(groups)
    s_scr, p_scr = scratch[3 * ng], scratch[3 * ng + 1]
    dn_t = (((1,), (1,)), ((), ()))

    for gi, (w, d) in enumerate(groups):
        back = w // d
        nb = (S // d) // qb
        nblk = nb * d
        two = nb > 1
        kw = 2 * qb if two else qb
        dist = dist2 if two else dist1
        lim_all = jnp.full((qb, kw), back, jnp.int32)
        lim_first = jnp.minimum(lax.broadcasted_iota(jnp.int32, (qb, kw), 0), back)
        m_s, l_s, acc_s = scratch[3 * gi:3 * gi + 3]
        un = math.gcd(nblk, unroll)

        def locate(idx, d=d):
            if d == 1:
                n = idx
                start = pl.multiple_of(idx * qb, qb)
                prev = pl.multiple_of(jnp.maximum(idx - 1, 0) * qb, qb)
            else:
                n = idx // d
                start = n * (qb * d) + idx % d
                prev = jnp.where(n > 0, start - qb * d, start)
            return start, prev, n

        def gather(ref, start, prev, d=d, two=two):
            if two:
                return jnp.concatenate([ref[rows(prev, d), :], ref[rows(start, d), :]], axis=0).astype(BF16)
            return ref[rows(start, d), :].astype(BF16)

        def scores(idx, c, kw=kw, locate=locate, gather=gather, d=d):
            start, prev, _ = locate(idx)
            q = q_ref[rows(start, d), :].astype(BF16)
            s_scr[pl.ds(pl.multiple_of(idx * qb, qb), qb), 0:kw] = lax.dot_general(
                q, gather(k_ref, start, prev), dn_t, preferred_element_type=F32)
            return c

        def softmax(idx, c, kw=kw, locate=locate, d=d, two=two, dist=dist, lim_all=lim_all,
                    lim_first=lim_first, m_s=m_s, l_s=l_s):
            start, _, n = locate(idx)
            blk = pl.ds(pl.multiple_of(idx * qb, qb), qb)
            lim = jnp.where(n > 0, lim_all, lim_first) if two else lim_all
            valid = (dist >= 0) & (dist <= lim)
            s = jnp.where(valid, s_scr[blk, 0:kw], NEG_INF)
            m_b = jnp.max(s, axis=-1, keepdims=True)
            p_scr[blk, 0:kw] = jnp.exp(s - m_b).astype(BF16)
            m_s[rows(start, d), :] = jnp.broadcast_to(m_b, (qb, E))
            return c

        def values(idx, c, kw=kw, locate=locate, gather=gather, d=d, acc_s=acc_s, l_s=l_s):
            start, prev, _ = locate(idx)
            p = p_scr[pl.ds(pl.multiple_of(idx * qb, qb), qb), 0:kw]
            v1 = jnp.concatenate([gather(v_ref, start, prev), jnp.ones((kw, E), BF16)], axis=1)
            r = jnp.dot(p, v1, preferred_element_type=F32)
            acc_s[rows(start, d), :] = r[:, :E]
            l_s[rows(start, d), :] = r[:, E:]
            return c

        lax.fori_loop(0, nblk, scores, 0, unroll=un)
        lax.fori_loop(0, nblk, softmax, 0, unroll=un)
        lax.fori_loop(0, nblk, values, 0, unroll=un)

    ng = len(groups)

    def merge(i, c):
        rs = pl.ds(pl.multiple_of(i * qb, qb), qb)
        ms = [scratch[3 * g][rs, :] for g in range(ng)]
        m = functools.reduce(jnp.maximum, ms)
        num = den = None
        for g in range(ng):
            wg = jnp.exp(ms[g] - m)
            n_g = wg * scratch[3 * g + 2][rs, :]
            d_g = wg * scratch[3 * g + 1][rs, :]
            num = n_g if num is None else num + n_g
            den = d_g if den is None else den + d_g
        o_ref[rs, :] = (num / den).astype(o_ref.dtype)
        return c

    lax.fori_loop(0, S // qb, merge, 0, unroll=2)


def _attn_prompt(q, q_off, k, v, groups, qb):
    B, S, D = k.shape
    nh = D // LANES
    for (w, d) in groups:
        assert S % (d * qb) == 0 and w // d <= qb
    spec = pl.BlockSpec((None, S, LANES), lambda b, h: (b, 0, h))
    q_spec = pl.BlockSpec((None, S, LANES), lambda b, h: (b, 0, q_off + h))
    return pl.pallas_call(
        functools.partial(_attn_prompt_kernel, groups=tuple(groups), qb=qb, unroll=16),
        grid=(B, nh),
        in_specs=[q_spec, spec, spec],
        out_specs=spec,
        out_shape=jax.ShapeDtypeStruct((B, S, D), BF16),
        scratch_shapes=[pltpu.VMEM((S, LANES), F32)] * (3 * len(groups))
        + [pltpu.VMEM((S, 2 * qb), F32), pltpu.VMEM((S, 2 * qb), BF16)],
        compiler_params=_cparams(("parallel", "parallel")),
        name="attn_prompt",
    )(q, k, v)


def _attn_sample_kernel(q_ref, kn_ref, vn_ref, kc_ref, vc_ref, mc_ref, mn_ref, o_ref, kc_s, vc_s, *, T, nh):
    D = q_ref.shape[1]
    hd = D // nh
    wb = kc_s.shape[0]
    PADR = LANES
    for h in range(nh):
        kc_s[:, h * hd:(h + 1) * hd] = kc_ref[pl.ds(h, wb, stride=nh), :].astype(BF16)
        vc_s[:, h * hd:(h + 1) * hd] = vc_ref[pl.ds(h, wb, stride=nh), :].astype(BF16)
    lane_head = lax.broadcasted_iota(jnp.int32, (SUBLANES, D), 1) // (D // nh)
    sub = lax.broadcasted_iota(jnp.int32, (SUBLANES, D), 0)
    head_mask = lane_head == sub
    qrows = []
    for t in range(T):
        qt = jnp.broadcast_to(q_ref[t:t + 1, :], (SUBLANES, D))
        qrows.append(jnp.where(head_mask, qt, 0.0))
    qrows.append(jnp.zeros((PADR - T * SUBLANES, D), F32))
    qbd = jnp.concatenate(qrows, axis=0).astype(BF16)
    npad = kn_ref.shape[0]
    zpad = jnp.zeros((PADR - npad, D), BF16)
    kn = jnp.concatenate([kn_ref[...].astype(BF16), zpad], axis=0)
    vn = jnp.concatenate([vn_ref[...].astype(BF16), zpad], axis=0)
    dn = (((1,), (1,)), ((), ()))
    s_c = lax.dot_general(kc_s[...], qbd, dn, preferred_element_type=F32)
    s_n = lax.dot_general(kn, qbd, dn, preferred_element_type=F32)
    mult_c = mc_ref[...]
    mult_n = mn_ref[...]
    s_c = jnp.where(mult_c > 0.0, s_c, NEG_INF)
    s_n = jnp.where(mult_n > 0.0, s_n, NEG_INF)
    m = jnp.maximum(jnp.max(s_c, axis=0, keepdims=True), jnp.max(s_n, axis=0, keepdims=True))
    m = jnp.where(m == NEG_INF, 0.0, m)
    p_c = (mult_c * jnp.exp(s_c - m)).T
    p_n = (mult_n * jnp.exp(s_n - m)).T
    den = jnp.sum(p_c, axis=1, keepdims=True) + jnp.sum(p_n, axis=1, keepdims=True)
    acc = jnp.dot(p_c.astype(BF16), vc_s[...], preferred_element_type=F32)
    acc = acc + jnp.dot(p_n.astype(BF16), vn, preferred_element_type=F32)
    out = acc / jnp.where(den > 0.0, den, 1.0)
    o_ref[...] = jnp.zeros_like(o_ref)
    for t in range(T):
        blk = out[t * SUBLANES:(t + 1) * SUBLANES, :]
        o_ref[t:t + 1, :] = jnp.sum(jnp.where(head_mask, blk, 0.0), axis=0, keepdims=True)


def _attn_sample(q, kn, vn, kc, vc, mult_c, mult_n, T, nh):
    B, Tp, D = q.shape
    wb = kc.shape[1] // nh
    assert nh == SUBLANES and T * SUBLANES <= LANES
    small = pl.BlockSpec((None, Tp, D), lambda b: (b, 0, 0))
    cache = pl.BlockSpec((None, wb * nh, D // nh), lambda b: (b, 0, 0))
    return pl.pallas_call(
        functools.partial(_attn_sample_kernel, T=T, nh=nh),
        grid=(B,),
        in_specs=[small, small, small, cache, cache,
                  pl.BlockSpec(mult_c.shape, lambda b: (0, 0)), pl.BlockSpec(mult_n.shape, lambda b: (0, 0))],
        out_specs=small,
        out_shape=jax.ShapeDtypeStruct((B, Tp, D), F32),
        scratch_shapes=[pltpu.VMEM((wb, D), BF16), pltpu.VMEM((wb, D), BF16)],
        compiler_params=_cparams(("parallel",)),
        name="attn_sample",
    )(q, kn, vn, kc, vc, mult_c, mult_n)


def _mlstm_pre_kernel(xm_ref, tail0_ref, cw_ref, cb_ref, wq_ref, wk_ref, wv_ref, wg_ref, bg_ref,
                      q_ref, k_ref, v_ref, xc_ref, g_ref, xe_ref, *, k_scale):
    t = pl.program_id(1)

    @pl.when(t == 0)
    def _():
        xe_ref[0:SUBLANES, :] = tail0_ref[...]

    xm = xm_ref[...]
    xc = _silu(_conv4(xe_ref, xm, None, cw_ref, cb_ref[...]))
    xcb = xc.astype(BF16)
    xmb = xm.astype(BF16)
    ng, gw = wq_ref.shape[0], wq_ref.shape[1]
    qs, ks, vs = [], [], []
    for g in range(ng):
        sl = slice(g * gw, (g + 1) * gw)
        qs.append(jnp.dot(xcb[:, sl], wq_ref[g], preferred_element_type=F32))
        ks.append(jnp.dot(xcb[:, sl], wk_ref[g], preferred_element_type=F32))
        vs.append(jnp.dot(xmb[:, sl], wv_ref[g], preferred_element_type=F32))
    q = jnp.concatenate(qs, axis=1)
    k = jnp.concatenate(ks, axis=1)
    v = jnp.concatenate(vs, axis=1)
    qkv = jnp.concatenate([q, k, v], axis=1).astype(BF16)
    g_ref[...] = jnp.dot(qkv, wg_ref[...], preferred_element_type=F32) + bg_ref[...]
    q_ref[...] = q.astype(q_ref.dtype)
    k_ref[...] = (k * k_scale).astype(k_ref.dtype)
    v_ref[...] = v.astype(v_ref.dtype)
    xc_ref[...] = xc.astype(xc_ref.dtype)


def _mlstm_pre(xmz, tail0, cw, cb, wq_bd, wk_bd, wv_bd, wg, bg, ts, k_scale):
    B, S, C2 = xmz.shape
    C = C2 // 2
    wspec = lambda a: pl.BlockSpec(a.shape, lambda b, t: (0,) * a.ndim)
    act = pl.BlockSpec((None, ts, C), lambda b, t: (b, t, 0))
    return pl.pallas_call(
        functools.partial(_mlstm_pre_kernel, k_scale=k_scale),
        grid=(B, S // ts),
        in_specs=[act, pl.BlockSpec((None, SUBLANES, C), lambda b, t: (b, 0, 0)),
                  wspec(cw), wspec(cb), wspec(wq_bd), wspec(wk_bd), wspec(wv_bd), wspec(wg), wspec(bg)],
        out_specs=[act, act, act, act, pl.BlockSpec((None, ts, 2 * LANES), lambda b, t: (b, t, 0))],
        out_shape=[jax.ShapeDtypeStruct((B, S, C), BF16)] * 4 + [jax.ShapeDtypeStruct((B, S, 2 * LANES), F32)],
        scratch_shapes=[pltpu.VMEM((SUBLANES + ts, C), F32)],
        compiler_params=_cparams(("parallel", "arbitrary")),
        name="mlstm_pre",
    )(xmz, tail0, cw, cb, wq_bd, wk_bd, wv_bd, wg, bg)


def _mlstm_core_kernel(*refs, c, n_valid, nh, zero_init):
    q_ref, k_ref, v_ref, xc_ref, z_ref, g_ref, nw_ref, sk_ref = refs[:8]
    pos = 8
    if not zero_init:
        c0_ref, n0_ref, m0_ref = refs[8:11]
        pos = 11
    y_ref, C_ref, n_ref, m_ref = refs[pos:pos + 4]
    ch = pl.program_id(1)
    c_in = q_ref.shape[0]
    DH = q_ref.shape[1] // nh

    @pl.when(ch == 0)
    def _():
        if zero_init:
            C_ref[...] = jnp.zeros_like(C_ref)
            n_ref[...] = jnp.zeros_like(n_ref)
            m_ref[...] = jnp.zeros_like(m_ref)
        else:
            C_ref[...] = c0_ref[...]
            n_ref[...] = n0_ref[...]
            m_ref[...] = m0_ref[...]

    def padded(x):
        if c_in == c:
            return x
        return jnp.concatenate([x, jnp.zeros((c - c_in, x.shape[1]), x.dtype)], axis=0)

    g = padded(g_ref[...])
    row = lax.broadcasted_iota(jnp.int32, (c, LANES), 0)
    valid = row < n_valid
    gi = jnp.where(valid, g[:, :LANES], NEG_INF)
    lf = jnp.where(valid, -_softplus(-g[:, LANES:]), 0.0)
    tri_f = (lax.broadcasted_iota(jnp.int32, (c, c), 1) <= lax.broadcasted_iota(jnp.int32, (c, c), 0))
    bcum = jnp.dot(tri_f.astype(F32), lf, preferred_element_type=F32, precision=lax.Precision.HIGHEST)
    bcum_t = bcum.T
    gi_t = gi.T
    causal = tri_f

    for h in range(nh):
        sl = slice(h * DH, (h + 1) * DH)
        qh = padded(q_ref[:, sl])
        kh = padded(k_ref[:, sl])
        vh = padded(v_ref[:, sl])
        bc_col = bcum[:, h:h + 1]
        bc_row = bcum_t[h:h + 1, :]
        i_row = gi_t[h:h + 1, :]
        i_col = gi[:, h:h + 1]
        m_prev = m_ref[h:h + 1, 0:1]
        log_intra = jnp.where(causal, bc_col - bc_row + i_row, NEG_INF)
        log_inter = bc_col + m_prev
        m_t = jnp.maximum(log_inter, jnp.max(log_intra, axis=1, keepdims=True))
        qk = lax.dot_general(qh, kh, (((1,), (1,)), ((), ())), preferred_element_type=F32)
        w_intra = jnp.exp(log_intra - m_t) * qk
        w_inter = jnp.exp(log_inter - m_t)
        C_old = C_ref[h]
        n_old = n_ref[h:h + 1, :]
        num = jnp.dot(w_intra.astype(BF16), vh, preferred_element_type=F32)
        num = num + w_inter * jnp.dot(qh, C_old.astype(BF16), preferred_element_type=F32)
        n8 = jnp.broadcast_to(n_old, (SUBLANES, DH)).astype(BF16)
        qn = lax.dot_general(qh, n8, (((1,), (1,)), ((), ())), preferred_element_type=F32)[:, 0:1]
        den = jnp.sum(w_intra, axis=1, keepdims=True) + w_inter * qn
        hh = num / jnp.maximum(jnp.abs(den), jnp.exp(-m_t))
        m_new = m_t[c - 1:c, :]
        bc_last = bc_col[c - 1:c, :]
        w_state = jnp.exp(bc_last - bc_col + i_col - m_new)
        decay = jnp.exp(bc_last + m_prev - m_new)
        kw = kh.astype(F32) * w_state
        C_ref[h] = decay * C_old + lax.dot_general(
            kw.astype(BF16), vh, (((0,), (0,)), ((), ())), preferred_element_type=F32)
        n_ref[h:h + 1, :] = decay * n_old + jnp.sum(kw, axis=0, keepdims=True)
        m_ref[h:h + 1, :] = jnp.broadcast_to(m_new, (1, LANES))
        hn = hh * lax.rsqrt(jnp.mean(hh * hh, axis=-1, keepdims=True) + EPS)
        hn = hn[:c_in] * nw_ref[:, sl]
        y = (hn + sk_ref[:, sl] * xc_ref[:, sl].astype(F32)) * _silu(z_ref[:, sl])
        y_ref[:, sl] = y.astype(y_ref.dtype)


def _mlstm_core(q, k, v, xc, xmz, gates, nw, sk, state, c, n_valid, nh):
    B, S, C = q.shape
    c_in = min(c, S)
    DH = C // nh
    zero_init = state is None
    act = pl.BlockSpec((None, c_in, C), lambda b, t: (b, t, 0))
    in_specs = [act, act, act, act,
                pl.BlockSpec((None, c_in, C), lambda b, t: (b, t, 1)),
                pl.BlockSpec((None, c_in, 2 * LANES), lambda b, t: (b, t, 0)),
                pl.BlockSpec((1, C), lambda b, t: (0, 0)), pl.BlockSpec((1, C), lambda b, t: (0, 0))]
    args = [q, k, v, xc, xmz, gates, nw, sk]
    C_spec = pl.BlockSpec((None, nh, DH, DH), lambda b, t: (b, 0, 0, 0))
    n_spec = pl.BlockSpec((None, nh, DH), lambda b, t: (b, 0, 0))
    m_spec = pl.BlockSpec((None, nh, LANES), lambda b, t: (b, 0, 0))
    if not zero_init:
        in_specs += [C_spec, n_spec, m_spec]
        args += list(state)
    return pl.pallas_call(
        functools.partial(_mlstm_core_kernel, c=c, n_valid=n_valid, nh=nh, zero_init=zero_init),
        grid=(B, S // c_in),
        in_specs=in_specs,
        out_specs=[act, C_spec, n_spec, m_spec],
        out_shape=[jax.ShapeDtypeStruct((B, S, C), BF16),
                   jax.ShapeDtypeStruct((B, nh, DH, DH), F32),
                   jax.ShapeDtypeStruct((B, nh, DH), F32),
                   jax.ShapeDtypeStruct((B, nh, LANES), F32)],
        compiler_params=_cparams(("parallel", "arbitrary")),
        name="mlstm_core",
    )(*args)


def _rope_tables(pos, hd):
    rot = hd // 4
    half = rot // 2
    inv = ROPE_THETA ** (-np.arange(0, rot, 2, dtype=np.float64) / rot)
    ang = pos.astype(np.float64)[:, None] * inv[None, :]
    cos, sin = np.cos(ang), np.sin(ang)
    n = pos.shape[0]
    c = np.concatenate([cos, cos, np.ones((n, hd - rot))], axis=1)
    sa = np.concatenate([-sin, np.zeros((n, hd - half))], axis=1)
    sb = np.concatenate([np.zeros((n, half)), sin, np.zeros((n, hd - rot))], axis=1)
    return tuple(jnp.asarray(t, F32) for t in (c, sa, sb))


def _regroup_block_diag(w, group):
    nb, bi, bo = w.shape
    per = group // bi
    a = w.reshape(nb // per, per * bi, bo)
    cols = np.arange(per * bo)
    tile_cols = jnp.asarray(cols[None, :] % bo == np.arange(bo)[:, None], BF16)
    same_block = jnp.asarray((np.arange(per * bi)[:, None] // bi) == (cols[None, :] // bo), BF16)
    tiled = jnp.einsum("gro,oc->grc", a.astype(BF16), tile_cols, preferred_element_type=F32)
    return tiled.astype(BF16) * same_block


def _sample_multiplicity(wb, T, nh):
    col_t = np.arange(LANES) // nh
    col_ok = col_t < T

    def mult(delta):
        m = np.zeros(delta.shape, np.float32)
        for (w, d) in DILATIONS:
            m = m + ((delta >= 0) & (delta % d == 0) & (delta <= (w // d) * d))
        return m

    idx = np.arange(wb)[:, None]
    mc = np.where(col_ok[None, :], mult(wb + col_t[None, :] - idx), 0.0)
    tn = np.arange(LANES)[:, None]
    mn = np.where(col_ok[None, :] & (tn < T), mult(col_t[None, :] - tn), 0.0)
    return jnp.asarray(mc, F32), jnp.asarray(mn, F32)


def _layer0(x, rope, p, prompt, state):
    Bg, L, D = x.shape
    M = Bg * L
    hd = D // H_B
    tm = _tile(M, 1024)
    tm_in = _tile(rope[0].shape[0], 512)
    x2 = x.reshape(M, D)
    xagq, k, v = _norm_proj(x2, p["g_mix_pre0"], p["w_in_ab"], ((0, 3), (3, 1), (4, 1)),
                            tm_in, D, rope=rope, rope_q=2, rope_k=3, q_scale=hd ** -0.5)
    xagq3 = xagq.reshape(Bg, L, 3 * D)
    lru_w = (p["conv_a_w"], p["conv_a_b"], p["wa_bd"], p["lru_ba"], p["wx_bd"], p["lru_bx"], p["lam"])
    if prompt:
        ya, h_last = _rglru_prompt(xagq3, *lru_w, ts=_tile(L, 256))
        ya = ya.reshape(M, D)
        h_last = h_last.reshape(Bg, D)
        new_conv = xagq3[:, L - (CONV_W - 1):, :D]
        yb = _attn_prompt(xagq3, 2 * D // LANES, k.reshape(Bg, L, D), v.reshape(Bg, L, D),
                          sorted(DILATIONS, key=lambda wd: -wd[1]), QBLOCK).reshape(M, D)
        wb = min(MAX_WINDOW, L)
        new_k = k.reshape(Bg, L, H_B, hd)[:, L - wb:]
        new_v = v.reshape(Bg, L, H_B, hd)[:, L - wb:]
    else:
        conv0, h0, k_buf, v_buf = state
        xagq_t = xagq3.transpose(1, 0, 2)
        xa_t, ga_t = xagq_t[:, :, :D], xagq_t[:, :, D:2 * D]
        st_t = conv0.transpose(1, 0, 2)
        ya_t, h_last = _rglru_sample(xa_t, ga_t, st_t, h0, *lru_w)
        ya = ya_t.transpose(1, 0, 2).reshape(M, D)
        new_conv = jnp.concatenate([conv0, xagq3[:, :, :D]], axis=1)[:, -(CONV_W - 1):]
        wb = k_buf.shape[1]
        tp = 16
        pad = lambda a: jnp.pad(a.reshape(Bg, L, D), ((0, 0), (0, tp - L), (0, 0)))
        mc, mn = _sample_multiplicity(wb, L, H_B)
        q = xagq3[:, :, 2 * D:]
        yb = _attn_sample(pad(q), pad(k), pad(v), k_buf.reshape(Bg, wb * H_B, hd), v_buf.reshape(Bg, wb * H_B, hd),
                          mc, mn, L, H_B)[:, :L].reshape(M, D)
        new_k = k.reshape(Bg, L, H_B, hd)
        new_v = v.reshape(Bg, L, H_B, hd)
    x1 = _out_proj([ya, yb], [p["w_out_a"], p["w_out_b"]], x2, p["g_mix_post0"], tm)
    x2o = _ffn(x1, p["g_ffn_pre0"], p["w_up0"], p["w_dn0"], p["g_ffn_post0"], _tile(M, 1024), 1024)
    return x2o.reshape(Bg, L, D), (new_conv, h_last, new_k, new_v)


def _layer1(x, p, prompt, state):
    Bg, L, D = x.shape
    M = Bg * L
    x2 = x.reshape(M, D)
    tm = _tile(M, 1024)
    DC = p["w_in_c"].shape[1] // 2
    DH = DC // NH_C
    (xmz,) = _norm_proj(x2, p["g_mix_pre1"], p["w_in_c"], ((0, 2 * DC // 1024),), _tile(M, 512), 1024)
    xmz3 = xmz.reshape(Bg, L, 2 * DC)
    pre_w = (p["conv_c_w"], p["conv_c_b"], p["wq_bd"], p["wk_bd"], p["wv_bd"], p["wg"], p["bg"])
    if prompt:
        tail0 = jnp.zeros((Bg, SUBLANES, DC), F32)
        c = _tile(L, 256)
        q, k, v, xc, gates = _mlstm_pre(xmz3, tail0, *pre_w, ts=c, k_scale=DH ** -0.5)
        y, C, n, m = _mlstm_core(q, k, v, xc, xmz3, gates, p["mlstm_norm"], p["mlstm_skip"], None, c, c, NH_C)
        new_conv = xmz3[:, L - (CONV_W - 1):, :DC]
        y = y.reshape(M, DC)
    else:
        conv0, C0, n0, m0 = state
        lp = 16
        xmz_p = jnp.pad(xmz3, ((0, 0), (0, lp - L), (0, 0)))
        tail0 = jnp.pad(conv0, ((0, 0), (SUBLANES - (CONV_W - 1), 0), (0, 0)))
        q, k, v, xc, gates = _mlstm_pre(xmz_p, tail0, *pre_w, ts=lp, k_scale=DH ** -0.5)
        m0b = jnp.broadcast_to(m0[:, :, None], m0.shape + (LANES,))
        y, C, n, m = _mlstm_core(q, k, v, xc, xmz_p, gates, p["mlstm_norm"], p["mlstm_skip"],
                                 (C0, n0, m0b), 128, L, NH_C)
        new_conv = jnp.concatenate([conv0, xmz3[:, :, :DC]], axis=1)[:, -(CONV_W - 1):]
        y = y[:, :L].reshape(M, DC)
    x1 = _out_proj([y], [p["w_out_c"]], x2, p["g_mix_post1"], tm)
    x2o = _ffn(x1, p["g_ffn_pre1"], p["w_up1"], p["w_dn1"], p["g_ffn_post1"], _tile(M, 1024), 1024)
    return x2o.reshape(Bg, L, D), (new_conv, C, n, m[:, :, 0])


def kernel(x_prompt, x_sample, state_rglru_conv, state_rglru_h, cache_swa_k, cache_swa_v, state_mlstm_conv, state_mlstm_C, state_mlstm_n, state_mlstm_m, norm_mix_pre, norm_mix_post, norm_ffn_pre, norm_ffn_post, w_ffn_up, w_ffn_down, w_in_ab, conv_a_w, conv_a_b, lru_wa, lru_ba, lru_wx, lru_bx, lru_lambda, w_out_ab, w_in_c, conv_c_w, conv_c_b, mlstm_wq, mlstm_wk, mlstm_wv, mlstm_w_gate, mlstm_b_gate, mlstm_norm, mlstm_skip, w_out_c):
    B, S, D = x_prompt.shape
    Bs, Ts, _ = x_sample.shape
    DA = conv_a_w.shape[-1]
    DC = conv_c_w.shape[-1]
    hd = D // H_B
    row = lambda a: a.reshape(1, -1)
    wg = jnp.zeros((3 * DC, 2 * LANES), F32)
    wg = wg.at[:, :NH_C].set(mlstm_w_gate[0][:, :NH_C]).at[:, LANES:LANES + NH_C].set(mlstm_w_gate[0][:, NH_C:])
    bg = jnp.zeros((1, 2 * LANES), F32)
    bg = bg.at[0, :NH_C].set(mlstm_b_gate[0][:NH_C]).at[0, LANES:LANES + NH_C].set(mlstm_b_gate[0][NH_C:])
    p = {
        "g_mix_pre0": row(norm_mix_pre[0]), "g_mix_post0": row(norm_mix_post[0]),
        "g_ffn_pre0": row(norm_ffn_pre[0]), "g_ffn_post0": row(norm_ffn_post[0]),
        "g_mix_pre1": row(norm_mix_pre[1]), "g_mix_post1": row(norm_mix_post[1]),
        "g_ffn_pre1": row(norm_ffn_pre[1]), "g_ffn_post1": row(norm_ffn_post[1]),
        "w_up0": w_ffn_up[0].astype(BF16), "w_dn0": w_ffn_down[0].astype(BF16),
        "w_up1": w_ffn_up[1].astype(BF16), "w_dn1": w_ffn_down[1].astype(BF16),
        "w_in_ab": w_in_ab[0].astype(BF16),
        "conv_a_w": conv_a_w[0], "conv_a_b": row(conv_a_b[0]),
        "wa_bd": _regroup_block_diag(lru_wa[0], MXU_DIM), "lru_ba": row(lru_ba[0]),
        "wx_bd": _regroup_block_diag(lru_wx[0], MXU_DIM), "lru_bx": row(lru_bx[0]),
        "lam": row(lru_lambda[0]),
        "w_out_a": w_out_ab[0][:DA].astype(BF16), "w_out_b": w_out_ab[0][DA:].astype(BF16),
        "w_in_c": w_in_c[0].astype(BF16),
        "conv_c_w": conv_c_w[0], "conv_c_b": row(conv_c_b[0]),
        "wq_bd": _regroup_block_diag(mlstm_wq[0], MXU_DIM),
        "wk_bd": _regroup_block_diag(mlstm_wk[0], MXU_DIM),
        "wv_bd": _regroup_block_diag(mlstm_wv[0], MXU_DIM),
        "wg": wg.astype(BF16), "bg": bg,
        "mlstm_norm": row(mlstm_norm[0]), "mlstm_skip": row(mlstm_skip[0]),
        "w_out_c": w_out_c[0].astype(BF16),
    }
    dt = state_rglru_conv.dtype

    xp, st0 = _layer0(x_prompt, _rope_tables(np.arange(S), hd), p, True, None)
    xp, st1 = _layer1(xp, p, True, None)
    pos_s = PAST_LEN + (np.arange(Bs * Ts) % Ts)
    xs, ss0 = _layer0(x_sample, _rope_tables(pos_s, hd), p, False,
                      (state_rglru_conv[0], state_rglru_h[0], cache_swa_k[0], cache_swa_v[0]))
    xs, ss1 = _layer1(xs, p, False, (state_mlstm_conv[0], state_mlstm_C[0], state_mlstm_n[0], state_mlstm_m[0]))

    lead = lambda a: a[None].astype(dt)
    return (xp, xs,
            lead(st0[0]), lead(ss0[0]), lead(st0[1]), lead(ss0[1]),
            lead(st0[2]), lead(ss0[2]), lead(st0[3]), lead(ss0[3]),
            lead(st1[0]), lead(ss1[0]), lead(st1[1]), lead(ss1[1]),
            lead(st1[2]), lead(ss1[2]), lead(st1[3]), lead(ss1[3]))
```

```python
import functools
import math

import numpy as np
import jax
import jax.numpy as jnp
from jax import lax
from jax.experimental import pallas as pl
from jax.experimental.pallas import tpu as pltpu

F32 = jnp.float32
BF16 = jnp.bfloat16

NA_BLOCKS = 16
CONV_W = 4
LRU_C = 8.0
H_B = 8
ROPE_THETA = 500000.0
DILATIONS = ((128, 1), (512, 4), (2048, 16))
MAX_WINDOW = 2048
QBLOCK = 128
NH_C = 4
QKV_BLOCK = 4
PAST_LEN = 16384
EPS = 1e-6

LANES = 128
SUBLANES = 8
MXU_DIM = 256
VMEM_LIMIT = 56 * 1024 * 1024

NEG_INF = float("-inf")


def _cparams(sem):
    return pltpu.CompilerParams(dimension_semantics=sem, vmem_limit_bytes=VMEM_LIMIT)


def _tile(n, pref):
    t = min(n, pref)
    while n % t:
        t -= 1
    return t


def _rms(x, g):
    return x * lax.rsqrt(jnp.mean(x * x, axis=-1, keepdims=True) + EPS) * g


def _softplus(z):
    return jnp.maximum(z, 0.0) + jnp.log1p(jnp.exp(-jnp.abs(z)))


def _gelu_tanh(x):
    c = math.sqrt(2.0 / math.pi)
    return x * (0.5 * (1.0 + jnp.tanh(c * (x + 0.044715 * (x * x * x)))))


def _silu(x):
    return x * jax.nn.sigmoid(x)


def _rope_tile(y, c, sa, sb):
    outs = []
    for h in range(y.shape[1] // LANES):
        yh = y[:, h * LANES:(h + 1) * LANES]
        outs.append(yh * c + pltpu.roll(yh, LANES - 16, 1) * sa + pltpu.roll(yh, 16, 1) * sb)
    return jnp.concatenate(outs, axis=1)


def _norm_proj_kernel(*refs, groups, tn, rope_q, rope_k, q_scale):
    n_out = len(groups)
    x_ref, g_ref, w_ref = refs[:3]
    pos = 3
    if rope_q is not None:
        c_ref, sa_ref, sb_ref = refs[3:6]
        pos = 6
    out_refs = refs[pos:pos + n_out]
    xn = _rms(x_ref[...], g_ref[...]).astype(BF16)
    for (start, count, _, act), o_ref in zip(groups, out_refs):
        for jj in range(start, start + count):
            y = jnp.dot(xn, w_ref[:, jj * tn:(jj + 1) * tn], preferred_element_type=F32)
            if rope_q is not None and jj == rope_q:
                y = _rope_tile(y, c_ref[...], sa_ref[...], sb_ref[...]) * q_scale
            elif rope_q is not None and jj == rope_k:
                y = _rope_tile(y, c_ref[...], sa_ref[...], sb_ref[...])
            if act is not None:
                y = act(y)
            o_ref[:, (jj - start) * tn:(jj - start + 1) * tn] = y.astype(o_ref.dtype)


def _norm_proj(x, g, w, groups, tm, tn, rope=None, rope_q=None, rope_k=None, q_scale=1.0):
    M, K = x.shape
    in_specs = [
        pl.BlockSpec((tm, K), lambda i: (i, 0)),
        pl.BlockSpec((1, K), lambda i: (0, 0)),
        pl.BlockSpec(w.shape, lambda i: (0, 0)),
    ]
    args = [x, g, w]
    if rope is not None:
        assert rope[0].shape[0] % tm == 0 and M % tm == 0
        pos_blocks = rope[0].shape[0] // tm
        for t in rope:
            in_specs.append(pl.BlockSpec((tm, LANES), lambda i, pb=pos_blocks: (i % pb, 0)))
            args.append(t)
    out_specs, out_shapes = [], []
    for (_, count, dtype, _) in groups:
        out_specs.append(pl.BlockSpec((tm, count * tn), lambda i: (i, 0)))
        out_shapes.append(jax.ShapeDtypeStruct((M, count * tn), dtype))
    kern = functools.partial(_norm_proj_kernel, groups=tuple(groups), tn=tn,
                             rope_q=rope_q if rope is not None else None, rope_k=rope_k, q_scale=q_scale)
    return pl.pallas_call(
        kern,
        grid=(M // tm,),
        in_specs=in_specs,
        out_specs=out_specs,
        out_shape=out_shapes,
        compiler_params=_cparams(("parallel",)),
        name="norm_proj",
    )(*args)


def _out_proj_kernel(*refs, n_in):
    a_refs = refs[:n_in]
    w_refs = refs[n_in:2 * n_in]
    x_ref, g_ref, o_ref = refs[2 * n_in:]
    y = None
    for a_ref, w_ref in zip(a_refs, w_refs):
        d = jnp.dot(a_ref[...].astype(BF16), w_ref[...], preferred_element_type=F32)
        y = d if y is None else y + d
    o_ref[...] = x_ref[...] + _rms(y, g_ref[...])


def _out_proj(acts, ws, x, g, tm):
    M, D = x.shape
    n_in = len(acts)
    in_specs = [pl.BlockSpec((tm, a.shape[1]), lambda i: (i, 0)) for a in acts]
    in_specs += [pl.BlockSpec(w.shape, lambda i: (0, 0)) for w in ws]
    in_specs += [pl.BlockSpec((tm, D), lambda i: (i, 0)), pl.BlockSpec((1, D), lambda i: (0, 0))]
    return pl.pallas_call(
        functools.partial(_out_proj_kernel, n_in=n_in),
        grid=(M // tm,),
        in_specs=in_specs,
        out_specs=pl.BlockSpec((tm, D), lambda i: (i, 0)),
        out_shape=jax.ShapeDtypeStruct((M, D), F32),
        compiler_params=_cparams(("parallel",)),
        name="out_proj",
    )(*acts, *ws, x, g)


def _ffn_kernel(x_ref, g1_ref, wu_ref, wd_ref, g2_ref, o_ref, xn_ref, acc_ref):
    f = pl.program_id(1)

    @pl.when(f == 0)
    def _():
        xn_ref[...] = _rms(x_ref[...], g1_ref[...]).astype(BF16)
        acc_ref[...] = jnp.zeros_like(acc_ref)

    hid = jnp.dot(xn_ref[...], wu_ref[...], preferred_element_type=F32)
    hid = jnp.square(jnp.maximum(hid, 0.0))
    acc_ref[...] += jnp.dot(hid.astype(BF16), wd_ref[...], preferred_element_type=F32)

    @pl.when(f == pl.num_programs(1) - 1)
    def _():
        o_ref[...] = x_ref[...] + _rms(acc_ref[...], g2_ref[...])


def _ffn(x, g1, wu, wd, g2, tm, tf):
    M, D = x.shape
    FF = wu.shape[1]
    return pl.pallas_call(
        _ffn_kernel,
        grid=(M // tm, FF // tf),
        in_specs=[
            pl.BlockSpec((tm, D), lambda i, f: (i, 0)),
            pl.BlockSpec((1, D), lambda i, f: (0, 0)),
            pl.BlockSpec((D, tf), lambda i, f: (0, f)),
            pl.BlockSpec((tf, D), lambda i, f: (f, 0)),
            pl.BlockSpec((1, D), lambda i, f: (0, 0)),
        ],
        out_specs=pl.BlockSpec((tm, D), lambda i, f: (i, 0)),
        out_shape=jax.ShapeDtypeStruct((M, D), F32),
        scratch_shapes=[pltpu.VMEM((tm, D), BF16), pltpu.VMEM((tm, D), F32)],
        compiler_params=_cparams(("parallel", "arbitrary")),
        name="ffn",
    )(x, g1, wu, wd, g2)


def _lru_gates(xc, wa_ref, ba, wx_ref, bx, lam):
    xcb = xc.astype(BF16)
    ng = wa_ref.shape[0]
    gw = wa_ref.shape[1]
    ra, ri = [], []
    for g in range(ng):
        xs = xcb[:, g * gw:(g + 1) * gw]
        ra.append(jnp.dot(xs, wa_ref[g], preferred_element_type=F32))
        ri.append(jnp.dot(xs, wx_ref[g], preferred_element_type=F32))
    r = jax.nn.sigmoid(jnp.concatenate(ra, axis=1) + ba)
    i = jax.nn.sigmoid(jnp.concatenate(ri, axis=1) + bx)
    log_a = -LRU_C * r * _softplus(-lam)
    a = jnp.exp(log_a)
    om = 1.0 - a * a
    b = jnp.where(om > 0.0, om * lax.rsqrt(om), 0.0) * (i * xc)
    return a, b


def _conv4(xe_ref, x, first, w_ref, b):
    T = x.shape[0]
    if first is not None:
        @pl.when(first)
        def _():
            xe_ref[0:SUBLANES, :] = jnp.zeros((SUBLANES, x.shape[1]), x.dtype)
    xe_ref[SUBLANES:, :] = x
    y = b
    for j in range(CONV_W):
        y = y + w_ref[j:j + 1, :] * xe_ref[pl.ds(SUBLANES - (CONV_W - 1) + j, T), :]
    if T >= SUBLANES:
        xe_ref[0:SUBLANES, :] = x[T - SUBLANES:, :]
    return y


def _rglru_prompt_kernel(xa_ref, ga_ref, cw_ref, cb_ref, wa_ref, ba_ref, wx_ref, bx_ref, lam_ref,
                         ya_ref, hl_ref, xe_ref, hc_ref, a_s, b_s, h_s):
    t = pl.program_id(1)
    T = xa_ref.shape[0]

    @pl.when(t == 0)
    def _():
        hc_ref[...] = jnp.zeros_like(hc_ref)

    xc = _conv4(xe_ref, xa_ref[...], t == 0, cw_ref, cb_ref[...])
    a, b = _lru_gates(xc, wa_ref, ba_ref[...], wx_ref, bx_ref[...], lam_ref[...])
    a_s[...] = a
    b_s[...] = b

    def body(s, h):
        h = a_s[pl.ds(s, 1), :] * h + b_s[pl.ds(s, 1), :]
        h_s[pl.ds(s, 1), :] = h
        return h

    h = lax.fori_loop(0, T, body, hc_ref[...], unroll=8)
    hc_ref[...] = h
    ya_ref[...] = (h_s[...] * _gelu_tanh(ga_ref[...])).astype(ya_ref.dtype)

    @pl.when(t == pl.num_programs(1) - 1)
    def _():
        hl_ref[...] = h


def _rglru_prompt(xaga, cw, cb, wa_bd, ba, wx_bd, bx, lam, ts):
    B, S, _ = xaga.shape
    C = cw.shape[-1]
    wspec = lambda a: pl.BlockSpec(a.shape, lambda b, t: (0,) * a.ndim)
    return pl.pallas_call(
        _rglru_prompt_kernel,
        grid=(B, S // ts),
        in_specs=[
            pl.BlockSpec((None, ts, C), lambda b, t: (b, t, 0)),
            pl.BlockSpec((None, ts, C), lambda b, t: (b, t, 1)),
            wspec(cw), wspec(cb), wspec(wa_bd), wspec(ba), wspec(wx_bd), wspec(bx), wspec(lam),
        ],
        out_specs=[
            pl.BlockSpec((None, ts, C), lambda b, t: (b, t, 0)),
            pl.BlockSpec((None, 1, C), lambda b, t: (b, 0, 0)),
        ],
        out_shape=[jax.ShapeDtypeStruct((B, S, C), BF16), jax.ShapeDtypeStruct((B, 1, C), F32)],
        scratch_shapes=[pltpu.VMEM((SUBLANES + ts, C), F32), pltpu.VMEM((1, C), F32),
                        pltpu.VMEM((ts, C), F32), pltpu.VMEM((ts, C), F32), pltpu.VMEM((ts, C), F32)],
        compiler_params=_cparams(("parallel", "arbitrary")),
        name="rglru_prompt",
    )(xaga, xaga, cw, cb, wa_bd, ba, wx_bd, bx, lam)


def _rglru_sample_kernel(xa_ref, ga_ref, st_ref, h0_ref, cw_ref, cb_ref, wa_ref, ba_ref, wx_ref, bx_ref,
                         lam_ref, ya_ref, hl_ref):
    T = xa_ref.shape[0]
    rows = [st_ref[j] for j in range(CONV_W - 1)] + [xa_ref[s] for s in range(T)]
    h = h0_ref[...]
    for s in range(T):
        y = cb_ref[...]
        for j in range(CONV_W):
            y = y + cw_ref[j:j + 1, :] * rows[s + j]
        a, b = _lru_gates(y, wa_ref, ba_ref[...], wx_ref, bx_ref[...], lam_ref[...])
        h = a * h + b
        ya_ref[s] = (h * _gelu_tanh(ga_ref[s])).astype(ya_ref.dtype)
    hl_ref[...] = h


def _rglru_sample(xa_t, ga_t, st_t, h0, cw, cb, wa_bd, ba, wx_bd, bx, lam):
    T, B, C = xa_t.shape
    args = (xa_t, ga_t, st_t, h0, cw, cb, wa_bd, ba, wx_bd, bx, lam)
    return pl.pallas_call(
        _rglru_sample_kernel,
        grid=(1,),
        in_specs=[pl.BlockSpec(a.shape, lambda i, n=a.ndim: (0,) * n) for a in args],
        out_specs=[pl.BlockSpec((T, B, C), lambda i: (0, 0, 0)), pl.BlockSpec((B, C), lambda i: (0, 0))],
        out_shape=[jax.ShapeDtypeStruct((T, B, C), F32), jax.ShapeDtypeStruct((B, C), F32)],
        compiler_params=_cparams(("arbitrary",)),
        name="rglru_sample",
    )(*args)


def _attn_prompt_kernel(q_ref, k_ref, v_ref, o_ref, *scratch, groups, qb, unroll):
    S = q_ref.shape[0]
    E = q_ref.shape[1]
    dist2 = (lax.broadcasted_iota(jnp.int32, (qb, 2 * qb), 0) + qb
             - lax.broadcasted_iota(jnp.int32, (qb, 2 * qb), 1))
    dist1 = lax.broadcasted_iota(jnp.int32, (qb, qb), 0) - lax.broadcasted_iota(jnp.int32, (qb, qb), 1)

    def rows(start, d):
        return pl.ds(start, qb) if d == 1 else pl.ds(start, qb, stride=d)

    ng = len(groups)
    s_scr, p_scr = scratch[3 * ng], scratch[3 * ng + 1]
    dn_t = (((1,), (1,)), ((), ()))

    for gi, (w, d) in enumerate(groups):
        back = w // d
        nb = (S // d) // qb
        nblk = nb * d
        two = nb > 1
        kw = 2 * qb if two else qb
        dist = dist2 if two else dist1
        lim_all = jnp.full((qb, kw), back, jnp.int32)
        lim_first = jnp.minimum(lax.broadcasted_iota(jnp.int32, (qb, kw), 0), back)
        m_s, l_s, acc_s = scratch[3 * gi:3 * gi + 3]
        un = math.gcd(nblk, unroll)

        def locate(idx, d=d):
            if d == 1:
                n = idx
                start = pl.multiple_of(idx * qb, qb)
                prev = pl.multiple_of(jnp.maximum(idx - 1, 0) * qb, qb)
            else:
                n = idx // d
                start = n * (qb * d) + idx % d
                prev = jnp.where(n > 0, start - qb * d, start)
            return start, prev, n

        def gather(ref, start, prev, d=d, two=two):
            if two:
                return jnp.concatenate([ref[rows(prev, d), :], ref[rows(start, d), :]], axis=0).astype(BF16)
            return ref[rows(start, d), :].astype(BF16)

        def scores(idx, c, kw=kw, locate=locate, gather=gather, d=d):
            start, prev, _ = locate(idx)
            q = q_ref[rows(start, d), :].astype(BF16)
            s_scr[pl.ds(pl.multiple_of(idx * qb, qb), qb), 0:kw] = lax.dot_general(
                q, gather(k_ref, start, prev), dn_t, preferred_element_type=F32)
            return c

        def softmax(idx, c, kw=kw, locate=locate, d=d, two=two, dist=dist, lim_all=lim_all,
                    lim_first=lim_first, m_s=m_s, l_s=l_s):
            start, _, n = locate(idx)
            blk = pl.ds(pl.multiple_of(idx * qb, qb), qb)
            lim = jnp.where(n > 0, lim_all, lim_first) if two else lim_all
            valid = (dist >= 0) & (dist <= lim)
            s = jnp.where(valid, s_scr[blk, 0:kw], NEG_INF)
            m_b = jnp.max(s, axis=-1, keepdims=True)
            p_scr[blk, 0:kw] = jnp.exp(s - m_b).astype(BF16)
            m_s[rows(start, d), :] = jnp.broadcast_to(m_b, (qb, E))
            return c

        def values(idx, c, kw=kw, locate=locate, gather=gather, d=d, acc_s=acc_s, l_s=l_s):
            start, prev, _ = locate(idx)
            p = p_scr[pl.ds(pl.multiple_of(idx * qb, qb), qb), 0:kw]
            v1 = jnp.concatenate([gather(v_ref, start, prev), jnp.ones((kw, E), BF16)], axis=1)
            r = jnp.dot(p, v1, preferred_element_type=F32)
            acc_s[rows(start, d), :] = r[:, :E]
            l_s[rows(start, d), :] = r[:, E:]
            return c

        lax.fori_loop(0, nblk, scores, 0, unroll=un)
        lax.fori_loop(0, nblk, softmax, 0, unroll=un)
        lax.fori_loop(0, nblk, values, 0, unroll=un)

    ng = len(groups)

    def merge(i, c):
        rs = pl.ds(pl.multiple_of(i * qb, qb), qb)
        ms = [scratch[3 * g][rs, :] for g in range(ng)]
        m = functools.reduce(jnp.maximum, ms)
        num = den = None
        for g in range(ng):
            wg = jnp.exp(ms[g] - m)
            n_g = wg * scratch[3 * g + 2][rs, :]
            d_g = wg * scratch[3 * g + 1][rs, :]
            num = n_g if num is None else num + n_g
            den = d_g if den is None else den + d_g
        o_ref[rs, :] = (num / den).astype(o_ref.dtype)
        return c

    lax.fori_loop(0, S // qb, merge, 0, unroll=2)


def _attn_prompt(q, q_off, k, v, groups, qb):
    B, S, D = k.shape
    nh = D // LANES
    for (w, d) in groups:
        assert S % (d * qb) == 0 and w // d <= qb
    spec = pl.BlockSpec((None, S, LANES), lambda b, h: (b, 0, h))
    q_spec = pl.BlockSpec((None, S, LANES), lambda b, h: (b, 0, q_off + h))
    return pl.pallas_call(
        functools.partial(_attn_prompt_kernel, groups=tuple(groups), qb=qb, unroll=16),
        grid=(B, nh),
        in_specs=[q_spec, spec, spec],
        out_specs=spec,
        out_shape=jax.ShapeDtypeStruct((B, S, D), BF16),
        scratch_shapes=[pltpu.VMEM((S, LANES), F32)] * (3 * len(groups))
        + [pltpu.VMEM((S, 2 * qb), F32), pltpu.VMEM((S, 2 * qb), BF16)],
        compiler_params=_cparams(("parallel", "parallel")),
        name="attn_prompt",
    )(q, k, v)


def _attn_sample_kernel(q_ref, kn_ref, vn_ref, kt_ref, vt_ref, ks_ref, vs_ref, mt_ref, ms_ref, mn_ref, o_ref):
    nq, hd = q_ref.shape
    qb = jnp.concatenate([q_ref[...], jnp.zeros((LANES - nq, hd), F32)], axis=0).astype(BF16)
    nn = kn_ref.shape[0]
    zn = jnp.zeros((LANES - nn, hd), F32)
    key_sets = (
        (kt_ref[...], vt_ref[...], mt_ref[...]),
        (ks_ref[...].reshape(-1, hd), vs_ref[...].reshape(-1, hd), ms_ref[...]),
        (jnp.concatenate([kn_ref[...], zn], axis=0), jnp.concatenate([vn_ref[...], zn], axis=0), mn_ref[...]),
    )
    scored = []
    for k, v, mult in key_sets:
        s = lax.dot_general(k.astype(BF16), qb, (((1,), (1,)), ((), ())), preferred_element_type=F32)
        scored.append((jnp.where(mult > 0.0, s, NEG_INF), mult, v))
    m = functools.reduce(jnp.maximum, [jnp.max(s, axis=0, keepdims=True) for s, _, _ in scored])
    m = jnp.where(m == NEG_INF, 0.0, m)
    acc = None
    for s, mult, v in scored:
        p = (mult * jnp.exp(s - m)).astype(BF16)
        v1 = jnp.concatenate([v.astype(BF16), jnp.ones(v.shape, BF16)], axis=1)
        r = lax.dot_general(p, v1, (((0,), (0,)), ((), ())), preferred_element_type=F32)
        acc = r if acc is None else acc + r
    o_ref[...] = acc[:nq, :hd] / acc[:nq, hd:]


def _attn_sample(q, kn, vn, kc, vc, T, nh):
    B, nq, hd = q.shape
    wb = kc.shape[1] // nh
    far = max(DILATIONS, key=lambda wd: wd[0])
    near = [g for g in DILATIONS if g != far]
    R = max(w for w, _ in near)
    d_far = far[1]
    assert near and wb % R == 0 and wb % d_far == 0 and T <= d_far and nq == T * nh and nq <= LANES
    A = wb // d_far

    def mult(delta, groups):
        m = np.zeros(delta.shape, np.float32)
        for (w, d) in groups:
            m = m + ((delta >= 0) & (delta % d == 0) & (delta <= (w // d) * d))
        return m

    def table(pos_of_row, head_of_row, groups):
        lane = np.arange(LANES)
        t, hq = lane // nh, lane % nh
        delta = (wb + t)[None, :] - pos_of_row[:, None]
        tab = mult(delta, groups) * (head_of_row[:, None] == hq[None, :]) * (t < T)[None, :]
        return jnp.asarray(tab, F32)

    rt = np.arange(R * nh)
    mt = table(wb - R + rt // nh, rt % nh, near)
    rs = np.arange(A * T * nh)
    ms = table((rs // (T * nh)) * d_far + (rs % (T * nh)) // nh, rs % nh, [far])
    rn = np.arange(LANES)
    mn = table(np.where(rn < nq, wb + rn // nh, -10 ** 9), rn % nh, DILATIONS)

    small = pl.BlockSpec((None, nq, hd), lambda b: (b, 0, 0))
    tail = pl.BlockSpec((None, R * nh, hd), lambda b: (b, wb // R - 1, 0))
    strided = pl.BlockSpec((None, A, T * nh, hd), lambda b: (b, 0, 0, 0))
    const = lambda a: pl.BlockSpec(a.shape, lambda b: (0, 0))
    kc4 = kc.reshape(B, A, d_far * nh, hd)
    vc4 = vc.reshape(B, A, d_far * nh, hd)
    return pl.pallas_call(
        _attn_sample_kernel,
        grid=(B,),
        in_specs=[small, small, small, tail, tail, strided, strided, const(mt), const(ms), const(mn)],
        out_specs=small,
        out_shape=jax.ShapeDtypeStruct((B, nq, hd), F32),
        compiler_params=_cparams(("parallel",)),
        name="attn_sample",
    )(q, kn, vn, kc, vc, kc4, vc4, mt, ms, mn)


def _mlstm_qkv_gates(xc, xm, wq_ref, wk_ref, wv_ref, wg_ref, bg_ref, q_ref, k_ref, v_ref, xc_ref, g_ref, k_scale):
    xcb = xc.astype(BF16)
    xmb = xm.astype(BF16)
    ng, gw = wq_ref.shape[0], wq_ref.shape[1]
    qs, ks, vs = [], [], []
    for g in range(ng):
        sl = slice(g * gw, (g + 1) * gw)
        qs.append(jnp.dot(xcb[:, sl], wq_ref[g], preferred_element_type=F32))
        ks.append(jnp.dot(xcb[:, sl], wk_ref[g], preferred_element_type=F32))
        vs.append(jnp.dot(xmb[:, sl], wv_ref[g], preferred_element_type=F32))
    q = jnp.concatenate(qs, axis=1)
    k = jnp.concatenate(ks, axis=1)
    v = jnp.concatenate(vs, axis=1)
    qkv = jnp.concatenate([q, k, v], axis=1).astype(BF16)
    g_ref[...] = jnp.dot(qkv, wg_ref[...], preferred_element_type=F32) + bg_ref[...]
    q_ref[...] = q.astype(q_ref.dtype)
    k_ref[...] = (k * k_scale).astype(k_ref.dtype)
    v_ref[...] = v.astype(v_ref.dtype)
    xc_ref[...] = xcb.astype(xc_ref.dtype)


def _mlstm_pre_kernel(xm_ref, tail0_ref, cw_ref, cb_ref, wq_ref, wk_ref, wv_ref, wg_ref, bg_ref,
                      q_ref, k_ref, v_ref, xc_ref, g_ref, xe_ref, *, k_scale):
    @pl.when(pl.program_id(1) == 0)
    def _():
        xe_ref[0:SUBLANES, :] = tail0_ref[...]

    xm = xm_ref[...]
    xc = _silu(_conv4(xe_ref, xm, None, cw_ref, cb_ref[...]))
    _mlstm_qkv_gates(xc, xm, wq_ref, wk_ref, wv_ref, wg_ref, bg_ref, q_ref, k_ref, v_ref, xc_ref, g_ref, k_scale)


def _mlstm_pre_sample_kernel(xm_ref, st_ref, cw_ref, cb_ref, wq_ref, wk_ref, wv_ref, wg_ref, bg_ref,
                             q_ref, k_ref, v_ref, xc_ref, g_ref, *, k_scale):
    T = xm_ref.shape[0]
    rows = [st_ref[j] for j in range(CONV_W - 1)] + [xm_ref[s] for s in range(T)]
    conv = []
    for s in range(T):
        y = cb_ref[...]
        for j in range(CONV_W):
            y = y + cw_ref[j:j + 1, :] * rows[s + j]
        conv.append(y)
    xc = _silu(jnp.concatenate(conv, axis=0))
    xm = jnp.concatenate(rows[CONV_W - 1:], axis=0)
    _mlstm_qkv_gates(xc, xm, wq_ref, wk_ref, wv_ref, wg_ref, bg_ref, q_ref, k_ref, v_ref, xc_ref, g_ref, k_scale)


def _mlstm_pre_sample(xm_t, st_t, cw, cb, wq_bd, wk_bd, wv_bd, wg, bg, k_scale):
    T, B, C = xm_t.shape
    args = (xm_t, st_t, cw, cb, wq_bd, wk_bd, wv_bd, wg, bg)
    full = lambda shape: pl.BlockSpec(shape, lambda i: (0,) * len(shape))
    return pl.pallas_call(
        functools.partial(_mlstm_pre_sample_kernel, k_scale=k_scale),
        grid=(1,),
        in_specs=[full(a.shape) for a in args],
        out_specs=[full((T * B, C))] * 4 + [full((T * B, 2 * LANES))],
        out_shape=[jax.ShapeDtypeStruct((T * B, C), BF16)] * 4 + [jax.ShapeDtypeStruct((T * B, 2 * LANES), F32)],
        compiler_params=_cparams(("arbitrary",)),
        name="mlstm_pre_sample",
    )(*args)


def _mlstm_pre(xm, tail0, cw, cb, wq_bd, wk_bd, wv_bd, wg, bg, ts, k_scale):
    B, S, C = xm.shape
    wspec = lambda a: pl.BlockSpec(a.shape, lambda b, t: (0,) * a.ndim)
    act = pl.BlockSpec((None, ts, C), lambda b, t: (b, t, 0))
    return pl.pallas_call(
        functools.partial(_mlstm_pre_kernel, k_scale=k_scale),
        grid=(B, S // ts),
        in_specs=[act, pl.BlockSpec((None, SUBLANES, C), lambda b, t: (b, 0, 0)),
                  wspec(cw), wspec(cb), wspec(wq_bd), wspec(wk_bd), wspec(wv_bd), wspec(wg), wspec(bg)],
        out_specs=[act, act, act, act, pl.BlockSpec((None, ts, 2 * LANES), lambda b, t: (b, t, 0))],
        out_shape=[jax.ShapeDtypeStruct((B, S, C), BF16)] * 4 + [jax.ShapeDtypeStruct((B, S, 2 * LANES), F32)],
        scratch_shapes=[pltpu.VMEM((SUBLANES + ts, C), F32)],
        compiler_params=_cparams(("parallel", "arbitrary")),
        name="mlstm_pre",
    )(xm, tail0, cw, cb, wq_bd, wk_bd, wv_bd, wg, bg)


def _mlstm_core_kernel(*refs, c, n_valid, nh, zero_init):
    q_ref, k_ref, v_ref, xc_ref, gz_ref, g_ref, nw_ref, sk_ref = refs[:8]
    pos = 8
    if not zero_init:
        c0_ref, n0_ref, m0_ref = refs[8:11]
        pos = 11
    y_ref, C_ref, n_ref, m_ref = refs[pos:pos + 4]
    ch = pl.program_id(1)
    c_in = q_ref.shape[0]
    DH = q_ref.shape[1] // nh

    @pl.when(ch == 0)
    def _():
        if zero_init:
            C_ref[...] = jnp.zeros_like(C_ref)
            n_ref[...] = jnp.zeros_like(n_ref)
            m_ref[...] = jnp.zeros_like(m_ref)
        else:
            C_ref[...] = c0_ref[...]
            n_ref[...] = n0_ref[...]
            m_ref[...] = m0_ref[...]

    def padded(x):
        if c_in == c:
            return x
        return jnp.concatenate([x, jnp.zeros((c - c_in, x.shape[1]), x.dtype)], axis=0)

    g = padded(g_ref[...])
    row = lax.broadcasted_iota(jnp.int32, (c, LANES), 0)
    valid = row < n_valid
    gi = jnp.where(valid, g[:, :LANES], NEG_INF)
    lf = jnp.where(valid, -_softplus(-g[:, LANES:]), 0.0)
    tri_f = (lax.broadcasted_iota(jnp.int32, (c, c), 1) <= lax.broadcasted_iota(jnp.int32, (c, c), 0))
    bcum = jnp.dot(tri_f.astype(F32), lf, preferred_element_type=F32, precision=lax.Precision.HIGHEST)
    bcum_t = bcum.T
    gi_t = gi.T
    causal = tri_f

    for h in range(nh):
        sl = slice(h * DH, (h + 1) * DH)
        qh = padded(q_ref[:, sl])
        kh = padded(k_ref[:, sl])
        vh = padded(v_ref[:, sl])
        bc_col = bcum[:, h:h + 1]
        bc_row = bcum_t[h:h + 1, :]
        i_row = gi_t[h:h + 1, :]
        i_col = gi[:, h:h + 1]
        m_prev = m_ref[h:h + 1, 0:1]
        log_intra = jnp.where(causal, bc_col - bc_row + i_row, NEG_INF)
        log_inter = bc_col + m_prev
        m_t = jnp.maximum(log_inter, jnp.max(log_intra, axis=1, keepdims=True))
        qk = lax.dot_general(qh, kh, (((1,), (1,)), ((), ())), preferred_element_type=F32)
        w_intra = jnp.exp(log_intra - m_t) * qk
        w_inter = jnp.exp(log_inter - m_t)
        C_old = C_ref[h]
        n_old = n_ref[h:h + 1, :]
        num = jnp.dot(w_intra.astype(BF16), vh, preferred_element_type=F32)
        num = num + w_inter * jnp.dot(qh, C_old.astype(BF16), preferred_element_type=F32)
        n8 = jnp.broadcast_to(n_old, (SUBLANES, DH)).astype(BF16)
        qn = lax.dot_general(qh, n8, (((1,), (1,)), ((), ())), preferred_element_type=F32)[:, 0:1]
        den = jnp.sum(w_intra, axis=1, keepdims=True) + w_inter * qn
        hh = num / jnp.maximum(jnp.abs(den), jnp.exp(-m_t))
        m_new = m_t[c - 1:c, :]
        bc_last = bc_col[c - 1:c, :]
        w_state = jnp.exp(bc_last - bc_col + i_col - m_new)
        decay = jnp.exp(bc_last + m_prev - m_new)
        kw = kh.astype(F32) * w_state
        C_ref[h] = decay * C_old + lax.dot_general(
            kw.astype(BF16), vh, (((0,), (0,)), ((), ())), preferred_element_type=F32)
        n_ref[h:h + 1, :] = decay * n_old + jnp.sum(kw, axis=0, keepdims=True)
        m_ref[h:h + 1, :] = jnp.broadcast_to(m_new, (1, LANES))
        hn = hh * lax.rsqrt(jnp.mean(hh * hh, axis=-1, keepdims=True) + EPS)
        hn = hn[:c_in] * nw_ref[:, sl]
        y = (hn + sk_ref[:, sl] * xc_ref[:, sl].astype(F32)) * gz_ref[:, sl].astype(F32)
        y_ref[:, sl] = y.astype(y_ref.dtype)


def _mlstm_core(q, k, v, xc, gz, gates, nw, sk, state, c, n_valid, nh):
    B, S, C = q.shape
    c_in = min(c, S)
    DH = C // nh
    zero_init = state is None
    act = pl.BlockSpec((None, c_in, C), lambda b, t: (b, t, 0))
    in_specs = [act, act, act, act, act,
                pl.BlockSpec((None, c_in, 2 * LANES), lambda b, t: (b, t, 0)),
                pl.BlockSpec((1, C), lambda b, t: (0, 0)), pl.BlockSpec((1, C), lambda b, t: (0, 0))]
    args = [q, k, v, xc, gz, gates, nw, sk]
    C_spec = pl.BlockSpec((None, nh, DH, DH), lambda b, t: (b, 0, 0, 0))
    n_spec = pl.BlockSpec((None, nh, DH), lambda b, t: (b, 0, 0))
    m_spec = pl.BlockSpec((None, nh, LANES), lambda b, t: (b, 0, 0))
    if not zero_init:
        in_specs += [C_spec, n_spec, m_spec]
        args += list(state)
    return pl.pallas_call(
        functools.partial(_mlstm_core_kernel, c=c, n_valid=n_valid, nh=nh, zero_init=zero_init),
        grid=(B, S // c_in),
        in_specs=in_specs,
        out_specs=[act, C_spec, n_spec, m_spec],
        out_shape=[jax.ShapeDtypeStruct((B, S, C), BF16),
                   jax.ShapeDtypeStruct((B, nh, DH, DH), F32),
                   jax.ShapeDtypeStruct((B, nh, DH), F32),
                   jax.ShapeDtypeStruct((B, nh, LANES), F32)],
        compiler_params=_cparams(("parallel", "arbitrary")),
        name="mlstm_core",
    )(*args)


def _rope_tables(pos, hd):
    rot = hd // 4
    half = rot // 2
    inv = ROPE_THETA ** (-np.arange(0, rot, 2, dtype=np.float64) / rot)
    ang = pos.astype(np.float64)[:, None] * inv[None, :]
    cos, sin = np.cos(ang), np.sin(ang)
    n = pos.shape[0]
    c = np.concatenate([cos, cos, np.ones((n, hd - rot))], axis=1)
    sa = np.concatenate([-sin, np.zeros((n, hd - half))], axis=1)
    sb = np.concatenate([np.zeros((n, half)), sin, np.zeros((n, hd - rot))], axis=1)
    return tuple(jnp.asarray(t, F32) for t in (c, sa, sb))


def _regroup_block_diag(w, group):
    nb, bi, bo = w.shape
    per = group // bi
    a = w.reshape(nb // per, per * bi, bo)
    cols = np.arange(per * bo)
    tile_cols = jnp.asarray(cols[None, :] % bo == np.arange(bo)[:, None], BF16)
    same_block = jnp.asarray((np.arange(per * bi)[:, None] // bi) == (cols[None, :] // bo), BF16)
    tiled = jnp.einsum("gro,oc->grc", a.astype(BF16), tile_cols, preferred_element_type=F32)
    return tiled.astype(BF16) * same_block


def _layer0(x, rope, p, prompt, state):
    Bg, L, D = x.shape
    M = Bg * L
    hd = D // H_B
    tm = _tile(M, 1024)
    tm_in = _tile(rope[0].shape[0], 512)
    x2 = x.reshape(M, D)
    xagq, k, v = _norm_proj(x2, p["g_mix_pre0"], p["w_in_ab"], ((0, 3, F32, None), (3, 1, F32, None), (4, 1, F32, None)),
                            tm_in, D, rope=rope, rope_q=2, rope_k=3, q_scale=hd ** -0.5)
    xagq3 = xagq.reshape(Bg, L, 3 * D)
    lru_w = (p["conv_a_w"], p["conv_a_b"], p["wa_bd"], p["lru_ba"], p["wx_bd"], p["lru_bx"], p["lam"])
    if prompt:
        ya, h_last = _rglru_prompt(xagq3, *lru_w, ts=_tile(L, 256))
        ya = ya.reshape(M, D)
        h_last = h_last.reshape(Bg, D)
        new_conv = xagq3[:, L - (CONV_W - 1):, :D]
        yb = _attn_prompt(xagq3, 2 * D // LANES, k.reshape(Bg, L, D), v.reshape(Bg, L, D),
                          sorted(DILATIONS, key=lambda wd: -wd[1]), QBLOCK).reshape(M, D)
        wb = min(MAX_WINDOW, L)
        new_k = k.reshape(Bg, L, H_B, hd)[:, L - wb:]
        new_v = v.reshape(Bg, L, H_B, hd)[:, L - wb:]
    else:
        conv0, h0, k_buf, v_buf = state
        xagq_t = xagq3.transpose(1, 0, 2)
        xa_t, ga_t = xagq_t[:, :, :D], xagq_t[:, :, D:2 * D]
        st_t = conv0.transpose(1, 0, 2)
        ya_t, h_last = _rglru_sample(xa_t, ga_t, st_t, h0, *lru_w)
        ya = ya_t.transpose(1, 0, 2).reshape(M, D)
        new_conv = jnp.concatenate([conv0, xagq3[:, :, :D]], axis=1)[:, -(CONV_W - 1):]
        wb = k_buf.shape[1]
        native = lambda a: a.reshape(Bg, -1, hd)
        yb = _attn_sample(native(xagq3[:, :, 2 * D:]), native(k), native(v), native(k_buf), native(v_buf),
                          L, H_B).reshape(M, D)
        new_k = k.reshape(Bg, L, H_B, hd)
        new_v = v.reshape(Bg, L, H_B, hd)
    x1 = _out_proj([ya, yb], [p["w_out_a"], p["w_out_b"]], x2, p["g_mix_post0"], tm)
    x2o = _ffn(x1, p["g_ffn_pre0"], p["w_up0"], p["w_dn0"], p["g_ffn_post0"], _tile(M, 1024), 1024)
    return x2o.reshape(Bg, L, D), (new_conv, h_last, new_k, new_v)


def _layer1(x, p, prompt, state):
    Bg, L, D = x.shape
    M = Bg * L
    x2 = x.reshape(M, D)
    tm = _tile(M, 1024)
    DC = p["w_in_c"].shape[1] // 2
    DH = DC // NH_C
    nt = DC // 1024
    xm, gz = _norm_proj(x2, p["g_mix_pre1"], p["w_in_c"], ((0, nt, F32, None), (nt, nt, BF16, _silu)),
                        _tile(M, 512), 1024)
    xm3 = xm.reshape(Bg, L, DC)
    gz3 = gz.reshape(Bg, L, DC)
    pre_w = (p["conv_c_w"], p["conv_c_b"], p["wq_bd"], p["wk_bd"], p["wv_bd"], p["wg"], p["bg"])
    if prompt:
        tail0 = jnp.zeros((Bg, SUBLANES, DC), F32)
        c = _tile(L, 256)
        q, k, v, xc, gates = _mlstm_pre(xm3, tail0, *pre_w, ts=c, k_scale=DH ** -0.5)
        y, C, n, m = _mlstm_core(q, k, v, xc, gz3, gates, p["mlstm_norm"], p["mlstm_skip"], None, c, c, NH_C)
        new_conv = xm3[:, L - (CONV_W - 1):]
        y = y.reshape(M, DC)
    else:
        conv0, C0, n0, m0 = state
        lp = 16
        outs = _mlstm_pre_sample(xm3.transpose(1, 0, 2), conv0.transpose(1, 0, 2), *pre_w, k_scale=DH ** -0.5)
        batch_major = lambda a: jnp.pad(a.reshape(L, Bg, -1).transpose(1, 0, 2), ((0, 0), (0, lp - L), (0, 0)))
        q, k, v, xc, gates = [batch_major(a) for a in outs]
        gz_p = jnp.pad(gz3, ((0, 0), (0, lp - L), (0, 0)))
        m0b = jnp.broadcast_to(m0[:, :, None], m0.shape + (LANES,))
        y, C, n, m = _mlstm_core(q, k, v, xc, gz_p, gates, p["mlstm_norm"], p["mlstm_skip"],
                                 (C0, n0, m0b), 128, L, NH_C)
        new_conv = jnp.concatenate([conv0, xm3], axis=1)[:, -(CONV_W - 1):]
        y = y[:, :L].reshape(M, DC)
    x1 = _out_proj([y], [p["w_out_c"]], x2, p["g_mix_post1"], tm)
    x2o = _ffn(x1, p["g_ffn_pre1"], p["w_up1"], p["w_dn1"], p["g_ffn_post1"], _tile(M, 1024), 1024)
    return x2o.reshape(Bg, L, D), (new_conv, C, n, m[:, :, 0])


def kernel(x_prompt, x_sample, state_rglru_conv, state_rglru_h, cache_swa_k, cache_swa_v, state_mlstm_conv, state_mlstm_C, state_mlstm_n, state_mlstm_m, norm_mix_pre, norm_mix_post, norm_ffn_pre, norm_ffn_post, w_ffn_up, w_ffn_down, w_in_ab, conv_a_w, conv_a_b, lru_wa, lru_ba, lru_wx, lru_bx, lru_lambda, w_out_ab, w_in_c, conv_c_w, conv_c_b, mlstm_wq, mlstm_wk, mlstm_wv, mlstm_w_gate, mlstm_b_gate, mlstm_norm, mlstm_skip, w_out_c):
    B, S, D = x_prompt.shape
    Bs, Ts, _ = x_sample.shape
    DA = conv_a_w.shape[-1]
    DC = conv_c_w.shape[-1]
    hd = D // H_B
    row = lambda a: a.reshape(1, -1)
    wg = jnp.zeros((3 * DC, 2 * LANES), F32)
    wg = wg.at[:, :NH_C].set(mlstm_w_gate[0][:, :NH_C]).at[:, LANES:LANES + NH_C].set(mlstm_w_gate[0][:, NH_C:])
    bg = jnp.zeros((1, 2 * LANES), F32)
    bg = bg.at[0, :NH_C].set(mlstm_b_gate[0][:NH_C]).at[0, LANES:LANES + NH_C].set(mlstm_b_gate[0][NH_C:])
    p = {
        "g_mix_pre0": row(norm_mix_pre[0]), "g_mix_post0": row(norm_mix_post[0]),
        "g_ffn_pre0": row(norm_ffn_pre[0]), "g_ffn_post0": row(norm_ffn_post[0]),
        "g_mix_pre1": row(norm_mix_pre[1]), "g_mix_post1": row(norm_mix_post[1]),
        "g_ffn_pre1": row(norm_ffn_pre[1]), "g_ffn_post1": row(norm_ffn_post[1]),
        "w_up0": w_ffn_up[0].astype(BF16), "w_dn0": w_ffn_down[0].astype(BF16),
        "w_up1": w_ffn_up[1].astype(BF16), "w_dn1": w_ffn_down[1].astype(BF16),
        "w_in_ab": w_in_ab[0].astype(BF16),
        "conv_a_w": conv_a_w[0], "conv_a_b": row(conv_a_b[0]),
        "wa_bd": _regroup_block_diag(lru_wa[0], MXU_DIM), "lru_ba": row(lru_ba[0]),
        "wx_bd": _regroup_block_diag(lru_wx[0], MXU_DIM), "lru_bx": row(lru_bx[0]),
        "lam": row(lru_lambda[0]),
        "w_out_a": w_out_ab[0][:DA].astype(BF16), "w_out_b": w_out_ab[0][DA:].astype(BF16),
        "w_in_c": w_in_c[0].astype(BF16),
        "conv_c_w": conv_c_w[0], "conv_c_b": row(conv_c_b[0]),
        "wq_bd": _regroup_block_diag(mlstm_wq[0], MXU_DIM),
        "wk_bd": _regroup_block_diag(mlstm_wk[0], MXU_DIM),
        "wv_bd": _regroup_block_diag(mlstm_wv[0], MXU_DIM),
        "wg": wg.astype(BF16), "bg": bg,
        "mlstm_norm": row(mlstm_norm[0]), "mlstm_skip": row(mlstm_skip[0]),
        "w_out_c": w_out_c[0].astype(BF16),
    }
    dt = state_rglru_conv.dtype

    xp, st0 = _layer0(x_prompt, _rope_tables(np.arange(S), hd), p, True, None)
    xp, st1 = _layer1(xp, p, True, None)
    pos_s = PAST_LEN + (np.arange(Bs * Ts) % Ts)
    xs, ss0 = _layer0(x_sample, _rope_tables(pos_s, hd), p, False,
                      (state_rglru_conv[0], state_rglru_h[0], cache_swa_k[0], cache_swa_v[0]))
    xs, ss1 = _layer1(xs, p, False, (state_mlstm_conv[0], state_mlstm_C[0], state_mlstm_n[0], state_mlstm_m[0]))

    lead = lambda a: a[None].astype(dt)
    return (xp, xs,
            lead(st0[0]), lead(ss0[0]), lead(st0[1]), lead(ss0[1]),
            lead(st0[2]), lead(ss0[2]), lead(st0[3]), lead(ss0[3]),
            lead(st1[0]), lead(ss1[0]), lead(st1[1]), lead(ss1[1]),
            lead(st1[2]), lead(ss1[2]), lead(st1[3]), lead(ss1[3]))
```

```python
import functools
import math

import numpy as np
import jax
import jax.numpy as jnp
from jax import lax
from jax.experimental import pallas as pl
from jax.experimental.pallas import tpu as pltpu

F32 = jnp.float32
BF16 = jnp.bfloat16

NA_BLOCKS = 16
CONV_W = 4
LRU_C = 8.0
H_B = 8
ROPE_THETA = 500000.0
DILATIONS = ((128, 1), (512, 4), (2048, 16))
MAX_WINDOW = 2048
QBLOCK = 128
NH_C = 4
QKV_BLOCK = 4
PAST_LEN = 16384
EPS = 1e-6

LANES = 128
SUBLANES = 8
MXU_DIM = 256
VMEM_LIMIT = 56 * 1024 * 1024

NEG_INF = float("-inf")


def _cparams(sem):
    return pltpu.CompilerParams(dimension_semantics=sem, vmem_limit_bytes=VMEM_LIMIT)


def _tile(n, pref):
    t = min(n, pref)
    while n % t:
        t -= 1
    return t


def _rms(x, g):
    return x * lax.rsqrt(jnp.mean(x * x, axis=-1, keepdims=True) + EPS) * g


def _softplus(z):
    return jnp.maximum(z, 0.0) + jnp.log1p(jnp.exp(-jnp.abs(z)))


def _gelu_tanh(x):
    c = math.sqrt(2.0 / math.pi)
    return x * (0.5 * (1.0 + jnp.tanh(c * (x + 0.044715 * (x * x * x)))))


def _silu(x):
    return x * jax.nn.sigmoid(x)


def _rope_tile(y, c, sa, sb):
    outs = []
    for h in range(y.shape[1] // LANES):
        yh = y[:, h * LANES:(h + 1) * LANES]
        outs.append(yh * c + pltpu.roll(yh, LANES - 16, 1) * sa + pltpu.roll(yh, 16, 1) * sb)
    return jnp.concatenate(outs, axis=1)


def _norm_proj_kernel(*refs, groups, tn, rope_q, rope_k, q_scale):
    n_out = len(groups)
    x_ref, g_ref, w_ref = refs[:3]
    pos = 3
    if rope_q is not None:
        c_ref, sa_ref, sb_ref = refs[3:6]
        pos = 6
    out_refs = refs[pos:pos + n_out]
    xn = _rms(x_ref[...], g_ref[...]).astype(BF16)
    for (start, count, _, act), o_ref in zip(groups, out_refs):
        for jj in range(start, start + count):
            y = jnp.dot(xn, w_ref[:, jj * tn:(jj + 1) * tn], preferred_element_type=F32)
            if rope_q is not None and jj == rope_q:
                y = _rope_tile(y, c_ref[...], sa_ref[...], sb_ref[...]) * q_scale
            elif rope_q is not None and jj == rope_k:
                y = _rope_tile(y, c_ref[...], sa_ref[...], sb_ref[...])
            if act is not None:
                y = act(y)
            o_ref[:, (jj - start) * tn:(jj - start + 1) * tn] = y.astype(o_ref.dtype)


def _norm_proj(x, g, w, groups, tm, tn, rope=None, rope_q=None, rope_k=None, q_scale=1.0):
    M, K = x.shape
    in_specs = [
        pl.BlockSpec((tm, K), lambda i: (i, 0)),
        pl.BlockSpec((1, K), lambda i: (0, 0)),
        pl.BlockSpec(w.shape, lambda i: (0, 0)),
    ]
    args = [x, g, w]
    if rope is not None:
        assert rope[0].shape[0] % tm == 0 and M % tm == 0
        pos_blocks = rope[0].shape[0] // tm
        for t in rope:
            in_specs.append(pl.BlockSpec((tm, LANES), lambda i, pb=pos_blocks: (i % pb, 0)))
            args.append(t)
    out_specs, out_shapes = [], []
    for (_, count, dtype, _) in groups:
        out_specs.append(pl.BlockSpec((tm, count * tn), lambda i: (i, 0)))
        out_shapes.append(jax.ShapeDtypeStruct((M, count * tn), dtype))
    kern = functools.partial(_norm_proj_kernel, groups=tuple(groups), tn=tn,
                             rope_q=rope_q if rope is not None else None, rope_k=rope_k, q_scale=q_scale)
    return pl.pallas_call(
        kern,
        grid=(M // tm,),
        in_specs=in_specs,
        out_specs=out_specs,
        out_shape=out_shapes,
        compiler_params=_cparams(("parallel",)),
        name="norm_proj",
    )(*args)


def _out_proj_kernel(*refs, n_in):
    a_refs = refs[:n_in]
    w_ref, x_ref, g_ref, o_ref = refs[n_in:]
    y, row = None, 0
    for a_ref in a_refs:
        kk = a_ref.shape[1]
        d = jnp.dot(a_ref[...].astype(BF16), w_ref[row:row + kk, :].astype(BF16), preferred_element_type=F32)
        y = d if y is None else y + d
        row += kk
    o_ref[...] = x_ref[...] + _rms(y, g_ref[...])


def _out_proj(acts, w, x, g, tm):
    M, D = x.shape
    n_in = len(acts)
    assert sum(a.shape[1] for a in acts) == w.shape[0]
    in_specs = [pl.BlockSpec((tm, a.shape[1]), lambda i: (i, 0)) for a in acts]
    in_specs += [pl.BlockSpec(w.shape, lambda i: (0, 0)),
                 pl.BlockSpec((tm, D), lambda i: (i, 0)), pl.BlockSpec((1, D), lambda i: (0, 0))]
    return pl.pallas_call(
        functools.partial(_out_proj_kernel, n_in=n_in),
        grid=(M // tm,),
        in_specs=in_specs,
        out_specs=pl.BlockSpec((tm, D), lambda i: (i, 0)),
        out_shape=jax.ShapeDtypeStruct((M, D), F32),
        compiler_params=_cparams(("parallel",)),
        name="out_proj",
    )(*acts, w, x, g)


def _ffn_kernel(x_ref, g1_ref, wu_ref, wd_ref, g2_ref, o_ref, xn_ref, acc_ref):
    f = pl.program_id(1)

    @pl.when(f == 0)
    def _():
        xn_ref[...] = _rms(x_ref[...], g1_ref[...]).astype(BF16)
        acc_ref[...] = jnp.zeros_like(acc_ref)

    hid = jnp.dot(xn_ref[...], wu_ref[...].astype(BF16), preferred_element_type=F32)
    hid = jnp.square(jnp.maximum(hid, 0.0))
    acc_ref[...] += jnp.dot(hid.astype(BF16), wd_ref[...].astype(BF16), preferred_element_type=F32)

    @pl.when(f == pl.num_programs(1) - 1)
    def _():
        o_ref[...] = x_ref[...] + _rms(acc_ref[...], g2_ref[...])


def _ffn(x, g1, wu, wd, layer, g2, tm, tf):
    M, D = x.shape
    FF = wu.shape[2]
    return pl.pallas_call(
        _ffn_kernel,
        grid=(M // tm, FF // tf),
        in_specs=[
            pl.BlockSpec((tm, D), lambda i, f: (i, 0)),
            pl.BlockSpec((1, D), lambda i, f: (0, 0)),
            pl.BlockSpec((None, D, tf), lambda i, f: (layer, 0, f)),
            pl.BlockSpec((None, tf, D), lambda i, f: (layer, f, 0)),
            pl.BlockSpec((1, D), lambda i, f: (0, 0)),
        ],
        out_specs=pl.BlockSpec((tm, D), lambda i, f: (i, 0)),
        out_shape=jax.ShapeDtypeStruct((M, D), F32),
        scratch_shapes=[pltpu.VMEM((tm, D), BF16), pltpu.VMEM((tm, D), F32)],
        compiler_params=_cparams(("parallel", "arbitrary")),
        name="ffn",
    )(x, g1, wu, wd, g2)


def _lru_gates(xc, wa_ref, ba, wx_ref, bx, lam):
    xcb = xc.astype(BF16)
    ng = wa_ref.shape[0]
    gw = wa_ref.shape[1]
    ra, ri = [], []
    for g in range(ng):
        xs = xcb[:, g * gw:(g + 1) * gw]
        ra.append(jnp.dot(xs, wa_ref[g], preferred_element_type=F32))
        ri.append(jnp.dot(xs, wx_ref[g], preferred_element_type=F32))
    r = jax.nn.sigmoid(jnp.concatenate(ra, axis=1) + ba)
    i = jax.nn.sigmoid(jnp.concatenate(ri, axis=1) + bx)
    log_a = -LRU_C * r * _softplus(-lam)
    a = jnp.exp(log_a)
    om = 1.0 - a * a
    b = jnp.where(om > 0.0, om * lax.rsqrt(om), 0.0) * (i * xc)
    return a, b


def _conv4(xe_ref, x, first, w_ref, b):
    T = x.shape[0]
    if first is not None:
        @pl.when(first)
        def _():
            xe_ref[0:SUBLANES, :] = jnp.zeros((SUBLANES, x.shape[1]), x.dtype)
    xe_ref[SUBLANES:, :] = x
    y = b
    for j in range(CONV_W):
        y = y + w_ref[j:j + 1, :] * xe_ref[pl.ds(SUBLANES - (CONV_W - 1) + j, T), :]
    if T >= SUBLANES:
        xe_ref[0:SUBLANES, :] = x[T - SUBLANES:, :]
    return y


def _rglru_prompt_kernel(xa_ref, ga_ref, cw_ref, cb_ref, wa_ref, ba_ref, wx_ref, bx_ref, lam_ref,
                         ya_ref, hl_ref, xe_ref, hc_ref, a_s, b_s, h_s):
    t = pl.program_id(1)
    T = xa_ref.shape[0]

    @pl.when(t == 0)
    def _():
        hc_ref[...] = jnp.zeros_like(hc_ref)

    xc = _conv4(xe_ref, xa_ref[...], t == 0, cw_ref, cb_ref[...])
    a, b = _lru_gates(xc, wa_ref, ba_ref[...], wx_ref, bx_ref[...], lam_ref[...])
    a_s[...] = a
    b_s[...] = b

    def body(s, h):
        h = a_s[pl.ds(s, 1), :] * h + b_s[pl.ds(s, 1), :]
        h_s[pl.ds(s, 1), :] = h
        return h

    h = lax.fori_loop(0, T, body, hc_ref[...], unroll=8)
    hc_ref[...] = h
    ya_ref[...] = (h_s[...] * _gelu_tanh(ga_ref[...])).astype(ya_ref.dtype)

    @pl.when(t == pl.num_programs(1) - 1)
    def _():
        hl_ref[...] = h


def _rglru_prompt(xaga, cw, cb, wa_bd, ba, wx_bd, bx, lam, ts):
    B, S, _ = xaga.shape
    C = cw.shape[-1]
    wspec = lambda a: pl.BlockSpec(a.shape, lambda b, t: (0,) * a.ndim)
    return pl.pallas_call(
        _rglru_prompt_kernel,
        grid=(B, S // ts),
        in_specs=[
            pl.BlockSpec((None, ts, C), lambda b, t: (b, t, 0)),
            pl.BlockSpec((None, ts, C), lambda b, t: (b, t, 1)),
            wspec(cw), wspec(cb), wspec(wa_bd), wspec(ba), wspec(wx_bd), wspec(bx), wspec(lam),
        ],
        out_specs=[
            pl.BlockSpec((None, ts, C), lambda b, t: (b, t, 0)),
            pl.BlockSpec((None, 1, C), lambda b, t: (b, 0, 0)),
        ],
        out_shape=[jax.ShapeDtypeStruct((B, S, C), BF16), jax.ShapeDtypeStruct((B, 1, C), F32)],
        scratch_shapes=[pltpu.VMEM((SUBLANES + ts, C), F32), pltpu.VMEM((1, C), F32),
                        pltpu.VMEM((ts, C), F32), pltpu.VMEM((ts, C), F32), pltpu.VMEM((ts, C), F32)],
        compiler_params=_cparams(("parallel", "arbitrary")),
        name="rglru_prompt",
    )(xaga, xaga, cw, cb, wa_bd, ba, wx_bd, bx, lam)


def _rglru_sample_kernel(xa_ref, ga_ref, st_ref, h0_ref, cw_ref, cb_ref, wa_ref, ba_ref, wx_ref, bx_ref,
                         lam_ref, ya_ref, hl_ref):
    T = xa_ref.shape[0]
    rows = [st_ref[j] for j in range(CONV_W - 1)] + [xa_ref[s] for s in range(T)]
    h = h0_ref[...]
    for s in range(T):
        y = cb_ref[...]
        for j in range(CONV_W):
            y = y + cw_ref[j:j + 1, :] * rows[s + j]
        a, b = _lru_gates(y, wa_ref, ba_ref[...], wx_ref, bx_ref[...], lam_ref[...])
        h = a * h + b
        ya_ref[s] = (h * _gelu_tanh(ga_ref[s])).astype(ya_ref.dtype)
    hl_ref[...] = h


def _rglru_sample(xa_t, ga_t, st_t, h0, cw, cb, wa_bd, ba, wx_bd, bx, lam):
    T, B, C = xa_t.shape
    args = (xa_t, ga_t, st_t, h0, cw, cb, wa_bd, ba, wx_bd, bx, lam)
    return pl.pallas_call(
        _rglru_sample_kernel,
        grid=(1,),
        in_specs=[pl.BlockSpec(a.shape, lambda i, n=a.ndim: (0,) * n) for a in args],
        out_specs=[pl.BlockSpec((T, B, C), lambda i: (0, 0, 0)), pl.BlockSpec((B, C), lambda i: (0, 0))],
        out_shape=[jax.ShapeDtypeStruct((T, B, C), F32), jax.ShapeDtypeStruct((B, C), F32)],
        compiler_params=_cparams(("arbitrary",)),
        name="rglru_sample",
    )(*args)


def _attn_prompt_kernel(q_ref, k_ref, v_ref, o_ref, *scratch, groups, qb, unroll):
    S = q_ref.shape[0]
    E = q_ref.shape[1]
    dist2 = (lax.broadcasted_iota(jnp.int32, (qb, 2 * qb), 0) + qb
             - lax.broadcasted_iota(jnp.int32, (qb, 2 * qb), 1))
    dist1 = lax.broadcasted_iota(jnp.int32, (qb, qb), 0) - lax.broadcasted_iota(jnp.int32, (qb, qb), 1)

    def rows(start, d):
        return pl.ds(start, qb) if d == 1 else pl.ds(start, qb, stride=d)

    ng = len(groups)
    s_scr, p_scr = scratch[3 * ng], scratch[3 * ng + 1]
    dn_t = (((1,), (1,)), ((), ()))

    for gi, (w, d) in enumerate(groups):
        back = w // d
        nb = (S // d) // qb
        nblk = nb * d
        two = nb > 1
        kw = 2 * qb if two else qb
        dist = dist2 if two else dist1
        lim_all = jnp.full((qb, kw), back, jnp.int32)
        lim_first = jnp.minimum(lax.broadcasted_iota(jnp.int32, (qb, kw), 0), back)
        m_s, l_s, acc_s = scratch[3 * gi:3 * gi + 3]
        un = math.gcd(nblk, unroll)

        def locate(idx, d=d):
            if d == 1:
                n = idx
                start = pl.multiple_of(idx * qb, qb)
                prev = pl.multiple_of(jnp.maximum(idx - 1, 0) * qb, qb)
            else:
                n = idx // d
                start = n * (qb * d) + idx % d
                prev = jnp.where(n > 0, start - qb * d, start)
            return start, prev, n

        def gather(ref, start, prev, d=d, two=two):
            if two:
                return jnp.concatenate([ref[rows(prev, d), :], ref[rows(start, d), :]], axis=0).astype(BF16)
            return ref[rows(start, d), :].astype(BF16)

        def scores(idx, c, kw=kw, locate=locate, gather=gather, d=d):
            start, prev, _ = locate(idx)
            q = q_ref[rows(start, d), :].astype(BF16)
            s_scr[pl.ds(pl.multiple_of(idx * qb, qb), qb), 0:kw] = lax.dot_general(
                q, gather(k_ref, start, prev), dn_t, preferred_element_type=F32)
            return c

        def softmax(idx, c, kw=kw, locate=locate, d=d, two=two, dist=dist, lim_all=lim_all,
                    lim_first=lim_first, m_s=m_s, l_s=l_s):
            start, _, n = locate(idx)
            blk = pl.ds(pl.multiple_of(idx * qb, qb), qb)
            lim = jnp.where(n > 0, lim_all, lim_first) if two else lim_all
            valid = (dist >= 0) & (dist <= lim)
            s = jnp.where(valid, s_scr[blk, 0:kw], NEG_INF)
            m_b = jnp.max(s, axis=-1, keepdims=True)
            p_scr[blk, 0:kw] = jnp.exp(s - m_b).astype(BF16)
            m_s[rows(start, d), :] = jnp.broadcast_to(m_b, (qb, E))
            return c

        def values(idx, c, kw=kw, locate=locate, gather=gather, d=d, acc_s=acc_s, l_s=l_s):
            start, prev, _ = locate(idx)
            p = p_scr[pl.ds(pl.multiple_of(idx * qb, qb), qb), 0:kw]
            v1 = jnp.concatenate([gather(v_ref, start, prev), jnp.ones((kw, E), BF16)], axis=1)
            r = jnp.dot(p, v1, preferred_element_type=F32)
            acc_s[rows(start, d), :] = r[:, :E]
            l_s[rows(start, d), :] = r[:, E:]
            return c

        lax.fori_loop(0, nblk, scores, 0, unroll=un)
        lax.fori_loop(0, nblk, softmax, 0, unroll=un)
        lax.fori_loop(0, nblk, values, 0, unroll=un)

    ng = len(groups)

    def merge(i, c):
        rs = pl.ds(pl.multiple_of(i * qb, qb), qb)
        ms = [scratch[3 * g][rs, :] for g in range(ng)]
        m = functools.reduce(jnp.maximum, ms)
        num = den = None
        for g in range(ng):
            wg = jnp.exp(ms[g] - m)
            n_g = wg * scratch[3 * g + 2][rs, :]
            d_g = wg * scratch[3 * g + 1][rs, :]
            num = n_g if num is None else num + n_g
            den = d_g if den is None else den + d_g
        o_ref[rs, :] = (num / den).astype(o_ref.dtype)
        return c

    lax.fori_loop(0, S // qb, merge, 0, unroll=2)


def _attn_prompt(q, q_off, k, v, groups, qb):
    B, S, D = k.shape
    nh = D // LANES
    for (w, d) in groups:
        assert S % (d * qb) == 0 and w // d <= qb
    spec = pl.BlockSpec((None, S, LANES), lambda b, h: (b, 0, h))
    q_spec = pl.BlockSpec((None, S, LANES), lambda b, h: (b, 0, q_off + h))
    return pl.pallas_call(
        functools.partial(_attn_prompt_kernel, groups=tuple(groups), qb=qb, unroll=16),
        grid=(B, nh),
        in_specs=[q_spec, spec, spec],
        out_specs=spec,
        out_shape=jax.ShapeDtypeStruct((B, S, D), BF16),
        scratch_shapes=[pltpu.VMEM((S, LANES), F32)] * (3 * len(groups))
        + [pltpu.VMEM((S, 2 * qb), F32), pltpu.VMEM((S, 2 * qb), BF16)],
        compiler_params=_cparams(("parallel", "parallel")),
        name="attn_prompt",
    )(q, k, v)


def _attn_sample_kernel(q_ref, kn_ref, vn_ref, kt_ref, vt_ref, ks_ref, vs_ref, mt_ref, ms_ref, mn_ref, o_ref):
    nq, hd = q_ref.shape
    qb = jnp.concatenate([q_ref[...], jnp.zeros((LANES - nq, hd), F32)], axis=0).astype(BF16)
    nn = kn_ref.shape[0]
    zn = jnp.zeros((LANES - nn, hd), F32)
    key_sets = (
        (kt_ref[...], vt_ref[...], mt_ref[...]),
        (ks_ref[...].reshape(-1, hd), vs_ref[...].reshape(-1, hd), ms_ref[...]),
        (jnp.concatenate([kn_ref[...], zn], axis=0), jnp.concatenate([vn_ref[...], zn], axis=0), mn_ref[...]),
    )
    scored = []
    for k, v, mult in key_sets:
        s = lax.dot_general(k.astype(BF16), qb, (((1,), (1,)), ((), ())), preferred_element_type=F32)
        scored.append((jnp.where(mult > 0.0, s, NEG_INF), mult, v))
    m = functools.reduce(jnp.maximum, [jnp.max(s, axis=0, keepdims=True) for s, _, _ in scored])
    m = jnp.where(m == NEG_INF, 0.0, m)
    acc = None
    for s, mult, v in scored:
        p = (mult * jnp.exp(s - m)).astype(BF16)
        v1 = jnp.concatenate([v.astype(BF16), jnp.ones(v.shape, BF16)], axis=1)
        r = lax.dot_general(p, v1, (((0,), (0,)), ((), ())), preferred_element_type=F32)
        acc = r if acc is None else acc + r
    o_ref[...] = acc[:nq, :hd] / acc[:nq, hd:]


def _attn_sample(q, kn, vn, kc, vc, T, nh):
    B, nq, hd = q.shape
    wb = kc.shape[1] // nh
    far = max(DILATIONS, key=lambda wd: wd[0])
    near = [g for g in DILATIONS if g != far]
    R = max(w for w, _ in near)
    d_far = far[1]
    assert near and wb % R == 0 and wb % d_far == 0 and T <= d_far and nq == T * nh and nq <= LANES
    A = wb // d_far

    def mult(delta, groups):
        m = np.zeros(delta.shape, np.float32)
        for (w, d) in groups:
            m = m + ((delta >= 0) & (delta % d == 0) & (delta <= (w // d) * d))
        return m

    def table(pos_of_row, head_of_row, groups):
        lane = np.arange(LANES)
        t, hq = lane // nh, lane % nh
        delta = (wb + t)[None, :] - pos_of_row[:, None]
        tab = mult(delta, groups) * (head_of_row[:, None] == hq[None, :]) * (t < T)[None, :]
        return jnp.asarray(tab, F32)

    rt = np.arange(R * nh)
    mt = table(wb - R + rt // nh, rt % nh, near)
    rs = np.arange(A * T * nh)
    ms = table((rs // (T * nh)) * d_far + (rs % (T * nh)) // nh, rs % nh, [far])
    rn = np.arange(LANES)
    mn = table(np.where(rn < nq, wb + rn // nh, -10 ** 9), rn % nh, DILATIONS)

    small = pl.BlockSpec((None, nq, hd), lambda b: (b, 0, 0))
    tail = pl.BlockSpec((None, R * nh, hd), lambda b: (b, wb // R - 1, 0))
    strided = pl.BlockSpec((None, A, T * nh, hd), lambda b: (b, 0, 0, 0))
    const = lambda a: pl.BlockSpec(a.shape, lambda b: (0, 0))
    kc4 = kc.reshape(B, A, d_far * nh, hd)
    vc4 = vc.reshape(B, A, d_far * nh, hd)
    return pl.pallas_call(
        _attn_sample_kernel,
        grid=(B,),
        in_specs=[small, small, small, tail, tail, strided, strided, const(mt), const(ms), const(mn)],
        out_specs=small,
        out_shape=jax.ShapeDtypeStruct((B, nq, hd), F32),
        compiler_params=_cparams(("parallel",)),
        name="attn_sample",
    )(q, kn, vn, kc, vc, kc4, vc4, mt, ms, mn)


def _mlstm_qkv_gates(xc, xm, wq_ref, wk_ref, wv_ref, wg_ref, bg_ref, q_ref, k_ref, v_ref, xc_ref, g_ref, k_scale):
    xcb = xc.astype(BF16)
    xmb = xm.astype(BF16)
    ng, gw = wq_ref.shape[0], wq_ref.shape[1]
    qs, ks, vs = [], [], []
    for g in range(ng):
        sl = slice(g * gw, (g + 1) * gw)
        qs.append(jnp.dot(xcb[:, sl], wq_ref[g], preferred_element_type=F32))
        ks.append(jnp.dot(xcb[:, sl], wk_ref[g], preferred_element_type=F32))
        vs.append(jnp.dot(xmb[:, sl], wv_ref[g], preferred_element_type=F32))
    q = jnp.concatenate(qs, axis=1)
    k = jnp.concatenate(ks, axis=1)
    v = jnp.concatenate(vs, axis=1)
    qkv = jnp.concatenate([q, k, v], axis=1).astype(BF16)
    g_ref[...] = jnp.dot(qkv, wg_ref[...], preferred_element_type=F32) + bg_ref[...]
    q_ref[...] = q.astype(q_ref.dtype)
    k_ref[...] = (k * k_scale).astype(k_ref.dtype)
    v_ref[...] = v.astype(v_ref.dtype)
    xc_ref[...] = xcb.astype(xc_ref.dtype)


def _mlstm_pre_sample_kernel(xm_ref, st_ref, cw_ref, cb_ref, wq_ref, wk_ref, wv_ref, wg_ref, bg_ref,
                             q_ref, k_ref, v_ref, xc_ref, g_ref, *, k_scale):
    T = xm_ref.shape[0]
    rows = [st_ref[j] for j in range(CONV_W - 1)] + [xm_ref[s] for s in range(T)]
    conv = []
    for s in range(T):
        y = cb_ref[...]
        for j in range(CONV_W):
            y = y + cw_ref[j:j + 1, :] * rows[s + j]
        conv.append(y)
    xc = _silu(jnp.concatenate(conv, axis=0))
    xm = jnp.concatenate(rows[CONV_W - 1:], axis=0)
    _mlstm_qkv_gates(xc, xm, wq_ref, wk_ref, wv_ref, wg_ref, bg_ref, q_ref, k_ref, v_ref, xc_ref, g_ref, k_scale)


def _mlstm_pre_sample(xm_t, st_t, cw, cb, wq_bd, wk_bd, wv_bd, wg, bg, k_scale):
    T, B, C = xm_t.shape
    args = (xm_t, st_t, cw, cb, wq_bd, wk_bd, wv_bd, wg, bg)
    full = lambda shape: pl.BlockSpec(shape, lambda i: (0,) * len(shape))
    return pl.pallas_call(
        functools.partial(_mlstm_pre_sample_kernel, k_scale=k_scale),
        grid=(1,),
        in_specs=[full(a.shape) for a in args],
        out_specs=[full((T * B, C))] * 4 + [full((T * B, 2 * LANES))],
        out_shape=[jax.ShapeDtypeStruct((T * B, C), BF16)] * 4 + [jax.ShapeDtypeStruct((T * B, 2 * LANES), F32)],
        compiler_params=_cparams(("arbitrary",)),
        name="mlstm_pre_sample",
    )(*args)


CONV_HALO = 16


def _mlstm_in_kernel(x_ref, xh_ref, g_ref, w_ref, cw_ref, cb_ref, wq_ref, wk_ref, wv_ref, wg_ref, bg_ref,
                     q_ref, k_ref, v_ref, xc_ref, gz_ref, gt_ref, tail_ref, *, tiles_per_seq, tn, k_scale):
    tm = x_ref.shape[0]
    C = cw_ref.shape[1]
    first = (pl.program_id(0) % tiles_per_seq) == 0
    g = g_ref[...]
    xn = _rms(x_ref[...], g)
    xe = jnp.concatenate([_rms(xh_ref[...], g), xn], axis=0).astype(BF16)
    xn = xn.astype(BF16)
    dead_halo = lax.broadcasted_iota(jnp.int32, (CONV_HALO + tm, tn), 0) < jnp.where(first, CONV_HALO, 0)
    xcs, xms = [], []
    for jj in range(C // tn):
        cs = slice(jj * tn, (jj + 1) * tn)
        xm_e = jnp.dot(xe, w_ref[:, cs], preferred_element_type=F32)
        xm_e = jnp.where(dead_halo, 0.0, xm_e)
        y = cb_ref[:, cs]
        for j in range(CONV_W):
            s = CONV_HALO - (CONV_W - 1) + j
            y = y + cw_ref[j:j + 1, cs] * xm_e[s:s + tm]
        xcs.append(_silu(y))
        xms.append(xm_e[CONV_HALO:])
        gz_ref[:, cs] = _silu(jnp.dot(xn, w_ref[:, C + jj * tn:C + (jj + 1) * tn],
                                      preferred_element_type=F32)).astype(gz_ref.dtype)
    xc = jnp.concatenate(xcs, axis=1)
    xm = jnp.concatenate(xms, axis=1)
    tail_ref[...] = xm[tm - SUBLANES:, :]
    _mlstm_qkv_gates(xc, xm, wq_ref, wk_ref, wv_ref, wg_ref, bg_ref, q_ref, k_ref, v_ref, xc_ref, gt_ref, k_scale)


def _mlstm_in(x, g, w, cw, cb, wq_bd, wk_bd, wv_bd, wg, bg, seq, tm, k_scale):
    M, K = x.shape
    C = cw.shape[1]
    assert seq % tm == 0 and tm % CONV_HALO == 0
    hb = tm // CONV_HALO
    const = lambda a: pl.BlockSpec(a.shape, lambda i: (0,) * a.ndim)
    act = pl.BlockSpec((tm, C), lambda i: (i, 0))
    return pl.pallas_call(
        functools.partial(_mlstm_in_kernel, tiles_per_seq=seq // tm, tn=1024, k_scale=k_scale),
        grid=(M // tm,),
        in_specs=[pl.BlockSpec((tm, K), lambda i: (i, 0)),
                  pl.BlockSpec((CONV_HALO, K), lambda i: (jnp.maximum(i * hb - 1, 0), 0)),
                  const(g), const(w), const(cw), const(cb), const(wq_bd), const(wk_bd), const(wv_bd),
                  const(wg), const(bg)],
        out_specs=[act, act, act, act, act, pl.BlockSpec((tm, 2 * LANES), lambda i: (i, 0)),
                   pl.BlockSpec((None, SUBLANES, C), lambda i: (i, 0, 0))],
        out_shape=[jax.ShapeDtypeStruct((M, C), BF16)] * 5 + [jax.ShapeDtypeStruct((M, 2 * LANES), F32),
                                                              jax.ShapeDtypeStruct((M // tm, SUBLANES, C), F32)],
        compiler_params=_cparams(("parallel",)),
        name="mlstm_in",
    )(x, x, g, w, cw, cb, wq_bd, wk_bd, wv_bd, wg, bg)


def _mlstm_core_kernel(*refs, c, n_valid, nh, zero_init):
    q_ref, k_ref, v_ref, xc_ref, gz_ref, g_ref, nw_ref, sk_ref = refs[:8]
    pos = 8
    if not zero_init:
        c0_ref, n0_ref, m0_ref = refs[8:11]
        pos = 11
    y_ref, C_ref, n_ref, m_ref = refs[pos:pos + 4]
    ch = pl.program_id(1)
    c_in = q_ref.shape[0]
    DH = q_ref.shape[1] // nh

    @pl.when(ch == 0)
    def _():
        if zero_init:
            C_ref[...] = jnp.zeros_like(C_ref)
            n_ref[...] = jnp.zeros_like(n_ref)
            m_ref[...] = jnp.zeros_like(m_ref)
        else:
            C_ref[...] = c0_ref[...]
            n_ref[...] = n0_ref[...]
            m_ref[...] = m0_ref[...]

    def padded(x):
        if c_in == c:
            return x
        return jnp.concatenate([x, jnp.zeros((c - c_in, x.shape[1]), x.dtype)], axis=0)

    g = padded(g_ref[...])
    row = lax.broadcasted_iota(jnp.int32, (c, LANES), 0)
    valid = row < n_valid
    gi = jnp.where(valid, g[:, :LANES], NEG_INF)
    lf = jnp.where(valid, -_softplus(-g[:, LANES:]), 0.0)
    tri_f = (lax.broadcasted_iota(jnp.int32, (c, c), 1) <= lax.broadcasted_iota(jnp.int32, (c, c), 0))
    bcum = jnp.dot(tri_f.astype(F32), lf, preferred_element_type=F32, precision=lax.Precision.HIGHEST)
    bcum_t = bcum.T
    gi_t = gi.T
    causal = tri_f

    for h in range(nh):
        sl = slice(h * DH, (h + 1) * DH)
        qh = padded(q_ref[:, sl])
        kh = padded(k_ref[:, sl])
        vh = padded(v_ref[:, sl])
        bc_col = bcum[:, h:h + 1]
        bc_row = bcum_t[h:h + 1, :]
        i_row = gi_t[h:h + 1, :]
        i_col = gi[:, h:h + 1]
        m_prev = m_ref[h:h + 1, 0:1]
        log_intra = jnp.where(causal, bc_col - bc_row + i_row, NEG_INF)
        log_inter = bc_col + m_prev
        m_t = jnp.maximum(log_inter, jnp.max(log_intra, axis=1, keepdims=True))
        qk = lax.dot_general(qh, kh, (((1,), (1,)), ((), ())), preferred_element_type=F32)
        w_intra = jnp.exp(log_intra - m_t) * qk
        w_inter = jnp.exp(log_inter - m_t)
        C_old = C_ref[h]
        n_old = n_ref[h:h + 1, :]
        num = jnp.dot(w_intra.astype(BF16), vh, preferred_element_type=F32)
        num = num + w_inter * jnp.dot(qh, C_old.astype(BF16), preferred_element_type=F32)
        n8 = jnp.broadcast_to(n_old, (SUBLANES, DH)).astype(BF16)
        qn = lax.dot_general(qh, n8, (((1,), (1,)), ((), ())), preferred_element_type=F32)[:, 0:1]
        den = jnp.sum(w_intra, axis=1, keepdims=True) + w_inter * qn
        hh = num / jnp.maximum(jnp.abs(den), jnp.exp(-m_t))
        m_new = m_t[c - 1:c, :]
        bc_last = bc_col[c - 1:c, :]
        w_state = jnp.exp(bc_last - bc_col + i_col - m_new)
        decay = jnp.exp(bc_last + m_prev - m_new)
        kw = kh.astype(F32) * w_state
        C_ref[h] = decay * C_old + lax.dot_general(
            kw.astype(BF16), vh, (((0,), (0,)), ((), ())), preferred_element_type=F32)
        n_ref[h:h + 1, :] = decay * n_old + jnp.sum(kw, axis=0, keepdims=True)
        m_ref[h:h + 1, :] = jnp.broadcast_to(m_new, (1, LANES))
        hn = hh * lax.rsqrt(jnp.mean(hh * hh, axis=-1, keepdims=True) + EPS)
        hn = hn[:c_in] * nw_ref[:, sl]
        y = (hn + sk_ref[:, sl] * xc_ref[:, sl].astype(F32)) * gz_ref[:, sl].astype(F32)
        y_ref[:, sl] = y.astype(y_ref.dtype)


def _mlstm_core(q, k, v, xc, gz, gates, nw, sk, state, c, n_valid, nh):
    B, S, C = q.shape
    c_in = min(c, S)
    DH = C // nh
    zero_init = state is None
    act = pl.BlockSpec((None, c_in, C), lambda b, t: (b, t, 0))
    in_specs = [act, act, act, act, act,
                pl.BlockSpec((None, c_in, 2 * LANES), lambda b, t: (b, t, 0)),
                pl.BlockSpec((1, C), lambda b, t: (0, 0)), pl.BlockSpec((1, C), lambda b, t: (0, 0))]
    args = [q, k, v, xc, gz, gates, nw, sk]
    C_spec = pl.BlockSpec((None, nh, DH, DH), lambda b, t: (b, 0, 0, 0))
    n_spec = pl.BlockSpec((None, nh, DH), lambda b, t: (b, 0, 0))
    m_spec = pl.BlockSpec((None, nh, LANES), lambda b, t: (b, 0, 0))
    if not zero_init:
        in_specs += [C_spec, n_spec, m_spec]
        args += list(state)
    return pl.pallas_call(
        functools.partial(_mlstm_core_kernel, c=c, n_valid=n_valid, nh=nh, zero_init=zero_init),
        grid=(B, S // c_in),
        in_specs=in_specs,
        out_specs=[act, C_spec, n_spec, m_spec],
        out_shape=[jax.ShapeDtypeStruct((B, S, C), BF16),
                   jax.ShapeDtypeStruct((B, nh, DH, DH), F32),
                   jax.ShapeDtypeStruct((B, nh, DH), F32),
                   jax.ShapeDtypeStruct((B, nh, LANES), F32)],
        compiler_params=_cparams(("parallel", "arbitrary")),
        name="mlstm_core",
    )(*args)


def _rope_tables(pos, hd):
    rot = hd // 4
    half = rot // 2
    inv = ROPE_THETA ** (-np.arange(0, rot, 2, dtype=np.float64) / rot)
    ang = pos.astype(np.float64)[:, None] * inv[None, :]
    cos, sin = np.cos(ang), np.sin(ang)
    n = pos.shape[0]
    c = np.concatenate([cos, cos, np.ones((n, hd - rot))], axis=1)
    sa = np.concatenate([-sin, np.zeros((n, hd - half))], axis=1)
    sb = np.concatenate([np.zeros((n, half)), sin, np.zeros((n, hd - rot))], axis=1)
    return tuple(jnp.asarray(t, F32) for t in (c, sa, sb))


def _regroup_block_diag(w, group):
    nb, bi, bo = w.shape
    per = group // bi
    a = w.reshape(nb // per, per * bi, bo)
    cols = np.arange(per * bo)
    tile_cols = jnp.asarray(cols[None, :] % bo == np.arange(bo)[:, None], BF16)
    same_block = jnp.asarray((np.arange(per * bi)[:, None] // bi) == (cols[None, :] // bo), BF16)
    tiled = jnp.einsum("gro,oc->grc", a.astype(BF16), tile_cols, preferred_element_type=F32)
    return tiled.astype(BF16) * same_block


def _layer0(x, rope, p, prompt, state):
    Bg, L, D = x.shape
    M = Bg * L
    hd = D // H_B
    tm = _tile(M, 1024)
    tm_in = _tile(rope[0].shape[0], 512)
    x2 = x.reshape(M, D)
    xagq, k, v = _norm_proj(x2, p["g_mix_pre0"], p["w_in_ab"], ((0, 3, F32, None), (3, 1, F32, None), (4, 1, F32, None)),
                            tm_in, D, rope=rope, rope_q=2, rope_k=3, q_scale=hd ** -0.5)
    xagq3 = xagq.reshape(Bg, L, 3 * D)
    lru_w = (p["conv_a_w"], p["conv_a_b"], p["wa_bd"], p["lru_ba"], p["wx_bd"], p["lru_bx"], p["lam"])
    if prompt:
        ya, h_last = _rglru_prompt(xagq3, *lru_w, ts=_tile(L, 256))
        ya = ya.reshape(M, D)
        h_last = h_last.reshape(Bg, D)
        new_conv = xagq3[:, L - (CONV_W - 1):, :D]
        yb = _attn_prompt(xagq3, 2 * D // LANES, k.reshape(Bg, L, D), v.reshape(Bg, L, D),
                          sorted(DILATIONS, key=lambda wd: -wd[1]), QBLOCK).reshape(M, D)
        wb = min(MAX_WINDOW, L)
        new_k = k.reshape(Bg, L, H_B, hd)[:, L - wb:]
        new_v = v.reshape(Bg, L, H_B, hd)[:, L - wb:]
    else:
        conv0, h0, k_buf, v_buf = state
        xagq_t = xagq3.transpose(1, 0, 2)
        xa_t, ga_t = xagq_t[:, :, :D], xagq_t[:, :, D:2 * D]
        st_t = conv0.transpose(1, 0, 2)
        ya_t, h_last = _rglru_sample(xa_t, ga_t, st_t, h0, *lru_w)
        ya = ya_t.transpose(1, 0, 2).reshape(M, D)
        new_conv = jnp.concatenate([conv0, xagq3[:, :, :D]], axis=1)[:, -(CONV_W - 1):]
        wb = k_buf.shape[1]
        native = lambda a: a.reshape(Bg, -1, hd)
        yb = _attn_sample(native(xagq3[:, :, 2 * D:]), native(k), native(v), native(k_buf), native(v_buf),
                          L, H_B).reshape(M, D)
        new_k = k.reshape(Bg, L, H_B, hd)
        new_v = v.reshape(Bg, L, H_B, hd)
    x1 = _out_proj([ya, yb], p["w_out_ab"], x2, p["g_mix_post0"], _tile(M, 512))
    x2o = _ffn(x1, p["g_ffn_pre0"], p["w_up"], p["w_dn"], 0, p["g_ffn_post0"], _tile(M, 1024), 1024)
    return x2o.reshape(Bg, L, D), (new_conv, h_last, new_k, new_v)


def _layer1(x, p, prompt, state):
    Bg, L, D = x.shape
    M = Bg * L
    x2 = x.reshape(M, D)
    tm = _tile(M, 1024)
    DC = p["w_in_c"].shape[1] // 2
    DH = DC // NH_C
    pre_w = (p["conv_c_w"], p["conv_c_b"], p["wq_bd"], p["wk_bd"], p["wv_bd"], p["wg"], p["bg"])
    if prompt:
        c = _tile(L, 256)
        *acts, tails = _mlstm_in(x2, p["g_mix_pre1"], p["w_in_c"], *pre_w, seq=L, tm=c, k_scale=DH ** -0.5)
        q, k, v, xc, gz, gates = [a.reshape(Bg, L, -1) for a in acts]
        y, C, n, m = _mlstm_core(q, k, v, xc, gz, gates, p["mlstm_norm"], p["mlstm_skip"], None, c, c, NH_C)
        new_conv = tails.reshape(Bg, L // c, SUBLANES, DC)[:, -1, SUBLANES - (CONV_W - 1):]
        y = y.reshape(M, DC)
    else:
        nt = DC // 1024
        xm, gz = _norm_proj(x2, p["g_mix_pre1"], p["w_in_c"], ((0, nt, F32, None), (nt, nt, BF16, _silu)),
                            _tile(M, 512), 1024)
        xm3 = xm.reshape(Bg, L, DC)
        gz3 = gz.reshape(Bg, L, DC)
        conv0, C0, n0, m0 = state
        lp = 16
        outs = _mlstm_pre_sample(xm3.transpose(1, 0, 2), conv0.transpose(1, 0, 2), *pre_w, k_scale=DH ** -0.5)
        batch_major = lambda a: jnp.pad(a.reshape(L, Bg, -1).transpose(1, 0, 2), ((0, 0), (0, lp - L), (0, 0)))
        q, k, v, xc, gates = [batch_major(a) for a in outs]
        gz_p = jnp.pad(gz3, ((0, 0), (0, lp - L), (0, 0)))
        m0b = jnp.broadcast_to(m0[:, :, None], m0.shape + (LANES,))
        y, C, n, m = _mlstm_core(q, k, v, xc, gz_p, gates, p["mlstm_norm"], p["mlstm_skip"],
                                 (C0, n0, m0b), 128, L, NH_C)
        new_conv = jnp.concatenate([conv0, xm3], axis=1)[:, -(CONV_W - 1):]
        y = y[:, :L].reshape(M, DC)
    x1 = _out_proj([y], p["w_out_c"], x2, p["g_mix_post1"], _tile(M, 512))
    x2o = _ffn(x1, p["g_ffn_pre1"], p["w_up"], p["w_dn"], 1, p["g_ffn_post1"], _tile(M, 1024), 1024)
    return x2o.reshape(Bg, L, D), (new_conv, C, n, m[:, :, 0])


def kernel(x_prompt, x_sample, state_rglru_conv, state_rglru_h, cache_swa_k, cache_swa_v, state_mlstm_conv, state_mlstm_C, state_mlstm_n, state_mlstm_m, norm_mix_pre, norm_mix_post, norm_ffn_pre, norm_ffn_post, w_ffn_up, w_ffn_down, w_in_ab, conv_a_w, conv_a_b, lru_wa, lru_ba, lru_wx, lru_bx, lru_lambda, w_out_ab, w_in_c, conv_c_w, conv_c_b, mlstm_wq, mlstm_wk, mlstm_wv, mlstm_w_gate, mlstm_b_gate, mlstm_norm, mlstm_skip, w_out_c):
    B, S, D = x_prompt.shape
    Bs, Ts, _ = x_sample.shape
    DA = conv_a_w.shape[-1]
    DC = conv_c_w.shape[-1]
    hd = D // H_B
    row = lambda a: a.reshape(1, -1)
    lane_pad = lambda a: jnp.pad(a, ((0, 0), (0, LANES - NH_C)))
    wg = jnp.concatenate([lane_pad(mlstm_w_gate[0][:, :NH_C]), lane_pad(mlstm_w_gate[0][:, NH_C:])], axis=1)
    bg = jnp.concatenate([lane_pad(mlstm_b_gate[:, :NH_C]), lane_pad(mlstm_b_gate[:, NH_C:])], axis=1)
    p = {
        "g_mix_pre0": row(norm_mix_pre[0]), "g_mix_post0": row(norm_mix_post[0]),
        "g_ffn_pre0": row(norm_ffn_pre[0]), "g_ffn_post0": row(norm_ffn_post[0]),
        "g_mix_pre1": row(norm_mix_pre[1]), "g_mix_post1": row(norm_mix_post[1]),
        "g_ffn_pre1": row(norm_ffn_pre[1]), "g_ffn_post1": row(norm_ffn_post[1]),
        "w_up": w_ffn_up, "w_dn": w_ffn_down,
        "w_in_ab": w_in_ab[0].astype(BF16),
        "conv_a_w": conv_a_w[0], "conv_a_b": row(conv_a_b[0]),
        "wa_bd": _regroup_block_diag(lru_wa[0], MXU_DIM), "lru_ba": row(lru_ba[0]),
        "wx_bd": _regroup_block_diag(lru_wx[0], MXU_DIM), "lru_bx": row(lru_bx[0]),
        "lam": row(lru_lambda[0]),
        "w_out_ab": w_out_ab[0],
        "w_in_c": w_in_c[0].astype(BF16),
        "conv_c_w": conv_c_w[0], "conv_c_b": row(conv_c_b[0]),
        "wq_bd": _regroup_block_diag(mlstm_wq[0], MXU_DIM),
        "wk_bd": _regroup_block_diag(mlstm_wk[0], MXU_DIM),
        "wv_bd": _regroup_block_diag(mlstm_wv[0], MXU_DIM),
        "wg": wg.astype(BF16), "bg": bg,
        "mlstm_norm": row(mlstm_norm[0]), "mlstm_skip": row(mlstm_skip[0]),
        "w_out_c": w_out_c[0],
    }
    dt = state_rglru_conv.dtype

    xp, st0 = _layer0(x_prompt, _rope_tables(np.arange(S), hd), p, True, None)
    xp, st1 = _layer1(xp, p, True, None)
    pos_s = PAST_LEN + (np.arange(Bs * Ts) % Ts)
    xs, ss0 = _layer0(x_sample, _rope_tables(pos_s, hd), p, False,
                      (state_rglru_conv[0], state_rglru_h[0], cache_swa_k[0], cache_swa_v[0]))
    xs, ss1 = _layer1(xs, p, False, (state_mlstm_conv[0], state_mlstm_C[0], state_mlstm_n[0], state_mlstm_m[0]))

    lead = lambda a: a[None].astype(dt)
    return (xp, xs,
            lead(st0[0]), lead(ss0[0]), lead(st0[1]), lead(ss0[1]),
            lead(st0[2]), lead(ss0[2]), lead(st0[3]), lead(ss0[3]),
            lead(st1[0]), lead(ss1[0]), lead(st1[1]), lead(ss1[1]),
            lead(st1[2]), lead(ss1[2]), lead(st1[3]), lead(ss1[3]))
```

```python
import functools
import math

import numpy as np
import jax
import jax.numpy as jnp
from jax import lax
from jax.experimental import pallas as pl
from jax.experimental.pallas import tpu as pltpu

F32 = jnp.float32
BF16 = jnp.bfloat16

NA_BLOCKS = 16
CONV_W = 4
LRU_C = 8.0
H_B = 8
ROPE_THETA = 500000.0
DILATIONS = ((128, 1), (512, 4), (2048, 16))
MAX_WINDOW = 2048
QBLOCK = 128
NH_C = 4
QKV_BLOCK = 4
PAST_LEN = 16384
EPS = 1e-6

LANES = 128
SUBLANES = 8
MXU_DIM = 256
VMEM_LIMIT = 56 * 1024 * 1024

NEG_INF = float("-inf")


def _cparams(sem):
    return pltpu.CompilerParams(dimension_semantics=sem, vmem_limit_bytes=VMEM_LIMIT)


def _tile(n, pref):
    t = min(n, pref)
    while n % t:
        t -= 1
    return t


def _rms(x, g):
    return x * lax.rsqrt(jnp.mean(x * x, axis=-1, keepdims=True) + EPS) * g


def _softplus(z):
    return jnp.maximum(z, 0.0) + jnp.log1p(jnp.exp(-jnp.abs(z)))


def _gelu_tanh(x):
    c = math.sqrt(2.0 / math.pi)
    return x * (0.5 * (1.0 + jnp.tanh(c * (x + 0.044715 * (x * x * x)))))


def _silu(x):
    return x * jax.nn.sigmoid(x)


def _rope_tile(y, c, sa, sb):
    outs = []
    for h in range(y.shape[1] // LANES):
        yh = y[:, h * LANES:(h + 1) * LANES]
        outs.append(yh * c + pltpu.roll(yh, LANES - 16, 1) * sa + pltpu.roll(yh, 16, 1) * sb)
    return jnp.concatenate(outs, axis=1)


def _norm_proj_kernel(*refs, groups, tn, rope_q, rope_k, q_scale):
    n_out = len(groups)
    x_ref, g_ref, w_ref = refs[:3]
    pos = 3
    if rope_q is not None:
        c_ref, sa_ref, sb_ref = refs[3:6]
        pos = 6
    out_refs = refs[pos:pos + n_out]
    xn = _rms(x_ref[...], g_ref[...]).astype(BF16)
    for (start, count, _, act), o_ref in zip(groups, out_refs):
        for jj in range(start, start + count):
            y = jnp.dot(xn, w_ref[:, jj * tn:(jj + 1) * tn], preferred_element_type=F32)
            if rope_q is not None and jj == rope_q:
                y = _rope_tile(y, c_ref[...], sa_ref[...], sb_ref[...]) * q_scale
            elif rope_q is not None and jj == rope_k:
                y = _rope_tile(y, c_ref[...], sa_ref[...], sb_ref[...])
            if act is not None:
                y = act(y)
            o_ref[:, (jj - start) * tn:(jj - start + 1) * tn] = y.astype(o_ref.dtype)


def _norm_proj(x, g, w, groups, tm, tn, rope=None, rope_q=None, rope_k=None, q_scale=1.0):
    M, K = x.shape
    in_specs = [
        pl.BlockSpec((tm, K), lambda i: (i, 0)),
        pl.BlockSpec((1, K), lambda i: (0, 0)),
        pl.BlockSpec(w.shape, lambda i: (0, 0)),
    ]
    args = [x, g, w]
    if rope is not None:
        assert rope[0].shape[0] % tm == 0 and M % tm == 0
        pos_blocks = rope[0].shape[0] // tm
        for t in rope:
            in_specs.append(pl.BlockSpec((tm, LANES), lambda i, pb=pos_blocks: (i % pb, 0)))
            args.append(t)
    out_specs, out_shapes = [], []
    for (_, count, dtype, _) in groups:
        out_specs.append(pl.BlockSpec((tm, count * tn), lambda i: (i, 0)))
        out_shapes.append(jax.ShapeDtypeStruct((M, count * tn), dtype))
    kern = functools.partial(_norm_proj_kernel, groups=tuple(groups), tn=tn,
                             rope_q=rope_q if rope is not None else None, rope_k=rope_k, q_scale=q_scale)
    return pl.pallas_call(
        kern,
        grid=(M // tm,),
        in_specs=in_specs,
        out_specs=out_specs,
        out_shape=out_shapes,
        compiler_params=_cparams(("parallel",)),
        name="norm_proj",
    )(*args)


def _out_proj_kernel(*refs, n_in):
    a_refs = refs[:n_in]
    w_ref, x_ref, g_ref, gn_ref, o_ref, on_ref = refs[n_in:]
    y, row = None, 0
    for a_ref in a_refs:
        kk = a_ref.shape[1]
        d = jnp.dot(a_ref[...].astype(BF16), w_ref[row:row + kk, :].astype(BF16), preferred_element_type=F32)
        y = d if y is None else y + d
        row += kk
    x1 = x_ref[...] + _rms(y, g_ref[...])
    o_ref[...] = x1
    on_ref[...] = _rms(x1, gn_ref[...]).astype(on_ref.dtype)


def _out_proj(acts, w, x, g, g_next, tm):
    M, D = x.shape
    n_in = len(acts)
    assert sum(a.shape[1] for a in acts) == w.shape[0]
    row_tile = pl.BlockSpec((tm, D), lambda i: (i, 0))
    gain = pl.BlockSpec((1, D), lambda i: (0, 0))
    in_specs = [pl.BlockSpec((tm, a.shape[1]), lambda i: (i, 0)) for a in acts]
    in_specs += [pl.BlockSpec(w.shape, lambda i: (0, 0)), row_tile, gain, gain]
    return pl.pallas_call(
        functools.partial(_out_proj_kernel, n_in=n_in),
        grid=(M // tm,),
        in_specs=in_specs,
        out_specs=[row_tile, row_tile],
        out_shape=[jax.ShapeDtypeStruct((M, D), F32), jax.ShapeDtypeStruct((M, D), BF16)],
        compiler_params=_cparams(("parallel",)),
        name="out_proj",
    )(*acts, w, x, g, g_next)


def _ffn_kernel(x_ref, xn_ref, wu_ref, wd_ref, g2_ref, o_ref, acc_ref):
    f = pl.program_id(1)

    @pl.when(f == 0)
    def _():
        acc_ref[...] = jnp.zeros_like(acc_ref)

    hid = jnp.dot(xn_ref[...], wu_ref[...].astype(BF16), preferred_element_type=F32)
    hid = jnp.square(jnp.maximum(hid, 0.0))
    acc_ref[...] += jnp.dot(hid.astype(BF16), wd_ref[...].astype(BF16), preferred_element_type=F32)

    @pl.when(f == pl.num_programs(1) - 1)
    def _():
        o_ref[...] = x_ref[...] + _rms(acc_ref[...], g2_ref[...])


def _ffn(x, xn, wu, wd, layer, g2, tm, tf):
    M, D = x.shape
    FF = wu.shape[2]
    return pl.pallas_call(
        _ffn_kernel,
        grid=(M // tm, FF // tf),
        in_specs=[
            pl.BlockSpec((tm, D), lambda i, f: (i, 0)),
            pl.BlockSpec((tm, D), lambda i, f: (i, 0)),
            pl.BlockSpec((None, D, tf), lambda i, f: (layer, 0, f)),
            pl.BlockSpec((None, tf, D), lambda i, f: (layer, f, 0)),
            pl.BlockSpec((1, D), lambda i, f: (0, 0)),
        ],
        out_specs=pl.BlockSpec((tm, D), lambda i, f: (i, 0)),
        out_shape=jax.ShapeDtypeStruct((M, D), F32),
        scratch_shapes=[pltpu.VMEM((tm, D), F32)],
        compiler_params=_cparams(("parallel", "arbitrary")),
        name="ffn",
    )(x, xn, wu, wd, g2)


def _lru_gates(xc, wa_ref, ba, wx_ref, bx, lam):
    xcb = xc.astype(BF16)
    ng = wa_ref.shape[0]
    gw = wa_ref.shape[1]
    ra, ri = [], []
    for g in range(ng):
        xs = xcb[:, g * gw:(g + 1) * gw]
        ra.append(jnp.dot(xs, wa_ref[g], preferred_element_type=F32))
        ri.append(jnp.dot(xs, wx_ref[g], preferred_element_type=F32))
    r = jax.nn.sigmoid(jnp.concatenate(ra, axis=1) + ba)
    i = jax.nn.sigmoid(jnp.concatenate(ri, axis=1) + bx)
    log_a = -LRU_C * r * _softplus(-lam)
    a = jnp.exp(log_a)
    om = 1.0 - a * a
    b = jnp.where(om > 0.0, om * lax.rsqrt(om), 0.0) * (i * xc)
    return a, b


def _conv4_rows(xe, halo, w, b):
    a = w[0:1, :] * xe
    for j in range(1, CONV_W):
        a = pltpu.roll(a, 1, 0) + w[j:j + 1, :] * xe
    return b + a[halo:]


def _conv4(tail_ref, x, first, w_ref, b):
    @pl.when(first)
    def _():
        tail_ref[...] = jnp.zeros_like(tail_ref)

    xe = jnp.concatenate([tail_ref[...], x], axis=0)
    tail_ref[...] = x[x.shape[0] - SUBLANES:, :]
    return _conv4_rows(xe, SUBLANES, w_ref[...], b)


def _rglru_prompt_kernel(xa_ref, ga_ref, cw_ref, cb_ref, wa_ref, ba_ref, wx_ref, bx_ref, lam_ref,
                         ya_ref, hl_ref, xe_ref, hc_ref, a_s, b_s, h_s):
    t = pl.program_id(1)
    T = xa_ref.shape[0]

    @pl.when(t == 0)
    def _():
        hc_ref[...] = jnp.zeros_like(hc_ref)

    xc = _conv4(xe_ref, xa_ref[...], t == 0, cw_ref, cb_ref[...])
    a, b = _lru_gates(xc, wa_ref, ba_ref[...], wx_ref, bx_ref[...], lam_ref[...])
    a_s[...] = a
    b_s[...] = b

    def body(s, h):
        h = a_s[pl.ds(s, 1), :] * h + b_s[pl.ds(s, 1), :]
        h_s[pl.ds(s, 1), :] = h
        return h

    h = lax.fori_loop(0, T, body, hc_ref[...], unroll=8)
    hc_ref[...] = h
    ya_ref[...] = (h_s[...] * _gelu_tanh(ga_ref[...])).astype(ya_ref.dtype)

    @pl.when(t == pl.num_programs(1) - 1)
    def _():
        hl_ref[...] = h


def _rglru_prompt(xaga, cw, cb, wa_bd, ba, wx_bd, bx, lam, ts):
    B, S, _ = xaga.shape
    C = cw.shape[-1]
    wspec = lambda a: pl.BlockSpec(a.shape, lambda b, t: (0,) * a.ndim)
    return pl.pallas_call(
        _rglru_prompt_kernel,
        grid=(B, S // ts),
        in_specs=[
            pl.BlockSpec((None, ts, C), lambda b, t: (b, t, 0)),
            pl.BlockSpec((None, ts, C), lambda b, t: (b, t, 1)),
            wspec(cw), wspec(cb), wspec(wa_bd), wspec(ba), wspec(wx_bd), wspec(bx), wspec(lam),
        ],
        out_specs=[
            pl.BlockSpec((None, ts, C), lambda b, t: (b, t, 0)),
            pl.BlockSpec((None, 1, C), lambda b, t: (b, 0, 0)),
        ],
        out_shape=[jax.ShapeDtypeStruct((B, S, C), BF16), jax.ShapeDtypeStruct((B, 1, C), F32)],
        scratch_shapes=[pltpu.VMEM((SUBLANES, C), F32), pltpu.VMEM((1, C), F32),
                        pltpu.VMEM((ts, C), F32), pltpu.VMEM((ts, C), F32), pltpu.VMEM((ts, C), F32)],
        compiler_params=_cparams(("parallel", "arbitrary")),
        name="rglru_prompt",
    )(xaga, xaga, cw, cb, wa_bd, ba, wx_bd, bx, lam)


def _rglru_sample_kernel(xa_ref, ga_ref, st_ref, h0_ref, cw_ref, cb_ref, wa_ref, ba_ref, wx_ref, bx_ref,
                         lam_ref, ya_ref, hl_ref):
    T = xa_ref.shape[0]
    rows = [st_ref[j] for j in range(CONV_W - 1)] + [xa_ref[s] for s in range(T)]
    h = h0_ref[...]
    for s in range(T):
        y = cb_ref[...]
        for j in range(CONV_W):
            y = y + cw_ref[j:j + 1, :] * rows[s + j]
        a, b = _lru_gates(y, wa_ref, ba_ref[...], wx_ref, bx_ref[...], lam_ref[...])
        h = a * h + b
        ya_ref[s] = (h * _gelu_tanh(ga_ref[s])).astype(ya_ref.dtype)
    hl_ref[...] = h


def _rglru_sample(xa_t, ga_t, st_t, h0, cw, cb, wa_bd, ba, wx_bd, bx, lam):
    T, B, C = xa_t.shape
    args = (xa_t, ga_t, st_t, h0, cw, cb, wa_bd, ba, wx_bd, bx, lam)
    return pl.pallas_call(
        _rglru_sample_kernel,
        grid=(1,),
        in_specs=[pl.BlockSpec(a.shape, lambda i, n=a.ndim: (0,) * n) for a in args],
        out_specs=[pl.BlockSpec((T, B, C), lambda i: (0, 0, 0)), pl.BlockSpec((B, C), lambda i: (0, 0))],
        out_shape=[jax.ShapeDtypeStruct((T, B, C), F32), jax.ShapeDtypeStruct((B, C), F32)],
        compiler_params=_cparams(("arbitrary",)),
        name="rglru_sample",
    )(*args)


def _attn_prompt_kernel(q_ref, k_ref, v_ref, o_ref, *scratch, groups, qb, unroll):
    S = q_ref.shape[0]
    E = q_ref.shape[1]
    dist2 = (lax.broadcasted_iota(jnp.int32, (qb, 2 * qb), 0) + qb
             - lax.broadcasted_iota(jnp.int32, (qb, 2 * qb), 1))
    dist1 = lax.broadcasted_iota(jnp.int32, (qb, qb), 0) - lax.broadcasted_iota(jnp.int32, (qb, qb), 1)

    def rows(start, d):
        return pl.ds(start, qb) if d == 1 else pl.ds(start, qb, stride=d)

    ng = len(groups)
    s_scr, p_scr = scratch[3 * ng], scratch[3 * ng + 1]
    dn_t = (((1,), (1,)), ((), ()))

    for gi, (w, d) in enumerate(groups):
        back = w // d
        nb = (S // d) // qb
        nblk = nb * d
        two = nb > 1
        kw = 2 * qb if two else qb
        dist = dist2 if two else dist1
        lim_all = jnp.full((qb, kw), back, jnp.int32)
        lim_first = jnp.minimum(lax.broadcasted_iota(jnp.int32, (qb, kw), 0), back)
        m_s, l_s, acc_s = scratch[3 * gi:3 * gi + 3]
        un = math.gcd(nblk, unroll)

        def locate(idx, d=d):
            if d == 1:
                n = idx
                start = pl.multiple_of(idx * qb, qb)
                prev = pl.multiple_of(jnp.maximum(idx - 1, 0) * qb, qb)
            else:
                n = idx // d
                start = n * (qb * d) + idx % d
                prev = jnp.where(n > 0, start - qb * d, start)
            return start, prev, n

        def gather(ref, start, prev, d=d, two=two):
            if two:
                return jnp.concatenate([ref[rows(prev, d), :], ref[rows(start, d), :]], axis=0).astype(BF16)
            return ref[rows(start, d), :].astype(BF16)

        def scores(idx, c, kw=kw, locate=locate, gather=gather, d=d):
            start, prev, _ = locate(idx)
            q = q_ref[rows(start, d), :].astype(BF16)
            s_scr[pl.ds(pl.multiple_of(idx * qb, qb), qb), 0:kw] = lax.dot_general(
                q, gather(k_ref, start, prev), dn_t, preferred_element_type=F32)
            return c

        def softmax(idx, c, kw=kw, locate=locate, d=d, two=two, dist=dist, lim_all=lim_all,
                    lim_first=lim_first, m_s=m_s, l_s=l_s):
            start, _, n = locate(idx)
            blk = pl.ds(pl.multiple_of(idx * qb, qb), qb)
            lim = jnp.where(n > 0, lim_all, lim_first) if two else lim_all
            valid = (dist >= 0) & (dist <= lim)
            s = jnp.where(valid, s_scr[blk, 0:kw], NEG_INF)
            m_b = jnp.max(s, axis=-1, keepdims=True)
            p_scr[blk, 0:kw] = jnp.exp(s - m_b).astype(BF16)
            m_s[rows(start, d), :] = jnp.broadcast_to(m_b, (qb, E))
            return c

        def values(idx, c, kw=kw, locate=locate, gather=gather, d=d, acc_s=acc_s, l_s=l_s):
            start, prev, _ = locate(idx)
            p = p_scr[pl.ds(pl.multiple_of(idx * qb, qb), qb), 0:kw]
            v1 = jnp.concatenate([gather(v_ref, start, prev), jnp.ones((kw, E), BF16)], axis=1)
            r = jnp.dot(p, v1, preferred_element_type=F32)
            acc_s[rows(start, d), :] = r[:, :E]
            l_s[rows(start, d), :] = r[:, E:]
            return c

        lax.fori_loop(0, nblk, scores, 0, unroll=un)
        lax.fori_loop(0, nblk, softmax, 0, unroll=un)
        lax.fori_loop(0, nblk, values, 0, unroll=un)

    ng = len(groups)

    def merge(i, c):
        rs = pl.ds(pl.multiple_of(i * qb, qb), qb)
        ms = [scratch[3 * g][rs, :] for g in range(ng)]
        m = functools.reduce(jnp.maximum, ms)
        num = den = None
        for g in range(ng):
            wg = jnp.exp(ms[g] - m)
            n_g = wg * scratch[3 * g + 2][rs, :]
            d_g = wg * scratch[3 * g + 1][rs, :]
            num = n_g if num is None else num + n_g
            den = d_g if den is None else den + d_g
        o_ref[rs, :] = (num / den).astype(o_ref.dtype)
        return c

    lax.fori_loop(0, S // qb, merge, 0, unroll=2)


def _attn_prompt(q, q_off, k, v, groups, qb):
    B, S, D = k.shape
    nh = D // LANES
    for (w, d) in groups:
        assert S % (d * qb) == 0 and w // d <= qb
    spec = pl.BlockSpec((None, S, LANES), lambda b, h: (b, 0, h))
    q_spec = pl.BlockSpec((None, S, LANES), lambda b, h: (b, 0, q_off + h))
    return pl.pallas_call(
        functools.partial(_attn_prompt_kernel, groups=tuple(groups), qb=qb, unroll=16),
        grid=(B, nh),
        in_specs=[q_spec, spec, spec],
        out_specs=spec,
        out_shape=jax.ShapeDtypeStruct((B, S, D), BF16),
        scratch_shapes=[pltpu.VMEM((S, LANES), F32)] * (3 * len(groups))
        + [pltpu.VMEM((S, 2 * qb), F32), pltpu.VMEM((S, 2 * qb), BF16)],
        compiler_params=_cparams(("parallel", "parallel")),
        name="attn_prompt",
    )(q, k, v)


def _attn_sample_kernel(q_ref, kn_ref, vn_ref, kt_ref, vt_ref, ks_ref, vs_ref, mt_ref, ms_ref, mn_ref, o_ref):
    nq, hd = q_ref.shape
    qb = jnp.concatenate([q_ref[...], jnp.zeros((LANES - nq, hd), F32)], axis=0).astype(BF16)
    nn = kn_ref.shape[0]
    zn = jnp.zeros((LANES - nn, hd), F32)
    key_sets = (
        (kt_ref[...], vt_ref[...], mt_ref[...]),
        (ks_ref[...].reshape(-1, hd), vs_ref[...].reshape(-1, hd), ms_ref[...]),
        (jnp.concatenate([kn_ref[...], zn], axis=0), jnp.concatenate([vn_ref[...], zn], axis=0), mn_ref[...]),
    )
    scored = []
    for k, v, mult in key_sets:
        s = lax.dot_general(k.astype(BF16), qb, (((1,), (1,)), ((), ())), preferred_element_type=F32)
        scored.append((jnp.where(mult > 0.0, s, NEG_INF), mult, v))
    m = functools.reduce(jnp.maximum, [jnp.max(s, axis=0, keepdims=True) for s, _, _ in scored])
    m = jnp.where(m == NEG_INF, 0.0, m)
    acc = None
    for s, mult, v in scored:
        p = (mult * jnp.exp(s - m)).astype(BF16)
        v1 = jnp.concatenate([v.astype(BF16), jnp.ones(v.shape, BF16)], axis=1)
        r = lax.dot_general(p, v1, (((0,), (0,)), ((), ())), preferred_element_type=F32)
        acc = r if acc is None else acc + r
    o_ref[...] = acc[:nq, :hd] / acc[:nq, hd:]


def _attn_sample(q, kn, vn, kc, vc, T, nh):
    B, nq, hd = q.shape
    wb = kc.shape[1] // nh
    far = max(DILATIONS, key=lambda wd: wd[0])
    near = [g for g in DILATIONS if g != far]
    R = max(w for w, _ in near)
    d_far = far[1]
    assert near and wb % R == 0 and wb % d_far == 0 and T <= d_far and nq == T * nh and nq <= LANES
    A = wb // d_far

    def mult(delta, groups):
        m = np.zeros(delta.shape, np.float32)
        for (w, d) in groups:
            m = m + ((delta >= 0) & (delta % d == 0) & (delta <= (w // d) * d))
        return m

    def table(pos_of_row, head_of_row, groups):
        lane = np.arange(LANES)
        t, hq = lane // nh, lane % nh
        delta = (wb + t)[None, :] - pos_of_row[:, None]
        tab = mult(delta, groups) * (head_of_row[:, None] == hq[None, :]) * (t < T)[None, :]
        return jnp.asarray(tab, F32)

    rt = np.arange(R * nh)
    mt = table(wb - R + rt // nh, rt % nh, near)
    rs = np.arange(A * T * nh)
    ms = table((rs // (T * nh)) * d_far + (rs % (T * nh)) // nh, rs % nh, [far])
    rn = np.arange(LANES)
    mn = table(np.where(rn < nq, wb + rn // nh, -10 ** 9), rn % nh, DILATIONS)

    small = pl.BlockSpec((None, nq, hd), lambda b: (b, 0, 0))
    tail = pl.BlockSpec((None, R * nh, hd), lambda b: (b, wb // R - 1, 0))
    strided = pl.BlockSpec((None, A, T * nh, hd), lambda b: (b, 0, 0, 0))
    const = lambda a: pl.BlockSpec(a.shape, lambda b: (0, 0))
    kc4 = kc.reshape(B, A, d_far * nh, hd)
    vc4 = vc.reshape(B, A, d_far * nh, hd)
    return pl.pallas_call(
        _attn_sample_kernel,
        grid=(B,),
        in_specs=[small, small, small, tail, tail, strided, strided, const(mt), const(ms), const(mn)],
        out_specs=small,
        out_shape=jax.ShapeDtypeStruct((B, nq, hd), F32),
        compiler_params=_cparams(("parallel",)),
        name="attn_sample",
    )(q, kn, vn, kc, vc, kc4, vc4, mt, ms, mn)


def _mlstm_qkv_gates(xc, xm, wq_ref, wk_ref, wv_ref, wg_ref, bg_ref, q_ref, k_ref, v_ref, xc_ref, g_ref, k_scale):
    xcb = xc.astype(BF16)
    xmb = xm.astype(BF16)
    ng, gw = wq_ref.shape[0], wq_ref.shape[1]
    qs, ks, vs = [], [], []
    for g in range(ng):
        sl = slice(g * gw, (g + 1) * gw)
        qs.append(jnp.dot(xcb[:, sl], wq_ref[g], preferred_element_type=F32))
        ks.append(jnp.dot(xcb[:, sl], wk_ref[g], preferred_element_type=F32))
        vs.append(jnp.dot(xmb[:, sl], wv_ref[g], preferred_element_type=F32))
    q = jnp.concatenate(qs, axis=1)
    k = jnp.concatenate(ks, axis=1)
    v = jnp.concatenate(vs, axis=1)
    qkv = jnp.concatenate([q, k, v], axis=1).astype(BF16)
    g_ref[...] = jnp.dot(qkv, wg_ref[...], preferred_element_type=F32) + bg_ref[...]
    q_ref[...] = q.astype(q_ref.dtype)
    k_ref[...] = (k * k_scale).astype(k_ref.dtype)
    v_ref[...] = v.astype(v_ref.dtype)
    xc_ref[...] = xcb.astype(xc_ref.dtype)


def _mlstm_pre_sample_kernel(xm_ref, st_ref, cw_ref, cb_ref, wq_ref, wk_ref, wv_ref, wg_ref, bg_ref,
                             q_ref, k_ref, v_ref, xc_ref, g_ref, *, k_scale):
    T = xm_ref.shape[0]
    rows = [st_ref[j] for j in range(CONV_W - 1)] + [xm_ref[s] for s in range(T)]
    conv = []
    for s in range(T):
        y = cb_ref[...]
        for j in range(CONV_W):
            y = y + cw_ref[j:j + 1, :] * rows[s + j]
        conv.append(y)
    xc = _silu(jnp.concatenate(conv, axis=0))
    xm = jnp.concatenate(rows[CONV_W - 1:], axis=0)
    _mlstm_qkv_gates(xc, xm, wq_ref, wk_ref, wv_ref, wg_ref, bg_ref, q_ref, k_ref, v_ref, xc_ref, g_ref, k_scale)


def _mlstm_pre_sample(xm_t, st_t, cw, cb, wq_bd, wk_bd, wv_bd, wg, bg, k_scale):
    T, B, C = xm_t.shape
    args = (xm_t, st_t, cw, cb, wq_bd, wk_bd, wv_bd, wg, bg)
    full = lambda shape: pl.BlockSpec(shape, lambda i: (0,) * len(shape))
    return pl.pallas_call(
        functools.partial(_mlstm_pre_sample_kernel, k_scale=k_scale),
        grid=(1,),
        in_specs=[full(a.shape) for a in args],
        out_specs=[full((T * B, C))] * 4 + [full((T * B, 2 * LANES))],
        out_shape=[jax.ShapeDtypeStruct((T * B, C), BF16)] * 4 + [jax.ShapeDtypeStruct((T * B, 2 * LANES), F32)],
        compiler_params=_cparams(("arbitrary",)),
        name="mlstm_pre_sample",
    )(*args)


CONV_HALO = 16


def _mlstm_in_kernel(x_ref, xh_ref, g_ref, w_ref, cw_ref, cb_ref, wq_ref, wk_ref, wv_ref, wg_ref, bg_ref,
                     q_ref, k_ref, v_ref, xc_ref, gz_ref, gt_ref, tail_ref, *, tiles_per_seq, tn, k_scale):
    tm = x_ref.shape[0]
    C = cw_ref.shape[1]
    first = (pl.program_id(0) % tiles_per_seq) == 0
    g = g_ref[...]
    xn = _rms(x_ref[...], g)
    xe = jnp.concatenate([_rms(xh_ref[...], g), xn], axis=0).astype(BF16)
    xn = xn.astype(BF16)
    dead_halo = lax.broadcasted_iota(jnp.int32, (CONV_HALO + tm, tn), 0) < jnp.where(first, CONV_HALO, 0)
    xcs, xms = [], []
    for jj in range(C // tn):
        cs = slice(jj * tn, (jj + 1) * tn)
        xm_e = jnp.dot(xe, w_ref[:, cs], preferred_element_type=F32)
        xm_e = jnp.where(dead_halo, 0.0, xm_e)
        xcs.append(_silu(_conv4_rows(xm_e, CONV_HALO, cw_ref[:, cs], cb_ref[:, cs])))
        xms.append(xm_e[CONV_HALO:])
        gz_ref[:, cs] = _silu(jnp.dot(xn, w_ref[:, C + jj * tn:C + (jj + 1) * tn],
                                      preferred_element_type=F32)).astype(gz_ref.dtype)
    xc = jnp.concatenate(xcs, axis=1)
    xm = jnp.concatenate(xms, axis=1)
    tail_ref[...] = xm[tm - SUBLANES:, :]
    _mlstm_qkv_gates(xc, xm, wq_ref, wk_ref, wv_ref, wg_ref, bg_ref, q_ref, k_ref, v_ref, xc_ref, gt_ref, k_scale)


def _mlstm_in(x, g, w, cw, cb, wq_bd, wk_bd, wv_bd, wg, bg, seq, tm, k_scale):
    M, K = x.shape
    C = cw.shape[1]
    assert seq % tm == 0 and tm % CONV_HALO == 0
    hb = tm // CONV_HALO
    const = lambda a: pl.BlockSpec(a.shape, lambda i: (0,) * a.ndim)
    act = pl.BlockSpec((tm, C), lambda i: (i, 0))
    return pl.pallas_call(
        functools.partial(_mlstm_in_kernel, tiles_per_seq=seq // tm, tn=1024, k_scale=k_scale),
        grid=(M // tm,),
        in_specs=[pl.BlockSpec((tm, K), lambda i: (i, 0)),
                  pl.BlockSpec((CONV_HALO, K), lambda i: (jnp.maximum(i * hb - 1, 0), 0)),
                  const(g), const(w), const(cw), const(cb), const(wq_bd), const(wk_bd), const(wv_bd),
                  const(wg), const(bg)],
        out_specs=[act, act, act, act, act, pl.BlockSpec((tm, 2 * LANES), lambda i: (i, 0)),
                   pl.BlockSpec((None, SUBLANES, C), lambda i: (i, 0, 0))],
        out_shape=[jax.ShapeDtypeStruct((M, C), BF16)] * 5 + [jax.ShapeDtypeStruct((M, 2 * LANES), F32),
                                                              jax.ShapeDtypeStruct((M // tm, SUBLANES, C), F32)],
        compiler_params=_cparams(("parallel",)),
        name="mlstm_in",
    )(x, x, g, w, cw, cb, wq_bd, wk_bd, wv_bd, wg, bg)


def _mlstm_core_kernel(*refs, c, n_valid, nh, zero_init):
    q_ref, k_ref, v_ref, xc_ref, gz_ref, g_ref, nw_ref, sk_ref = refs[:8]
    pos = 8
    if not zero_init:
        c0_ref, n0_ref, m0_ref = refs[8:11]
        pos = 11
    y_ref, C_ref, n_ref, m_ref = refs[pos:pos + 4]
    ch = pl.program_id(1)
    c_in = q_ref.shape[0]
    DH = q_ref.shape[1] // nh

    @pl.when(ch == 0)
    def _():
        if zero_init:
            C_ref[...] = jnp.zeros_like(C_ref)
            n_ref[...] = jnp.zeros_like(n_ref)
            m_ref[...] = jnp.zeros_like(m_ref)
        else:
            C_ref[...] = c0_ref[...]
            n_ref[...] = n0_ref[...]
            m_ref[...] = m0_ref[...]

    def padded(x):
        if c_in == c:
            return x
        return jnp.concatenate([x, jnp.zeros((c - c_in, x.shape[1]), x.dtype)], axis=0)

    g = padded(g_ref[...])
    row = lax.broadcasted_iota(jnp.int32, (c, LANES), 0)
    valid = row < n_valid
    gi = jnp.where(valid, g[:, :LANES], NEG_INF)
    lf = jnp.where(valid, -_softplus(-g[:, LANES:]), 0.0)
    tri_f = (lax.broadcasted_iota(jnp.int32, (c, c), 1) <= lax.broadcasted_iota(jnp.int32, (c, c), 0))
    tri_b = tri_f.astype(BF16)
    bcum, rest = None, lf
    for _ in range(3):
        part = rest.astype(BF16)
        rest = rest - part.astype(F32)
        d = jnp.dot(tri_b, part, preferred_element_type=F32)
        bcum = d if bcum is None else bcum + d
    bcum_t = bcum.T
    gi_t = gi.T
    causal = tri_f

    for h in range(nh):
        sl = slice(h * DH, (h + 1) * DH)
        qh = padded(q_ref[:, sl])
        kh = padded(k_ref[:, sl])
        vh = padded(v_ref[:, sl])
        bc_col = bcum[:, h:h + 1]
        bc_row = bcum_t[h:h + 1, :]
        i_row = gi_t[h:h + 1, :]
        i_col = gi[:, h:h + 1]
        m_prev = m_ref[h:h + 1, 0:1]
        log_intra = jnp.where(causal, bc_col - bc_row + i_row, NEG_INF)
        log_inter = bc_col + m_prev
        m_t = jnp.maximum(log_inter, jnp.max(log_intra, axis=1, keepdims=True))
        qk = lax.dot_general(qh, kh, (((1,), (1,)), ((), ())), preferred_element_type=F32)
        w_intra = jnp.exp(log_intra - m_t) * qk
        w_inter = jnp.exp(log_inter - m_t)
        C_old = C_ref[h]
        n_old = n_ref[h:h + 1, :]
        num = jnp.dot(w_intra.astype(BF16), vh, preferred_element_type=F32)
        num = num + w_inter * jnp.dot(qh, C_old.astype(BF16), preferred_element_type=F32)
        n8 = jnp.broadcast_to(n_old, (SUBLANES, DH)).astype(BF16)
        qn = lax.dot_general(qh, n8, (((1,), (1,)), ((), ())), preferred_element_type=F32)[:, 0:1]
        den = jnp.sum(w_intra, axis=1, keepdims=True) + w_inter * qn
        hh = num / jnp.maximum(jnp.abs(den), jnp.exp(-m_t))
        m_new = m_t[c - 1:c, :]
        bc_last = bc_col[c - 1:c, :]
        w_state = jnp.exp(bc_last - bc_col + i_col - m_new)
        decay = jnp.exp(bc_last + m_prev - m_new)
        kw = kh.astype(F32) * w_state
        C_ref[h] = decay * C_old + lax.dot_general(
            kw.astype(BF16), vh, (((0,), (0,)), ((), ())), preferred_element_type=F32)
        n_ref[h:h + 1, :] = decay * n_old + jnp.sum(kw, axis=0, keepdims=True)
        m_ref[h:h + 1, :] = jnp.broadcast_to(m_new, (1, LANES))
        hn = hh * lax.rsqrt(jnp.mean(hh * hh, axis=-1, keepdims=True) + EPS)
        hn = hn[:c_in] * nw_ref[:, sl]
        y = (hn + sk_ref[:, sl] * xc_ref[:, sl].astype(F32)) * gz_ref[:, sl].astype(F32)
        y_ref[:, sl] = y.astype(y_ref.dtype)


def _mlstm_core(q, k, v, xc, gz, gates, nw, sk, state, c, n_valid, nh):
    B, S, C = q.shape
    c_in = min(c, S)
    DH = C // nh
    zero_init = state is None
    act = pl.BlockSpec((None, c_in, C), lambda b, t: (b, t, 0))
    in_specs = [act, act, act, act, act,
                pl.BlockSpec((None, c_in, 2 * LANES), lambda b, t: (b, t, 0)),
                pl.BlockSpec((1, C), lambda b, t: (0, 0)), pl.BlockSpec((1, C), lambda b, t: (0, 0))]
    args = [q, k, v, xc, gz, gates, nw, sk]
    C_spec = pl.BlockSpec((None, nh, DH, DH), lambda b, t: (b, 0, 0, 0))
    n_spec = pl.BlockSpec((None, nh, DH), lambda b, t: (b, 0, 0))
    m_spec = pl.BlockSpec((None, nh, LANES), lambda b, t: (b, 0, 0))
    if not zero_init:
        in_specs += [C_spec, n_spec, m_spec]
        args += list(state)
    return pl.pallas_call(
        functools.partial(_mlstm_core_kernel, c=c, n_valid=n_valid, nh=nh, zero_init=zero_init),
        grid=(B, S // c_in),
        in_specs=in_specs,
        out_specs=[act, C_spec, n_spec, m_spec],
        out_shape=[jax.ShapeDtypeStruct((B, S, C), BF16),
                   jax.ShapeDtypeStruct((B, nh, DH, DH), F32),
                   jax.ShapeDtypeStruct((B, nh, DH), F32),
                   jax.ShapeDtypeStruct((B, nh, LANES), F32)],
        compiler_params=_cparams(("parallel", "arbitrary")),
        name="mlstm_core",
    )(*args)


def _rope_tables(pos, hd):
    rot = hd // 4
    half = rot // 2
    inv = ROPE_THETA ** (-np.arange(0, rot, 2, dtype=np.float64) / rot)
    ang = pos.astype(np.float64)[:, None] * inv[None, :]
    cos, sin = np.cos(ang), np.sin(ang)
    n = pos.shape[0]
    c = np.concatenate([cos, cos, np.ones((n, hd - rot))], axis=1)
    sa = np.concatenate([-sin, np.zeros((n, hd - half))], axis=1)
    sb = np.concatenate([np.zeros((n, half)), sin, np.zeros((n, hd - rot))], axis=1)
    return tuple(jnp.asarray(t, F32) for t in (c, sa, sb))


def _regroup_block_diag(w, group):
    nb, bi, bo = w.shape
    per = group // bi
    a = w.reshape(nb // per, per * bi, bo)
    cols = np.arange(per * bo)
    tile_cols = jnp.asarray(cols[None, :] % bo == np.arange(bo)[:, None], BF16)
    same_block = jnp.asarray((np.arange(per * bi)[:, None] // bi) == (cols[None, :] // bo), BF16)
    tiled = jnp.einsum("gro,oc->grc", a.astype(BF16), tile_cols, preferred_element_type=F32)
    return tiled.astype(BF16) * same_block


def _layer0(x, rope, p, prompt, state):
    Bg, L, D = x.shape
    M = Bg * L
    hd = D // H_B
    tm = _tile(M, 1024)
    tm_in = _tile(rope[0].shape[0], 512)
    x2 = x.reshape(M, D)
    xagq, k, v = _norm_proj(x2, p["g_mix_pre0"], p["w_in_ab"], ((0, 3, F32, None), (3, 1, F32, None), (4, 1, F32, None)),
                            tm_in, D, rope=rope, rope_q=2, rope_k=3, q_scale=hd ** -0.5)
    xagq3 = xagq.reshape(Bg, L, 3 * D)
    lru_w = (p["conv_a_w"], p["conv_a_b"], p["wa_bd"], p["lru_ba"], p["wx_bd"], p["lru_bx"], p["lam"])
    if prompt:
        ya, h_last = _rglru_prompt(xagq3, *lru_w, ts=_tile(L, 256))
        ya = ya.reshape(M, D)
        h_last = h_last.reshape(Bg, D)
        new_conv = xagq3[:, L - (CONV_W - 1):, :D]
        yb = _attn_prompt(xagq3, 2 * D // LANES, k.reshape(Bg, L, D), v.reshape(Bg, L, D),
                          sorted(DILATIONS, key=lambda wd: -wd[1]), QBLOCK).reshape(M, D)
        wb = min(MAX_WINDOW, L)
        new_k = k.reshape(Bg, L, H_B, hd)[:, L - wb:]
        new_v = v.reshape(Bg, L, H_B, hd)[:, L - wb:]
    else:
        conv0, h0, k_buf, v_buf = state
        xagq_t = xagq3.transpose(1, 0, 2)
        xa_t, ga_t = xagq_t[:, :, :D], xagq_t[:, :, D:2 * D]
        st_t = conv0.transpose(1, 0, 2)
        ya_t, h_last = _rglru_sample(xa_t, ga_t, st_t, h0, *lru_w)
        ya = ya_t.transpose(1, 0, 2).reshape(M, D)
        new_conv = jnp.concatenate([conv0, xagq3[:, :, :D]], axis=1)[:, -(CONV_W - 1):]
        wb = k_buf.shape[1]
        native = lambda a: a.reshape(Bg, -1, hd)
        yb = _attn_sample(native(xagq3[:, :, 2 * D:]), native(k), native(v), native(k_buf), native(v_buf),
                          L, H_B).reshape(M, D)
        new_k = k.reshape(Bg, L, H_B, hd)
        new_v = v.reshape(Bg, L, H_B, hd)
    x1, x1n = _out_proj([ya, yb], p["w_out_ab"], x2, p["g_mix_post0"], p["g_ffn_pre0"], _tile(M, 512))
    x2o = _ffn(x1, x1n, p["w_up"], p["w_dn"], 0, p["g_ffn_post0"], _tile(M, 1024), 1024)
    return x2o.reshape(Bg, L, D), (new_conv, h_last, new_k, new_v)


def _layer1(x, p, prompt, state):
    Bg, L, D = x.shape
    M = Bg * L
    x2 = x.reshape(M, D)
    tm = _tile(M, 1024)
    DC = p["w_in_c"].shape[1] // 2
    DH = DC // NH_C
    pre_w = (p["conv_c_w"], p["conv_c_b"], p["wq_bd"], p["wk_bd"], p["wv_bd"], p["wg"], p["bg"])
    if prompt:
        tm_in = _tile(L, 256)
        c = _tile(L, 256)
        *acts, tails = _mlstm_in(x2, p["g_mix_pre1"], p["w_in_c"], *pre_w, seq=L, tm=tm_in, k_scale=DH ** -0.5)
        q, k, v, xc, gz, gates = [a.reshape(Bg, L, -1) for a in acts]
        y, C, n, m = _mlstm_core(q, k, v, xc, gz, gates, p["mlstm_norm"], p["mlstm_skip"], None, c, c, NH_C)
        new_conv = tails.reshape(Bg, L // tm_in, SUBLANES, DC)[:, -1, SUBLANES - (CONV_W - 1):]
        y = y.reshape(M, DC)
    else:
        nt = DC // 1024
        xm, gz = _norm_proj(x2, p["g_mix_pre1"], p["w_in_c"], ((0, nt, F32, None), (nt, nt, BF16, _silu)),
                            _tile(M, 512), 1024)
        xm3 = xm.reshape(Bg, L, DC)
        gz3 = gz.reshape(Bg, L, DC)
        conv0, C0, n0, m0 = state
        lp = 16
        outs = _mlstm_pre_sample(xm3.transpose(1, 0, 2), conv0.transpose(1, 0, 2), *pre_w, k_scale=DH ** -0.5)
        batch_major = lambda a: jnp.pad(a.reshape(L, Bg, -1).transpose(1, 0, 2), ((0, 0), (0, lp - L), (0, 0)))
        q, k, v, xc, gates = [batch_major(a) for a in outs]
        gz_p = jnp.pad(gz3, ((0, 0), (0, lp - L), (0, 0)))
        m0b = jnp.broadcast_to(m0[:, :, None], m0.shape + (LANES,))
        y, C, n, m = _mlstm_core(q, k, v, xc, gz_p, gates, p["mlstm_norm"], p["mlstm_skip"],
                                 (C0, n0, m0b), 128, L, NH_C)
        new_conv = jnp.concatenate([conv0, xm3], axis=1)[:, -(CONV_W - 1):]
        y = y[:, :L].reshape(M, DC)
    x1, x1n = _out_proj([y], p["w_out_c"], x2, p["g_mix_post1"], p["g_ffn_pre1"], _tile(M, 512))
    x2o = _ffn(x1, x1n, p["w_up"], p["w_dn"], 1, p["g_ffn_post1"], _tile(M, 1024), 1024)
    return x2o.reshape(Bg, L, D), (new_conv, C, n, m[:, :, 0])


def kernel(x_prompt, x_sample, state_rglru_conv, state_rglru_h, cache_swa_k, cache_swa_v, state_mlstm_conv, state_mlstm_C, state_mlstm_n, state_mlstm_m, norm_mix_pre, norm_mix_post, norm_ffn_pre, norm_ffn_post, w_ffn_up, w_ffn_down, w_in_ab, conv_a_w, conv_a_b, lru_wa, lru_ba, lru_wx, lru_bx, lru_lambda, w_out_ab, w_in_c, conv_c_w, conv_c_b, mlstm_wq, mlstm_wk, mlstm_wv, mlstm_w_gate, mlstm_b_gate, mlstm_norm, mlstm_skip, w_out_c):
    B, S, D = x_prompt.shape
    Bs, Ts, _ = x_sample.shape
    DA = conv_a_w.shape[-1]
    DC = conv_c_w.shape[-1]
    hd = D // H_B
    row = lambda a: a.reshape(1, -1)
    lane_pad = lambda a: jnp.pad(a, ((0, 0), (0, LANES - NH_C)))
    wg = jnp.concatenate([lane_pad(mlstm_w_gate[0][:, :NH_C]), lane_pad(mlstm_w_gate[0][:, NH_C:])], axis=1)
    bg = jnp.concatenate([lane_pad(mlstm_b_gate[:, :NH_C]), lane_pad(mlstm_b_gate[:, NH_C:])], axis=1)
    p = {
        "g_mix_pre0": row(norm_mix_pre[0]), "g_mix_post0": row(norm_mix_post[0]),
        "g_ffn_pre0": row(norm_ffn_pre[0]), "g_ffn_post0": row(norm_ffn_post[0]),
        "g_mix_pre1": row(norm_mix_pre[1]), "g_mix_post1": row(norm_mix_post[1]),
        "g_ffn_pre1": row(norm_ffn_pre[1]), "g_ffn_post1": row(norm_ffn_post[1]),
        "w_up": w_ffn_up, "w_dn": w_ffn_down,
        "w_in_ab": w_in_ab[0].astype(BF16),
        "conv_a_w": conv_a_w[0], "conv_a_b": row(conv_a_b[0]),
        "wa_bd": _regroup_block_diag(lru_wa[0], MXU_DIM), "lru_ba": row(lru_ba[0]),
        "wx_bd": _regroup_block_diag(lru_wx[0], MXU_DIM), "lru_bx": row(lru_bx[0]),
        "lam": row(lru_lambda[0]),
        "w_out_ab": w_out_ab[0],
        "w_in_c": w_in_c[0].astype(BF16),
        "conv_c_w": conv_c_w[0], "conv_c_b": row(conv_c_b[0]),
        "wq_bd": _regroup_block_diag(mlstm_wq[0], MXU_DIM),
        "wk_bd": _regroup_block_diag(mlstm_wk[0], MXU_DIM),
        "wv_bd": _regroup_block_diag(mlstm_wv[0], MXU_DIM),
        "wg": wg.astype(BF16), "bg": bg,
        "mlstm_norm": row(mlstm_norm[0]), "mlstm_skip": row(mlstm_skip[0]),
        "w_out_c": w_out_c[0],
    }
    dt = state_rglru_conv.dtype

    xp, st0 = _layer0(x_prompt, _rope_tables(np.arange(S), hd), p, True, None)
    xp, st1 = _layer1(xp, p, True, None)
    pos_s = PAST_LEN + (np.arange(Bs * Ts) % Ts)
    xs, ss0 = _layer0(x_sample, _rope_tables(pos_s, hd), p, False,
                      (state_rglru_conv[0], state_rglru_h[0], cache_swa_k[0], cache_swa_v[0]))
    xs, ss1 = _layer1(xs, p, False, (state_mlstm_conv[0], state_mlstm_C[0], state_mlstm_n[0], state_mlstm_m[0]))

    lead = lambda a: a[None].astype(dt)
    return (xp, xs,
            lead(st0[0]), lead(ss0[0]), lead(st0[1]), lead(ss0[1]),
            lead(st0[2]), lead(ss0[2]), lead(st0[3]), lead(ss0[3]),
            lead(st1[0]), lead(ss1[0]), lead(st1[1]), lead(ss1[1]),
            lead(st1[2]), lead(ss1[2]), lead(st1[3]), lead(ss1[3]))
```

```python
import functools
import math

import numpy as np
import jax
import jax.numpy as jnp
from jax import lax
from jax.experimental import pallas as pl
from jax.experimental.pallas import tpu as pltpu

F32 = jnp.float32
BF16 = jnp.bfloat16

NA_BLOCKS = 16
CONV_W = 4
LRU_C = 8.0
H_B = 8
ROPE_THETA = 500000.0
DILATIONS = ((128, 1), (512, 4), (2048, 16))
MAX_WINDOW = 2048
QBLOCK = 128
NH_C = 4
QKV_BLOCK = 4
PAST_LEN = 16384
EPS = 1e-6

LANES = 128
SUBLANES = 8
MXU_DIM = 256
VMEM_LIMIT = 56 * 1024 * 1024

NEG_INF = float("-inf")


def _cparams(sem):
    return pltpu.CompilerParams(dimension_semantics=sem, vmem_limit_bytes=VMEM_LIMIT)


def _tile(n, pref):
    t = min(n, pref)
    while n % t:
        t -= 1
    return t


def _rms(x, g):
    return x * lax.rsqrt(jnp.mean(x * x, axis=-1, keepdims=True) + EPS) * g


def _softplus(z):
    return jnp.maximum(z, 0.0) + jnp.log1p(jnp.exp(-jnp.abs(z)))


def _gelu_tanh(x):
    c = math.sqrt(2.0 / math.pi)
    return x * (0.5 * (1.0 + jnp.tanh(c * (x + 0.044715 * (x * x * x)))))


def _silu(x):
    return x * jax.nn.sigmoid(x)


def _rope_tile(y, c, sa, sb):
    outs = []
    for h in range(y.shape[1] // LANES):
        yh = y[:, h * LANES:(h + 1) * LANES]
        outs.append(yh * c + pltpu.roll(yh, LANES - 16, 1) * sa + pltpu.roll(yh, 16, 1) * sb)
    return jnp.concatenate(outs, axis=1)


def _norm_proj_kernel(*refs, groups, tn, rope_q, rope_k, q_scale):
    n_out = len(groups)
    x_ref, g_ref, w_ref = refs[:3]
    pos = 3
    if rope_q is not None:
        c_ref, sa_ref, sb_ref = refs[3:6]
        pos = 6
    out_refs = refs[pos:pos + n_out]
    xn = _rms(x_ref[...], g_ref[...]).astype(BF16)
    for (start, count, _, act), o_ref in zip(groups, out_refs):
        for jj in range(start, start + count):
            y = jnp.dot(xn, w_ref[:, jj * tn:(jj + 1) * tn], preferred_element_type=F32)
            if rope_q is not None and jj == rope_q:
                y = _rope_tile(y, c_ref[...], sa_ref[...], sb_ref[...]) * q_scale
            elif rope_q is not None and jj == rope_k:
                y = _rope_tile(y, c_ref[...], sa_ref[...], sb_ref[...])
            if act is not None:
                y = act(y)
            o_ref[:, (jj - start) * tn:(jj - start + 1) * tn] = y.astype(o_ref.dtype)


def _norm_proj(x, g, w, groups, tm, tn, rope=None, rope_q=None, rope_k=None, q_scale=1.0):
    M, K = x.shape
    in_specs = [
        pl.BlockSpec((tm, K), lambda i: (i, 0)),
        pl.BlockSpec((1, K), lambda i: (0, 0)),
        pl.BlockSpec(w.shape, lambda i: (0, 0)),
    ]
    args = [x, g, w]
    if rope is not None:
        assert rope[0].shape[0] % tm == 0 and M % tm == 0
        pos_blocks = rope[0].shape[0] // tm
        for t in rope:
            in_specs.append(pl.BlockSpec((tm, LANES), lambda i, pb=pos_blocks: (i % pb, 0)))
            args.append(t)
    out_specs, out_shapes = [], []
    for (_, count, dtype, _) in groups:
        out_specs.append(pl.BlockSpec((tm, count * tn), lambda i: (i, 0)))
        out_shapes.append(jax.ShapeDtypeStruct((M, count * tn), dtype))
    kern = functools.partial(_norm_proj_kernel, groups=tuple(groups), tn=tn,
                             rope_q=rope_q if rope is not None else None, rope_k=rope_k, q_scale=q_scale)
    return pl.pallas_call(
        kern,
        grid=(M // tm,),
        in_specs=in_specs,
        out_specs=out_specs,
        out_shape=out_shapes,
        compiler_params=_cparams(("parallel",)),
        name="norm_proj",
    )(*args)


def _out_proj_kernel(*refs, n_in):
    a_refs = refs[:n_in]
    w_ref, x_ref, g_ref, gn_ref, o_ref, on_ref = refs[n_in:]
    y, row = None, 0
    for a_ref in a_refs:
        kk = a_ref.shape[1]
        d = jnp.dot(a_ref[...].astype(BF16), w_ref[row:row + kk, :].astype(BF16), preferred_element_type=F32)
        y = d if y is None else y + d
        row += kk
    x1 = x_ref[...] + _rms(y, g_ref[...])
    o_ref[...] = x1
    on_ref[...] = _rms(x1, gn_ref[...]).astype(on_ref.dtype)


def _out_proj(acts, w, x, g, g_next, tm):
    M, D = x.shape
    n_in = len(acts)
    assert sum(a.shape[1] for a in acts) == w.shape[0]
    row_tile = pl.BlockSpec((tm, D), lambda i: (i, 0))
    gain = pl.BlockSpec((1, D), lambda i: (0, 0))
    in_specs = [pl.BlockSpec((tm, a.shape[1]), lambda i: (i, 0)) for a in acts]
    in_specs += [pl.BlockSpec(w.shape, lambda i: (0, 0)), row_tile, gain, gain]
    return pl.pallas_call(
        functools.partial(_out_proj_kernel, n_in=n_in),
        grid=(M // tm,),
        in_specs=in_specs,
        out_specs=[row_tile, row_tile],
        out_shape=[jax.ShapeDtypeStruct((M, D), F32), jax.ShapeDtypeStruct((M, D), BF16)],
        compiler_params=_cparams(("parallel",)),
        name="out_proj",
    )(*acts, w, x, g, g_next)


def _ffn_kernel(x_ref, xn_ref, wu_ref, wd_ref, g2_ref, o_ref, acc_ref):
    f = pl.program_id(1)

    @pl.when(f == 0)
    def _():
        acc_ref[...] = jnp.zeros_like(acc_ref)

    hid = jnp.dot(xn_ref[...], wu_ref[...].astype(BF16), preferred_element_type=F32)
    hid = jnp.square(jnp.maximum(hid, 0.0))
    acc_ref[...] += jnp.dot(hid.astype(BF16), wd_ref[...].astype(BF16), preferred_element_type=F32)

    @pl.when(f == pl.num_programs(1) - 1)
    def _():
        o_ref[...] = x_ref[...] + _rms(acc_ref[...], g2_ref[...])


def _ffn(x, xn, wu, wd, layer, g2, tm, tf):
    M, D = x.shape
    FF = wu.shape[2]
    return pl.pallas_call(
        _ffn_kernel,
        grid=(M // tm, FF // tf),
        in_specs=[
            pl.BlockSpec((tm, D), lambda i, f: (i, 0)),
            pl.BlockSpec((tm, D), lambda i, f: (i, 0)),
            pl.BlockSpec((None, D, tf), lambda i, f: (layer, 0, f)),
            pl.BlockSpec((None, tf, D), lambda i, f: (layer, f, 0)),
            pl.BlockSpec((1, D), lambda i, f: (0, 0)),
        ],
        out_specs=pl.BlockSpec((tm, D), lambda i, f: (i, 0)),
        out_shape=jax.ShapeDtypeStruct((M, D), F32),
        scratch_shapes=[pltpu.VMEM((tm, D), F32)],
        compiler_params=_cparams(("parallel", "arbitrary")),
        name="ffn",
    )(x, xn, wu, wd, g2)


def _lru_gates(xc, wa_ref, ba, wx_ref, bx, lam):
    xcb = xc.astype(BF16)
    ng = wa_ref.shape[0]
    gw = wa_ref.shape[1]
    ra, ri = [], []
    for g in range(ng):
        xs = xcb[:, g * gw:(g + 1) * gw]
        ra.append(jnp.dot(xs, wa_ref[g], preferred_element_type=F32))
        ri.append(jnp.dot(xs, wx_ref[g], preferred_element_type=F32))
    r = jax.nn.sigmoid(jnp.concatenate(ra, axis=1) + ba)
    i = jax.nn.sigmoid(jnp.concatenate(ri, axis=1) + bx)
    log_a = -LRU_C * r * _softplus(-lam)
    a = jnp.exp(log_a)
    om = 1.0 - a * a
    b = jnp.where(om > 0.0, om * lax.rsqrt(om), 0.0) * (i * xc)
    return a, b


def _conv4_rows(xe, halo, w, b):
    a = w[0:1, :] * xe
    for j in range(1, CONV_W):
        a = pltpu.roll(a, 1, 0) + w[j:j + 1, :] * xe
    return b + a[halo:]


def _conv4(tail_ref, x, first, w_ref, b):
    @pl.when(first)
    def _():
        tail_ref[...] = jnp.zeros_like(tail_ref)

    xe = jnp.concatenate([tail_ref[...], x], axis=0)
    tail_ref[...] = x[x.shape[0] - SUBLANES:, :]
    return _conv4_rows(xe, SUBLANES, w_ref[...], b)


def _rglru_prompt_kernel(xa_ref, ga_ref, cw_ref, cb_ref, wa_ref, ba_ref, wx_ref, bx_ref, lam_ref,
                         ya_ref, hl_ref, xe_ref, hc_ref, a_s, b_s, h_s):
    t = pl.program_id(1)
    T = xa_ref.shape[0]

    @pl.when(t == 0)
    def _():
        hc_ref[...] = jnp.zeros_like(hc_ref)

    xc = _conv4(xe_ref, xa_ref[...], t == 0, cw_ref, cb_ref[...])
    a, b = _lru_gates(xc, wa_ref, ba_ref[...], wx_ref, bx_ref[...], lam_ref[...])
    a_s[...] = a
    b_s[...] = b

    def body(s, h):
        h = a_s[pl.ds(s, 1), :] * h + b_s[pl.ds(s, 1), :]
        h_s[pl.ds(s, 1), :] = h
        return h

    h = lax.fori_loop(0, T, body, hc_ref[...], unroll=8)
    hc_ref[...] = h
    ya_ref[...] = (h_s[...] * _gelu_tanh(ga_ref[...])).astype(ya_ref.dtype)

    @pl.when(t == pl.num_programs(1) - 1)
    def _():
        hl_ref[...] = h


def _rglru_prompt(xaga, cw, cb, wa_bd, ba, wx_bd, bx, lam, ts):
    B, S, _ = xaga.shape
    C = cw.shape[-1]
    wspec = lambda a: pl.BlockSpec(a.shape, lambda b, t: (0,) * a.ndim)
    return pl.pallas_call(
        _rglru_prompt_kernel,
        grid=(B, S // ts),
        in_specs=[
            pl.BlockSpec((None, ts, C), lambda b, t: (b, t, 0)),
            pl.BlockSpec((None, ts, C), lambda b, t: (b, t, 1)),
            wspec(cw), wspec(cb), wspec(wa_bd), wspec(ba), wspec(wx_bd), wspec(bx), wspec(lam),
        ],
        out_specs=[
            pl.BlockSpec((None, ts, C), lambda b, t: (b, t, 0)),
            pl.BlockSpec((None, 1, C), lambda b, t: (b, 0, 0)),
        ],
        out_shape=[jax.ShapeDtypeStruct((B, S, C), BF16), jax.ShapeDtypeStruct((B, 1, C), F32)],
        scratch_shapes=[pltpu.VMEM((SUBLANES, C), F32), pltpu.VMEM((1, C), F32),
                        pltpu.VMEM((ts, C), F32), pltpu.VMEM((ts, C), F32), pltpu.VMEM((ts, C), F32)],
        compiler_params=_cparams(("parallel", "arbitrary")),
        name="rglru_prompt",
    )(xaga, xaga, cw, cb, wa_bd, ba, wx_bd, bx, lam)


def _rglru_sample_kernel(xa_ref, ga_ref, st_ref, h0_ref, cw_ref, cb_ref, wa_ref, ba_ref, wx_ref, bx_ref,
                         lam_ref, ya_ref, hl_ref):
    T = xa_ref.shape[0]
    rows = [st_ref[j] for j in range(CONV_W - 1)] + [xa_ref[s] for s in range(T)]
    h = h0_ref[...]
    for s in range(T):
        y = cb_ref[...]
        for j in range(CONV_W):
            y = y + cw_ref[j:j + 1, :] * rows[s + j]
        a, b = _lru_gates(y, wa_ref, ba_ref[...], wx_ref, bx_ref[...], lam_ref[...])
        h = a * h + b
        ya_ref[s] = (h * _gelu_tanh(ga_ref[s])).astype(ya_ref.dtype)
    hl_ref[...] = h


def _rglru_sample(xa_t, ga_t, st_t, h0, cw, cb, wa_bd, ba, wx_bd, bx, lam):
    T, B, C = xa_t.shape
    args = (xa_t, ga_t, st_t, h0, cw, cb, wa_bd, ba, wx_bd, bx, lam)
    return pl.pallas_call(
        _rglru_sample_kernel,
        grid=(1,),
        in_specs=[pl.BlockSpec(a.shape, lambda i, n=a.ndim: (0,) * n) for a in args],
        out_specs=[pl.BlockSpec((T, B, C), lambda i: (0, 0, 0)), pl.BlockSpec((B, C), lambda i: (0, 0))],
        out_shape=[jax.ShapeDtypeStruct((T, B, C), F32), jax.ShapeDtypeStruct((B, C), F32)],
        compiler_params=_cparams(("arbitrary",)),
        name="rglru_sample",
    )(*args)


def _attn_prompt_kernel(q_ref, k_ref, v_ref, o_ref, *scratch, groups, qb):
    S = q_ref.shape[0]
    E = q_ref.shape[1]
    dist2 = (lax.broadcasted_iota(jnp.int32, (qb, 2 * qb), 0) + qb
             - lax.broadcasted_iota(jnp.int32, (qb, 2 * qb), 1))
    dist1 = lax.broadcasted_iota(jnp.int32, (qb, qb), 0) - lax.broadcasted_iota(jnp.int32, (qb, qb), 1)

    def rows(start, d):
        return pl.ds(start, qb) if d == 1 else pl.ds(start, qb, stride=d)

    ng = len(groups)
    s_scr, p_scr = scratch[3 * ng], scratch[3 * ng + 1]
    dn_t = (((1,), (1,)), ((), ()))
    blocks = []
    for gi, (w, d) in enumerate(groups):
        nb = (S // d) // qb
        for n in range(nb):
            for r in range(d):
                start = n * qb * d + r
                blocks.append((gi, d, w // d, start, start - qb * d if n > 0 else None))

    def gather(ref, d, start, prev):
        if prev is None:
            return ref[rows(start, d), :].astype(BF16)
        return jnp.concatenate([ref[rows(prev, d), :], ref[rows(start, d), :]], axis=0).astype(BF16)

    def width(prev):
        return qb if prev is None else 2 * qb

    for i, (gi, d, back, start, prev) in enumerate(blocks):
        q = q_ref[rows(start, d), :].astype(BF16)
        s_scr[i * qb:(i + 1) * qb, 0:width(prev)] = lax.dot_general(
            q, gather(k_ref, d, start, prev), dn_t, preferred_element_type=F32)

    for i, (gi, d, back, start, prev) in enumerate(blocks):
        dist = dist1 if prev is None else dist2
        valid = (dist >= 0) & (dist <= back)
        s = jnp.where(valid, s_scr[i * qb:(i + 1) * qb, 0:width(prev)], NEG_INF)
        m_b = jnp.max(s, axis=-1, keepdims=True)
        p_scr[i * qb:(i + 1) * qb, 0:width(prev)] = jnp.exp(s - m_b).astype(BF16)
        scratch[3 * gi][rows(start, d), :] = jnp.broadcast_to(m_b, (qb, E))

    for i, (gi, d, back, start, prev) in enumerate(blocks):
        p = p_scr[i * qb:(i + 1) * qb, 0:width(prev)]
        v1 = jnp.concatenate([gather(v_ref, d, start, prev), jnp.ones((width(prev), E), BF16)], axis=1)
        r = jnp.dot(p, v1, preferred_element_type=F32)
        scratch[3 * gi + 2][rows(start, d), :] = r[:, :E]
        scratch[3 * gi + 1][rows(start, d), :] = r[:, E:]

    ng = len(groups)

    def merge(i, c):
        rs = pl.ds(pl.multiple_of(i * qb, qb), qb)
        ms = [scratch[3 * g][rs, :] for g in range(ng)]
        m = functools.reduce(jnp.maximum, ms)
        num = den = None
        for g in range(ng):
            wg = jnp.exp(ms[g] - m)
            n_g = wg * scratch[3 * g + 2][rs, :]
            d_g = wg * scratch[3 * g + 1][rs, :]
            num = n_g if num is None else num + n_g
            den = d_g if den is None else den + d_g
        o_ref[rs, :] = (num / den).astype(o_ref.dtype)
        return c

    lax.fori_loop(0, S // qb, merge, 0, unroll=2)


def _attn_prompt(q, q_off, k, v, groups, qb):
    B, S, D = k.shape
    nh = D // LANES
    for (w, d) in groups:
        assert S % (d * qb) == 0 and w // d <= qb
    spec = pl.BlockSpec((None, S, LANES), lambda b, h: (b, 0, h))
    q_spec = pl.BlockSpec((None, S, LANES), lambda b, h: (b, 0, q_off + h))
    return pl.pallas_call(
        functools.partial(_attn_prompt_kernel, groups=tuple(groups), qb=qb),
        grid=(B, nh),
        in_specs=[q_spec, spec, spec],
        out_specs=spec,
        out_shape=jax.ShapeDtypeStruct((B, S, D), BF16),
        scratch_shapes=[pltpu.VMEM((S, LANES), F32)] * (3 * len(groups))
        + [pltpu.VMEM((len(groups) * S, 2 * qb), F32), pltpu.VMEM((len(groups) * S, 2 * qb), BF16)],
        compiler_params=_cparams(("parallel", "parallel")),
        name="attn_prompt",
    )(q, k, v)


def _attn_sample_kernel(q_ref, kn_ref, vn_ref, kt_ref, vt_ref, ks_ref, vs_ref, mt_ref, ms_ref, mn_ref, o_ref):
    nq, hd = q_ref.shape
    qb = jnp.concatenate([q_ref[...], jnp.zeros((LANES - nq, hd), F32)], axis=0).astype(BF16)
    nn = kn_ref.shape[0]
    zn = jnp.zeros((LANES - nn, hd), F32)
    key_sets = (
        (kt_ref[...], vt_ref[...], mt_ref[...]),
        (ks_ref[...].reshape(-1, hd), vs_ref[...].reshape(-1, hd), ms_ref[...]),
        (jnp.concatenate([kn_ref[...], zn], axis=0), jnp.concatenate([vn_ref[...], zn], axis=0), mn_ref[...]),
    )
    scored = []
    for k, v, bias in key_sets:
        s = lax.dot_general(k.astype(BF16), qb, (((1,), (1,)), ((), ())), preferred_element_type=F32)
        scored.append((s + bias, v))
    m = functools.reduce(jnp.maximum, [jnp.max(s, axis=0, keepdims=True) for s, _ in scored])
    m = jnp.where(m == NEG_INF, 0.0, m)
    acc = None
    for s, v in scored:
        p = jnp.exp(s - m).astype(BF16)
        v1 = jnp.concatenate([v.astype(BF16), jnp.ones(v.shape, BF16)], axis=1)
        r = lax.dot_general(p, v1, (((0,), (0,)), ((), ())), preferred_element_type=F32)
        acc = r if acc is None else acc + r
    o_ref[...] = acc[:nq, :hd] / acc[:nq, hd:]


def _attn_sample(q, kn, vn, kc, vc, T, nh):
    B, nq, hd = q.shape
    wb = kc.shape[1] // nh
    far = max(DILATIONS, key=lambda wd: wd[0])
    near = [g for g in DILATIONS if g != far]
    R = max(w for w, _ in near)
    d_far = far[1]
    assert near and wb % R == 0 and wb % d_far == 0 and T <= d_far and nq == T * nh and nq <= LANES
    A = wb // d_far

    def mult(delta, groups):
        m = np.zeros(delta.shape, np.float32)
        for (w, d) in groups:
            m = m + ((delta >= 0) & (delta % d == 0) & (delta <= (w // d) * d))
        return m

    def table(pos_of_row, head_of_row, groups):
        lane = np.arange(LANES)
        t, hq = lane // nh, lane % nh
        delta = (wb + t)[None, :] - pos_of_row[:, None]
        tab = mult(delta, groups) * (head_of_row[:, None] == hq[None, :]) * (t < T)[None, :]
        return jnp.asarray(np.where(tab > 0, np.log(np.maximum(tab, 1.0)), NEG_INF), F32)

    rt = np.arange(R * nh)
    mt = table(wb - R + rt // nh, rt % nh, near)
    rs = np.arange(A * T * nh)
    ms = table((rs // (T * nh)) * d_far + (rs % (T * nh)) // nh, rs % nh, [far])
    rn = np.arange(LANES)
    mn = table(np.where(rn < nq, wb + rn // nh, -10 ** 9), rn % nh, DILATIONS)

    small = pl.BlockSpec((None, nq, hd), lambda b: (b, 0, 0))
    tail = pl.BlockSpec((None, R * nh, hd), lambda b: (b, wb // R - 1, 0))
    strided = pl.BlockSpec((None, A, T * nh, hd), lambda b: (b, 0, 0, 0))
    const = lambda a: pl.BlockSpec(a.shape, lambda b: (0, 0))
    kc4 = kc.reshape(B, A, d_far * nh, hd)
    vc4 = vc.reshape(B, A, d_far * nh, hd)
    return pl.pallas_call(
        _attn_sample_kernel,
        grid=(B,),
        in_specs=[small, small, small, tail, tail, strided, strided, const(mt), const(ms), const(mn)],
        out_specs=small,
        out_shape=jax.ShapeDtypeStruct((B, nq, hd), F32),
        compiler_params=_cparams(("parallel",)),
        name="attn_sample",
    )(q, kn, vn, kc, vc, kc4, vc4, mt, ms, mn)


def _mlstm_qkv_gates(xc, xm, wq_ref, wk_ref, wv_ref, wg_ref, bg_ref, q_ref, k_ref, v_ref, xc_ref, g_ref, k_scale):
    xcb = xc.astype(BF16)
    xmb = xm.astype(BF16)
    ng, gw = wq_ref.shape[0], wq_ref.shape[1]
    qs, ks, vs = [], [], []
    for g in range(ng):
        sl = slice(g * gw, (g + 1) * gw)
        qs.append(jnp.dot(xcb[:, sl], wq_ref[g], preferred_element_type=F32))
        ks.append(jnp.dot(xcb[:, sl], wk_ref[g], preferred_element_type=F32))
        vs.append(jnp.dot(xmb[:, sl], wv_ref[g], preferred_element_type=F32))
    q = jnp.concatenate(qs, axis=1)
    k = jnp.concatenate(ks, axis=1)
    v = jnp.concatenate(vs, axis=1)
    qkv = jnp.concatenate([q, k, v], axis=1).astype(BF16)
    g_ref[...] = jnp.dot(qkv, wg_ref[...], preferred_element_type=F32) + bg_ref[...]
    q_ref[...] = q.astype(q_ref.dtype)
    k_ref[...] = (k * k_scale).astype(k_ref.dtype)
    v_ref[...] = v.astype(v_ref.dtype)
    xc_ref[...] = xcb.astype(xc_ref.dtype)


def _mlstm_pre_sample_kernel(xm_ref, st_ref, cw_ref, cb_ref, wq_ref, wk_ref, wv_ref, wg_ref, bg_ref,
                             q_ref, k_ref, v_ref, xc_ref, g_ref, *, k_scale):
    T = xm_ref.shape[0]
    rows = [st_ref[j] for j in range(CONV_W - 1)] + [xm_ref[s] for s in range(T)]
    conv = []
    for s in range(T):
        y = cb_ref[...]
        for j in range(CONV_W):
            y = y + cw_ref[j:j + 1, :] * rows[s + j]
        conv.append(y)
    xc = _silu(jnp.concatenate(conv, axis=0))
    xm = jnp.concatenate(rows[CONV_W - 1:], axis=0)
    _mlstm_qkv_gates(xc, xm, wq_ref, wk_ref, wv_ref, wg_ref, bg_ref, q_ref, k_ref, v_ref, xc_ref, g_ref, k_scale)


def _mlstm_pre_sample(xm_t, st_t, cw, cb, wq_bd, wk_bd, wv_bd, wg, bg, k_scale):
    T, B, C = xm_t.shape
    args = (xm_t, st_t, cw, cb, wq_bd, wk_bd, wv_bd, wg, bg)
    full = lambda shape: pl.BlockSpec(shape, lambda i: (0,) * len(shape))
    return pl.pallas_call(
        functools.partial(_mlstm_pre_sample_kernel, k_scale=k_scale),
        grid=(1,),
        in_specs=[full(a.shape) for a in args],
        out_specs=[full((T * B, C))] * 4 + [full((T * B, 2 * LANES))],
        out_shape=[jax.ShapeDtypeStruct((T * B, C), BF16)] * 4 + [jax.ShapeDtypeStruct((T * B, 2 * LANES), F32)],
        compiler_params=_cparams(("arbitrary",)),
        name="mlstm_pre_sample",
    )(*args)


CONV_HALO = 16


def _mlstm_in_kernel(x_ref, xh_ref, g_ref, w_ref, cw_ref, cb_ref, wq_ref, wk_ref, wv_ref, wg_ref, bg_ref,
                     q_ref, k_ref, v_ref, xc_ref, gz_ref, gt_ref, tail_ref, *, tiles_per_seq, tn, k_scale):
    tm = x_ref.shape[0]
    C = cw_ref.shape[1]
    first = (pl.program_id(0) % tiles_per_seq) == 0
    g = g_ref[...]
    xn = _rms(x_ref[...], g)
    xe = jnp.concatenate([_rms(xh_ref[...], g), xn], axis=0).astype(BF16)
    xn = xn.astype(BF16)
    dead_halo = lax.broadcasted_iota(jnp.int32, (CONV_HALO + tm, tn), 0) < jnp.where(first, CONV_HALO, 0)
    xcs, xms = [], []
    for jj in range(C // tn):
        cs = slice(jj * tn, (jj + 1) * tn)
        xm_e = jnp.dot(xe, w_ref[:, cs], preferred_element_type=F32)
        xm_e = jnp.where(dead_halo, 0.0, xm_e)
        xcs.append(_silu(_conv4_rows(xm_e, CONV_HALO, cw_ref[:, cs], cb_ref[:, cs])))
        xms.append(xm_e[CONV_HALO:])
        gz_ref[:, cs] = _silu(jnp.dot(xn, w_ref[:, C + jj * tn:C + (jj + 1) * tn],
                                      preferred_element_type=F32)).astype(gz_ref.dtype)
    xc = jnp.concatenate(xcs, axis=1)
    xm = jnp.concatenate(xms, axis=1)
    tail_ref[...] = xm[tm - SUBLANES:, :]
    _mlstm_qkv_gates(xc, xm, wq_ref, wk_ref, wv_ref, wg_ref, bg_ref, q_ref, k_ref, v_ref, xc_ref, gt_ref, k_scale)


def _mlstm_in(x, g, w, cw, cb, wq_bd, wk_bd, wv_bd, wg, bg, seq, tm, k_scale):
    M, K = x.shape
    C = cw.shape[1]
    assert seq % tm == 0 and tm % CONV_HALO == 0
    hb = tm // CONV_HALO
    const = lambda a: pl.BlockSpec(a.shape, lambda i: (0,) * a.ndim)
    act = pl.BlockSpec((tm, C), lambda i: (i, 0))
    return pl.pallas_call(
        functools.partial(_mlstm_in_kernel, tiles_per_seq=seq // tm, tn=1024, k_scale=k_scale),
        grid=(M // tm,),
        in_specs=[pl.BlockSpec((tm, K), lambda i: (i, 0)),
                  pl.BlockSpec((CONV_HALO, K), lambda i: (jnp.maximum(i * hb - 1, 0), 0)),
                  const(g), const(w), const(cw), const(cb), const(wq_bd), const(wk_bd), const(wv_bd),
                  const(wg), const(bg)],
        out_specs=[act, act, act, act, act, pl.BlockSpec((tm, 2 * LANES), lambda i: (i, 0)),
                   pl.BlockSpec((None, SUBLANES, C), lambda i: (i, 0, 0))],
        out_shape=[jax.ShapeDtypeStruct((M, C), BF16)] * 5 + [jax.ShapeDtypeStruct((M, 2 * LANES), F32),
                                                              jax.ShapeDtypeStruct((M // tm, SUBLANES, C), F32)],
        compiler_params=_cparams(("parallel",)),
        name="mlstm_in",
    )(x, x, g, w, cw, cb, wq_bd, wk_bd, wv_bd, wg, bg)


def _mlstm_core_kernel(*refs, c, n_valid, nh, zero_init):
    q_ref, k_ref, v_ref, xc_ref, gz_ref, g_ref, nw_ref, sk_ref = refs[:8]
    pos = 8
    if not zero_init:
        c0_ref, n0_ref, m0_ref = refs[8:11]
        pos = 11
    y_ref, C_ref, n_ref, m_ref = refs[pos:pos + 4]
    ch = pl.program_id(1)
    c_in = q_ref.shape[0]
    DH = q_ref.shape[1] // nh

    @pl.when(ch == 0)
    def _():
        if zero_init:
            C_ref[...] = jnp.zeros_like(C_ref)
            n_ref[...] = jnp.zeros_like(n_ref)
            m_ref[...] = jnp.zeros_like(m_ref)
        else:
            C_ref[...] = c0_ref[...]
            n_ref[...] = n0_ref[...]
            m_ref[...] = m0_ref[...]

    def padded(x):
        if c_in == c:
            return x
        return jnp.concatenate([x, jnp.zeros((c - c_in, x.shape[1]), x.dtype)], axis=0)

    g = padded(g_ref[...])
    row = lax.broadcasted_iota(jnp.int32, (c, LANES), 0)
    valid = row < n_valid
    gi = jnp.where(valid, g[:, :LANES], NEG_INF)
    lf = jnp.where(valid, -_softplus(-g[:, LANES:]), 0.0)
    tri_f = (lax.broadcasted_iota(jnp.int32, (c, c), 1) <= lax.broadcasted_iota(jnp.int32, (c, c), 0))
    tri_b = tri_f.astype(BF16)
    bcum, rest = None, lf
    for _ in range(3):
        part = rest.astype(BF16)
        rest = rest - part.astype(F32)
        d = jnp.dot(tri_b, part, preferred_element_type=F32)
        bcum = d if bcum is None else bcum + d
    bcum_t = bcum.T
    gi_t = gi.T
    causal = tri_f

    for h in range(nh):
        sl = slice(h * DH, (h + 1) * DH)
        qh = padded(q_ref[:, sl])
        kh = padded(k_ref[:, sl])
        vh = padded(v_ref[:, sl])
        bc_col = bcum[:, h:h + 1]
        bc_row = bcum_t[h:h + 1, :]
        i_row = gi_t[h:h + 1, :]
        i_col = gi[:, h:h + 1]
        m_prev = m_ref[h:h + 1, 0:1]
        log_intra = jnp.where(causal, bc_col - bc_row + i_row, NEG_INF)
        log_inter = bc_col + m_prev
        m_t = jnp.maximum(log_inter, jnp.max(log_intra, axis=1, keepdims=True))
        qk = lax.dot_general(qh, kh, (((1,), (1,)), ((), ())), preferred_element_type=F32)
        w_intra = jnp.exp(log_intra - m_t) * qk
        w_inter = jnp.exp(log_inter - m_t)
        C_old = C_ref[h]
        n_old = n_ref[h:h + 1, :]
        num = jnp.dot(w_intra.astype(BF16), vh, preferred_element_type=F32)
        num = num + w_inter * jnp.dot(qh, C_old.astype(BF16), preferred_element_type=F32)
        n8 = jnp.broadcast_to(n_old, (SUBLANES, DH)).astype(BF16)
        qn = lax.dot_general(qh, n8, (((1,), (1,)), ((), ())), preferred_element_type=F32)[:, 0:1]
        den = jnp.sum(w_intra, axis=1, keepdims=True) + w_inter * qn
        hh = num / jnp.maximum(jnp.abs(den), jnp.exp(-m_t))
        m_new = m_t[c - 1:c, :]
        bc_last = bc_col[c - 1:c, :]
        w_state = jnp.exp(bc_last - bc_col + i_col - m_new)
        decay = jnp.exp(bc_last + m_prev - m_new)
        kw = kh.astype(F32) * w_state
        C_ref[h] = decay * C_old + lax.dot_general(
            kw.astype(BF16), vh, (((0,), (0,)), ((), ())), preferred_element_type=F32)
        n_ref[h:h + 1, :] = decay * n_old + jnp.sum(kw, axis=0, keepdims=True)
        m_ref[h:h + 1, :] = jnp.broadcast_to(m_new, (1, LANES))
        hn = hh * lax.rsqrt(jnp.mean(hh * hh, axis=-1, keepdims=True) + EPS)
        hn = hn[:c_in] * nw_ref[:, sl]
        y = (hn + sk_ref[:, sl] * xc_ref[:, sl].astype(F32)) * gz_ref[:, sl].astype(F32)
        y_ref[:, sl] = y.astype(y_ref.dtype)


def _mlstm_core(q, k, v, xc, gz, gates, nw, sk, state, c, n_valid, nh):
    B, S, C = q.shape
    c_in = min(c, S)
    DH = C // nh
    zero_init = state is None
    act = pl.BlockSpec((None, c_in, C), lambda b, t: (b, t, 0))
    in_specs = [act, act, act, act, act,
                pl.BlockSpec((None, c_in, 2 * LANES), lambda b, t: (b, t, 0)),
                pl.BlockSpec((1, C), lambda b, t: (0, 0)), pl.BlockSpec((1, C), lambda b, t: (0, 0))]
    args = [q, k, v, xc, gz, gates, nw, sk]
    C_spec = pl.BlockSpec((None, nh, DH, DH), lambda b, t: (b, 0, 0, 0))
    n_spec = pl.BlockSpec((None, nh, DH), lambda b, t: (b, 0, 0))
    m_spec = pl.BlockSpec((None, nh, LANES), lambda b, t: (b, 0, 0))
    if not zero_init:
        in_specs += [C_spec, n_spec, m_spec]
        args += list(state)
    return pl.pallas_call(
        functools.partial(_mlstm_core_kernel, c=c, n_valid=n_valid, nh=nh, zero_init=zero_init),
        grid=(B, S // c_in),
        in_specs=in_specs,
        out_specs=[act, C_spec, n_spec, m_spec],
        out_shape=[jax.ShapeDtypeStruct((B, S, C), BF16),
                   jax.ShapeDtypeStruct((B, nh, DH, DH), F32),
                   jax.ShapeDtypeStruct((B, nh, DH), F32),
                   jax.ShapeDtypeStruct((B, nh, LANES), F32)],
        compiler_params=_cparams(("parallel", "arbitrary")),
        name="mlstm_core",
    )(*args)


def _rope_tables(pos, hd):
    rot = hd // 4
    half = rot // 2
    inv = ROPE_THETA ** (-np.arange(0, rot, 2, dtype=np.float64) / rot)
    ang = pos.astype(np.float64)[:, None] * inv[None, :]
    cos, sin = np.cos(ang), np.sin(ang)
    n = pos.shape[0]
    c = np.concatenate([cos, cos, np.ones((n, hd - rot))], axis=1)
    sa = np.concatenate([-sin, np.zeros((n, hd - half))], axis=1)
    sb = np.concatenate([np.zeros((n, half)), sin, np.zeros((n, hd - rot))], axis=1)
    return tuple(jnp.asarray(t, F32) for t in (c, sa, sb))


def _regroup_block_diag(w, group):
    nb, bi, bo = w.shape
    per = group // bi
    a = w.reshape(nb // per, per * bi, bo)
    cols = np.arange(per * bo)
    tile_cols = jnp.asarray(cols[None, :] % bo == np.arange(bo)[:, None], BF16)
    same_block = jnp.asarray((np.arange(per * bi)[:, None] // bi) == (cols[None, :] // bo), BF16)
    tiled = jnp.einsum("gro,oc->grc", a.astype(BF16), tile_cols, preferred_element_type=F32)
    return tiled.astype(BF16) * same_block


def _layer0(x, rope, p, prompt, state):
    Bg, L, D = x.shape
    M = Bg * L
    hd = D // H_B
    tm = _tile(M, 1024)
    tm_in = _tile(rope[0].shape[0], 512)
    x2 = x.reshape(M, D)
    xagq, k, v = _norm_proj(x2, p["g_mix_pre0"], p["w_in_ab"], ((0, 3, F32, None), (3, 1, F32, None), (4, 1, F32, None)),
                            tm_in, D, rope=rope, rope_q=2, rope_k=3, q_scale=hd ** -0.5)
    xagq3 = xagq.reshape(Bg, L, 3 * D)
    lru_w = (p["conv_a_w"], p["conv_a_b"], p["wa_bd"], p["lru_ba"], p["wx_bd"], p["lru_bx"], p["lam"])
    if prompt:
        ya, h_last = _rglru_prompt(xagq3, *lru_w, ts=_tile(L, 256))
        ya = ya.reshape(M, D)
        h_last = h_last.reshape(Bg, D)
        new_conv = xagq3[:, L - (CONV_W - 1):, :D]
        yb = _attn_prompt(xagq3, 2 * D // LANES, k.reshape(Bg, L, D), v.reshape(Bg, L, D),
                          sorted(DILATIONS, key=lambda wd: -wd[1]), QBLOCK).reshape(M, D)
        wb = min(MAX_WINDOW, L)
        new_k = k.reshape(Bg, L, H_B, hd)[:, L - wb:]
        new_v = v.reshape(Bg, L, H_B, hd)[:, L - wb:]
    else:
        conv0, h0, k_buf, v_buf = state
        xagq_t = xagq3.transpose(1, 0, 2)
        xa_t, ga_t = xagq_t[:, :, :D], xagq_t[:, :, D:2 * D]
        st_t = conv0.transpose(1, 0, 2)
        ya_t, h_last = _rglru_sample(xa_t, ga_t, st_t, h0, *lru_w)
        ya = ya_t.transpose(1, 0, 2).reshape(M, D)
        new_conv = jnp.concatenate([conv0, xagq3[:, :, :D]], axis=1)[:, -(CONV_W - 1):]
        wb = k_buf.shape[1]
        native = lambda a: a.reshape(Bg, -1, hd)
        yb = _attn_sample(native(xagq3[:, :, 2 * D:]), native(k), native(v), native(k_buf), native(v_buf),
                          L, H_B).reshape(M, D)
        new_k = k.reshape(Bg, L, H_B, hd)
        new_v = v.reshape(Bg, L, H_B, hd)
    x1, x1n = _out_proj([ya, yb], p["w_out_ab"], x2, p["g_mix_post0"], p["g_ffn_pre0"], _tile(M, 512))
    x2o = _ffn(x1, x1n, p["w_up"], p["w_dn"], 0, p["g_ffn_post0"], _tile(M, 1024), 1024)
    return x2o.reshape(Bg, L, D), (new_conv, h_last, new_k, new_v)


def _layer1(x, p, prompt, state):
    Bg, L, D = x.shape
    M = Bg * L
    x2 = x.reshape(M, D)
    tm = _tile(M, 1024)
    DC = p["w_in_c"].shape[1] // 2
    DH = DC // NH_C
    pre_w = (p["conv_c_w"], p["conv_c_b"], p["wq_bd"], p["wk_bd"], p["wv_bd"], p["wg"], p["bg"])
    if prompt:
        tm_in = _tile(L, 256)
        c = _tile(L, 256)
        *acts, tails = _mlstm_in(x2, p["g_mix_pre1"], p["w_in_c"], *pre_w, seq=L, tm=tm_in, k_scale=DH ** -0.5)
        q, k, v, xc, gz, gates = [a.reshape(Bg, L, -1) for a in acts]
        y, C, n, m = _mlstm_core(q, k, v, xc, gz, gates, p["mlstm_norm"], p["mlstm_skip"], None, c, c, NH_C)
        new_conv = tails.reshape(Bg, L // tm_in, SUBLANES, DC)[:, -1, SUBLANES - (CONV_W - 1):]
        y = y.reshape(M, DC)
    else:
        nt = DC // 1024
        xm, gz = _norm_proj(x2, p["g_mix_pre1"], p["w_in_c"], ((0, nt, F32, None), (nt, nt, BF16, _silu)),
                            _tile(M, 512), 1024)
        xm3 = xm.reshape(Bg, L, DC)
        gz3 = gz.reshape(Bg, L, DC)
        conv0, C0, n0, m0 = state
        lp = 16
        outs = _mlstm_pre_sample(xm3.transpose(1, 0, 2), conv0.transpose(1, 0, 2), *pre_w, k_scale=DH ** -0.5)
        batch_major = lambda a: jnp.pad(a.reshape(L, Bg, -1).transpose(1, 0, 2), ((0, 0), (0, lp - L), (0, 0)))
        q, k, v, xc, gates = [batch_major(a) for a in outs]
        gz_p = jnp.pad(gz3, ((0, 0), (0, lp - L), (0, 0)))
        m0b = jnp.broadcast_to(m0[:, :, None], m0.shape + (LANES,))
        y, C, n, m = _mlstm_core(q, k, v, xc, gz_p, gates, p["mlstm_norm"], p["mlstm_skip"],
                                 (C0, n0, m0b), 128, L, NH_C)
        new_conv = jnp.concatenate([conv0, xm3], axis=1)[:, -(CONV_W - 1):]
        y = y[:, :L].reshape(M, DC)
    x1, x1n = _out_proj([y], p["w_out_c"], x2, p["g_mix_post1"], p["g_ffn_pre1"], _tile(M, 512))
    x2o = _ffn(x1, x1n, p["w_up"], p["w_dn"], 1, p["g_ffn_post1"], _tile(M, 1024), 1024)
    return x2o.reshape(Bg, L, D), (new_conv, C, n, m[:, :, 0])


def kernel(x_prompt, x_sample, state_rglru_conv, state_rglru_h, cache_swa_k, cache_swa_v, state_mlstm_conv, state_mlstm_C, state_mlstm_n, state_mlstm_m, norm_mix_pre, norm_mix_post, norm_ffn_pre, norm_ffn_post, w_ffn_up, w_ffn_down, w_in_ab, conv_a_w, conv_a_b, lru_wa, lru_ba, lru_wx, lru_bx, lru_lambda, w_out_ab, w_in_c, conv_c_w, conv_c_b, mlstm_wq, mlstm_wk, mlstm_wv, mlstm_w_gate, mlstm_b_gate, mlstm_norm, mlstm_skip, w_out_c):
    B, S, D = x_prompt.shape
    Bs, Ts, _ = x_sample.shape
    DA = conv_a_w.shape[-1]
    DC = conv_c_w.shape[-1]
    hd = D // H_B
    row = lambda a: a.reshape(1, -1)
    lane_pad = lambda a: jnp.pad(a, ((0, 0), (0, LANES - NH_C)))
    wg = jnp.concatenate([lane_pad(mlstm_w_gate[0][:, :NH_C]), lane_pad(mlstm_w_gate[0][:, NH_C:])], axis=1)
    bg = jnp.concatenate([lane_pad(mlstm_b_gate[:, :NH_C]), lane_pad(mlstm_b_gate[:, NH_C:])], axis=1)
    p = {
        "g_mix_pre0": row(norm_mix_pre[0]), "g_mix_post0": row(norm_mix_post[0]),
        "g_ffn_pre0": row(norm_ffn_pre[0]), "g_ffn_post0": row(norm_ffn_post[0]),
        "g_mix_pre1": row(norm_mix_pre[1]), "g_mix_post1": row(norm_mix_post[1]),
        "g_ffn_pre1": row(norm_ffn_pre[1]), "g_ffn_post1": row(norm_ffn_post[1]),
        "w_up": w_ffn_up, "w_dn": w_ffn_down,
        "w_in_ab": w_in_ab[0].astype(BF16),
        "conv_a_w": conv_a_w[0], "conv_a_b": row(conv_a_b[0]),
        "wa_bd": _regroup_block_diag(lru_wa[0], MXU_DIM), "lru_ba": row(lru_ba[0]),
        "wx_bd": _regroup_block_diag(lru_wx[0], MXU_DIM), "lru_bx": row(lru_bx[0]),
        "lam": row(lru_lambda[0]),
        "w_out_ab": w_out_ab[0],
        "w_in_c": w_in_c[0].astype(BF16),
        "conv_c_w": conv_c_w[0], "conv_c_b": row(conv_c_b[0]),
        "wq_bd": _regroup_block_diag(mlstm_wq[0], MXU_DIM),
        "wk_bd": _regroup_block_diag(mlstm_wk[0], MXU_DIM),
        "wv_bd": _regroup_block_diag(mlstm_wv[0], MXU_DIM),
        "wg": wg.astype(BF16), "bg": bg,
        "mlstm_norm": row(mlstm_norm[0]), "mlstm_skip": row(mlstm_skip[0]),
        "w_out_c": w_out_c[0],
    }
    dt = state_rglru_conv.dtype

    xp, st0 = _layer0(x_prompt, _rope_tables(np.arange(S), hd), p, True, None)
    xp, st1 = _layer1(xp, p, True, None)
    pos_s = PAST_LEN + (np.arange(Bs * Ts) % Ts)
    xs, ss0 = _layer0(x_sample, _rope_tables(pos_s, hd), p, False,
                      (state_rglru_conv[0], state_rglru_h[0], cache_swa_k[0], cache_swa_v[0]))
    xs, ss1 = _layer1(xs, p, False, (state_mlstm_conv[0], state_mlstm_C[0], state_mlstm_n[0], state_mlstm_m[0]))

    lead = lambda a: a[None].astype(dt)
    return (xp, xs,
            lead(st0[0]), lead(ss0[0]), lead(st0[1]), lead(ss0[1]),
            lead(st0[2]), lead(ss0[2]), lead(st0[3]), lead(ss0[3]),
            lead(st1[0]), lead(ss1[0]), lead(st1[1]), lead(ss1[1]),
            lead(st1[2]), lead(ss1[2]), lead(st1[3]), lead(ss1[3]))
```

```python
import functools
import math

import numpy as np
import jax
import jax.numpy as jnp
from jax import lax
from jax.experimental import pallas as pl
from jax.experimental.pallas import tpu as pltpu

F32 = jnp.float32
BF16 = jnp.bfloat16

NA_BLOCKS = 16
CONV_W = 4
LRU_C = 8.0
H_B = 8
ROPE_THETA = 500000.0
DILATIONS = ((128, 1), (512, 4), (2048, 16))
MAX_WINDOW = 2048
QBLOCK = 128
NH_C = 4
QKV_BLOCK = 4
PAST_LEN = 16384
EPS = 1e-6

LANES = 128
SUBLANES = 8
MXU_DIM = 256
VMEM_LIMIT = 56 * 1024 * 1024

NEG_INF = float("-inf")


def _cparams(sem):
    return pltpu.CompilerParams(dimension_semantics=sem, vmem_limit_bytes=VMEM_LIMIT)


def _tile(n, pref):
    t = min(n, pref)
    while n % t:
        t -= 1
    return t


def _rms(x, g):
    return x * lax.rsqrt(jnp.mean(x * x, axis=-1, keepdims=True) + EPS) * g


def _softplus(z):
    return jnp.maximum(z, 0.0) + jnp.log1p(jnp.exp(-jnp.abs(z)))


def _gelu_tanh(x):
    c = math.sqrt(2.0 / math.pi)
    return x * (0.5 * (1.0 + jnp.tanh(c * (x + 0.044715 * (x * x * x)))))


def _silu(x):
    return x * jax.nn.sigmoid(x)


def _rope_tile(y, c, sa, sb):
    outs = []
    for h in range(y.shape[1] // LANES):
        yh = y[:, h * LANES:(h + 1) * LANES]
        outs.append(yh * c + pltpu.roll(yh, LANES - 16, 1) * sa + pltpu.roll(yh, 16, 1) * sb)
    return jnp.concatenate(outs, axis=1)


def _norm_proj_kernel(*refs, groups, tn, rope_q, rope_k, q_scale):
    n_out = len(groups)
    x_ref, g_ref, w_ref = refs[:3]
    pos = 3
    if rope_q is not None:
        c_ref, sa_ref, sb_ref = refs[3:6]
        pos = 6
    out_refs = refs[pos:pos + n_out]
    xn = _rms(x_ref[...], g_ref[...]).astype(BF16)
    for (start, count, _, act), o_ref in zip(groups, out_refs):
        for jj in range(start, start + count):
            y = jnp.dot(xn, w_ref[:, jj * tn:(jj + 1) * tn], preferred_element_type=F32)
            if rope_q is not None and jj == rope_q:
                y = _rope_tile(y, c_ref[...], sa_ref[...], sb_ref[...]) * q_scale
            elif rope_q is not None and jj == rope_k:
                y = _rope_tile(y, c_ref[...], sa_ref[...], sb_ref[...])
            if act is not None:
                y = act(y)
            o_ref[:, (jj - start) * tn:(jj - start + 1) * tn] = y.astype(o_ref.dtype)


def _norm_proj(x, g, w, groups, tm, tn, rope=None, rope_q=None, rope_k=None, q_scale=1.0):
    M, K = x.shape
    in_specs = [
        pl.BlockSpec((tm, K), lambda i: (i, 0)),
        pl.BlockSpec((1, K), lambda i: (0, 0)),
        pl.BlockSpec(w.shape, lambda i: (0, 0)),
    ]
    args = [x, g, w]
    if rope is not None:
        assert rope[0].shape[0] % tm == 0 and M % tm == 0
        pos_blocks = rope[0].shape[0] // tm
        for t in rope:
            in_specs.append(pl.BlockSpec((tm, LANES), lambda i, pb=pos_blocks: (i % pb, 0)))
            args.append(t)
    out_specs, out_shapes = [], []
    for (_, count, dtype, _) in groups:
        out_specs.append(pl.BlockSpec((tm, count * tn), lambda i: (i, 0)))
        out_shapes.append(jax.ShapeDtypeStruct((M, count * tn), dtype))
    kern = functools.partial(_norm_proj_kernel, groups=tuple(groups), tn=tn,
                             rope_q=rope_q if rope is not None else None, rope_k=rope_k, q_scale=q_scale)
    return pl.pallas_call(
        kern,
        grid=(M // tm,),
        in_specs=in_specs,
        out_specs=out_specs,
        out_shape=out_shapes,
        compiler_params=_cparams(("parallel",)),
        name="norm_proj",
    )(*args)


def _out_proj_kernel(*refs, n_in):
    a_refs = refs[:n_in]
    w_ref, x_ref, g_ref, gn_ref, o_ref, on_ref = refs[n_in:]
    y, row = None, 0
    for a_ref in a_refs:
        kk = a_ref.shape[1]
        d = jnp.dot(a_ref[...].astype(BF16), w_ref[row:row + kk, :].astype(BF16), preferred_element_type=F32)
        y = d if y is None else y + d
        row += kk
    x1 = x_ref[...] + _rms(y, g_ref[...])
    o_ref[...] = x1
    on_ref[...] = _rms(x1, gn_ref[...]).astype(on_ref.dtype)


def _out_proj(acts, w, x, g, g_next, tm):
    M, D = x.shape
    n_in = len(acts)
    assert sum(a.shape[1] for a in acts) == w.shape[0]
    row_tile = pl.BlockSpec((tm, D), lambda i: (i, 0))
    gain = pl.BlockSpec((1, D), lambda i: (0, 0))
    in_specs = [pl.BlockSpec((tm, a.shape[1]), lambda i: (i, 0)) for a in acts]
    in_specs += [pl.BlockSpec(w.shape, lambda i: (0, 0)), row_tile, gain, gain]
    return pl.pallas_call(
        functools.partial(_out_proj_kernel, n_in=n_in),
        grid=(M // tm,),
        in_specs=in_specs,
        out_specs=[row_tile, row_tile],
        out_shape=[jax.ShapeDtypeStruct((M, D), F32), jax.ShapeDtypeStruct((M, D), BF16)],
        compiler_params=_cparams(("parallel",)),
        name="out_proj",
    )(*acts, w, x, g, g_next)


def _ffn_kernel(x_ref, xn_ref, wu_ref, wd_ref, g2_ref, o_ref, acc_ref):
    f = pl.program_id(1)

    @pl.when(f == 0)
    def _():
        acc_ref[...] = jnp.zeros_like(acc_ref)

    hid = jnp.dot(xn_ref[...], wu_ref[...].astype(BF16), preferred_element_type=F32)
    hid = jnp.square(jnp.maximum(hid, 0.0))
    acc_ref[...] += jnp.dot(hid.astype(BF16), wd_ref[...].astype(BF16), preferred_element_type=F32)

    @pl.when(f == pl.num_programs(1) - 1)
    def _():
        o_ref[...] = x_ref[...] + _rms(acc_ref[...], g2_ref[...])


def _ffn(x, xn, wu, wd, layer, g2, tm, tf):
    M, D = x.shape
    FF = wu.shape[2]
    return pl.pallas_call(
        _ffn_kernel,
        grid=(M // tm, FF // tf),
        in_specs=[
            pl.BlockSpec((tm, D), lambda i, f: (i, 0)),
            pl.BlockSpec((tm, D), lambda i, f: (i, 0)),
            pl.BlockSpec((None, D, tf), lambda i, f: (layer, 0, f)),
            pl.BlockSpec((None, tf, D), lambda i, f: (layer, f, 0)),
            pl.BlockSpec((1, D), lambda i, f: (0, 0)),
        ],
        out_specs=pl.BlockSpec((tm, D), lambda i, f: (i, 0)),
        out_shape=jax.ShapeDtypeStruct((M, D), F32),
        scratch_shapes=[pltpu.VMEM((tm, D), F32)],
        compiler_params=_cparams(("parallel", "arbitrary")),
        name="ffn",
    )(x, xn, wu, wd, g2)


def _lru_gates(xc, wa_ref, ba, wx_ref, bx, lam):
    xcb = xc.astype(BF16)
    ng = wa_ref.shape[0]
    gw = wa_ref.shape[1]
    ra, ri = [], []
    for g in range(ng):
        xs = xcb[:, g * gw:(g + 1) * gw]
        ra.append(jnp.dot(xs, wa_ref[g], preferred_element_type=F32))
        ri.append(jnp.dot(xs, wx_ref[g], preferred_element_type=F32))
    r = jax.nn.sigmoid(jnp.concatenate(ra, axis=1) + ba)
    i = jax.nn.sigmoid(jnp.concatenate(ri, axis=1) + bx)
    log_a = -LRU_C * r * _softplus(-lam)
    a = jnp.exp(log_a)
    om = 1.0 - a * a
    b = jnp.where(om > 0.0, om * lax.rsqrt(om), 0.0) * (i * xc)
    return a, b


def _conv4_rows(xe, halo, w, b):
    a = w[0:1, :] * xe
    for j in range(1, CONV_W):
        a = pltpu.roll(a, 1, 0) + w[j:j + 1, :] * xe
    return b + a[halo:]


def _conv4(tail_ref, x, first, w_ref, b):
    @pl.when(first)
    def _():
        tail_ref[...] = jnp.zeros_like(tail_ref)

    xe = jnp.concatenate([tail_ref[...], x], axis=0)
    tail_ref[...] = x[x.shape[0] - SUBLANES:, :]
    return _conv4_rows(xe, SUBLANES, w_ref[...], b)


def _rglru_prompt_kernel(xa_ref, ga_ref, cw_ref, cb_ref, wa_ref, ba_ref, wx_ref, bx_ref, lam_ref,
                         ya_ref, hl_ref, xe_ref, hc_ref, a_s, b_s, h_s):
    t = pl.program_id(1)
    T = xa_ref.shape[0]

    @pl.when(t == 0)
    def _():
        hc_ref[...] = jnp.zeros_like(hc_ref)

    xc = _conv4(xe_ref, xa_ref[...], t == 0, cw_ref, cb_ref[...])
    a, b = _lru_gates(xc, wa_ref, ba_ref[...], wx_ref, bx_ref[...], lam_ref[...])
    a_s[...] = a
    b_s[...] = b

    def body(s, h):
        h = a_s[pl.ds(s, 1), :] * h + b_s[pl.ds(s, 1), :]
        h_s[pl.ds(s, 1), :] = h
        return h

    h = lax.fori_loop(0, T, body, hc_ref[...], unroll=8)
    hc_ref[...] = h
    ya_ref[...] = (h_s[...] * _gelu_tanh(ga_ref[...])).astype(ya_ref.dtype)

    @pl.when(t == pl.num_programs(1) - 1)
    def _():
        hl_ref[...] = h


def _rglru_prompt(xaga, cw, cb, wa_bd, ba, wx_bd, bx, lam, ts):
    B, S, _ = xaga.shape
    C = cw.shape[-1]
    wspec = lambda a: pl.BlockSpec(a.shape, lambda b, t: (0,) * a.ndim)
    return pl.pallas_call(
        _rglru_prompt_kernel,
        grid=(B, S // ts),
        in_specs=[
            pl.BlockSpec((None, ts, C), lambda b, t: (b, t, 0)),
            pl.BlockSpec((None, ts, C), lambda b, t: (b, t, 1)),
            wspec(cw), wspec(cb), wspec(wa_bd), wspec(ba), wspec(wx_bd), wspec(bx), wspec(lam),
        ],
        out_specs=[
            pl.BlockSpec((None, ts, C), lambda b, t: (b, t, 0)),
            pl.BlockSpec((None, 1, C), lambda b, t: (b, 0, 0)),
        ],
        out_shape=[jax.ShapeDtypeStruct((B, S, C), BF16), jax.ShapeDtypeStruct((B, 1, C), F32)],
        scratch_shapes=[pltpu.VMEM((SUBLANES, C), F32), pltpu.VMEM((1, C), F32),
                        pltpu.VMEM((ts, C), F32), pltpu.VMEM((ts, C), F32), pltpu.VMEM((ts, C), F32)],
        compiler_params=_cparams(("parallel", "arbitrary")),
        name="rglru_prompt",
    )(xaga, xaga, cw, cb, wa_bd, ba, wx_bd, bx, lam)


def _rglru_sample_kernel(xa_ref, ga_ref, st_ref, h0_ref, cw_ref, cb_ref, wa_ref, ba_ref, wx_ref, bx_ref,
                         lam_ref, ya_ref, hl_ref):
    T = xa_ref.shape[0]
    rows = [st_ref[j] for j in range(CONV_W - 1)] + [xa_ref[s] for s in range(T)]
    h = h0_ref[...]
    for s in range(T):
        y = cb_ref[...]
        for j in range(CONV_W):
            y = y + cw_ref[j:j + 1, :] * rows[s + j]
        a, b = _lru_gates(y, wa_ref, ba_ref[...], wx_ref, bx_ref[...], lam_ref[...])
        h = a * h + b
        ya_ref[s] = (h * _gelu_tanh(ga_ref[s])).astype(ya_ref.dtype)
    hl_ref[...] = h


def _rglru_sample(xa_t, ga_t, st_t, h0, cw, cb, wa_bd, ba, wx_bd, bx, lam):
    T, B, C = xa_t.shape
    args = (xa_t, ga_t, st_t, h0, cw, cb, wa_bd, ba, wx_bd, bx, lam)
    return pl.pallas_call(
        _rglru_sample_kernel,
        grid=(1,),
        in_specs=[pl.BlockSpec(a.shape, lambda i, n=a.ndim: (0,) * n) for a in args],
        out_specs=[pl.BlockSpec((T, B, C), lambda i: (0, 0, 0)), pl.BlockSpec((B, C), lambda i: (0, 0))],
        out_shape=[jax.ShapeDtypeStruct((T, B, C), F32), jax.ShapeDtypeStruct((B, C), F32)],
        compiler_params=_cparams(("arbitrary",)),
        name="rglru_sample",
    )(*args)


def _attn_prompt_kernel(q_ref, k_ref, v_ref, o_ref, *scratch, groups, qb):
    S = q_ref.shape[0]
    E = q_ref.shape[1]
    dist2 = (lax.broadcasted_iota(jnp.int32, (qb, 2 * qb), 0) + qb
             - lax.broadcasted_iota(jnp.int32, (qb, 2 * qb), 1))
    dist1 = lax.broadcasted_iota(jnp.int32, (qb, qb), 0) - lax.broadcasted_iota(jnp.int32, (qb, qb), 1)

    def rows(start, d):
        return pl.ds(start, qb) if d == 1 else pl.ds(start, qb, stride=d)

    ng = len(groups)
    s_scr, p_scr = scratch[3 * ng], scratch[3 * ng + 1]
    dn_t = (((1,), (1,)), ((), ()))

    for gi, (w, d) in enumerate(groups):
        back = w // d
        nb = (S // d) // qb
        nblk = nb * d
        two = nb > 1
        kw = 2 * qb if two else qb
        dist = dist2 if two else dist1
        lim_all = jnp.full((qb, kw), back, jnp.int32)
        lim_first = jnp.minimum(lax.broadcasted_iota(jnp.int32, (qb, kw), 0), back)
        m_s, l_s, acc_s = scratch[3 * gi:3 * gi + 3]

        def locate(idx, d=d):
            if d == 1:
                n = idx
                start = pl.multiple_of(idx * qb, qb)
                prev = pl.multiple_of(jnp.maximum(idx - 1, 0) * qb, qb)
            else:
                n = idx // d
                start = n * (qb * d) + idx % d
                prev = jnp.where(n > 0, start - qb * d, start)
            return start, prev, n

        def gather(ref, start, prev, d=d, two=two):
            if two:
                return jnp.concatenate([ref[rows(prev, d), :], ref[rows(start, d), :]], axis=0).astype(BF16)
            return ref[rows(start, d), :].astype(BF16)

        def scores(idx, c, kw=kw, locate=locate, gather=gather, d=d):
            start, prev, _ = locate(idx)
            q = q_ref[rows(start, d), :].astype(BF16)
            s_scr[pl.ds(pl.multiple_of(idx * qb, qb), qb), 0:kw] = lax.dot_general(
                q, gather(k_ref, start, prev), dn_t, preferred_element_type=F32)
            return c

        def softmax(idx, c, kw=kw, locate=locate, d=d, two=two, dist=dist, lim_all=lim_all,
                    lim_first=lim_first, m_s=m_s):
            start, _, n = locate(idx)
            blk = pl.ds(pl.multiple_of(idx * qb, qb), qb)
            lim = jnp.where(n > 0, lim_all, lim_first) if two else lim_all
            valid = (dist >= 0) & (dist <= lim)
            s = jnp.where(valid, s_scr[blk, 0:kw], NEG_INF)
            m_b = jnp.max(s, axis=-1, keepdims=True)
            p_scr[blk, 0:kw] = jnp.exp(s - m_b).astype(BF16)
            m_s[rows(start, d), :] = jnp.broadcast_to(m_b, (qb, E))
            return c

        def values(idx, c, kw=kw, locate=locate, gather=gather, d=d, acc_s=acc_s, l_s=l_s):
            start, prev, _ = locate(idx)
            p = p_scr[pl.ds(pl.multiple_of(idx * qb, qb), qb), 0:kw]
            v1 = jnp.concatenate([gather(v_ref, start, prev), jnp.ones((kw, E), BF16)], axis=1)
            r = jnp.dot(p, v1, preferred_element_type=F32)
            acc_s[rows(start, d), :] = r[:, :E]
            l_s[rows(start, d), :] = r[:, E:]
            return c

        lax.fori_loop(0, nblk, scores, 0, unroll=True)
        lax.fori_loop(0, nblk, softmax, 0, unroll=True)
        lax.fori_loop(0, nblk, values, 0, unroll=True)

    ng = len(groups)

    def merge(i, c):
        rs = pl.ds(pl.multiple_of(i * qb, qb), qb)
        ms = [scratch[3 * g][rs, :] for g in range(ng)]
        m = functools.reduce(jnp.maximum, ms)
        num = den = None
        for g in range(ng):
            wg = jnp.exp(ms[g] - m)
            n_g = wg * scratch[3 * g + 2][rs, :]
            d_g = wg * scratch[3 * g + 1][rs, :]
            num = n_g if num is None else num + n_g
            den = d_g if den is None else den + d_g
        o_ref[rs, :] = (num / den).astype(o_ref.dtype)
        return c

    lax.fori_loop(0, S // qb, merge, 0, unroll=2)


def _attn_prompt(q, q_off, k, v, groups, qb):
    B, S, D = k.shape
    nh = D // LANES
    for (w, d) in groups:
        assert S % (d * qb) == 0 and w // d <= qb
    spec = pl.BlockSpec((None, S, LANES), lambda b, h: (b, 0, h))
    q_spec = pl.BlockSpec((None, S, LANES), lambda b, h: (b, 0, q_off + h))
    return pl.pallas_call(
        functools.partial(_attn_prompt_kernel, groups=tuple(groups), qb=qb),
        grid=(B, nh),
        in_specs=[q_spec, spec, spec],
        out_specs=spec,
        out_shape=jax.ShapeDtypeStruct((B, S, D), BF16),
        scratch_shapes=[pltpu.VMEM((S, LANES), F32)] * (3 * len(groups))
        + [pltpu.VMEM((S, 2 * qb), F32), pltpu.VMEM((S, 2 * qb), BF16)],
        compiler_params=_cparams(("parallel", "parallel")),
        name="attn_prompt",
    )(q, k, v)


def _attn_sample_kernel(q_ref, kn_ref, vn_ref, kt_ref, vt_ref, ks_ref, vs_ref, mt_ref, ms_ref, mn_ref, o_ref):
    nq, hd = q_ref.shape
    qb = jnp.concatenate([q_ref[...], jnp.zeros((LANES - nq, hd), F32)], axis=0).astype(BF16)
    nn = kn_ref.shape[0]
    zn = jnp.zeros((LANES - nn, hd), F32)
    key_sets = (
        (kt_ref[...], vt_ref[...], mt_ref[...]),
        (ks_ref[...].reshape(-1, hd), vs_ref[...].reshape(-1, hd), ms_ref[...]),
        (jnp.concatenate([kn_ref[...], zn], axis=0), jnp.concatenate([vn_ref[...], zn], axis=0), mn_ref[...]),
    )
    scored = []
    for k, v, bias in key_sets:
        s = lax.dot_general(k.astype(BF16), qb, (((1,), (1,)), ((), ())), preferred_element_type=F32)
        scored.append((s + bias, v))
    m = functools.reduce(jnp.maximum, [jnp.max(s, axis=0, keepdims=True) for s, _ in scored])
    m = jnp.where(m == NEG_INF, 0.0, m)
    acc = None
    for s, v in scored:
        p = jnp.exp(s - m).astype(BF16)
        v1 = jnp.concatenate([v.astype(BF16), jnp.ones(v.shape, BF16)], axis=1)
        r = lax.dot_general(p, v1, (((0,), (0,)), ((), ())), preferred_element_type=F32)
        acc = r if acc is None else acc + r
    o_ref[...] = acc[:nq, :hd] / acc[:nq, hd:]


def _attn_sample(q, kn, vn, kc, vc, T, nh):
    B, nq, hd = q.shape
    wb = kc.shape[1] // nh
    far = max(DILATIONS, key=lambda wd: wd[0])
    near = [g for g in DILATIONS if g != far]
    R = max(w for w, _ in near)
    d_far = far[1]
    assert near and wb % R == 0 and wb % d_far == 0 and T <= d_far and nq == T * nh and nq <= LANES
    A = wb // d_far

    def mult(delta, groups):
        m = np.zeros(delta.shape, np.float32)
        for (w, d) in groups:
            m = m + ((delta >= 0) & (delta % d == 0) & (delta <= (w // d) * d))
        return m

    def table(pos_of_row, head_of_row, groups):
        lane = np.arange(LANES)
        t, hq = lane // nh, lane % nh
        delta = (wb + t)[None, :] - pos_of_row[:, None]
        tab = mult(delta, groups) * (head_of_row[:, None] == hq[None, :]) * (t < T)[None, :]
        return jnp.asarray(np.where(tab > 0, np.log(np.maximum(tab, 1.0)), NEG_INF), F32)

    rt = np.arange(R * nh)
    mt = table(wb - R + rt // nh, rt % nh, near)
    rs = np.arange(A * T * nh)
    ms = table((rs // (T * nh)) * d_far + (rs % (T * nh)) // nh, rs % nh, [far])
    rn = np.arange(LANES)
    mn = table(np.where(rn < nq, wb + rn // nh, -10 ** 9), rn % nh, DILATIONS)

    small = pl.BlockSpec((None, nq, hd), lambda b: (b, 0, 0))
    tail = pl.BlockSpec((None, R * nh, hd), lambda b: (b, wb // R - 1, 0))
    strided = pl.BlockSpec((None, A, T * nh, hd), lambda b: (b, 0, 0, 0))
    const = lambda a: pl.BlockSpec(a.shape, lambda b: (0, 0))
    kc4 = kc.reshape(B, A, d_far * nh, hd)
    vc4 = vc.reshape(B, A, d_far * nh, hd)
    return pl.pallas_call(
        _attn_sample_kernel,
        grid=(B,),
        in_specs=[small, small, small, tail, tail, strided, strided, const(mt), const(ms), const(mn)],
        out_specs=small,
        out_shape=jax.ShapeDtypeStruct((B, nq, hd), F32),
        compiler_params=_cparams(("parallel",)),
        name="attn_sample",
    )(q, kn, vn, kc, vc, kc4, vc4, mt, ms, mn)


def _mlstm_qkv_gates(xc, xm, wq_ref, wk_ref, wv_ref, wg_ref, bg_ref, q_ref, k_ref, v_ref, xc_ref, g_ref, k_scale):
    xcb = xc.astype(BF16)
    xmb = xm.astype(BF16)
    ng, gw = wq_ref.shape[0], wq_ref.shape[1]
    qs, ks, vs = [], [], []
    for g in range(ng):
        sl = slice(g * gw, (g + 1) * gw)
        qs.append(jnp.dot(xcb[:, sl], wq_ref[g], preferred_element_type=F32))
        ks.append(jnp.dot(xcb[:, sl], wk_ref[g], preferred_element_type=F32))
        vs.append(jnp.dot(xmb[:, sl], wv_ref[g], preferred_element_type=F32))
    k = jnp.concatenate(ks, axis=1)
    qb = jnp.concatenate(qs, axis=1).astype(BF16)
    vb = jnp.concatenate(vs, axis=1).astype(BF16)
    qkv = jnp.concatenate([qb, k.astype(BF16), vb], axis=1)
    g_ref[...] = jnp.dot(qkv, wg_ref[...], preferred_element_type=F32) + bg_ref[...]
    q_ref[...] = qb
    k_ref[...] = (k * k_scale).astype(k_ref.dtype)
    v_ref[...] = vb
    xc_ref[...] = xcb


def _mlstm_pre_sample_kernel(xm_ref, st_ref, cw_ref, cb_ref, wq_ref, wk_ref, wv_ref, wg_ref, bg_ref,
                             q_ref, k_ref, v_ref, xc_ref, g_ref, *, k_scale):
    T = xm_ref.shape[0]
    rows = [st_ref[j] for j in range(CONV_W - 1)] + [xm_ref[s] for s in range(T)]
    conv = []
    for s in range(T):
        y = cb_ref[...]
        for j in range(CONV_W):
            y = y + cw_ref[j:j + 1, :] * rows[s + j]
        conv.append(y)
    xc = _silu(jnp.concatenate(conv, axis=0))
    xm = jnp.concatenate(rows[CONV_W - 1:], axis=0)
    _mlstm_qkv_gates(xc, xm, wq_ref, wk_ref, wv_ref, wg_ref, bg_ref, q_ref, k_ref, v_ref, xc_ref, g_ref, k_scale)


def _mlstm_pre_sample(xm_t, st_t, cw, cb, wq_bd, wk_bd, wv_bd, wg, bg, k_scale):
    T, B, C = xm_t.shape
    args = (xm_t, st_t, cw, cb, wq_bd, wk_bd, wv_bd, wg, bg)
    full = lambda shape: pl.BlockSpec(shape, lambda i: (0,) * len(shape))
    return pl.pallas_call(
        functools.partial(_mlstm_pre_sample_kernel, k_scale=k_scale),
        grid=(1,),
        in_specs=[full(a.shape) for a in args],
        out_specs=[full((T * B, C))] * 4 + [full((T * B, 2 * LANES))],
        out_shape=[jax.ShapeDtypeStruct((T * B, C), BF16)] * 4 + [jax.ShapeDtypeStruct((T * B, 2 * LANES), F32)],
        compiler_params=_cparams(("arbitrary",)),
        name="mlstm_pre_sample",
    )(*args)


CONV_HALO = 16


def _mlstm_in_kernel(x_ref, xh_ref, g_ref, w_ref, cw_ref, cb_ref, wq_ref, wk_ref, wv_ref, wg_ref, bg_ref,
                     q_ref, k_ref, v_ref, xc_ref, gz_ref, gt_ref, tail_ref, *, tiles_per_seq, tn, k_scale):
    tm = x_ref.shape[0]
    C = cw_ref.shape[1]
    first = (pl.program_id(0) % tiles_per_seq) == 0
    g = g_ref[...]
    xn = _rms(x_ref[...], g)
    xe = jnp.concatenate([_rms(xh_ref[...], g), xn], axis=0).astype(BF16)
    xn = xn.astype(BF16)
    dead_halo = lax.broadcasted_iota(jnp.int32, (CONV_HALO + tm, tn), 0) < jnp.where(first, CONV_HALO, 0)
    xcs, xms = [], []
    for jj in range(C // tn):
        cs = slice(jj * tn, (jj + 1) * tn)
        xm_e = jnp.dot(xe, w_ref[:, cs], preferred_element_type=F32)
        xm_e = jnp.where(dead_halo, 0.0, xm_e)
        xcs.append(_silu(_conv4_rows(xm_e, CONV_HALO, cw_ref[:, cs], cb_ref[:, cs])))
        xms.append(xm_e[CONV_HALO:])
        gz_ref[:, cs] = _silu(jnp.dot(xn, w_ref[:, C + jj * tn:C + (jj + 1) * tn],
                                      preferred_element_type=F32)).astype(gz_ref.dtype)
    xc = jnp.concatenate(xcs, axis=1)
    xm = jnp.concatenate(xms, axis=1)
    tail_ref[...] = xm[tm - SUBLANES:, :]
    _mlstm_qkv_gates(xc, xm, wq_ref, wk_ref, wv_ref, wg_ref, bg_ref, q_ref, k_ref, v_ref, xc_ref, gt_ref, k_scale)


def _mlstm_in(x, g, w, cw, cb, wq_bd, wk_bd, wv_bd, wg, bg, seq, tm, k_scale):
    M, K = x.shape
    C = cw.shape[1]
    assert seq % tm == 0 and tm % CONV_HALO == 0
    hb = tm // CONV_HALO
    const = lambda a: pl.BlockSpec(a.shape, lambda i: (0,) * a.ndim)
    act = pl.BlockSpec((tm, C), lambda i: (i, 0))
    return pl.pallas_call(
        functools.partial(_mlstm_in_kernel, tiles_per_seq=seq // tm, tn=1024, k_scale=k_scale),
        grid=(M // tm,),
        in_specs=[pl.BlockSpec((tm, K), lambda i: (i, 0)),
                  pl.BlockSpec((CONV_HALO, K), lambda i: (jnp.maximum(i * hb - 1, 0), 0)),
                  const(g), const(w), const(cw), const(cb), const(wq_bd), const(wk_bd), const(wv_bd),
                  const(wg), const(bg)],
        out_specs=[act, act, act, act, act, pl.BlockSpec((tm, 2 * LANES), lambda i: (i, 0)),
                   pl.BlockSpec((None, SUBLANES, C), lambda i: (i, 0, 0))],
        out_shape=[jax.ShapeDtypeStruct((M, C), BF16)] * 5 + [jax.ShapeDtypeStruct((M, 2 * LANES), F32),
                                                              jax.ShapeDtypeStruct((M // tm, SUBLANES, C), F32)],
        compiler_params=_cparams(("parallel",)),
        name="mlstm_in",
    )(x, x, g, w, cw, cb, wq_bd, wk_bd, wv_bd, wg, bg)


def _mlstm_core_kernel(*refs, c, n_valid, nh, zero_init):
    q_ref, k_ref, v_ref, xc_ref, gz_ref, g_ref, nw_ref, sk_ref = refs[:8]
    pos = 8
    if not zero_init:
        c0_ref, n0_ref, m0_ref = refs[8:11]
        pos = 11
    y_ref, C_ref, n_ref, m_ref = refs[pos:pos + 4]
    ch = pl.program_id(1)
    c_in = q_ref.shape[0]
    DH = q_ref.shape[1] // nh

    @pl.when(ch == 0)
    def _():
        if zero_init:
            C_ref[...] = jnp.zeros_like(C_ref)
            n_ref[...] = jnp.zeros_like(n_ref)
            m_ref[...] = jnp.zeros_like(m_ref)
        else:
            C_ref[...] = c0_ref[...]
            n_ref[...] = n0_ref[...]
            m_ref[...] = m0_ref[...]

    def padded(x):
        if c_in == c:
            return x
        return jnp.concatenate([x, jnp.zeros((c - c_in, x.shape[1]), x.dtype)], axis=0)

    g = padded(g_ref[...])
    row = lax.broadcasted_iota(jnp.int32, (c, LANES), 0)
    valid = row < n_valid
    gi = jnp.where(valid, g[:, :LANES], NEG_INF)
    lf = jnp.where(valid, -_softplus(-g[:, LANES:]), 0.0)
    tri_f = (lax.broadcasted_iota(jnp.int32, (c, c), 1) <= lax.broadcasted_iota(jnp.int32, (c, c), 0))
    tri_b = tri_f.astype(BF16)
    bcum, rest = None, lf
    for _ in range(3):
        part = rest.astype(BF16)
        rest = rest - part.astype(F32)
        d = jnp.dot(tri_b, part, preferred_element_type=F32)
        bcum = d if bcum is None else bcum + d
    bcum_t = bcum.T
    gi_t = gi.T
    causal = tri_f

    for h in range(nh):
        sl = slice(h * DH, (h + 1) * DH)
        qh = padded(q_ref[:, sl])
        kh = padded(k_ref[:, sl])
        vh = padded(v_ref[:, sl])
        bc_col = bcum[:, h:h + 1]
        bc_row = bcum_t[h:h + 1, :]
        i_row = gi_t[h:h + 1, :]
        i_col = gi[:, h:h + 1]
        m_prev = m_ref[h:h + 1, 0:1]
        log_intra = jnp.where(causal, bc_col - bc_row + i_row, NEG_INF)
        log_inter = bc_col + m_prev
        m_t = jnp.maximum(log_inter, jnp.max(log_intra, axis=1, keepdims=True))
        qk = lax.dot_general(qh, kh, (((1,), (1,)), ((), ())), preferred_element_type=F32)
        w_intra = jnp.exp(log_intra - m_t) * qk
        w_inter = jnp.exp(log_inter - m_t)
        C_old = C_ref[h]
        n_old = n_ref[h:h + 1, :]
        num = jnp.dot(w_intra.astype(BF16), vh, preferred_element_type=F32)
        num = num + w_inter * jnp.dot(qh, C_old.astype(BF16), preferred_element_type=F32)
        n8 = jnp.broadcast_to(n_old, (SUBLANES, DH)).astype(BF16)
        qn = lax.dot_general(qh, n8, (((1,), (1,)), ((), ())), preferred_element_type=F32)[:, 0:1]
        den = jnp.sum(w_intra, axis=1, keepdims=True) + w_inter * qn
        hh = num / jnp.maximum(jnp.abs(den), jnp.exp(-m_t))
        m_new = m_t[c - 1:c, :]
        bc_last = bc_col[c - 1:c, :]
        w_state = jnp.exp(bc_last - bc_col + i_col - m_new)
        decay = jnp.exp(bc_last + m_prev - m_new)
        kw = kh.astype(F32) * w_state
        C_ref[h] = decay * C_old + lax.dot_general(
            kw.astype(BF16), vh, (((0,), (0,)), ((), ())), preferred_element_type=F32)
        n_ref[h:h + 1, :] = decay * n_old + jnp.sum(kw, axis=0, keepdims=True)
        m_ref[h:h + 1, :] = jnp.broadcast_to(m_new, (1, LANES))
        hn = hh * lax.rsqrt(jnp.mean(hh * hh, axis=-1, keepdims=True) + EPS)
        hn = hn[:c_in] * nw_ref[:, sl]
        y = (hn + sk_ref[:, sl] * xc_ref[:, sl].astype(F32)) * gz_ref[:, sl].astype(F32)
        y_ref[:, sl] = y.astype(y_ref.dtype)


def _mlstm_core(q, k, v, xc, gz, gates, nw, sk, state, c, n_valid, nh):
    B, S, C = q.shape
    c_in = min(c, S)
    DH = C // nh
    zero_init = state is None
    act = pl.BlockSpec((None, c_in, C), lambda b, t: (b, t, 0))
    in_specs = [act, act, act, act, act,
                pl.BlockSpec((None, c_in, 2 * LANES), lambda b, t: (b, t, 0)),
                pl.BlockSpec((1, C), lambda b, t: (0, 0)), pl.BlockSpec((1, C), lambda b, t: (0, 0))]
    args = [q, k, v, xc, gz, gates, nw, sk]
    C_spec = pl.BlockSpec((None, nh, DH, DH), lambda b, t: (b, 0, 0, 0))
    n_spec = pl.BlockSpec((None, nh, DH), lambda b, t: (b, 0, 0))
    m_spec = pl.BlockSpec((None, nh, LANES), lambda b, t: (b, 0, 0))
    if not zero_init:
        in_specs += [C_spec, n_spec, m_spec]
        args += list(state)
    return pl.pallas_call(
        functools.partial(_mlstm_core_kernel, c=c, n_valid=n_valid, nh=nh, zero_init=zero_init),
        grid=(B, S // c_in),
        in_specs=in_specs,
        out_specs=[act, C_spec, n_spec, m_spec],
        out_shape=[jax.ShapeDtypeStruct((B, S, C), BF16),
                   jax.ShapeDtypeStruct((B, nh, DH, DH), F32),
                   jax.ShapeDtypeStruct((B, nh, DH), F32),
                   jax.ShapeDtypeStruct((B, nh, LANES), F32)],
        compiler_params=_cparams(("parallel", "arbitrary")),
        name="mlstm_core",
    )(*args)


def _rope_tables(pos, hd):
    rot = hd // 4
    half = rot // 2
    inv = ROPE_THETA ** (-np.arange(0, rot, 2, dtype=np.float64) / rot)
    ang = pos.astype(np.float64)[:, None] * inv[None, :]
    cos, sin = np.cos(ang), np.sin(ang)
    n = pos.shape[0]
    c = np.concatenate([cos, cos, np.ones((n, hd - rot))], axis=1)
    sa = np.concatenate([-sin, np.zeros((n, hd - half))], axis=1)
    sb = np.concatenate([np.zeros((n, half)), sin, np.zeros((n, hd - rot))], axis=1)
    return tuple(jnp.asarray(t, F32) for t in (c, sa, sb))


def _regroup_block_diag(w, group):
    nb, bi, bo = w.shape
    per = group // bi
    a = w.reshape(nb // per, per * bi, bo)
    cols = np.arange(per * bo)
    tile_cols = jnp.asarray(cols[None, :] % bo == np.arange(bo)[:, None], BF16)
    same_block = jnp.asarray((np.arange(per * bi)[:, None] // bi) == (cols[None, :] // bo), BF16)
    tiled = jnp.einsum("gro,oc->grc", a.astype(BF16), tile_cols, preferred_element_type=F32)
    return tiled.astype(BF16) * same_block


def _layer0(x, rope, p, prompt, state):
    Bg, L, D = x.shape
    M = Bg * L
    hd = D // H_B
    tm = _tile(M, 1024)
    tm_in = _tile(rope[0].shape[0], 512)
    x2 = x.reshape(M, D)
    xagq, k, v = _norm_proj(x2, p["g_mix_pre0"], p["w_in_ab"], ((0, 3, F32, None), (3, 1, F32, None), (4, 1, F32, None)),
                            tm_in, D, rope=rope, rope_q=2, rope_k=3, q_scale=hd ** -0.5)
    xagq3 = xagq.reshape(Bg, L, 3 * D)
    lru_w = (p["conv_a_w"], p["conv_a_b"], p["wa_bd"], p["lru_ba"], p["wx_bd"], p["lru_bx"], p["lam"])
    if prompt:
        ya, h_last = _rglru_prompt(xagq3, *lru_w, ts=_tile(L, 256))
        ya = ya.reshape(M, D)
        h_last = h_last.reshape(Bg, D)
        new_conv = xagq3[:, L - (CONV_W - 1):, :D]
        yb = _attn_prompt(xagq3, 2 * D // LANES, k.reshape(Bg, L, D), v.reshape(Bg, L, D),
                          sorted(DILATIONS, key=lambda wd: -wd[1]), QBLOCK).reshape(M, D)
        wb = min(MAX_WINDOW, L)
        new_k = k.reshape(Bg, L, H_B, hd)[:, L - wb:]
        new_v = v.reshape(Bg, L, H_B, hd)[:, L - wb:]
    else:
        conv0, h0, k_buf, v_buf = state
        xagq_t = xagq3.transpose(1, 0, 2)
        xa_t, ga_t = xagq_t[:, :, :D], xagq_t[:, :, D:2 * D]
        st_t = conv0.transpose(1, 0, 2)
        ya_t, h_last = _rglru_sample(xa_t, ga_t, st_t, h0, *lru_w)
        ya = ya_t.transpose(1, 0, 2).reshape(M, D)
        new_conv = jnp.concatenate([conv0, xagq3[:, :, :D]], axis=1)[:, -(CONV_W - 1):]
        wb = k_buf.shape[1]
        native = lambda a: a.reshape(Bg, -1, hd)
        yb = _attn_sample(native(xagq3[:, :, 2 * D:]), native(k), native(v), native(k_buf), native(v_buf),
                          L, H_B).reshape(M, D)
        new_k = k.reshape(Bg, L, H_B, hd)
        new_v = v.reshape(Bg, L, H_B, hd)
    x1, x1n = _out_proj([ya, yb], p["w_out_ab"], x2, p["g_mix_post0"], p["g_ffn_pre0"], _tile(M, 512))
    x2o = _ffn(x1, x1n, p["w_up"], p["w_dn"], 0, p["g_ffn_post0"], _tile(M, 1024), 1024)
    return x2o.reshape(Bg, L, D), (new_conv, h_last, new_k, new_v)


def _layer1(x, p, prompt, state):
    Bg, L, D = x.shape
    M = Bg * L
    x2 = x.reshape(M, D)
    tm = _tile(M, 1024)
    DC = p["w_in_c"].shape[1] // 2
    DH = DC // NH_C
    pre_w = (p["conv_c_w"], p["conv_c_b"], p["wq_bd"], p["wk_bd"], p["wv_bd"], p["wg"], p["bg"])
    if prompt:
        tm_in = _tile(L, 256)
        c = _tile(L, 256)
        *acts, tails = _mlstm_in(x2, p["g_mix_pre1"], p["w_in_c"], *pre_w, seq=L, tm=tm_in, k_scale=DH ** -0.5)
        q, k, v, xc, gz, gates = [a.reshape(Bg, L, -1) for a in acts]
        y, C, n, m = _mlstm_core(q, k, v, xc, gz, gates, p["mlstm_norm"], p["mlstm_skip"], None, c, c, NH_C)
        new_conv = tails.reshape(Bg, L // tm_in, SUBLANES, DC)[:, -1, SUBLANES - (CONV_W - 1):]
        y = y.reshape(M, DC)
    else:
        nt = DC // 1024
        xm, gz = _norm_proj(x2, p["g_mix_pre1"], p["w_in_c"], ((0, nt, F32, None), (nt, nt, BF16, _silu)),
                            _tile(M, 512), 1024)
        xm3 = xm.reshape(Bg, L, DC)
        gz3 = gz.reshape(Bg, L, DC)
        conv0, C0, n0, m0 = state
        lp = 16
        outs = _mlstm_pre_sample(xm3.transpose(1, 0, 2), conv0.transpose(1, 0, 2), *pre_w, k_scale=DH ** -0.5)
        batch_major = lambda a: jnp.pad(a.reshape(L, Bg, -1).transpose(1, 0, 2), ((0, 0), (0, lp - L), (0, 0)))
        q, k, v, xc, gates = [batch_major(a) for a in outs]
        gz_p = jnp.pad(gz3, ((0, 0), (0, lp - L), (0, 0)))
        m0b = jnp.broadcast_to(m0[:, :, None], m0.shape + (LANES,))
        y, C, n, m = _mlstm_core(q, k, v, xc, gz_p, gates, p["mlstm_norm"], p["mlstm_skip"],
                                 (C0, n0, m0b), 32, L, NH_C)
        new_conv = jnp.concatenate([conv0, xm3], axis=1)[:, -(CONV_W - 1):]
        y = y[:, :L].reshape(M, DC)
    x1, x1n = _out_proj([y], p["w_out_c"], x2, p["g_mix_post1"], p["g_ffn_pre1"], _tile(M, 512))
    x2o = _ffn(x1, x1n, p["w_up"], p["w_dn"], 1, p["g_ffn_post1"], _tile(M, 1024), 1024)
    return x2o.reshape(Bg, L, D), (new_conv, C, n, m[:, :, 0])


def kernel(x_prompt, x_sample, state_rglru_conv, state_rglru_h, cache_swa_k, cache_swa_v, state_mlstm_conv, state_mlstm_C, state_mlstm_n, state_mlstm_m, norm_mix_pre, norm_mix_post, norm_ffn_pre, norm_ffn_post, w_ffn_up, w_ffn_down, w_in_ab, conv_a_w, conv_a_b, lru_wa, lru_ba, lru_wx, lru_bx, lru_lambda, w_out_ab, w_in_c, conv_c_w, conv_c_b, mlstm_wq, mlstm_wk, mlstm_wv, mlstm_w_gate, mlstm_b_gate, mlstm_norm, mlstm_skip, w_out_c):
    B, S, D = x_prompt.shape
    Bs, Ts, _ = x_sample.shape
    DA = conv_a_w.shape[-1]
    DC = conv_c_w.shape[-1]
    hd = D // H_B
    row = lambda a: a.reshape(1, -1)
    lane_pad = lambda a: jnp.pad(a, ((0, 0), (0, LANES - NH_C)))
    wg = jnp.concatenate([lane_pad(mlstm_w_gate[0][:, :NH_C]), lane_pad(mlstm_w_gate[0][:, NH_C:])], axis=1)
    bg = jnp.concatenate([lane_pad(mlstm_b_gate[:, :NH_C]), lane_pad(mlstm_b_gate[:, NH_C:])], axis=1)
    p = {
        "g_mix_pre0": row(norm_mix_pre[0]), "g_mix_post0": row(norm_mix_post[0]),
        "g_ffn_pre0": row(norm_ffn_pre[0]), "g_ffn_post0": row(norm_ffn_post[0]),
        "g_mix_pre1": row(norm_mix_pre[1]), "g_mix_post1": row(norm_mix_post[1]),
        "g_ffn_pre1": row(norm_ffn_pre[1]), "g_ffn_post1": row(norm_ffn_post[1]),
        "w_up": w_ffn_up, "w_dn": w_ffn_down,
        "w_in_ab": w_in_ab[0].astype(BF16),
        "conv_a_w": conv_a_w[0], "conv_a_b": row(conv_a_b[0]),
        "wa_bd": _regroup_block_diag(lru_wa[0], MXU_DIM), "lru_ba": row(lru_ba[0]),
        "wx_bd": _regroup_block_diag(lru_wx[0], MXU_DIM), "lru_bx": row(lru_bx[0]),
        "lam": row(lru_lambda[0]),
        "w_out_ab": w_out_ab[0],
        "w_in_c": w_in_c[0].astype(BF16),
        "conv_c_w": conv_c_w[0], "conv_c_b": row(conv_c_b[0]),
        "wq_bd": _regroup_block_diag(mlstm_wq[0], MXU_DIM),
        "wk_bd": _regroup_block_diag(mlstm_wk[0], MXU_DIM),
        "wv_bd": _regroup_block_diag(mlstm_wv[0], MXU_DIM),
        "wg": wg.astype(BF16), "bg": bg,
        "mlstm_norm": row(mlstm_norm[0]), "mlstm_skip": row(mlstm_skip[0]),
        "w_out_c": w_out_c[0],
    }
    dt = state_rglru_conv.dtype

    xp, st0 = _layer0(x_prompt, _rope_tables(np.arange(S), hd), p, True, None)
    xp, st1 = _layer1(xp, p, True, None)
    pos_s = PAST_LEN + (np.arange(Bs * Ts) % Ts)
    xs, ss0 = _layer0(x_sample, _rope_tables(pos_s, hd), p, False,
                      (state_rglru_conv[0], state_rglru_h[0], cache_swa_k[0], cache_swa_v[0]))
    xs, ss1 = _layer1(xs, p, False, (state_mlstm_conv[0], state_mlstm_C[0], state_mlstm_n[0], state_mlstm_m[0]))

    lead = lambda a: a[None].astype(dt)
    return (xp, xs,
            lead(st0[0]), lead(ss0[0]), lead(st0[1]), lead(ss0[1]),
            lead(st0[2]), lead(ss0[2]), lead(st0[3]), lead(ss0[3]),
            lead(st1[0]), lead(ss1[0]), lead(st1[1]), lead(ss1[1]),
            lead(st1[2]), lead(ss1[2]), lead(st1[3]), lead(ss1[3]))
```

```python
import functools
import math

import numpy as np
import jax
import jax.numpy as jnp
from jax import lax
from jax.experimental import pallas as pl
from jax.experimental.pallas import tpu as pltpu

F32 = jnp.float32
BF16 = jnp.bfloat16

NA_BLOCKS = 16
CONV_W = 4
LRU_C = 8.0
H_B = 8
ROPE_THETA = 500000.0
DILATIONS = ((128, 1), (512, 4), (2048, 16))
MAX_WINDOW = 2048
QBLOCK = 128
NH_C = 4
QKV_BLOCK = 4
PAST_LEN = 16384
EPS = 1e-6

LANES = 128
SUBLANES = 8
MXU_DIM = 256
VMEM_LIMIT = 56 * 1024 * 1024

NEG_INF = float("-inf")


def _cparams(sem):
    return pltpu.CompilerParams(dimension_semantics=sem, vmem_limit_bytes=VMEM_LIMIT)


def _tile(n, pref):
    t = min(n, pref)
    while n % t:
        t -= 1
    return t


def _rms(x, g):
    return x * lax.rsqrt(jnp.mean(x * x, axis=-1, keepdims=True) + EPS) * g


def _softplus(z):
    return jnp.maximum(z, 0.0) + jnp.log1p(jnp.exp(-jnp.abs(z)))


def _gelu_tanh(x):
    c = math.sqrt(2.0 / math.pi)
    return x * (0.5 * (1.0 + jnp.tanh(c * (x + 0.044715 * (x * x * x)))))


def _silu(x):
    return x * jax.nn.sigmoid(x)


def _rope_tile(y, c, sa, sb):
    outs = []
    for h in range(y.shape[1] // LANES):
        yh = y[:, h * LANES:(h + 1) * LANES]
        outs.append(yh * c + pltpu.roll(yh, LANES - 16, 1) * sa + pltpu.roll(yh, 16, 1) * sb)
    return jnp.concatenate(outs, axis=1)


def _norm_proj_kernel(*refs, groups, tn, rope_q, rope_k, q_scale):
    n_out = len(groups)
    x_ref, g_ref, w_ref = refs[:3]
    pos = 3
    if rope_q is not None:
        c_ref, sa_ref, sb_ref = refs[3:6]
        pos = 6
    out_refs = refs[pos:pos + n_out]
    xn = _rms(x_ref[...], g_ref[...]).astype(BF16)
    for (start, count, _, act), o_ref in zip(groups, out_refs):
        for jj in range(start, start + count):
            y = jnp.dot(xn, w_ref[:, jj * tn:(jj + 1) * tn], preferred_element_type=F32)
            if rope_q is not None and jj == rope_q:
                y = _rope_tile(y, c_ref[...], sa_ref[...], sb_ref[...]) * q_scale
            elif rope_q is not None and jj == rope_k:
                y = _rope_tile(y, c_ref[...], sa_ref[...], sb_ref[...])
            if act is not None:
                y = act(y)
            o_ref[:, (jj - start) * tn:(jj - start + 1) * tn] = y.astype(o_ref.dtype)


def _norm_proj(x, g, w, groups, tm, tn, rope=None, rope_q=None, rope_k=None, q_scale=1.0):
    M, K = x.shape
    in_specs = [
        pl.BlockSpec((tm, K), lambda i: (i, 0)),
        pl.BlockSpec((1, K), lambda i: (0, 0)),
        pl.BlockSpec(w.shape, lambda i: (0, 0)),
    ]
    args = [x, g, w]
    if rope is not None:
        assert rope[0].shape[0] % tm == 0 and M % tm == 0
        pos_blocks = rope[0].shape[0] // tm
        for t in rope:
            in_specs.append(pl.BlockSpec((tm, LANES), lambda i, pb=pos_blocks: (i % pb, 0)))
            args.append(t)
    out_specs, out_shapes = [], []
    for (_, count, dtype, _) in groups:
        out_specs.append(pl.BlockSpec((tm, count * tn), lambda i: (i, 0)))
        out_shapes.append(jax.ShapeDtypeStruct((M, count * tn), dtype))
    kern = functools.partial(_norm_proj_kernel, groups=tuple(groups), tn=tn,
                             rope_q=rope_q if rope is not None else None, rope_k=rope_k, q_scale=q_scale)
    return pl.pallas_call(
        kern,
        grid=(M // tm,),
        in_specs=in_specs,
        out_specs=out_specs,
        out_shape=out_shapes,
        compiler_params=_cparams(("parallel",)),
        name="norm_proj",
    )(*args)


def _out_proj_kernel(*refs, n_in):
    a_refs = refs[:n_in]
    w_ref, x_ref, g_ref, gn_ref, o_ref, on_ref, wb_ref = refs[n_in:]

    @pl.when(pl.program_id(0) == 0)
    def _():
        wb_ref[...] = w_ref[...].astype(BF16)

    tm = x_ref.shape[0]
    half = tm // 2 if tm % (4 * SUBLANES) == 0 else tm
    for r0 in range(0, tm, half):
        rs = slice(r0, r0 + half)
        y, row = None, 0
        for a_ref in a_refs:
            kk = a_ref.shape[1]
            d = jnp.dot(a_ref[rs, :].astype(BF16), wb_ref[row:row + kk, :], preferred_element_type=F32)
            y = d if y is None else y + d
            row += kk
        x1 = x_ref[rs, :] + _rms(y, g_ref[...])
        o_ref[rs, :] = x1
        on_ref[rs, :] = _rms(x1, gn_ref[...]).astype(on_ref.dtype)


def _out_proj(acts, w, x, g, g_next, tm):
    M, D = x.shape
    n_in = len(acts)
    assert sum(a.shape[1] for a in acts) == w.shape[0]
    row_tile = pl.BlockSpec((tm, D), lambda i: (i, 0))
    gain = pl.BlockSpec((1, D), lambda i: (0, 0))
    in_specs = [pl.BlockSpec((tm, a.shape[1]), lambda i: (i, 0)) for a in acts]
    in_specs += [pl.BlockSpec(w.shape, lambda i: (0, 0)), row_tile, gain, gain]
    return pl.pallas_call(
        functools.partial(_out_proj_kernel, n_in=n_in),
        grid=(M // tm,),
        in_specs=in_specs,
        out_specs=[row_tile, row_tile],
        out_shape=[jax.ShapeDtypeStruct((M, D), F32), jax.ShapeDtypeStruct((M, D), BF16)],
        scratch_shapes=[pltpu.VMEM(w.shape, BF16)],
        compiler_params=_cparams(("arbitrary",)),
        name="out_proj",
    )(*acts, w, x, g, g_next)


def _ffn_kernel(x_ref, xn_ref, wu_ref, wd_ref, g2_ref, o_ref, acc_ref):
    f = pl.program_id(1)

    @pl.when(f == 0)
    def _():
        acc_ref[...] = jnp.zeros_like(acc_ref)

    hid = jnp.dot(xn_ref[...], wu_ref[...].astype(BF16), preferred_element_type=F32)
    hid = jnp.square(jnp.maximum(hid, 0.0))
    acc_ref[...] += jnp.dot(hid.astype(BF16), wd_ref[...].astype(BF16), preferred_element_type=F32)

    @pl.when(f == pl.num_programs(1) - 1)
    def _():
        o_ref[...] = x_ref[...] + _rms(acc_ref[...], g2_ref[...])


def _ffn(x, xn, wu, wd, layer, g2, tm, tf):
    M, D = x.shape
    FF = wu.shape[2]
    return pl.pallas_call(
        _ffn_kernel,
        grid=(M // tm, FF // tf),
        in_specs=[
            pl.BlockSpec((tm, D), lambda i, f: (i, 0)),
            pl.BlockSpec((tm, D), lambda i, f: (i, 0)),
            pl.BlockSpec((None, D, tf), lambda i, f: (layer, 0, f)),
            pl.BlockSpec((None, tf, D), lambda i, f: (layer, f, 0)),
            pl.BlockSpec((1, D), lambda i, f: (0, 0)),
        ],
        out_specs=pl.BlockSpec((tm, D), lambda i, f: (i, 0)),
        out_shape=jax.ShapeDtypeStruct((M, D), F32),
        scratch_shapes=[pltpu.VMEM((tm, D), F32)],
        compiler_params=_cparams(("parallel", "arbitrary")),
        name="ffn",
    )(x, xn, wu, wd, g2)


def _lru_gates(xc, wa_ref, ba, wx_ref, bx, lam):
    xcb = xc.astype(BF16)
    ng = wa_ref.shape[0]
    gw = wa_ref.shape[1]
    ra, ri = [], []
    for g in range(ng):
        xs = xcb[:, g * gw:(g + 1) * gw]
        ra.append(jnp.dot(xs, wa_ref[g], preferred_element_type=F32))
        ri.append(jnp.dot(xs, wx_ref[g], preferred_element_type=F32))
    r = jax.nn.sigmoid(jnp.concatenate(ra, axis=1) + ba)
    i = jax.nn.sigmoid(jnp.concatenate(ri, axis=1) + bx)
    log_a = -LRU_C * r * _softplus(-lam)
    a = jnp.exp(log_a)
    om = 1.0 - a * a
    b = jnp.where(om > 0.0, om * lax.rsqrt(om), 0.0) * (i * xc)
    return a, b


def _conv4_rows(xe, halo, w, b):
    a = w[0:1, :] * xe
    for j in range(1, CONV_W):
        a = pltpu.roll(a, 1, 0) + w[j:j + 1, :] * xe
    return b + a[halo:]


def _conv4(tail_ref, x, first, w_ref, b):
    @pl.when(first)
    def _():
        tail_ref[...] = jnp.zeros_like(tail_ref)

    xe = jnp.concatenate([tail_ref[...], x], axis=0)
    tail_ref[...] = x[x.shape[0] - SUBLANES:, :]
    return _conv4_rows(xe, SUBLANES, w_ref[...], b)


def _rglru_prompt_kernel(xa_ref, ga_ref, cw_ref, cb_ref, wa_ref, ba_ref, wx_ref, bx_ref, lam_ref,
                         ya_ref, hl_ref, xe_ref, hc_ref, a_s, b_s, h_s):
    t = pl.program_id(1)
    T = xa_ref.shape[0]

    @pl.when(t == 0)
    def _():
        hc_ref[...] = jnp.zeros_like(hc_ref)

    xc = _conv4(xe_ref, xa_ref[...], t == 0, cw_ref, cb_ref[...])
    a, b = _lru_gates(xc, wa_ref, ba_ref[...], wx_ref, bx_ref[...], lam_ref[...])
    a_s[...] = a
    b_s[...] = b

    def body(s, h):
        h = a_s[pl.ds(s, 1), :] * h + b_s[pl.ds(s, 1), :]
        h_s[pl.ds(s, 1), :] = h
        return h

    h = lax.fori_loop(0, T, body, hc_ref[...], unroll=8)
    hc_ref[...] = h
    ya_ref[...] = (h_s[...] * _gelu_tanh(ga_ref[...])).astype(ya_ref.dtype)

    @pl.when(t == pl.num_programs(1) - 1)
    def _():
        hl_ref[...] = h


def _rglru_prompt(xaga, cw, cb, wa_bd, ba, wx_bd, bx, lam, ts):
    B, S, _ = xaga.shape
    C = cw.shape[-1]
    wspec = lambda a: pl.BlockSpec(a.shape, lambda b, t: (0,) * a.ndim)
    return pl.pallas_call(
        _rglru_prompt_kernel,
        grid=(B, S // ts),
        in_specs=[
            pl.BlockSpec((None, ts, C), lambda b, t: (b, t, 0)),
            pl.BlockSpec((None, ts, C), lambda b, t: (b, t, 1)),
            wspec(cw), wspec(cb), wspec(wa_bd), wspec(ba), wspec(wx_bd), wspec(bx), wspec(lam),
        ],
        out_specs=[
            pl.BlockSpec((None, ts, C), lambda b, t: (b, t, 0)),
            pl.BlockSpec((None, 1, C), lambda b, t: (b, 0, 0)),
        ],
        out_shape=[jax.ShapeDtypeStruct((B, S, C), BF16), jax.ShapeDtypeStruct((B, 1, C), F32)],
        scratch_shapes=[pltpu.VMEM((SUBLANES, C), F32), pltpu.VMEM((1, C), F32),
                        pltpu.VMEM((ts, C), F32), pltpu.VMEM((ts, C), F32), pltpu.VMEM((ts, C), F32)],
        compiler_params=_cparams(("parallel", "arbitrary")),
        name="rglru_prompt",
    )(xaga, xaga, cw, cb, wa_bd, ba, wx_bd, bx, lam)


def _rglru_sample_kernel(xa_ref, ga_ref, st_ref, h0_ref, cw_ref, cb_ref, wa_ref, ba_ref, wx_ref, bx_ref,
                         lam_ref, ya_ref, hl_ref):
    T = xa_ref.shape[0]
    rows = [st_ref[j] for j in range(CONV_W - 1)] + [xa_ref[s] for s in range(T)]
    h = h0_ref[...]
    for s in range(T):
        y = cb_ref[...]
        for j in range(CONV_W):
            y = y + cw_ref[j:j + 1, :] * rows[s + j]
        a, b = _lru_gates(y, wa_ref, ba_ref[...], wx_ref, bx_ref[...], lam_ref[...])
        h = a * h + b
        ya_ref[s] = (h * _gelu_tanh(ga_ref[s])).astype(ya_ref.dtype)
    hl_ref[...] = h


def _rglru_sample(xa_t, ga_t, st_t, h0, cw, cb, wa_bd, ba, wx_bd, bx, lam):
    T, B, C = xa_t.shape
    args = (xa_t, ga_t, st_t, h0, cw, cb, wa_bd, ba, wx_bd, bx, lam)
    return pl.pallas_call(
        _rglru_sample_kernel,
        grid=(1,),
        in_specs=[pl.BlockSpec(a.shape, lambda i, n=a.ndim: (0,) * n) for a in args],
        out_specs=[pl.BlockSpec((T, B, C), lambda i: (0, 0, 0)), pl.BlockSpec((B, C), lambda i: (0, 0))],
        out_shape=[jax.ShapeDtypeStruct((T, B, C), F32), jax.ShapeDtypeStruct((B, C), F32)],
        compiler_params=_cparams(("arbitrary",)),
        name="rglru_sample",
    )(*args)


def _attn_prompt_kernel(q_ref, k_ref, v_ref, o_ref, *scratch, groups, qb):
    S = q_ref.shape[0]
    E = q_ref.shape[1]
    dist2 = (lax.broadcasted_iota(jnp.int32, (qb, 2 * qb), 0) + qb
             - lax.broadcasted_iota(jnp.int32, (qb, 2 * qb), 1))
    dist1 = lax.broadcasted_iota(jnp.int32, (qb, qb), 0) - lax.broadcasted_iota(jnp.int32, (qb, qb), 1)

    def rows(start, d):
        return pl.ds(start, qb) if d == 1 else pl.ds(start, qb, stride=d)

    ng = len(groups)
    s_scr, p_scr = scratch[3 * ng], scratch[3 * ng + 1]
    dn_t = (((1,), (1,)), ((), ()))

    for gi, (w, d) in enumerate(groups):
        back = w // d
        nb = (S // d) // qb
        nblk = nb * d
        two = nb > 1
        kw = 2 * qb if two else qb
        dist = dist2 if two else dist1
        lim_all = jnp.full((qb, kw), back, jnp.int32)
        lim_first = jnp.minimum(lax.broadcasted_iota(jnp.int32, (qb, kw), 0), back)
        m_s, l_s, acc_s = scratch[3 * gi:3 * gi + 3]

        def locate(idx, d=d):
            if d == 1:
                n = idx
                start = pl.multiple_of(idx * qb, qb)
                prev = pl.multiple_of(jnp.maximum(idx - 1, 0) * qb, qb)
            else:
                n = idx // d
                start = n * (qb * d) + idx % d
                prev = jnp.where(n > 0, start - qb * d, start)
            return start, prev, n

        def gather(ref, start, prev, d=d, two=two):
            if two:
                return jnp.concatenate([ref[rows(prev, d), :], ref[rows(start, d), :]], axis=0).astype(BF16)
            return ref[rows(start, d), :].astype(BF16)

        def scores(idx, c, kw=kw, locate=locate, gather=gather, d=d):
            start, prev, _ = locate(idx)
            q = q_ref[rows(start, d), :].astype(BF16)
            s_scr[pl.ds(pl.multiple_of(idx * qb, qb), qb), 0:kw] = lax.dot_general(
                q, gather(k_ref, start, prev), dn_t, preferred_element_type=F32)
            return c

        def softmax(idx, c, kw=kw, locate=locate, d=d, two=two, dist=dist, lim_all=lim_all,
                    lim_first=lim_first, m_s=m_s):
            start, _, n = locate(idx)
            blk = pl.ds(pl.multiple_of(idx * qb, qb), qb)
            lim = jnp.where(n > 0, lim_all, lim_first) if two else lim_all
            valid = (dist >= 0) & (dist <= lim)
            s = jnp.where(valid, s_scr[blk, 0:kw], NEG_INF)
            m_b = jnp.max(s, axis=-1, keepdims=True)
            p_scr[blk, 0:kw] = jnp.exp(s - m_b).astype(BF16)
            m_s[rows(start, d), :] = jnp.broadcast_to(m_b, (qb, E))
            return c

        def values(idx, c, kw=kw, locate=locate, gather=gather, d=d, acc_s=acc_s, l_s=l_s):
            start, prev, _ = locate(idx)
            p = p_scr[pl.ds(pl.multiple_of(idx * qb, qb), qb), 0:kw]
            v1 = jnp.concatenate([gather(v_ref, start, prev), jnp.ones((kw, E), BF16)], axis=1)
            r = jnp.dot(p, v1, preferred_element_type=F32)
            acc_s[rows(start, d), :] = r[:, :E]
            l_s[rows(start, d), :] = r[:, E:]
            return c

        lax.fori_loop(0, nblk, scores, 0, unroll=True)
        lax.fori_loop(0, nblk, softmax, 0, unroll=True)
        lax.fori_loop(0, nblk, values, 0, unroll=True)

    ng = len(groups)

    def merge(i, c):
        rs = pl.ds(pl.multiple_of(i * qb, qb), qb)
        ms = [scratch[3 * g][rs, :] for g in range(ng)]
        m = functools.reduce(jnp.maximum, ms)
        num = den = None
        for g in range(ng):
            wg = jnp.exp(ms[g] - m)
            n_g = wg * scratch[3 * g + 2][rs, :]
            d_g = wg * scratch[3 * g + 1][rs, :]
            num = n_g if num is None else num + n_g
            den = d_g if den is None else den + d_g
        o_ref[rs, :] = (num / den).astype(o_ref.dtype)
        return c

    lax.fori_loop(0, S // qb, merge, 0, unroll=2)


def _attn_prompt(q, q_off, k, v, groups, qb):
    B, S, D = k.shape
    nh = D // LANES
    for (w, d) in groups:
        assert S % (d * qb) == 0 and w // d <= qb
    spec = pl.BlockSpec((None, S, LANES), lambda b, h: (b, 0, h))
    q_spec = pl.BlockSpec((None, S, LANES), lambda b, h: (b, 0, q_off + h))
    return pl.pallas_call(
        functools.partial(_attn_prompt_kernel, groups=tuple(groups), qb=qb),
        grid=(B, nh),
        in_specs=[q_spec, spec, spec],
        out_specs=spec,
        out_shape=jax.ShapeDtypeStruct((B, S, D), BF16),
        scratch_shapes=[pltpu.VMEM((S, LANES), F32)] * (3 * len(groups))
        + [pltpu.VMEM((S, 2 * qb), F32), pltpu.VMEM((S, 2 * qb), BF16)],
        compiler_params=_cparams(("parallel", "parallel")),
        name="attn_prompt",
    )(q, k, v)


def _attn_sample_kernel(q_ref, kn_ref, vn_ref, kt_ref, vt_ref, ks_ref, vs_ref, mt_ref, ms_ref, mn_ref, o_ref):
    nq, hd = q_ref.shape
    qb = jnp.concatenate([q_ref[...], jnp.zeros((LANES - nq, hd), F32)], axis=0).astype(BF16)
    nn = kn_ref.shape[0]
    zn = jnp.zeros((LANES - nn, hd), F32)
    key_sets = (
        (kt_ref[...], vt_ref[...], mt_ref[...]),
        (ks_ref[...].reshape(-1, hd), vs_ref[...].reshape(-1, hd), ms_ref[...]),
        (jnp.concatenate([kn_ref[...], zn], axis=0), jnp.concatenate([vn_ref[...], zn], axis=0), mn_ref[...]),
    )
    scored = []
    for k, v, bias in key_sets:
        s = lax.dot_general(k.astype(BF16), qb, (((1,), (1,)), ((), ())), preferred_element_type=F32)
        scored.append((s + bias, v))
    m = functools.reduce(jnp.maximum, [jnp.max(s, axis=0, keepdims=True) for s, _ in scored])
    m = jnp.where(m == NEG_INF, 0.0, m)
    acc = None
    for s, v in scored:
        p = jnp.exp(s - m).astype(BF16)
        v1 = jnp.concatenate([v.astype(BF16), jnp.ones(v.shape, BF16)], axis=1)
        r = lax.dot_general(p, v1, (((0,), (0,)), ((), ())), preferred_element_type=F32)
        acc = r if acc is None else acc + r
    o_ref[...] = acc[:nq, :hd] / acc[:nq, hd:]


def _attn_sample(q, kn, vn, kc, vc, T, nh):
    B, nq, hd = q.shape
    wb = kc.shape[1] // nh
    far = max(DILATIONS, key=lambda wd: wd[0])
    near = [g for g in DILATIONS if g != far]
    R = max(w for w, _ in near)
    d_far = far[1]
    assert near and wb % R == 0 and wb % d_far == 0 and T <= d_far and nq == T * nh and nq <= LANES
    A = wb // d_far

    def mult(delta, groups):
        m = np.zeros(delta.shape, np.float32)
        for (w, d) in groups:
            m = m + ((delta >= 0) & (delta % d == 0) & (delta <= (w // d) * d))
        return m

    def table(pos_of_row, head_of_row, groups):
        lane = np.arange(LANES)
        t, hq = lane // nh, lane % nh
        delta = (wb + t)[None, :] - pos_of_row[:, None]
        tab = mult(delta, groups) * (head_of_row[:, None] == hq[None, :]) * (t < T)[None, :]
        return jnp.asarray(np.where(tab > 0, np.log(np.maximum(tab, 1.0)), NEG_INF), F32)

    rt = np.arange(R * nh)
    mt = table(wb - R + rt // nh, rt % nh, near)
    rs = np.arange(A * T * nh)
    ms = table((rs // (T * nh)) * d_far + (rs % (T * nh)) // nh, rs % nh, [far])
    rn = np.arange(LANES)
    mn = table(np.where(rn < nq, wb + rn // nh, -10 ** 9), rn % nh, DILATIONS)

    small = pl.BlockSpec((None, nq, hd), lambda b: (b, 0, 0))
    tail = pl.BlockSpec((None, R * nh, hd), lambda b: (b, wb // R - 1, 0))
    strided = pl.BlockSpec((None, A, T * nh, hd), lambda b: (b, 0, 0, 0))
    const = lambda a: pl.BlockSpec(a.shape, lambda b: (0, 0))
    kc4 = kc.reshape(B, A, d_far * nh, hd)
    vc4 = vc.reshape(B, A, d_far * nh, hd)
    return pl.pallas_call(
        _attn_sample_kernel,
        grid=(B,),
        in_specs=[small, small, small, tail, tail, strided, strided, const(mt), const(ms), const(mn)],
        out_specs=small,
        out_shape=jax.ShapeDtypeStruct((B, nq, hd), F32),
        compiler_params=_cparams(("parallel",)),
        name="attn_sample",
    )(q, kn, vn, kc, vc, kc4, vc4, mt, ms, mn)


def _mlstm_qkv_gates(xc, xm, wq_ref, wk_ref, wv_ref, wg_ref, bg_ref, q_ref, k_ref, v_ref, xc_ref, g_ref, k_scale):
    xcb = xc.astype(BF16)
    xmb = xm.astype(BF16)
    ng, gw = wq_ref.shape[0], wq_ref.shape[1]
    qs, ks, vs = [], [], []
    for g in range(ng):
        sl = slice(g * gw, (g + 1) * gw)
        qs.append(jnp.dot(xcb[:, sl], wq_ref[g], preferred_element_type=F32))
        ks.append(jnp.dot(xcb[:, sl], wk_ref[g], preferred_element_type=F32))
        vs.append(jnp.dot(xmb[:, sl], wv_ref[g], preferred_element_type=F32))
    k = jnp.concatenate(ks, axis=1)
    qb = jnp.concatenate(qs, axis=1).astype(BF16)
    vb = jnp.concatenate(vs, axis=1).astype(BF16)
    qkv = jnp.concatenate([qb, k.astype(BF16), vb], axis=1)
    g_ref[...] = jnp.dot(qkv, wg_ref[...], preferred_element_type=F32) + bg_ref[...]
    q_ref[...] = qb
    k_ref[...] = (k * k_scale).astype(k_ref.dtype)
    v_ref[...] = vb
    xc_ref[...] = xcb


def _mlstm_pre_sample_kernel(xm_ref, st_ref, cw_ref, cb_ref, wq_ref, wk_ref, wv_ref, wg_ref, bg_ref,
                             q_ref, k_ref, v_ref, xc_ref, g_ref, *, k_scale):
    T = xm_ref.shape[0]
    rows = [st_ref[j] for j in range(CONV_W - 1)] + [xm_ref[s] for s in range(T)]
    conv = []
    for s in range(T):
        y = cb_ref[...]
        for j in range(CONV_W):
            y = y + cw_ref[j:j + 1, :] * rows[s + j]
        conv.append(y)
    xc = _silu(jnp.concatenate(conv, axis=0))
    xm = jnp.concatenate(rows[CONV_W - 1:], axis=0)
    _mlstm_qkv_gates(xc, xm, wq_ref, wk_ref, wv_ref, wg_ref, bg_ref, q_ref, k_ref, v_ref, xc_ref, g_ref, k_scale)


def _mlstm_pre_sample(xm_t, st_t, cw, cb, wq_bd, wk_bd, wv_bd, wg, bg, k_scale):
    T, B, C = xm_t.shape
    args = (xm_t, st_t, cw, cb, wq_bd, wk_bd, wv_bd, wg, bg)
    full = lambda shape: pl.BlockSpec(shape, lambda i: (0,) * len(shape))
    return pl.pallas_call(
        functools.partial(_mlstm_pre_sample_kernel, k_scale=k_scale),
        grid=(1,),
        in_specs=[full(a.shape) for a in args],
        out_specs=[full((T * B, C))] * 4 + [full((T * B, 2 * LANES))],
        out_shape=[jax.ShapeDtypeStruct((T * B, C), BF16)] * 4 + [jax.ShapeDtypeStruct((T * B, 2 * LANES), F32)],
        compiler_params=_cparams(("arbitrary",)),
        name="mlstm_pre_sample",
    )(*args)


CONV_HALO = 16


def _mlstm_in_kernel(x_ref, xh_ref, g_ref, w_ref, cw_ref, cb_ref, wq_ref, wk_ref, wv_ref, wg_ref, bg_ref,
                     q_ref, k_ref, v_ref, xc_ref, gz_ref, gt_ref, tail_ref, *, tiles_per_seq, tn, k_scale):
    tm = x_ref.shape[0]
    C = cw_ref.shape[1]
    first = (pl.program_id(0) % tiles_per_seq) == 0
    g = g_ref[...]
    xn = _rms(x_ref[...], g)
    xe = jnp.concatenate([_rms(xh_ref[...], g), xn], axis=0).astype(BF16)
    xn = xn.astype(BF16)
    dead_halo = lax.broadcasted_iota(jnp.int32, (CONV_HALO + tm, tn), 0) < jnp.where(first, CONV_HALO, 0)
    xcs, xms = [], []
    for jj in range(C // tn):
        cs = slice(jj * tn, (jj + 1) * tn)
        xm_e = jnp.dot(xe, w_ref[:, cs], preferred_element_type=F32)
        xm_e = jnp.where(dead_halo, 0.0, xm_e)
        xcs.append(_silu(_conv4_rows(xm_e, CONV_HALO, cw_ref[:, cs], cb_ref[:, cs])))
        xms.append(xm_e[CONV_HALO:])
        gz_ref[:, cs] = _silu(jnp.dot(xn, w_ref[:, C + jj * tn:C + (jj + 1) * tn],
                                      preferred_element_type=F32)).astype(gz_ref.dtype)
    xc = jnp.concatenate(xcs, axis=1)
    xm = jnp.concatenate(xms, axis=1)
    tail_ref[...] = xm[tm - SUBLANES:, :]
    _mlstm_qkv_gates(xc, xm, wq_ref, wk_ref, wv_ref, wg_ref, bg_ref, q_ref, k_ref, v_ref, xc_ref, gt_ref, k_scale)


def _mlstm_in(x, g, w, cw, cb, wq_bd, wk_bd, wv_bd, wg, bg, seq, tm, k_scale):
    M, K = x.shape
    C = cw.shape[1]
    assert seq % tm == 0 and tm % CONV_HALO == 0
    hb = tm // CONV_HALO
    const = lambda a: pl.BlockSpec(a.shape, lambda i: (0,) * a.ndim)
    act = pl.BlockSpec((tm, C), lambda i: (i, 0))
    return pl.pallas_call(
        functools.partial(_mlstm_in_kernel, tiles_per_seq=seq // tm, tn=1024, k_scale=k_scale),
        grid=(M // tm,),
        in_specs=[pl.BlockSpec((tm, K), lambda i: (i, 0)),
                  pl.BlockSpec((CONV_HALO, K), lambda i: (jnp.maximum(i * hb - 1, 0), 0)),
                  const(g), const(w), const(cw), const(cb), const(wq_bd), const(wk_bd), const(wv_bd),
                  const(wg), const(bg)],
        out_specs=[act, act, act, act, act, pl.BlockSpec((tm, 2 * LANES), lambda i: (i, 0)),
                   pl.BlockSpec((None, SUBLANES, C), lambda i: (i, 0, 0))],
        out_shape=[jax.ShapeDtypeStruct((M, C), BF16)] * 5 + [jax.ShapeDtypeStruct((M, 2 * LANES), F32),
                                                              jax.ShapeDtypeStruct((M // tm, SUBLANES, C), F32)],
        compiler_params=_cparams(("parallel",)),
        name="mlstm_in",
    )(x, x, g, w, cw, cb, wq_bd, wk_bd, wv_bd, wg, bg)


def _mlstm_core_kernel(*refs, c, n_valid, nh, zero_init):
    q_ref, k_ref, v_ref, xc_ref, gz_ref, g_ref, nw_ref, sk_ref = refs[:8]
    pos = 8
    if not zero_init:
        c0_ref, n0_ref, m0_ref = refs[8:11]
        pos = 11
    y_ref, C_ref, n_ref, m_ref = refs[pos:pos + 4]
    ch = pl.program_id(1)
    c_in = q_ref.shape[0]
    DH = q_ref.shape[1] // nh

    @pl.when(ch == 0)
    def _():
        if zero_init:
            C_ref[...] = jnp.zeros_like(C_ref)
            n_ref[...] = jnp.zeros_like(n_ref)
            m_ref[...] = jnp.zeros_like(m_ref)
        else:
            C_ref[...] = c0_ref[...]
            n_ref[...] = n0_ref[...]
            m_ref[...] = m0_ref[...]

    def padded(x):
        if c_in == c:
            return x
        return jnp.concatenate([x, jnp.zeros((c - c_in, x.shape[1]), x.dtype)], axis=0)

    g = padded(g_ref[...])
    row = lax.broadcasted_iota(jnp.int32, (c, LANES), 0)
    valid = row < n_valid
    gi = jnp.where(valid, g[:, :LANES], NEG_INF)
    lf = jnp.where(valid, -_softplus(-g[:, LANES:]), 0.0)
    tri_f = (lax.broadcasted_iota(jnp.int32, (c, c), 1) <= lax.broadcasted_iota(jnp.int32, (c, c), 0))
    tri_b = tri_f.astype(BF16)
    bcum, rest = None, lf
    for _ in range(3):
        part = rest.astype(BF16)
        rest = rest - part.astype(F32)
        d = jnp.dot(tri_b, part, preferred_element_type=F32)
        bcum = d if bcum is None else bcum + d
    bcum_t = bcum.T
    gi_t = gi.T
    causal = tri_f

    for h in range(nh):
        sl = slice(h * DH, (h + 1) * DH)
        qh = padded(q_ref[:, sl])
        kh = padded(k_ref[:, sl])
        vh = padded(v_ref[:, sl])
        bc_col = bcum[:, h:h + 1]
        bc_row = bcum_t[h:h + 1, :]
        i_row = gi_t[h:h + 1, :]
        i_col = gi[:, h:h + 1]
        m_prev = m_ref[h:h + 1, 0:1]
        log_intra = jnp.where(causal, bc_col - bc_row + i_row, NEG_INF)
        log_inter = bc_col + m_prev
        m_t = jnp.maximum(log_inter, jnp.max(log_intra, axis=1, keepdims=True))
        qk = lax.dot_general(qh, kh, (((1,), (1,)), ((), ())), preferred_element_type=F32)
        w_intra = jnp.exp(log_intra - m_t) * qk
        w_inter = jnp.exp(log_inter - m_t)
        C_old = C_ref[h]
        n_old = n_ref[h:h + 1, :]
        num = jnp.dot(w_intra.astype(BF16), vh, preferred_element_type=F32)
        num = num + w_inter * jnp.dot(qh, C_old.astype(BF16), preferred_element_type=F32)
        n8 = jnp.broadcast_to(n_old, (SUBLANES, DH)).astype(BF16)
        qn = lax.dot_general(qh, n8, (((1,), (1,)), ((), ())), preferred_element_type=F32)[:, 0:1]
        den = jnp.sum(w_intra, axis=1, keepdims=True) + w_inter * qn
        hh = num * (1.0 / jnp.maximum(jnp.abs(den), jnp.exp(-m_t)))
        m_new = m_t[c - 1:c, :]
        bc_last = bc_col[c - 1:c, :]
        w_state = jnp.exp(bc_last - bc_col + i_col - m_new)
        decay = jnp.exp(bc_last + m_prev - m_new)
        kw = kh.astype(F32) * w_state
        C_ref[h] = decay * C_old + lax.dot_general(
            kw.astype(BF16), vh, (((0,), (0,)), ((), ())), preferred_element_type=F32)
        n_ref[h:h + 1, :] = decay * n_old + jnp.sum(kw, axis=0, keepdims=True)
        m_ref[h:h + 1, :] = jnp.broadcast_to(m_new, (1, LANES))
        hn = hh * lax.rsqrt(jnp.mean(hh * hh, axis=-1, keepdims=True) + EPS)
        hn = hn[:c_in] * nw_ref[:, sl]
        y = (hn + sk_ref[:, sl] * xc_ref[:, sl].astype(F32)) * gz_ref[:, sl].astype(F32)
        y_ref[:, sl] = y.astype(y_ref.dtype)


def _mlstm_core(q, k, v, xc, gz, gates, nw, sk, state, c, n_valid, nh):
    B, S, C = q.shape
    c_in = min(c, S)
    DH = C // nh
    zero_init = state is None
    act = pl.BlockSpec((None, c_in, C), lambda b, t: (b, t, 0))
    in_specs = [act, act, act, act, act,
                pl.BlockSpec((None, c_in, 2 * LANES), lambda b, t: (b, t, 0)),
                pl.BlockSpec((1, C), lambda b, t: (0, 0)), pl.BlockSpec((1, C), lambda b, t: (0, 0))]
    args = [q, k, v, xc, gz, gates, nw, sk]
    C_spec = pl.BlockSpec((None, nh, DH, DH), lambda b, t: (b, 0, 0, 0))
    n_spec = pl.BlockSpec((None, nh, DH), lambda b, t: (b, 0, 0))
    m_spec = pl.BlockSpec((None, nh, LANES), lambda b, t: (b, 0, 0))
    if not zero_init:
        in_specs += [C_spec, n_spec, m_spec]
        args += list(state)
    return pl.pallas_call(
        functools.partial(_mlstm_core_kernel, c=c, n_valid=n_valid, nh=nh, zero_init=zero_init),
        grid=(B, S // c_in),
        in_specs=in_specs,
        out_specs=[act, C_spec, n_spec, m_spec],
        out_shape=[jax.ShapeDtypeStruct((B, S, C), BF16),
                   jax.ShapeDtypeStruct((B, nh, DH, DH), F32),
                   jax.ShapeDtypeStruct((B, nh, DH), F32),
                   jax.ShapeDtypeStruct((B, nh, LANES), F32)],
        compiler_params=_cparams(("parallel", "arbitrary")),
        name="mlstm_core",
    )(*args)


def _rope_tables(pos, hd):
    rot = hd // 4
    half = rot // 2
    inv = ROPE_THETA ** (-np.arange(0, rot, 2, dtype=np.float64) / rot)
    ang = pos.astype(np.float64)[:, None] * inv[None, :]
    cos, sin = np.cos(ang), np.sin(ang)
    n = pos.shape[0]
    c = np.concatenate([cos, cos, np.ones((n, hd - rot))], axis=1)
    sa = np.concatenate([-sin, np.zeros((n, hd - half))], axis=1)
    sb = np.concatenate([np.zeros((n, half)), sin, np.zeros((n, hd - rot))], axis=1)
    return tuple(jnp.asarray(t, F32) for t in (c, sa, sb))


def _regroup_block_diag(w, group):
    nb, bi, bo = w.shape
    per = group // bi
    a = w.reshape(nb // per, per * bi, bo)
    cols = np.arange(per * bo)
    tile_cols = jnp.asarray(cols[None, :] % bo == np.arange(bo)[:, None], BF16)
    same_block = jnp.asarray((np.arange(per * bi)[:, None] // bi) == (cols[None, :] // bo), BF16)
    tiled = jnp.einsum("gro,oc->grc", a.astype(BF16), tile_cols, preferred_element_type=F32)
    return tiled.astype(BF16) * same_block


def _layer0(x, rope, p, prompt, state):
    Bg, L, D = x.shape
    M = Bg * L
    hd = D // H_B
    tm = _tile(M, 1024)
    tm_in = _tile(rope[0].shape[0], 512)
    x2 = x.reshape(M, D)
    xagq, k, v = _norm_proj(x2, p["g_mix_pre0"], p["w_in_ab"], ((0, 3, F32, None), (3, 1, F32, None), (4, 1, F32, None)),
                            tm_in, D, rope=rope, rope_q=2, rope_k=3, q_scale=hd ** -0.5)
    xagq3 = xagq.reshape(Bg, L, 3 * D)
    lru_w = (p["conv_a_w"], p["conv_a_b"], p["wa_bd"], p["lru_ba"], p["wx_bd"], p["lru_bx"], p["lam"])
    if prompt:
        ya, h_last = _rglru_prompt(xagq3, *lru_w, ts=_tile(L, 512))
        ya = ya.reshape(M, D)
        h_last = h_last.reshape(Bg, D)
        new_conv = xagq3[:, L - (CONV_W - 1):, :D]
        yb = _attn_prompt(xagq3, 2 * D // LANES, k.reshape(Bg, L, D), v.reshape(Bg, L, D),
                          sorted(DILATIONS, key=lambda wd: -wd[1]), QBLOCK).reshape(M, D)
        wb = min(MAX_WINDOW, L)
        new_k = k.reshape(Bg, L, H_B, hd)[:, L - wb:]
        new_v = v.reshape(Bg, L, H_B, hd)[:, L - wb:]
    else:
        conv0, h0, k_buf, v_buf = state
        xagq_t = xagq3.transpose(1, 0, 2)
        xa_t, ga_t = xagq_t[:, :, :D], xagq_t[:, :, D:2 * D]
        st_t = conv0.transpose(1, 0, 2)
        ya_t, h_last = _rglru_sample(xa_t, ga_t, st_t, h0, *lru_w)
        ya = ya_t.transpose(1, 0, 2).reshape(M, D)
        new_conv = jnp.concatenate([conv0, xagq3[:, :, :D]], axis=1)[:, -(CONV_W - 1):]
        wb = k_buf.shape[1]
        native = lambda a: a.reshape(Bg, -1, hd)
        yb = _attn_sample(native(xagq3[:, :, 2 * D:]), native(k), native(v), native(k_buf), native(v_buf),
                          L, H_B).reshape(M, D)
        new_k = k.reshape(Bg, L, H_B, hd)
        new_v = v.reshape(Bg, L, H_B, hd)
    x1, x1n = _out_proj([ya, yb], p["w_out_ab"], x2, p["g_mix_post0"], p["g_ffn_pre0"], _tile(M, 512))
    x2o = _ffn(x1, x1n, p["w_up"], p["w_dn"], 0, p["g_ffn_post0"], _tile(M, 1024), 1024)
    return x2o.reshape(Bg, L, D), (new_conv, h_last, new_k, new_v)


def _layer1(x, p, prompt, state):
    Bg, L, D = x.shape
    M = Bg * L
    x2 = x.reshape(M, D)
    tm = _tile(M, 1024)
    DC = p["w_in_c"].shape[1] // 2
    DH = DC // NH_C
    pre_w = (p["conv_c_w"], p["conv_c_b"], p["wq_bd"], p["wk_bd"], p["wv_bd"], p["wg"], p["bg"])
    if prompt:
        tm_in = _tile(L, 256)
        c = _tile(L, 256)
        *acts, tails = _mlstm_in(x2, p["g_mix_pre1"], p["w_in_c"], *pre_w, seq=L, tm=tm_in, k_scale=DH ** -0.5)
        q, k, v, xc, gz, gates = [a.reshape(Bg, L, -1) for a in acts]
        y, C, n, m = _mlstm_core(q, k, v, xc, gz, gates, p["mlstm_norm"], p["mlstm_skip"], None, c, c, NH_C)
        new_conv = tails.reshape(Bg, L // tm_in, SUBLANES, DC)[:, -1, SUBLANES - (CONV_W - 1):]
        y = y.reshape(M, DC)
    else:
        nt = DC // 1024
        xm, gz = _norm_proj(x2, p["g_mix_pre1"], p["w_in_c"], ((0, nt, F32, None), (nt, nt, BF16, _silu)),
                            _tile(M, 512), 1024)
        xm3 = xm.reshape(Bg, L, DC)
        gz3 = gz.reshape(Bg, L, DC)
        conv0, C0, n0, m0 = state
        lp = 16
        outs = _mlstm_pre_sample(xm3.transpose(1, 0, 2), conv0.transpose(1, 0, 2), *pre_w, k_scale=DH ** -0.5)
        batch_major = lambda a: jnp.pad(a.reshape(L, Bg, -1).transpose(1, 0, 2), ((0, 0), (0, lp - L), (0, 0)))
        q, k, v, xc, gates = [batch_major(a) for a in outs]
        gz_p = jnp.pad(gz3, ((0, 0), (0, lp - L), (0, 0)))
        m0b = jnp.broadcast_to(m0[:, :, None], m0.shape + (LANES,))
        y, C, n, m = _mlstm_core(q, k, v, xc, gz_p, gates, p["mlstm_norm"], p["mlstm_skip"],
                                 (C0, n0, m0b), 32, L, NH_C)
        new_conv = jnp.concatenate([conv0, xm3], axis=1)[:, -(CONV_W - 1):]
        y = y[:, :L].reshape(M, DC)
    x1, x1n = _out_proj([y], p["w_out_c"], x2, p["g_mix_post1"], p["g_ffn_pre1"], _tile(M, 512))
    x2o = _ffn(x1, x1n, p["w_up"], p["w_dn"], 1, p["g_ffn_post1"], _tile(M, 1024), 1024)
    return x2o.reshape(Bg, L, D), (new_conv, C, n, m[:, :, 0])


def kernel(x_prompt, x_sample, state_rglru_conv, state_rglru_h, cache_swa_k, cache_swa_v, state_mlstm_conv, state_mlstm_C, state_mlstm_n, state_mlstm_m, norm_mix_pre, norm_mix_post, norm_ffn_pre, norm_ffn_post, w_ffn_up, w_ffn_down, w_in_ab, conv_a_w, conv_a_b, lru_wa, lru_ba, lru_wx, lru_bx, lru_lambda, w_out_ab, w_in_c, conv_c_w, conv_c_b, mlstm_wq, mlstm_wk, mlstm_wv, mlstm_w_gate, mlstm_b_gate, mlstm_norm, mlstm_skip, w_out_c):
    B, S, D = x_prompt.shape
    Bs, Ts, _ = x_sample.shape
    DA = conv_a_w.shape[-1]
    DC = conv_c_w.shape[-1]
    hd = D // H_B
    row = lambda a: a.reshape(1, -1)
    lane_pad = lambda a: jnp.pad(a, ((0, 0), (0, LANES - NH_C)))
    wg = jnp.concatenate([lane_pad(mlstm_w_gate[0][:, :NH_C]), lane_pad(mlstm_w_gate[0][:, NH_C:])], axis=1)
    bg = jnp.concatenate([lane_pad(mlstm_b_gate[:, :NH_C]), lane_pad(mlstm_b_gate[:, NH_C:])], axis=1)
    p = {
        "g_mix_pre0": row(norm_mix_pre[0]), "g_mix_post0": row(norm_mix_post[0]),
        "g_ffn_pre0": row(norm_ffn_pre[0]), "g_ffn_post0": row(norm_ffn_post[0]),
        "g_mix_pre1": row(norm_mix_pre[1]), "g_mix_post1": row(norm_mix_post[1]),
        "g_ffn_pre1": row(norm_ffn_pre[1]), "g_ffn_post1": row(norm_ffn_post[1]),
        "w_up": w_ffn_up, "w_dn": w_ffn_down,
        "w_in_ab": w_in_ab[0].astype(BF16),
        "conv_a_w": conv_a_w[0], "conv_a_b": row(conv_a_b[0]),
        "wa_bd": _regroup_block_diag(lru_wa[0], MXU_DIM), "lru_ba": row(lru_ba[0]),
        "wx_bd": _regroup_block_diag(lru_wx[0], MXU_DIM), "lru_bx": row(lru_bx[0]),
        "lam": row(lru_lambda[0]),
        "w_out_ab": w_out_ab[0],
        "w_in_c": w_in_c[0].astype(BF16),
        "conv_c_w": conv_c_w[0], "conv_c_b": row(conv_c_b[0]),
        "wq_bd": _regroup_block_diag(mlstm_wq[0], MXU_DIM),
        "wk_bd": _regroup_block_diag(mlstm_wk[0], MXU_DIM),
        "wv_bd": _regroup_block_diag(mlstm_wv[0], MXU_DIM),
        "wg": wg.astype(BF16), "bg": bg,
        "mlstm_norm": row(mlstm_norm[0]), "mlstm_skip": row(mlstm_skip[0]),
        "w_out_c": w_out_c[0],
    }
    dt = state_rglru_conv.dtype

    xp, st0 = _layer0(x_prompt, _rope_tables(np.arange(S), hd), p, True, None)
    xp, st1 = _layer1(xp, p, True, None)
    pos_s = PAST_LEN + (np.arange(Bs * Ts) % Ts)
    xs, ss0 = _layer0(x_sample, _rope_tables(pos_s, hd), p, False,
                      (state_rglru_conv[0], state_rglru_h[0], cache_swa_k[0], cache_swa_v[0]))
    xs, ss1 = _layer1(xs, p, False, (state_mlstm_conv[0], state_mlstm_C[0], state_mlstm_n[0], state_mlstm_m[0]))

    lead = lambda a: a[None].astype(dt)
    return (xp, xs,
            lead(st0[0]), lead(ss0[0]), lead(st0[1]), lead(ss0[1]),
            lead(st0[2]), lead(ss0[2]), lead(st0[3]), lead(ss0[3]),
            lead(st1[0]), lead(ss1[0]), lead(st1[1]), lead(ss1[1]),
            lead(st1[2]), lead(ss1[2]), lead(st1[3]), lead(ss1[3]))
```

```python
import functools
import math

import numpy as np
import jax
import jax.numpy as jnp
from jax import lax
from jax.experimental import pallas as pl
from jax.experimental.pallas import tpu as pltpu

F32 = jnp.float32
BF16 = jnp.bfloat16

NA_BLOCKS = 16
CONV_W = 4
LRU_C = 8.0
H_B = 8
ROPE_THETA = 500000.0
DILATIONS = ((128, 1), (512, 4), (2048, 16))
MAX_WINDOW = 2048
QBLOCK = 128
NH_C = 4
QKV_BLOCK = 4
PAST_LEN = 16384
EPS = 1e-6

LANES = 128
SUBLANES = 8
MXU_DIM = 256
VMEM_LIMIT = 56 * 1024 * 1024

NEG_INF = float("-inf")


def _cparams(sem):
    return pltpu.CompilerParams(dimension_semantics=sem, vmem_limit_bytes=VMEM_LIMIT)


def _tile(n, pref):
    t = min(n, pref)
    while n % t:
        t -= 1
    return t


def _rms(x, g):
    return x * lax.rsqrt(jnp.mean(x * x, axis=-1, keepdims=True) + EPS) * g


def _softplus(z):
    return jnp.maximum(z, 0.0) + jnp.log1p(jnp.exp(-jnp.abs(z)))


def _gelu_tanh(x):
    c = math.sqrt(2.0 / math.pi)
    return x * (0.5 * (1.0 + jnp.tanh(c * (x + 0.044715 * (x * x * x)))))


def _silu(x):
    return x * jax.nn.sigmoid(x)


def _rope_tile(y, c, sa, sb):
    outs = []
    for h in range(y.shape[1] // LANES):
        yh = y[:, h * LANES:(h + 1) * LANES]
        outs.append(yh * c + pltpu.roll(yh, LANES - 16, 1) * sa + pltpu.roll(yh, 16, 1) * sb)
    return jnp.concatenate(outs, axis=1)


def _norm_proj_kernel(*refs, groups, tn, rope_q, rope_k, q_scale):
    n_out = len(groups)
    x_ref, g_ref, w_ref = refs[:3]
    pos = 3
    if rope_q is not None:
        c_ref, sa_ref, sb_ref = refs[3:6]
        pos = 6
    out_refs = refs[pos:pos + n_out]
    xn = _rms(x_ref[...], g_ref[...]).astype(BF16)
    for (start, count, _, act), o_ref in zip(groups, out_refs):
        for jj in range(start, start + count):
            y = jnp.dot(xn, w_ref[:, jj * tn:(jj + 1) * tn], preferred_element_type=F32)
            if rope_q is not None and jj == rope_q:
                y = _rope_tile(y, c_ref[...], sa_ref[...], sb_ref[...]) * q_scale
            elif rope_q is not None and jj == rope_k:
                y = _rope_tile(y, c_ref[...], sa_ref[...], sb_ref[...])
            if act is not None:
                y = act(y)
            o_ref[:, (jj - start) * tn:(jj - start + 1) * tn] = y.astype(o_ref.dtype)


def _norm_proj(x, g, w, groups, tm, tn, rope=None, rope_q=None, rope_k=None, q_scale=1.0):
    M, K = x.shape
    in_specs = [
        pl.BlockSpec((tm, K), lambda i: (i, 0)),
        pl.BlockSpec((1, K), lambda i: (0, 0)),
        pl.BlockSpec(w.shape, lambda i: (0, 0)),
    ]
    args = [x, g, w]
    if rope is not None:
        assert rope[0].shape[0] % tm == 0 and M % tm == 0
        pos_blocks = rope[0].shape[0] // tm
        for t in rope:
            in_specs.append(pl.BlockSpec((tm, LANES), lambda i, pb=pos_blocks: (i % pb, 0)))
            args.append(t)
    out_specs, out_shapes = [], []
    for (_, count, dtype, _) in groups:
        out_specs.append(pl.BlockSpec((tm, count * tn), lambda i: (i, 0)))
        out_shapes.append(jax.ShapeDtypeStruct((M, count * tn), dtype))
    kern = functools.partial(_norm_proj_kernel, groups=tuple(groups), tn=tn,
                             rope_q=rope_q if rope is not None else None, rope_k=rope_k, q_scale=q_scale)
    return pl.pallas_call(
        kern,
        grid=(M // tm,),
        in_specs=in_specs,
        out_specs=out_specs,
        out_shape=out_shapes,
        compiler_params=_cparams(("parallel",)),
        name="norm_proj",
    )(*args)


def _out_proj_kernel(*refs, n_in):
    a_refs = refs[:n_in]
    w_ref, x_ref, g_ref, gn_ref, o_ref, on_ref, wb_ref = refs[n_in:]

    @pl.when(pl.program_id(0) == 0)
    def _():
        wb_ref[...] = w_ref[...].astype(BF16)

    tm = x_ref.shape[0]
    half = tm // 2 if tm % (4 * SUBLANES) == 0 else tm
    for r0 in range(0, tm, half):
        rs = slice(r0, r0 + half)
        y, row = None, 0
        for a_ref in a_refs:
            kk = a_ref.shape[1]
            d = jnp.dot(a_ref[rs, :].astype(BF16), wb_ref[row:row + kk, :], preferred_element_type=F32)
            y = d if y is None else y + d
            row += kk
        x1 = x_ref[rs, :] + _rms(y, g_ref[...])
        o_ref[rs, :] = x1
        on_ref[rs, :] = _rms(x1, gn_ref[...]).astype(on_ref.dtype)


def _out_proj(acts, w, x, g, g_next, tm):
    M, D = x.shape
    n_in = len(acts)
    assert sum(a.shape[1] for a in acts) == w.shape[0]
    row_tile = pl.BlockSpec((tm, D), lambda i: (i, 0))
    gain = pl.BlockSpec((1, D), lambda i: (0, 0))
    in_specs = [pl.BlockSpec((tm, a.shape[1]), lambda i: (i, 0)) for a in acts]
    in_specs += [pl.BlockSpec(w.shape, lambda i: (0, 0)), row_tile, gain, gain]
    return pl.pallas_call(
        functools.partial(_out_proj_kernel, n_in=n_in),
        grid=(M // tm,),
        in_specs=in_specs,
        out_specs=[row_tile, row_tile],
        out_shape=[jax.ShapeDtypeStruct((M, D), F32), jax.ShapeDtypeStruct((M, D), BF16)],
        scratch_shapes=[pltpu.VMEM(w.shape, BF16)],
        compiler_params=_cparams(("arbitrary",)),
        name="out_proj",
    )(*acts, w, x, g, g_next)


def _ffn_kernel(x_ref, xn_ref, wu_ref, wd_ref, g2_ref, o_ref, acc_ref):
    f = pl.program_id(1)

    @pl.when(f == 0)
    def _():
        acc_ref[...] = jnp.zeros_like(acc_ref)

    hid = jnp.dot(xn_ref[...], wu_ref[...].astype(BF16), preferred_element_type=F32)
    hid = jnp.square(jnp.maximum(hid, 0.0))
    acc_ref[...] += jnp.dot(hid.astype(BF16), wd_ref[...].astype(BF16), preferred_element_type=F32)

    @pl.when(f == pl.num_programs(1) - 1)
    def _():
        o_ref[...] = x_ref[...] + _rms(acc_ref[...], g2_ref[...])


def _ffn(x, xn, wu, wd, layer, g2, tm, tf):
    M, D = x.shape
    FF = wu.shape[2]
    return pl.pallas_call(
        _ffn_kernel,
        grid=(M // tm, FF // tf),
        in_specs=[
            pl.BlockSpec((tm, D), lambda i, f: (i, 0)),
            pl.BlockSpec((tm, D), lambda i, f: (i, 0)),
            pl.BlockSpec((None, D, tf), lambda i, f: (layer, 0, f)),
            pl.BlockSpec((None, tf, D), lambda i, f: (layer, f, 0)),
            pl.BlockSpec((1, D), lambda i, f: (0, 0)),
        ],
        out_specs=pl.BlockSpec((tm, D), lambda i, f: (i, 0)),
        out_shape=jax.ShapeDtypeStruct((M, D), F32),
        scratch_shapes=[pltpu.VMEM((tm, D), F32)],
        compiler_params=_cparams(("parallel", "arbitrary")),
        name="ffn",
    )(x, xn, wu, wd, g2)


def _lru_gates(xc, wa_ref, ba, wx_ref, bx, lam):
    xcb = xc.astype(BF16)
    ng = wa_ref.shape[0]
    gw = wa_ref.shape[1]
    ra, ri = [], []
    for g in range(ng):
        xs = xcb[:, g * gw:(g + 1) * gw]
        ra.append(jnp.dot(xs, wa_ref[g], preferred_element_type=F32))
        ri.append(jnp.dot(xs, wx_ref[g], preferred_element_type=F32))
    r = jax.nn.sigmoid(jnp.concatenate(ra, axis=1) + ba)
    i = jax.nn.sigmoid(jnp.concatenate(ri, axis=1) + bx)
    log_a = -LRU_C * r * _softplus(-lam)
    a = jnp.exp(log_a)
    om = 1.0 - a * a
    b = jnp.where(om > 0.0, om * lax.rsqrt(om), 0.0) * (i * xc)
    return a, b


def _conv4_rows(xe, halo, w, b):
    a = w[0:1, :] * xe
    for j in range(1, CONV_W):
        a = pltpu.roll(a, 1, 0) + w[j:j + 1, :] * xe
    return b + a[halo:]


def _conv4(tail_ref, x, first, w_ref, b):
    @pl.when(first)
    def _():
        tail_ref[...] = jnp.zeros_like(tail_ref)

    xe = jnp.concatenate([tail_ref[...], x], axis=0)
    tail_ref[...] = x[x.shape[0] - SUBLANES:, :]
    return _conv4_rows(xe, SUBLANES, w_ref[...], b)


def _rglru_prompt_kernel(xa_ref, ga_ref, cw_ref, cb_ref, wa_ref, ba_ref, wx_ref, bx_ref, lam_ref,
                         ya_ref, hl_ref, xe_ref, hc_ref, a_s, b_s, h_s):
    t = pl.program_id(1)
    T = xa_ref.shape[0]

    @pl.when(t == 0)
    def _():
        hc_ref[...] = jnp.zeros_like(hc_ref)

    xc = _conv4(xe_ref, xa_ref[...], t == 0, cw_ref, cb_ref[...])
    a, b = _lru_gates(xc, wa_ref, ba_ref[...], wx_ref, bx_ref[...], lam_ref[...])
    a_s[...] = a
    b_s[...] = b

    def body(s, h):
        h = a_s[pl.ds(s, 1), :] * h + b_s[pl.ds(s, 1), :]
        h_s[pl.ds(s, 1), :] = h
        return h

    h = lax.fori_loop(0, T, body, hc_ref[...], unroll=8)
    hc_ref[...] = h
    ya_ref[...] = (h_s[...] * _gelu_tanh(ga_ref[...])).astype(ya_ref.dtype)

    @pl.when(t == pl.num_programs(1) - 1)
    def _():
        hl_ref[...] = h


def _rglru_prompt(xaga, cw, cb, wa_bd, ba, wx_bd, bx, lam, ts):
    B, S, _ = xaga.shape
    C = cw.shape[-1]
    wspec = lambda a: pl.BlockSpec(a.shape, lambda b, t: (0,) * a.ndim)
    return pl.pallas_call(
        _rglru_prompt_kernel,
        grid=(B, S // ts),
        in_specs=[
            pl.BlockSpec((None, ts, C), lambda b, t: (b, t, 0)),
            pl.BlockSpec((None, ts, C), lambda b, t: (b, t, 1)),
            wspec(cw), wspec(cb), wspec(wa_bd), wspec(ba), wspec(wx_bd), wspec(bx), wspec(lam),
        ],
        out_specs=[
            pl.BlockSpec((None, ts, C), lambda b, t: (b, t, 0)),
            pl.BlockSpec((None, 1, C), lambda b, t: (b, 0, 0)),
        ],
        out_shape=[jax.ShapeDtypeStruct((B, S, C), BF16), jax.ShapeDtypeStruct((B, 1, C), F32)],
        scratch_shapes=[pltpu.VMEM((SUBLANES, C), F32), pltpu.VMEM((1, C), F32),
                        pltpu.VMEM((ts, C), F32), pltpu.VMEM((ts, C), F32), pltpu.VMEM((ts, C), F32)],
        compiler_params=_cparams(("parallel", "arbitrary")),
        name="rglru_prompt",
    )(xaga, xaga, cw, cb, wa_bd, ba, wx_bd, bx, lam)


def _rglru_sample_kernel(xa_ref, ga_ref, st_ref, h0_ref, cw_ref, cb_ref, wa_ref, ba_ref, wx_ref, bx_ref,
                         lam_ref, ya_ref, hl_ref):
    T = xa_ref.shape[0]
    rows = [st_ref[j] for j in range(CONV_W - 1)] + [xa_ref[s] for s in range(T)]
    h = h0_ref[...]
    for s in range(T):
        y = cb_ref[...]
        for j in range(CONV_W):
            y = y + cw_ref[j:j + 1, :] * rows[s + j]
        a, b = _lru_gates(y, wa_ref, ba_ref[...], wx_ref, bx_ref[...], lam_ref[...])
        h = a * h + b
        ya_ref[s] = (h * _gelu_tanh(ga_ref[s])).astype(ya_ref.dtype)
    hl_ref[...] = h


def _rglru_sample(xa_t, ga_t, st_t, h0, cw, cb, wa_bd, ba, wx_bd, bx, lam):
    T, B, C = xa_t.shape
    args = (xa_t, ga_t, st_t, h0, cw, cb, wa_bd, ba, wx_bd, bx, lam)
    return pl.pallas_call(
        _rglru_sample_kernel,
        grid=(1,),
        in_specs=[pl.BlockSpec(a.shape, lambda i, n=a.ndim: (0,) * n) for a in args],
        out_specs=[pl.BlockSpec((T, B, C), lambda i: (0, 0, 0)), pl.BlockSpec((B, C), lambda i: (0, 0))],
        out_shape=[jax.ShapeDtypeStruct((T, B, C), F32), jax.ShapeDtypeStruct((B, C), F32)],
        compiler_params=_cparams(("arbitrary",)),
        name="rglru_sample",
    )(*args)


def _attn_prompt_kernel(q_ref, k_ref, v_ref, o_ref, *scratch, groups, qb):
    S = q_ref.shape[0]
    E = q_ref.shape[1]
    dist2 = (lax.broadcasted_iota(jnp.int32, (qb, 2 * qb), 0) + qb
             - lax.broadcasted_iota(jnp.int32, (qb, 2 * qb), 1))
    dist1 = lax.broadcasted_iota(jnp.int32, (qb, qb), 0) - lax.broadcasted_iota(jnp.int32, (qb, qb), 1)

    def rows(start, d):
        return pl.ds(start, qb) if d == 1 else pl.ds(start, qb, stride=d)

    ng = len(groups)
    s_scr, p_scr = scratch[3 * ng], scratch[3 * ng + 1]
    dn_t = (((1,), (1,)), ((), ()))

    for gi, (w, d) in enumerate(groups):
        back = w // d
        nb = (S // d) // qb
        nblk = nb * d
        two = nb > 1
        kw = 2 * qb if two else qb
        dist = dist2 if two else dist1
        lim_all = jnp.full((qb, kw), back, jnp.int32)
        lim_first = jnp.minimum(lax.broadcasted_iota(jnp.int32, (qb, kw), 0), back)
        m_s, l_s, acc_s = scratch[3 * gi:3 * gi + 3]

        def locate(idx, d=d):
            if d == 1:
                n = idx
                start = pl.multiple_of(idx * qb, qb)
                prev = pl.multiple_of(jnp.maximum(idx - 1, 0) * qb, qb)
            else:
                n = idx // d
                start = n * (qb * d) + idx % d
                prev = jnp.where(n > 0, start - qb * d, start)
            return start, prev, n

        def gather(ref, start, prev, d=d, two=two):
            if two:
                return jnp.concatenate([ref[rows(prev, d), :], ref[rows(start, d), :]], axis=0).astype(BF16)
            return ref[rows(start, d), :].astype(BF16)

        def scores(idx, c, kw=kw, locate=locate, gather=gather, d=d):
            start, prev, _ = locate(idx)
            q = q_ref[rows(start, d), :].astype(BF16)
            s_scr[pl.ds(pl.multiple_of(idx * qb, qb), qb), 0:kw] = lax.dot_general(
                q, gather(k_ref, start, prev), dn_t, preferred_element_type=F32)
            return c

        def softmax(idx, c, kw=kw, locate=locate, d=d, two=two, dist=dist, lim_all=lim_all,
                    lim_first=lim_first, m_s=m_s):
            start, _, n = locate(idx)
            blk = pl.ds(pl.multiple_of(idx * qb, qb), qb)
            lim = jnp.where(n > 0, lim_all, lim_first) if two else lim_all
            valid = (dist >= 0) & (dist <= lim)
            s = jnp.where(valid, s_scr[blk, 0:kw], NEG_INF)
            m_b = jnp.max(s, axis=-1, keepdims=True)
            p_scr[blk, 0:kw] = jnp.exp(s - m_b).astype(BF16)
            m_s[rows(start, d), :] = jnp.broadcast_to(m_b, (qb, E))
            return c

        def values(idx, c, kw=kw, locate=locate, gather=gather, d=d, acc_s=acc_s, l_s=l_s):
            start, prev, _ = locate(idx)
            p = p_scr[pl.ds(pl.multiple_of(idx * qb, qb), qb), 0:kw]
            v1 = jnp.concatenate([gather(v_ref, start, prev), jnp.ones((kw, E), BF16)], axis=1)
            r = jnp.dot(p, v1, preferred_element_type=F32)
            acc_s[rows(start, d), :] = r[:, :E]
            l_s[rows(start, d), :] = r[:, E:]
            return c

        lax.fori_loop(0, nblk, scores, 0, unroll=True)
        lax.fori_loop(0, nblk, softmax, 0, unroll=True)
        lax.fori_loop(0, nblk, values, 0, unroll=True)

    ng = len(groups)

    def merge(i, c):
        rs = pl.ds(pl.multiple_of(i * qb, qb), qb)
        ms = [scratch[3 * g][rs, :] for g in range(ng)]
        m = functools.reduce(jnp.maximum, ms)
        num = den = None
        for g in range(ng):
            wg = jnp.exp(ms[g] - m)
            n_g = wg * scratch[3 * g + 2][rs, :]
            d_g = wg * scratch[3 * g + 1][rs, :]
            num = n_g if num is None else num + n_g
            den = d_g if den is None else den + d_g
        o_ref[rs, :] = (num / den).astype(o_ref.dtype)
        return c

    lax.fori_loop(0, S // qb, merge, 0, unroll=2)


def _attn_prompt(q, q_off, k, v, groups, qb):
    B, S, D = k.shape
    nh = D // LANES
    for (w, d) in groups:
        assert S % (d * qb) == 0 and w // d <= qb
    spec = pl.BlockSpec((None, S, LANES), lambda b, h: (b, 0, h))
    q_spec = pl.BlockSpec((None, S, LANES), lambda b, h: (b, 0, q_off + h))
    return pl.pallas_call(
        functools.partial(_attn_prompt_kernel, groups=tuple(groups), qb=qb),
        grid=(B, nh),
        in_specs=[q_spec, spec, spec],
        out_specs=spec,
        out_shape=jax.ShapeDtypeStruct((B, S, D), BF16),
        scratch_shapes=[pltpu.VMEM((S, LANES), F32)] * (3 * len(groups))
        + [pltpu.VMEM((S, 2 * qb), F32), pltpu.VMEM((S, 2 * qb), BF16)],
        compiler_params=_cparams(("parallel", "parallel")),
        name="attn_prompt",
    )(q, k, v)


def _attn_sample_kernel(q_ref, kn_ref, vn_ref, kt_ref, vt_ref, ks_ref, vs_ref, mt_ref, ms_ref, mn_ref, o_ref):
    nq, hd = q_ref.shape
    qb = jnp.concatenate([q_ref[...], jnp.zeros((LANES - nq, hd), F32)], axis=0).astype(BF16)
    nn = kn_ref.shape[0]
    zn = jnp.zeros((LANES - nn, hd), F32)
    key_sets = (
        (kt_ref[...], vt_ref[...], mt_ref[...]),
        (ks_ref[...].reshape(-1, hd), vs_ref[...].reshape(-1, hd), ms_ref[...]),
        (jnp.concatenate([kn_ref[...], zn], axis=0), jnp.concatenate([vn_ref[...], zn], axis=0), mn_ref[...]),
    )
    scored = []
    for k, v, bias in key_sets:
        s = lax.dot_general(k.astype(BF16), qb, (((1,), (1,)), ((), ())), preferred_element_type=F32)
        scored.append((s + bias, v))
    m = functools.reduce(jnp.maximum, [jnp.max(s, axis=0, keepdims=True) for s, _ in scored])
    m = jnp.where(m == NEG_INF, 0.0, m)
    acc = None
    for s, v in scored:
        p = jnp.exp(s - m).astype(BF16)
        v1 = jnp.concatenate([v.astype(BF16), jnp.ones(v.shape, BF16)], axis=1)
        r = lax.dot_general(p, v1, (((0,), (0,)), ((), ())), preferred_element_type=F32)
        acc = r if acc is None else acc + r
    o_ref[...] = acc[:nq, :hd] / acc[:nq, hd:]


def _attn_sample(q, kn, vn, kc, vc, T, nh):
    B, nq, hd = q.shape
    wb = kc.shape[1] // nh
    far = max(DILATIONS, key=lambda wd: wd[0])
    near = [g for g in DILATIONS if g != far]
    R = max(w for w, _ in near)
    d_far = far[1]
    assert near and wb % R == 0 and wb % d_far == 0 and T <= d_far and nq == T * nh and nq <= LANES
    A = wb // d_far

    def mult(delta, groups):
        m = np.zeros(delta.shape, np.float32)
        for (w, d) in groups:
            m = m + ((delta >= 0) & (delta % d == 0) & (delta <= (w // d) * d))
        return m

    def table(pos_of_row, head_of_row, groups):
        lane = np.arange(LANES)
        t, hq = lane // nh, lane % nh
        delta = (wb + t)[None, :] - pos_of_row[:, None]
        tab = mult(delta, groups) * (head_of_row[:, None] == hq[None, :]) * (t < T)[None, :]
        return jnp.asarray(np.where(tab > 0, np.log(np.maximum(tab, 1.0)), NEG_INF), F32)

    rt = np.arange(R * nh)
    mt = table(wb - R + rt // nh, rt % nh, near)
    rs = np.arange(A * T * nh)
    ms = table((rs // (T * nh)) * d_far + (rs % (T * nh)) // nh, rs % nh, [far])
    rn = np.arange(LANES)
    mn = table(np.where(rn < nq, wb + rn // nh, -10 ** 9), rn % nh, DILATIONS)

    small = pl.BlockSpec((None, nq, hd), lambda b: (b, 0, 0))
    tail = pl.BlockSpec((None, R * nh, hd), lambda b: (b, wb // R - 1, 0))
    strided = pl.BlockSpec((None, A, T * nh, hd), lambda b: (b, 0, 0, 0))
    const = lambda a: pl.BlockSpec(a.shape, lambda b: (0, 0))
    kc4 = kc.reshape(B, A, d_far * nh, hd)
    vc4 = vc.reshape(B, A, d_far * nh, hd)
    return pl.pallas_call(
        _attn_sample_kernel,
        grid=(B,),
        in_specs=[small, small, small, tail, tail, strided, strided, const(mt), const(ms), const(mn)],
        out_specs=small,
        out_shape=jax.ShapeDtypeStruct((B, nq, hd), F32),
        compiler_params=_cparams(("parallel",)),
        name="attn_sample",
    )(q, kn, vn, kc, vc, kc4, vc4, mt, ms, mn)


def _mlstm_qkv_gates(xc, xm, wq_ref, wk_ref, wv_ref, wg_ref, bg_ref, q_ref, k_ref, v_ref, xc_ref, g_ref, k_scale,
                     rs=slice(None)):
    xcb = xc.astype(BF16)
    xmb = xm.astype(BF16)
    ng, gw = wq_ref.shape[0], wq_ref.shape[1]
    qs, ks, vs = [], [], []
    for g in range(ng):
        sl = slice(g * gw, (g + 1) * gw)
        qs.append(jnp.dot(xcb[:, sl], wq_ref[g], preferred_element_type=F32))
        ks.append(jnp.dot(xcb[:, sl], wk_ref[g], preferred_element_type=F32))
        vs.append(jnp.dot(xmb[:, sl], wv_ref[g], preferred_element_type=F32))
    k = jnp.concatenate(ks, axis=1)
    qb = jnp.concatenate(qs, axis=1).astype(BF16)
    vb = jnp.concatenate(vs, axis=1).astype(BF16)
    qkv = jnp.concatenate([qb, k.astype(BF16), vb], axis=1)
    g_ref[rs, :] = jnp.dot(qkv, wg_ref[...], preferred_element_type=F32) + bg_ref[...]
    q_ref[rs, :] = qb
    k_ref[rs, :] = (k * k_scale).astype(k_ref.dtype)
    v_ref[rs, :] = vb
    xc_ref[rs, :] = xcb


def _mlstm_pre_sample_kernel(xm_ref, st_ref, cw_ref, cb_ref, wq_ref, wk_ref, wv_ref, wg_ref, bg_ref,
                             q_ref, k_ref, v_ref, xc_ref, g_ref, *, k_scale):
    T = xm_ref.shape[0]
    rows = [st_ref[j] for j in range(CONV_W - 1)] + [xm_ref[s] for s in range(T)]
    conv = []
    for s in range(T):
        y = cb_ref[...]
        for j in range(CONV_W):
            y = y + cw_ref[j:j + 1, :] * rows[s + j]
        conv.append(y)
    xc = _silu(jnp.concatenate(conv, axis=0))
    xm = jnp.concatenate(rows[CONV_W - 1:], axis=0)
    _mlstm_qkv_gates(xc, xm, wq_ref, wk_ref, wv_ref, wg_ref, bg_ref, q_ref, k_ref, v_ref, xc_ref, g_ref, k_scale)


def _mlstm_pre_sample(xm_t, st_t, cw, cb, wq_bd, wk_bd, wv_bd, wg, bg, k_scale):
    T, B, C = xm_t.shape
    args = (xm_t, st_t, cw, cb, wq_bd, wk_bd, wv_bd, wg, bg)
    full = lambda shape: pl.BlockSpec(shape, lambda i: (0,) * len(shape))
    return pl.pallas_call(
        functools.partial(_mlstm_pre_sample_kernel, k_scale=k_scale),
        grid=(1,),
        in_specs=[full(a.shape) for a in args],
        out_specs=[full((T * B, C))] * 4 + [full((T * B, 2 * LANES))],
        out_shape=[jax.ShapeDtypeStruct((T * B, C), BF16)] * 4 + [jax.ShapeDtypeStruct((T * B, 2 * LANES), F32)],
        compiler_params=_cparams(("arbitrary",)),
        name="mlstm_pre_sample",
    )(*args)


CONV_HALO = 16


def _mlstm_in_kernel(x_ref, xh_ref, g_ref, w_ref, cw_ref, cb_ref, wq_ref, wk_ref, wv_ref, wg_ref, bg_ref,
                     q_ref, k_ref, v_ref, xc_ref, gz_ref, gt_ref, tail_ref, *, tiles_per_seq, tn, k_scale):
    tm = x_ref.shape[0]
    C = cw_ref.shape[1]
    first = (pl.program_id(0) % tiles_per_seq) == 0
    g = g_ref[...]
    xn = _rms(x_ref[...], g)
    xe = jnp.concatenate([_rms(xh_ref[...], g), xn], axis=0).astype(BF16)
    xn = xn.astype(BF16)
    dead_halo = lax.broadcasted_iota(jnp.int32, (CONV_HALO + tm, tn), 0) < jnp.where(first, CONV_HALO, 0)
    xmes = []
    for jj in range(C // tn):
        cs = slice(jj * tn, (jj + 1) * tn)
        xm_e = jnp.dot(xe, w_ref[:, cs], preferred_element_type=F32)
        xmes.append(jnp.where(dead_halo, 0.0, xm_e))
        gz_ref[:, cs] = _silu(jnp.dot(xn, w_ref[:, C + jj * tn:C + (jj + 1) * tn],
                                      preferred_element_type=F32)).astype(gz_ref.dtype)
    half = tm // 2
    for r0 in range(0, tm, half):
        xc = jnp.concatenate(
            [_silu(_conv4_rows(xm_e[r0:r0 + CONV_HALO + half], CONV_HALO, cw_ref[:, jj * tn:(jj + 1) * tn],
                               cb_ref[:, jj * tn:(jj + 1) * tn])) for jj, xm_e in enumerate(xmes)], axis=1)
        xm = jnp.concatenate([xm_e[CONV_HALO + r0:CONV_HALO + r0 + half] for xm_e in xmes], axis=1)
        _mlstm_qkv_gates(xc, xm, wq_ref, wk_ref, wv_ref, wg_ref, bg_ref, q_ref, k_ref, v_ref, xc_ref, gt_ref,
                         k_scale, rs=slice(r0, r0 + half))
    tail_ref[...] = jnp.concatenate([xm_e[CONV_HALO + tm - SUBLANES:] for xm_e in xmes], axis=1)


def _mlstm_in(x, g, w, cw, cb, wq_bd, wk_bd, wv_bd, wg, bg, seq, tm, k_scale):
    M, K = x.shape
    C = cw.shape[1]
    assert seq % tm == 0 and tm % CONV_HALO == 0
    hb = tm // CONV_HALO
    const = lambda a: pl.BlockSpec(a.shape, lambda i: (0,) * a.ndim)
    act = pl.BlockSpec((tm, C), lambda i: (i, 0))
    return pl.pallas_call(
        functools.partial(_mlstm_in_kernel, tiles_per_seq=seq // tm, tn=1024, k_scale=k_scale),
        grid=(M // tm,),
        in_specs=[pl.BlockSpec((tm, K), lambda i: (i, 0)),
                  pl.BlockSpec((CONV_HALO, K), lambda i: (jnp.maximum(i * hb - 1, 0), 0)),
                  const(g), const(w), const(cw), const(cb), const(wq_bd), const(wk_bd), const(wv_bd),
                  const(wg), const(bg)],
        out_specs=[act, act, act, act, act, pl.BlockSpec((tm, 2 * LANES), lambda i: (i, 0)),
                   pl.BlockSpec((None, SUBLANES, C), lambda i: (i, 0, 0))],
        out_shape=[jax.ShapeDtypeStruct((M, C), BF16)] * 5 + [jax.ShapeDtypeStruct((M, 2 * LANES), F32),
                                                              jax.ShapeDtypeStruct((M // tm, SUBLANES, C), F32)],
        compiler_params=_cparams(("parallel",)),
        name="mlstm_in",
    )(x, x, g, w, cw, cb, wq_bd, wk_bd, wv_bd, wg, bg)


def _mlstm_core_kernel(*refs, c, n_valid, nh, zero_init):
    q_ref, k_ref, v_ref, xc_ref, gz_ref, g_ref, nw_ref, sk_ref = refs[:8]
    pos = 8
    if not zero_init:
        c0_ref, n0_ref, m0_ref = refs[8:11]
        pos = 11
    y_ref, C_ref, n_ref, m_ref = refs[pos:pos + 4]
    ch = pl.program_id(1)
    c_in = q_ref.shape[0]
    DH = q_ref.shape[1] // nh

    @pl.when(ch == 0)
    def _():
        if zero_init:
            C_ref[...] = jnp.zeros_like(C_ref)
            n_ref[...] = jnp.zeros_like(n_ref)
            m_ref[...] = jnp.zeros_like(m_ref)
        else:
            C_ref[...] = c0_ref[...]
            n_ref[...] = n0_ref[...]
            m_ref[...] = m0_ref[...]

    def padded(x):
        if c_in == c:
            return x
        return jnp.concatenate([x, jnp.zeros((c - c_in, x.shape[1]), x.dtype)], axis=0)

    g = padded(g_ref[...])
    row = lax.broadcasted_iota(jnp.int32, (c, LANES), 0)
    valid = row < n_valid
    gi = jnp.where(valid, g[:, :LANES], NEG_INF)
    lf = jnp.where(valid, -_softplus(-g[:, LANES:]), 0.0)
    tri_f = (lax.broadcasted_iota(jnp.int32, (c, c), 1) <= lax.broadcasted_iota(jnp.int32, (c, c), 0))
    tri_b = tri_f.astype(BF16)
    bcum, rest = None, lf
    for _ in range(3):
        part = rest.astype(BF16)
        rest = rest - part.astype(F32)
        d = jnp.dot(tri_b, part, preferred_element_type=F32)
        bcum = d if bcum is None else bcum + d
    bcum_t = bcum.T
    gi_t = gi.T
    causal = tri_f

    for h in range(nh):
        sl = slice(h * DH, (h + 1) * DH)
        qh = padded(q_ref[:, sl])
        kh = padded(k_ref[:, sl])
        vh = padded(v_ref[:, sl])
        bc_col = bcum[:, h:h + 1]
        bc_row = bcum_t[h:h + 1, :]
        i_row = gi_t[h:h + 1, :]
        i_col = gi[:, h:h + 1]
        m_prev = m_ref[h:h + 1, 0:1]
        log_intra = jnp.where(causal, bc_col - bc_row + i_row, NEG_INF)
        log_inter = bc_col + m_prev
        m_t = jnp.maximum(log_inter, jnp.max(log_intra, axis=1, keepdims=True))
        qk = lax.dot_general(qh, kh, (((1,), (1,)), ((), ())), preferred_element_type=F32)
        w_intra = jnp.exp(log_intra - m_t) * qk
        w_inter = jnp.exp(log_inter - m_t)
        C_old = C_ref[h]
        n_old = n_ref[h:h + 1, :]
        num = jnp.dot(w_intra.astype(BF16), vh, preferred_element_type=F32)
        num = num + w_inter * jnp.dot(qh, C_old.astype(BF16), preferred_element_type=F32)
        n8 = jnp.broadcast_to(n_old, (SUBLANES, DH)).astype(BF16)
        qn = lax.dot_general(qh, n8, (((1,), (1,)), ((), ())), preferred_element_type=F32)[:, 0:1]
        den = jnp.sum(w_intra, axis=1, keepdims=True) + w_inter * qn
        hh = num * (1.0 / jnp.maximum(jnp.abs(den), jnp.exp(-m_t)))
        m_new = m_t[c - 1:c, :]
        bc_last = bc_col[c - 1:c, :]
        w_state = jnp.exp(bc_last - bc_col + i_col - m_new)
        decay = jnp.exp(bc_last + m_prev - m_new)
        kw = kh.astype(F32) * w_state
        C_ref[h] = decay * C_old + lax.dot_general(
            kw.astype(BF16), vh, (((0,), (0,)), ((), ())), preferred_element_type=F32)
        n_ref[h:h + 1, :] = decay * n_old + jnp.sum(kw, axis=0, keepdims=True)
        m_ref[h:h + 1, :] = jnp.broadcast_to(m_new, (1, LANES))
        hn = hh * lax.rsqrt(jnp.mean(hh * hh, axis=-1, keepdims=True) + EPS)
        hn = hn[:c_in] * nw_ref[:, sl]
        y = (hn + sk_ref[:, sl] * xc_ref[:, sl].astype(F32)) * gz_ref[:, sl].astype(F32)
        y_ref[:, sl] = y.astype(y_ref.dtype)


def _mlstm_core(q, k, v, xc, gz, gates, nw, sk, state, c, n_valid, nh):
    B, S, C = q.shape
    c_in = min(c, S)
    DH = C // nh
    zero_init = state is None
    act = pl.BlockSpec((None, c_in, C), lambda b, t: (b, t, 0))
    in_specs = [act, act, act, act, act,
                pl.BlockSpec((None, c_in, 2 * LANES), lambda b, t: (b, t, 0)),
                pl.BlockSpec((1, C), lambda b, t: (0, 0)), pl.BlockSpec((1, C), lambda b, t: (0, 0))]
    args = [q, k, v, xc, gz, gates, nw, sk]
    C_spec = pl.BlockSpec((None, nh, DH, DH), lambda b, t: (b, 0, 0, 0))
    n_spec = pl.BlockSpec((None, nh, DH), lambda b, t: (b, 0, 0))
    m_spec = pl.BlockSpec((None, nh, LANES), lambda b, t: (b, 0, 0))
    if not zero_init:
        in_specs += [C_spec, n_spec, m_spec]
        args += list(state)
    return pl.pallas_call(
        functools.partial(_mlstm_core_kernel, c=c, n_valid=n_valid, nh=nh, zero_init=zero_init),
        grid=(B, S // c_in),
        in_specs=in_specs,
        out_specs=[act, C_spec, n_spec, m_spec],
        out_shape=[jax.ShapeDtypeStruct((B, S, C), BF16),
                   jax.ShapeDtypeStruct((B, nh, DH, DH), F32),
                   jax.ShapeDtypeStruct((B, nh, DH), F32),
                   jax.ShapeDtypeStruct((B, nh, LANES), F32)],
        compiler_params=_cparams(("parallel", "arbitrary")),
        name="mlstm_core",
    )(*args)


def _rope_tables(pos, hd):
    rot = hd // 4
    half = rot // 2
    inv = ROPE_THETA ** (-np.arange(0, rot, 2, dtype=np.float64) / rot)
    ang = pos.astype(np.float64)[:, None] * inv[None, :]
    cos, sin = np.cos(ang), np.sin(ang)
    n = pos.shape[0]
    c = np.concatenate([cos, cos, np.ones((n, hd - rot))], axis=1)
    sa = np.concatenate([-sin, np.zeros((n, hd - half))], axis=1)
    sb = np.concatenate([np.zeros((n, half)), sin, np.zeros((n, hd - rot))], axis=1)
    return tuple(jnp.asarray(t, F32) for t in (c, sa, sb))


def _regroup_block_diag(w, group):
    nb, bi, bo = w.shape
    per = group // bi
    a = w.reshape(nb // per, per * bi, bo)
    cols = np.arange(per * bo)
    tile_cols = jnp.asarray(cols[None, :] % bo == np.arange(bo)[:, None], BF16)
    same_block = jnp.asarray((np.arange(per * bi)[:, None] // bi) == (cols[None, :] // bo), BF16)
    tiled = jnp.einsum("gro,oc->grc", a.astype(BF16), tile_cols, preferred_element_type=F32)
    return tiled.astype(BF16) * same_block


def _layer0(x, rope, p, prompt, state):
    Bg, L, D = x.shape
    M = Bg * L
    hd = D // H_B
    tm = _tile(M, 1024)
    tm_in = _tile(rope[0].shape[0], 512)
    x2 = x.reshape(M, D)
    xagq, k, v = _norm_proj(x2, p["g_mix_pre0"], p["w_in_ab"], ((0, 3, F32, None), (3, 1, F32, None), (4, 1, F32, None)),
                            tm_in, D, rope=rope, rope_q=2, rope_k=3, q_scale=hd ** -0.5)
    xagq3 = xagq.reshape(Bg, L, 3 * D)
    lru_w = (p["conv_a_w"], p["conv_a_b"], p["wa_bd"], p["lru_ba"], p["wx_bd"], p["lru_bx"], p["lam"])
    if prompt:
        ya, h_last = _rglru_prompt(xagq3, *lru_w, ts=_tile(L, 512))
        ya = ya.reshape(M, D)
        h_last = h_last.reshape(Bg, D)
        new_conv = xagq3[:, L - (CONV_W - 1):, :D]
        yb = _attn_prompt(xagq3, 2 * D // LANES, k.reshape(Bg, L, D), v.reshape(Bg, L, D),
                          sorted(DILATIONS, key=lambda wd: -wd[1]), QBLOCK).reshape(M, D)
        wb = min(MAX_WINDOW, L)
        new_k = k.reshape(Bg, L, H_B, hd)[:, L - wb:]
        new_v = v.reshape(Bg, L, H_B, hd)[:, L - wb:]
    else:
        conv0, h0, k_buf, v_buf = state
        xagq_t = xagq3.transpose(1, 0, 2)
        xa_t, ga_t = xagq_t[:, :, :D], xagq_t[:, :, D:2 * D]
        st_t = conv0.transpose(1, 0, 2)
        ya_t, h_last = _rglru_sample(xa_t, ga_t, st_t, h0, *lru_w)
        ya = ya_t.transpose(1, 0, 2).reshape(M, D)
        new_conv = jnp.concatenate([conv0, xagq3[:, :, :D]], axis=1)[:, -(CONV_W - 1):]
        wb = k_buf.shape[1]
        native = lambda a: a.reshape(Bg, -1, hd)
        yb = _attn_sample(native(xagq3[:, :, 2 * D:]), native(k), native(v), native(k_buf), native(v_buf),
                          L, H_B).reshape(M, D)
        new_k = k.reshape(Bg, L, H_B, hd)
        new_v = v.reshape(Bg, L, H_B, hd)
    x1, x1n = _out_proj([ya, yb], p["w_out_ab"], x2, p["g_mix_post0"], p["g_ffn_pre0"], _tile(M, 512))
    x2o = _ffn(x1, x1n, p["w_up"], p["w_dn"], 0, p["g_ffn_post0"], _tile(M, 1024), 1024)
    return x2o.reshape(Bg, L, D), (new_conv, h_last, new_k, new_v)


def _layer1(x, p, prompt, state):
    Bg, L, D = x.shape
    M = Bg * L
    x2 = x.reshape(M, D)
    tm = _tile(M, 1024)
    DC = p["w_in_c"].shape[1] // 2
    DH = DC // NH_C
    pre_w = (p["conv_c_w"], p["conv_c_b"], p["wq_bd"], p["wk_bd"], p["wv_bd"], p["wg"], p["bg"])
    if prompt:
        tm_in = _tile(L, 256)
        c = _tile(L, 256)
        *acts, tails = _mlstm_in(x2, p["g_mix_pre1"], p["w_in_c"], *pre_w, seq=L, tm=tm_in, k_scale=DH ** -0.5)
        q, k, v, xc, gz, gates = [a.reshape(Bg, L, -1) for a in acts]
        y, C, n, m = _mlstm_core(q, k, v, xc, gz, gates, p["mlstm_norm"], p["mlstm_skip"], None, c, c, NH_C)
        new_conv = tails.reshape(Bg, L // tm_in, SUBLANES, DC)[:, -1, SUBLANES - (CONV_W - 1):]
        y = y.reshape(M, DC)
    else:
        nt = DC // 1024
        xm, gz = _norm_proj(x2, p["g_mix_pre1"], p["w_in_c"], ((0, nt, F32, None), (nt, nt, BF16, _silu)),
                            _tile(M, 512), 1024)
        xm3 = xm.reshape(Bg, L, DC)
        gz3 = gz.reshape(Bg, L, DC)
        conv0, C0, n0, m0 = state
        lp = 16
        outs = _mlstm_pre_sample(xm3.transpose(1, 0, 2), conv0.transpose(1, 0, 2), *pre_w, k_scale=DH ** -0.5)
        batch_major = lambda a: jnp.pad(a.reshape(L, Bg, -1).transpose(1, 0, 2), ((0, 0), (0, lp - L), (0, 0)))
        q, k, v, xc, gates = [batch_major(a) for a in outs]
        gz_p = jnp.pad(gz3, ((0, 0), (0, lp - L), (0, 0)))
        m0b = jnp.broadcast_to(m0[:, :, None], m0.shape + (LANES,))
        y, C, n, m = _mlstm_core(q, k, v, xc, gz_p, gates, p["mlstm_norm"], p["mlstm_skip"],
                                 (C0, n0, m0b), 32, L, NH_C)
        new_conv = jnp.concatenate([conv0, xm3], axis=1)[:, -(CONV_W - 1):]
        y = y[:, :L].reshape(M, DC)
    x1, x1n = _out_proj([y], p["w_out_c"], x2, p["g_mix_post1"], p["g_ffn_pre1"], _tile(M, 512))
    x2o = _ffn(x1, x1n, p["w_up"], p["w_dn"], 1, p["g_ffn_post1"], _tile(M, 1024), 1024)
    return x2o.reshape(Bg, L, D), (new_conv, C, n, m[:, :, 0])


def kernel(x_prompt, x_sample, state_rglru_conv, state_rglru_h, cache_swa_k, cache_swa_v, state_mlstm_conv, state_mlstm_C, state_mlstm_n, state_mlstm_m, norm_mix_pre, norm_mix_post, norm_ffn_pre, norm_ffn_post, w_ffn_up, w_ffn_down, w_in_ab, conv_a_w, conv_a_b, lru_wa, lru_ba, lru_wx, lru_bx, lru_lambda, w_out_ab, w_in_c, conv_c_w, conv_c_b, mlstm_wq, mlstm_wk, mlstm_wv, mlstm_w_gate, mlstm_b_gate, mlstm_norm, mlstm_skip, w_out_c):
    B, S, D = x_prompt.shape
    Bs, Ts, _ = x_sample.shape
    DA = conv_a_w.shape[-1]
    DC = conv_c_w.shape[-1]
    hd = D // H_B
    row = lambda a: a.reshape(1, -1)
    lane_pad = lambda a: jnp.pad(a, ((0, 0), (0, LANES - NH_C)))
    wg = jnp.concatenate([lane_pad(mlstm_w_gate[0][:, :NH_C]), lane_pad(mlstm_w_gate[0][:, NH_C:])], axis=1)
    bg = jnp.concatenate([lane_pad(mlstm_b_gate[:, :NH_C]), lane_pad(mlstm_b_gate[:, NH_C:])], axis=1)
    p = {
        "g_mix_pre0": row(norm_mix_pre[0]), "g_mix_post0": row(norm_mix_post[0]),
        "g_ffn_pre0": row(norm_ffn_pre[0]), "g_ffn_post0": row(norm_ffn_post[0]),
        "g_mix_pre1": row(norm_mix_pre[1]), "g_mix_post1": row(norm_mix_post[1]),
        "g_ffn_pre1": row(norm_ffn_pre[1]), "g_ffn_post1": row(norm_ffn_post[1]),
        "w_up": w_ffn_up, "w_dn": w_ffn_down,
        "w_in_ab": w_in_ab[0].astype(BF16),
        "conv_a_w": conv_a_w[0], "conv_a_b": row(conv_a_b[0]),
        "wa_bd": _regroup_block_diag(lru_wa[0], MXU_DIM), "lru_ba": row(lru_ba[0]),
        "wx_bd": _regroup_block_diag(lru_wx[0], MXU_DIM), "lru_bx": row(lru_bx[0]),
        "lam": row(lru_lambda[0]),
        "w_out_ab": w_out_ab[0],
        "w_in_c": w_in_c[0].astype(BF16),
        "conv_c_w": conv_c_w[0], "conv_c_b": row(conv_c_b[0]),
        "wq_bd": _regroup_block_diag(mlstm_wq[0], MXU_DIM),
        "wk_bd": _regroup_block_diag(mlstm_wk[0], MXU_DIM),
        "wv_bd": _regroup_block_diag(mlstm_wv[0], MXU_DIM),
        "wg": wg.astype(BF16), "bg": bg,
        "mlstm_norm": row(mlstm_norm[0]), "mlstm_skip": row(mlstm_skip[0]),
        "w_out_c": w_out_c[0],
    }
    dt = state_rglru_conv.dtype

    xp, st0 = _layer0(x_prompt, _rope_tables(np.arange(S), hd), p, True, None)
    xp, st1 = _layer1(xp, p, True, None)
    pos_s = PAST_LEN + (np.arange(Bs * Ts) % Ts)
    xs, ss0 = _layer0(x_sample, _rope_tables(pos_s, hd), p, False,
                      (state_rglru_conv[0], state_rglru_h[0], cache_swa_k[0], cache_swa_v[0]))
    xs, ss1 = _layer1(xs, p, False, (state_mlstm_conv[0], state_mlstm_C[0], state_mlstm_n[0], state_mlstm_m[0]))

    lead = lambda a: a[None].astype(dt)
    return (xp, xs,
            lead(st0[0]), lead(ss0[0]), lead(st0[1]), lead(ss0[1]),
            lead(st0[2]), lead(ss0[2]), lead(st0[3]), lead(ss0[3]),
            lead(st1[0]), lead(ss1[0]), lead(st1[1]), lead(ss1[1]),
            lead(st1[2]), lead(ss1[2]), lead(st1[3]), lead(ss1[3]))
```

```python
import functools
import math

import numpy as np
import jax
import jax.numpy as jnp
from jax import lax
from jax.experimental import pallas as pl
from jax.experimental.pallas import tpu as pltpu

F32 = jnp.float32
BF16 = jnp.bfloat16

NA_BLOCKS = 16
CONV_W = 4
LRU_C = 8.0
H_B = 8
ROPE_THETA = 500000.0
DILATIONS = ((128, 1), (512, 4), (2048, 16))
MAX_WINDOW = 2048
QBLOCK = 128
NH_C = 4
QKV_BLOCK = 4
PAST_LEN = 16384
EPS = 1e-6

LANES = 128
SUBLANES = 8
MXU_DIM = 256
VMEM_LIMIT = 56 * 1024 * 1024

NEG_INF = float("-inf")


def _cparams(sem):
    return pltpu.CompilerParams(dimension_semantics=sem, vmem_limit_bytes=VMEM_LIMIT)


def _tile(n, pref):
    t = min(n, pref)
    while n % t:
        t -= 1
    return t


def _rms(x, g):
    return x * lax.rsqrt(jnp.mean(x * x, axis=-1, keepdims=True) + EPS) * g


def _softplus(z):
    return jnp.maximum(z, 0.0) + jnp.log1p(jnp.exp(-jnp.abs(z)))


def _gelu_tanh(x):
    c = math.sqrt(2.0 / math.pi)
    return x * (0.5 * (1.0 + jnp.tanh(c * (x + 0.044715 * (x * x * x)))))


def _silu(x):
    return x * jax.nn.sigmoid(x)


def _rope_tile(y, c, sa, sb):
    outs = []
    for h in range(y.shape[1] // LANES):
        yh = y[:, h * LANES:(h + 1) * LANES]
        outs.append(yh * c + pltpu.roll(yh, LANES - 16, 1) * sa + pltpu.roll(yh, 16, 1) * sb)
    return jnp.concatenate(outs, axis=1)


def _norm_proj_kernel(*refs, groups, tn, rope_q, rope_k, q_scale):
    n_out = len(groups)
    x_ref, g_ref, w_ref = refs[:3]
    pos = 3
    if rope_q is not None:
        c_ref, sa_ref, sb_ref = refs[3:6]
        pos = 6
    out_refs = refs[pos:pos + n_out]
    xn = _rms(x_ref[...], g_ref[...]).astype(BF16)
    for (start, count, _, act), o_ref in zip(groups, out_refs):
        for jj in range(start, start + count):
            y = jnp.dot(xn, w_ref[:, jj * tn:(jj + 1) * tn], preferred_element_type=F32)
            if rope_q is not None and jj == rope_q:
                y = _rope_tile(y, c_ref[...], sa_ref[...], sb_ref[...]) * q_scale
            elif rope_q is not None and jj == rope_k:
                y = _rope_tile(y, c_ref[...], sa_ref[...], sb_ref[...])
            if act is not None:
                y = act(y)
            o_ref[:, (jj - start) * tn:(jj - start + 1) * tn] = y.astype(o_ref.dtype)


def _norm_proj(x, g, w, groups, tm, tn, rope=None, rope_q=None, rope_k=None, q_scale=1.0):
    M, K = x.shape
    in_specs = [
        pl.BlockSpec((tm, K), lambda i: (i, 0)),
        pl.BlockSpec((1, K), lambda i: (0, 0)),
        pl.BlockSpec(w.shape, lambda i: (0, 0)),
    ]
    args = [x, g, w]
    if rope is not None:
        assert rope[0].shape[0] % tm == 0 and M % tm == 0
        pos_blocks = rope[0].shape[0] // tm
        for t in rope:
            in_specs.append(pl.BlockSpec((tm, LANES), lambda i, pb=pos_blocks: (i % pb, 0)))
            args.append(t)
    out_specs, out_shapes = [], []
    for (_, count, dtype, _) in groups:
        out_specs.append(pl.BlockSpec((tm, count * tn), lambda i: (i, 0)))
        out_shapes.append(jax.ShapeDtypeStruct((M, count * tn), dtype))
    kern = functools.partial(_norm_proj_kernel, groups=tuple(groups), tn=tn,
                             rope_q=rope_q if rope is not None else None, rope_k=rope_k, q_scale=q_scale)
    return pl.pallas_call(
        kern,
        grid=(M // tm,),
        in_specs=in_specs,
        out_specs=out_specs,
        out_shape=out_shapes,
        compiler_params=_cparams(("parallel",)),
        name="norm_proj",
    )(*args)


def _out_proj_kernel(*refs, n_in):
    a_refs = refs[:n_in]
    w_ref, x_ref, g_ref, gn_ref, o_ref, on_ref, wb_ref = refs[n_in:]

    @pl.when(pl.program_id(0) == 0)
    def _():
        wb_ref[...] = w_ref[...].astype(BF16)

    tm = x_ref.shape[0]
    half = min(tm, 2 * LANES)
    for r0 in range(0, tm, half):
        rs = slice(r0, r0 + half)
        y, row = None, 0
        for a_ref in a_refs:
            kk = a_ref.shape[1]
            d = jnp.dot(a_ref[rs, :].astype(BF16), wb_ref[row:row + kk, :], preferred_element_type=F32)
            y = d if y is None else y + d
            row += kk
        x1 = x_ref[rs, :] + _rms(y, g_ref[...])
        o_ref[rs, :] = x1
        on_ref[rs, :] = _rms(x1, gn_ref[...]).astype(on_ref.dtype)


def _out_proj(acts, w, x, g, g_next, tm):
    M, D = x.shape
    n_in = len(acts)
    assert sum(a.shape[1] for a in acts) == w.shape[0]
    row_tile = pl.BlockSpec((tm, D), lambda i: (i, 0))
    gain = pl.BlockSpec((1, D), lambda i: (0, 0))
    in_specs = [pl.BlockSpec((tm, a.shape[1]), lambda i: (i, 0)) for a in acts]
    in_specs += [pl.BlockSpec(w.shape, lambda i: (0, 0), pipeline_mode=pl.Buffered(1)), row_tile, gain, gain]
    return pl.pallas_call(
        functools.partial(_out_proj_kernel, n_in=n_in),
        grid=(M // tm,),
        in_specs=in_specs,
        out_specs=[row_tile, row_tile],
        out_shape=[jax.ShapeDtypeStruct((M, D), F32), jax.ShapeDtypeStruct((M, D), BF16)],
        scratch_shapes=[pltpu.VMEM(w.shape, BF16)],
        compiler_params=_cparams(("arbitrary",)),
        name="out_proj",
    )(*acts, w, x, g, g_next)


def _ffn_kernel(x_ref, xn_ref, wu_ref, wd_ref, g2_ref, o_ref, acc_ref):
    f = pl.program_id(1)

    @pl.when(f == 0)
    def _():
        acc_ref[...] = jnp.zeros_like(acc_ref)

    hid = jnp.dot(xn_ref[...], wu_ref[...].astype(BF16), preferred_element_type=F32)
    hid = jnp.square(jnp.maximum(hid, 0.0))
    acc_ref[...] += jnp.dot(hid.astype(BF16), wd_ref[...].astype(BF16), preferred_element_type=F32)

    @pl.when(f == pl.num_programs(1) - 1)
    def _():
        o_ref[...] = x_ref[...] + _rms(acc_ref[...], g2_ref[...])


def _ffn(x, xn, wu, wd, layer, g2, tm, tf):
    M, D = x.shape
    FF = wu.shape[2]
    return pl.pallas_call(
        _ffn_kernel,
        grid=(M // tm, FF // tf),
        in_specs=[
            pl.BlockSpec((tm, D), lambda i, f: (i, 0)),
            pl.BlockSpec((tm, D), lambda i, f: (i, 0)),
            pl.BlockSpec((None, D, tf), lambda i, f: (layer, 0, f)),
            pl.BlockSpec((None, tf, D), lambda i, f: (layer, f, 0)),
            pl.BlockSpec((1, D), lambda i, f: (0, 0)),
        ],
        out_specs=pl.BlockSpec((tm, D), lambda i, f: (i, 0)),
        out_shape=jax.ShapeDtypeStruct((M, D), F32),
        scratch_shapes=[pltpu.VMEM((tm, D), F32)],
        compiler_params=_cparams(("parallel", "arbitrary")),
        name="ffn",
    )(x, xn, wu, wd, g2)


def _lru_gates(xc, wa_ref, ba, wx_ref, bx, lam):
    xcb = xc.astype(BF16)
    ng = wa_ref.shape[0]
    gw = wa_ref.shape[1]
    ra, ri = [], []
    for g in range(ng):
        xs = xcb[:, g * gw:(g + 1) * gw]
        ra.append(jnp.dot(xs, wa_ref[g], preferred_element_type=F32))
        ri.append(jnp.dot(xs, wx_ref[g], preferred_element_type=F32))
    r = jax.nn.sigmoid(jnp.concatenate(ra, axis=1) + ba)
    i = jax.nn.sigmoid(jnp.concatenate(ri, axis=1) + bx)
    log_a = -LRU_C * r * _softplus(-lam)
    a = jnp.exp(log_a)
    om = 1.0 - a * a
    b = jnp.where(om > 0.0, om * lax.rsqrt(om), 0.0) * (i * xc)
    return a, b


def _conv4_rows(xe, halo, w, b):
    a = w[0:1, :] * xe
    for j in range(1, CONV_W):
        a = pltpu.roll(a, 1, 0) + w[j:j + 1, :] * xe
    return b + a[halo:]


def _conv4(tail_ref, x, first, w_ref, b):
    @pl.when(first)
    def _():
        tail_ref[...] = jnp.zeros_like(tail_ref)

    xe = jnp.concatenate([tail_ref[...], x], axis=0)
    tail_ref[...] = x[x.shape[0] - SUBLANES:, :]
    return _conv4_rows(xe, SUBLANES, w_ref[...], b)


def _rglru_prompt_kernel(xa_ref, ga_ref, cw_ref, cb_ref, wa_ref, ba_ref, wx_ref, bx_ref, lam_ref,
                         ya_ref, hl_ref, xe_ref, hc_ref, a_s, b_s, h_s):
    t = pl.program_id(1)
    T = xa_ref.shape[0]

    @pl.when(t == 0)
    def _():
        hc_ref[...] = jnp.zeros_like(hc_ref)

    xc = _conv4(xe_ref, xa_ref[...], t == 0, cw_ref, cb_ref[...])
    a, b = _lru_gates(xc, wa_ref, ba_ref[...], wx_ref, bx_ref[...], lam_ref[...])
    a_s[...] = a
    b_s[...] = b

    def body(s, h):
        h = a_s[pl.ds(s, 1), :] * h + b_s[pl.ds(s, 1), :]
        h_s[pl.ds(s, 1), :] = h
        return h

    h = lax.fori_loop(0, T, body, hc_ref[...], unroll=8)
    hc_ref[...] = h
    ya_ref[...] = (h_s[...] * _gelu_tanh(ga_ref[...])).astype(ya_ref.dtype)

    @pl.when(t == pl.num_programs(1) - 1)
    def _():
        hl_ref[...] = h


def _rglru_prompt(xaga, cw, cb, wa_bd, ba, wx_bd, bx, lam, ts):
    B, S, _ = xaga.shape
    C = cw.shape[-1]
    wspec = lambda a: pl.BlockSpec(a.shape, lambda b, t: (0,) * a.ndim)
    return pl.pallas_call(
        _rglru_prompt_kernel,
        grid=(B, S // ts),
        in_specs=[
            pl.BlockSpec((None, ts, C), lambda b, t: (b, t, 0)),
            pl.BlockSpec((None, ts, C), lambda b, t: (b, t, 1)),
            wspec(cw), wspec(cb), wspec(wa_bd), wspec(ba), wspec(wx_bd), wspec(bx), wspec(lam),
        ],
        out_specs=[
            pl.BlockSpec((None, ts, C), lambda b, t: (b, t, 0)),
            pl.BlockSpec((None, 1, C), lambda b, t: (b, 0, 0)),
        ],
        out_shape=[jax.ShapeDtypeStruct((B, S, C), BF16), jax.ShapeDtypeStruct((B, 1, C), F32)],
        scratch_shapes=[pltpu.VMEM((SUBLANES, C), F32), pltpu.VMEM((1, C), F32),
                        pltpu.VMEM((ts, C), F32), pltpu.VMEM((ts, C), F32), pltpu.VMEM((ts, C), F32)],
        compiler_params=_cparams(("parallel", "arbitrary")),
        name="rglru_prompt",
    )(xaga, xaga, cw, cb, wa_bd, ba, wx_bd, bx, lam)


def _rglru_sample_kernel(xa_ref, ga_ref, st_ref, h0_ref, cw_ref, cb_ref, wa_ref, ba_ref, wx_ref, bx_ref,
                         lam_ref, ya_ref, hl_ref):
    T = xa_ref.shape[0]
    rows = [st_ref[j] for j in range(CONV_W - 1)] + [xa_ref[s] for s in range(T)]
    h = h0_ref[...]
    for s in range(T):
        y = cb_ref[...]
        for j in range(CONV_W):
            y = y + cw_ref[j:j + 1, :] * rows[s + j]
        a, b = _lru_gates(y, wa_ref, ba_ref[...], wx_ref, bx_ref[...], lam_ref[...])
        h = a * h + b
        ya_ref[s] = (h * _gelu_tanh(ga_ref[s])).astype(ya_ref.dtype)
    hl_ref[...] = h


def _rglru_sample(xa_t, ga_t, st_t, h0, cw, cb, wa_bd, ba, wx_bd, bx, lam):
    T, B, C = xa_t.shape
    args = (xa_t, ga_t, st_t, h0, cw, cb, wa_bd, ba, wx_bd, bx, lam)
    return pl.pallas_call(
        _rglru_sample_kernel,
        grid=(1,),
        in_specs=[pl.BlockSpec(a.shape, lambda i, n=a.ndim: (0,) * n) for a in args],
        out_specs=[pl.BlockSpec((T, B, C), lambda i: (0, 0, 0)), pl.BlockSpec((B, C), lambda i: (0, 0))],
        out_shape=[jax.ShapeDtypeStruct((T, B, C), F32), jax.ShapeDtypeStruct((B, C), F32)],
        compiler_params=_cparams(("arbitrary",)),
        name="rglru_sample",
    )(*args)


def _attn_prompt_kernel(q_ref, k_ref, v_ref, o_ref, *scratch, groups, qb):
    S = q_ref.shape[0]
    E = q_ref.shape[1]
    dist2 = (lax.broadcasted_iota(jnp.int32, (qb, 2 * qb), 0) + qb
             - lax.broadcasted_iota(jnp.int32, (qb, 2 * qb), 1))
    dist1 = lax.broadcasted_iota(jnp.int32, (qb, qb), 0) - lax.broadcasted_iota(jnp.int32, (qb, qb), 1)

    def rows(start, d):
        return pl.ds(start, qb) if d == 1 else pl.ds(start, qb, stride=d)

    ng = len(groups)
    s_scr, p_scr = scratch[3 * ng], scratch[3 * ng + 1]
    dn_t = (((1,), (1,)), ((), ()))

    for gi, (w, d) in enumerate(groups):
        back = w // d
        nb = (S // d) // qb
        nblk = nb * d
        two = nb > 1
        kw = 2 * qb if two else qb
        dist = dist2 if two else dist1
        lim_all = jnp.full((qb, kw), back, jnp.int32)
        lim_first = jnp.minimum(lax.broadcasted_iota(jnp.int32, (qb, kw), 0), back)
        m_s, l_s, acc_s = scratch[3 * gi:3 * gi + 3]

        def locate(idx, d=d):
            if d == 1:
                n = idx
                start = pl.multiple_of(idx * qb, qb)
                prev = pl.multiple_of(jnp.maximum(idx - 1, 0) * qb, qb)
            else:
                n = idx // d
                start = n * (qb * d) + idx % d
                prev = jnp.where(n > 0, start - qb * d, start)
            return start, prev, n

        def gather(ref, start, prev, d=d, two=two):
            if two:
                return jnp.concatenate([ref[rows(prev, d), :], ref[rows(start, d), :]], axis=0).astype(BF16)
            return ref[rows(start, d), :].astype(BF16)

        def scores(idx, c, kw=kw, locate=locate, gather=gather, d=d):
            start, prev, _ = locate(idx)
            q = q_ref[rows(start, d), :].astype(BF16)
            s_scr[pl.ds(pl.multiple_of(idx * qb, qb), qb), 0:kw] = lax.dot_general(
                q, gather(k_ref, start, prev), dn_t, preferred_element_type=F32)
            return c

        def softmax(idx, c, kw=kw, locate=locate, d=d, two=two, dist=dist, lim_all=lim_all,
                    lim_first=lim_first, m_s=m_s):
            start, _, n = locate(idx)
            blk = pl.ds(pl.multiple_of(idx * qb, qb), qb)
            lim = jnp.where(n > 0, lim_all, lim_first) if two else lim_all
            valid = (dist >= 0) & (dist <= lim)
            s = jnp.where(valid, s_scr[blk, 0:kw], NEG_INF)
            m_b = jnp.max(s, axis=-1, keepdims=True)
            p_scr[blk, 0:kw] = jnp.exp(s - m_b).astype(BF16)
            m_s[rows(start, d), :] = jnp.broadcast_to(m_b, (qb, E))
            return c

        def values(idx, c, kw=kw, locate=locate, gather=gather, d=d, acc_s=acc_s, l_s=l_s):
            start, prev, _ = locate(idx)
            p = p_scr[pl.ds(pl.multiple_of(idx * qb, qb), qb), 0:kw]
            v1 = jnp.concatenate([gather(v_ref, start, prev), jnp.ones((kw, E), BF16)], axis=1)
            r = jnp.dot(p, v1, preferred_element_type=F32)
            acc_s[rows(start, d), :] = r[:, :E]
            l_s[rows(start, d), :] = r[:, E:]
            return c

        lax.fori_loop(0, nblk, scores, 0, unroll=True)
        lax.fori_loop(0, nblk, softmax, 0, unroll=True)
        lax.fori_loop(0, nblk, values, 0, unroll=True)

    ng = len(groups)

    def merge(i, c):
        rs = pl.ds(pl.multiple_of(i * qb, qb), qb)
        ms = [scratch[3 * g][rs, :] for g in range(ng)]
        m = functools.reduce(jnp.maximum, ms)
        num = den = None
        for g in range(ng):
            wg = jnp.exp(ms[g] - m)
            n_g = wg * scratch[3 * g + 2][rs, :]
            d_g = wg * scratch[3 * g + 1][rs, :]
            num = n_g if num is None else num + n_g
            den = d_g if den is None else den + d_g
        o_ref[rs, :] = (num / den).astype(o_ref.dtype)
        return c

    lax.fori_loop(0, S // qb, merge, 0, unroll=2)


def _attn_prompt(q, q_off, k, v, groups, qb):
    B, S, D = k.shape
    nh = D // LANES
    for (w, d) in groups:
        assert S % (d * qb) == 0 and w // d <= qb
    spec = pl.BlockSpec((None, S, LANES), lambda b, h: (b, 0, h))
    q_spec = pl.BlockSpec((None, S, LANES), lambda b, h: (b, 0, q_off + h))
    return pl.pallas_call(
        functools.partial(_attn_prompt_kernel, groups=tuple(groups), qb=qb),
        grid=(B, nh),
        in_specs=[q_spec, spec, spec],
        out_specs=spec,
        out_shape=jax.ShapeDtypeStruct((B, S, D), BF16),
        scratch_shapes=[pltpu.VMEM((S, LANES), F32)] * (3 * len(groups))
        + [pltpu.VMEM((S, 2 * qb), F32), pltpu.VMEM((S, 2 * qb), BF16)],
        compiler_params=_cparams(("parallel", "parallel")),
        name="attn_prompt",
    )(q, k, v)


def _attn_sample_kernel(q_ref, kn_ref, vn_ref, kt_ref, vt_ref, ks_ref, vs_ref, mt_ref, ms_ref, mn_ref, o_ref):
    nq, hd = q_ref.shape
    qb = jnp.concatenate([q_ref[...], jnp.zeros((LANES - nq, hd), F32)], axis=0).astype(BF16)
    nn = kn_ref.shape[0]
    zn = jnp.zeros((LANES - nn, hd), F32)
    key_sets = (
        (kt_ref[...], vt_ref[...], mt_ref[...]),
        (ks_ref[...].reshape(-1, hd), vs_ref[...].reshape(-1, hd), ms_ref[...]),
        (jnp.concatenate([kn_ref[...], zn], axis=0), jnp.concatenate([vn_ref[...], zn], axis=0), mn_ref[...]),
    )
    scored = []
    for k, v, bias in key_sets:
        s = lax.dot_general(k.astype(BF16), qb, (((1,), (1,)), ((), ())), preferred_element_type=F32)
        scored.append((s + bias, v))
    m = functools.reduce(jnp.maximum, [jnp.max(s, axis=0, keepdims=True) for s, _ in scored])
    m = jnp.where(m == NEG_INF, 0.0, m)
    acc = None
    for s, v in scored:
        p = jnp.exp(s - m).astype(BF16)
        v1 = jnp.concatenate([v.astype(BF16), jnp.ones(v.shape, BF16)], axis=1)
        r = lax.dot_general(p, v1, (((0,), (0,)), ((), ())), preferred_element_type=F32)
        acc = r if acc is None else acc + r
    o_ref[...] = acc[:nq, :hd] / acc[:nq, hd:]


def _attn_sample(q, kn, vn, kc, vc, T, nh):
    B, nq, hd = q.shape
    wb = kc.shape[1] // nh
    far = max(DILATIONS, key=lambda wd: wd[0])
    near = [g for g in DILATIONS if g != far]
    R = max(w for w, _ in near)
    d_far = far[1]
    assert near and wb % R == 0 and wb % d_far == 0 and T <= d_far and nq == T * nh and nq <= LANES
    A = wb // d_far

    def mult(delta, groups):
        m = np.zeros(delta.shape, np.float32)
        for (w, d) in groups:
            m = m + ((delta >= 0) & (delta % d == 0) & (delta <= (w // d) * d))
        return m

    def table(pos_of_row, head_of_row, groups):
        lane = np.arange(LANES)
        t, hq = lane // nh, lane % nh
        delta = (wb + t)[None, :] - pos_of_row[:, None]
        tab = mult(delta, groups) * (head_of_row[:, None] == hq[None, :]) * (t < T)[None, :]
        return jnp.asarray(np.where(tab > 0, np.log(np.maximum(tab, 1.0)), NEG_INF), F32)

    rt = np.arange(R * nh)
    mt = table(wb - R + rt // nh, rt % nh, near)
    rs = np.arange(A * T * nh)
    ms = table((rs // (T * nh)) * d_far + (rs % (T * nh)) // nh, rs % nh, [far])
    rn = np.arange(LANES)
    mn = table(np.where(rn < nq, wb + rn // nh, -10 ** 9), rn % nh, DILATIONS)

    small = pl.BlockSpec((None, nq, hd), lambda b: (b, 0, 0))
    tail = pl.BlockSpec((None, R * nh, hd), lambda b: (b, wb // R - 1, 0))
    strided = pl.BlockSpec((None, A, T * nh, hd), lambda b: (b, 0, 0, 0))
    const = lambda a: pl.BlockSpec(a.shape, lambda b: (0, 0))
    kc4 = kc.reshape(B, A, d_far * nh, hd)
    vc4 = vc.reshape(B, A, d_far * nh, hd)
    return pl.pallas_call(
        _attn_sample_kernel,
        grid=(B,),
        in_specs=[small, small, small, tail, tail, strided, strided, const(mt), const(ms), const(mn)],
        out_specs=small,
        out_shape=jax.ShapeDtypeStruct((B, nq, hd), F32),
        compiler_params=_cparams(("parallel",)),
        name="attn_sample",
    )(q, kn, vn, kc, vc, kc4, vc4, mt, ms, mn)


def _mlstm_qkv_gates(xc, xm, wq_ref, wk_ref, wv_ref, wg_ref, bg_ref, q_ref, k_ref, v_ref, xc_ref, g_ref, k_scale,
                     rs=slice(None)):
    xcb = xc.astype(BF16)
    xmb = xm.astype(BF16)
    ng, gw = wq_ref.shape[0], wq_ref.shape[1]
    qs, ks, vs = [], [], []
    for g in range(ng):
        sl = slice(g * gw, (g + 1) * gw)
        qs.append(jnp.dot(xcb[:, sl], wq_ref[g], preferred_element_type=F32))
        ks.append(jnp.dot(xcb[:, sl], wk_ref[g], preferred_element_type=F32))
        vs.append(jnp.dot(xmb[:, sl], wv_ref[g], preferred_element_type=F32))
    k = jnp.concatenate(ks, axis=1)
    qb = jnp.concatenate(qs, axis=1).astype(BF16)
    vb = jnp.concatenate(vs, axis=1).astype(BF16)
    qkv = jnp.concatenate([qb, k.astype(BF16), vb], axis=1)
    g_ref[rs, :] = jnp.dot(qkv, wg_ref[...], preferred_element_type=F32) + bg_ref[...]
    q_ref[rs, :] = qb
    k_ref[rs, :] = (k * k_scale).astype(k_ref.dtype)
    v_ref[rs, :] = vb
    xc_ref[rs, :] = xcb


def _mlstm_pre_sample_kernel(xm_ref, st_ref, cw_ref, cb_ref, wq_ref, wk_ref, wv_ref, wg_ref, bg_ref,
                             q_ref, k_ref, v_ref, xc_ref, g_ref, *, k_scale):
    T = xm_ref.shape[0]
    rows = [st_ref[j] for j in range(CONV_W - 1)] + [xm_ref[s] for s in range(T)]
    conv = []
    for s in range(T):
        y = cb_ref[...]
        for j in range(CONV_W):
            y = y + cw_ref[j:j + 1, :] * rows[s + j]
        conv.append(y)
    xc = _silu(jnp.concatenate(conv, axis=0))
    xm = jnp.concatenate(rows[CONV_W - 1:], axis=0)
    _mlstm_qkv_gates(xc, xm, wq_ref, wk_ref, wv_ref, wg_ref, bg_ref, q_ref, k_ref, v_ref, xc_ref, g_ref, k_scale)


def _mlstm_pre_sample(xm_t, st_t, cw, cb, wq_bd, wk_bd, wv_bd, wg, bg, k_scale):
    T, B, C = xm_t.shape
    args = (xm_t, st_t, cw, cb, wq_bd, wk_bd, wv_bd, wg, bg)
    full = lambda shape: pl.BlockSpec(shape, lambda i: (0,) * len(shape))
    return pl.pallas_call(
        functools.partial(_mlstm_pre_sample_kernel, k_scale=k_scale),
        grid=(1,),
        in_specs=[full(a.shape) for a in args],
        out_specs=[full((T * B, C))] * 4 + [full((T * B, 2 * LANES))],
        out_shape=[jax.ShapeDtypeStruct((T * B, C), BF16)] * 4 + [jax.ShapeDtypeStruct((T * B, 2 * LANES), F32)],
        compiler_params=_cparams(("arbitrary",)),
        name="mlstm_pre_sample",
    )(*args)


CONV_HALO = 16


def _mlstm_in_kernel(x_ref, xh_ref, g_ref, w_ref, cw_ref, cb_ref, wq_ref, wk_ref, wv_ref, wg_ref, bg_ref,
                     q_ref, k_ref, v_ref, xc_ref, gz_ref, gt_ref, tail_ref, *, tiles_per_seq, tn, k_scale):
    tm = x_ref.shape[0]
    C = cw_ref.shape[1]
    first = (pl.program_id(0) % tiles_per_seq) == 0
    g = g_ref[...]
    xn = _rms(x_ref[...], g)
    xe = jnp.concatenate([_rms(xh_ref[...], g), xn], axis=0).astype(BF16)
    xn = xn.astype(BF16)
    dead_halo = lax.broadcasted_iota(jnp.int32, (CONV_HALO + tm, tn), 0) < jnp.where(first, CONV_HALO, 0)
    xmes = []
    for jj in range(C // tn):
        cs = slice(jj * tn, (jj + 1) * tn)
        xm_e = jnp.dot(xe, w_ref[:, cs], preferred_element_type=F32)
        xmes.append(jnp.where(dead_halo, 0.0, xm_e))
        gz_ref[:, cs] = _silu(jnp.dot(xn, w_ref[:, C + jj * tn:C + (jj + 1) * tn],
                                      preferred_element_type=F32)).astype(gz_ref.dtype)
    half = LANES
    for r0 in range(0, tm, half):
        xc = jnp.concatenate(
            [_silu(_conv4_rows(xm_e[r0:r0 + CONV_HALO + half], CONV_HALO, cw_ref[:, jj * tn:(jj + 1) * tn],
                               cb_ref[:, jj * tn:(jj + 1) * tn])) for jj, xm_e in enumerate(xmes)], axis=1)
        xm = jnp.concatenate([xm_e[CONV_HALO + r0:CONV_HALO + r0 + half] for xm_e in xmes], axis=1)
        _mlstm_qkv_gates(xc, xm, wq_ref, wk_ref, wv_ref, wg_ref, bg_ref, q_ref, k_ref, v_ref, xc_ref, gt_ref,
                         k_scale, rs=slice(r0, r0 + half))
    tail_ref[...] = jnp.concatenate([xm_e[CONV_HALO + tm - SUBLANES:] for xm_e in xmes], axis=1)


def _mlstm_in(x, g, w, cw, cb, wq_bd, wk_bd, wv_bd, wg, bg, seq, tm, k_scale):
    M, K = x.shape
    C = cw.shape[1]
    assert seq % tm == 0 and tm % CONV_HALO == 0
    hb = tm // CONV_HALO
    const = lambda a: pl.BlockSpec(a.shape, lambda i: (0,) * a.ndim, pipeline_mode=pl.Buffered(1))
    act = pl.BlockSpec((tm, C), lambda i: (i, 0))
    return pl.pallas_call(
        functools.partial(_mlstm_in_kernel, tiles_per_seq=seq // tm, tn=1024, k_scale=k_scale),
        grid=(M // tm,),
        in_specs=[pl.BlockSpec((tm, K), lambda i: (i, 0)),
                  pl.BlockSpec((CONV_HALO, K), lambda i: (jnp.maximum(i * hb - 1, 0), 0)),
                  const(g), const(w), const(cw), const(cb), const(wq_bd), const(wk_bd), const(wv_bd),
                  const(wg), const(bg)],
        out_specs=[act, act, act, act, act, pl.BlockSpec((tm, 2 * LANES), lambda i: (i, 0)),
                   pl.BlockSpec((None, SUBLANES, C), lambda i: (i, 0, 0))],
        out_shape=[jax.ShapeDtypeStruct((M, C), BF16)] * 5 + [jax.ShapeDtypeStruct((M, 2 * LANES), F32),
                                                              jax.ShapeDtypeStruct((M // tm, SUBLANES, C), F32)],
        compiler_params=_cparams(("parallel",)),
        name="mlstm_in",
    )(x, x, g, w, cw, cb, wq_bd, wk_bd, wv_bd, wg, bg)


def _mlstm_core_kernel(*refs, c, n_valid, nh, zero_init):
    q_ref, k_ref, v_ref, xc_ref, gz_ref, g_ref, nw_ref, sk_ref = refs[:8]
    pos = 8
    if not zero_init:
        c0_ref, n0_ref, m0_ref = refs[8:11]
        pos = 11
    y_ref, C_ref, n_ref, m_ref = refs[pos:pos + 4]
    ch = pl.program_id(1)
    c_in = q_ref.shape[0]
    DH = q_ref.shape[1] // nh

    @pl.when(ch == 0)
    def _():
        if zero_init:
            C_ref[...] = jnp.zeros_like(C_ref)
            n_ref[...] = jnp.zeros_like(n_ref)
            m_ref[...] = jnp.zeros_like(m_ref)
        else:
            C_ref[...] = c0_ref[...]
            n_ref[...] = n0_ref[...]
            m_ref[...] = m0_ref[...]

    def padded(x):
        if c_in == c:
            return x
        return jnp.concatenate([x, jnp.zeros((c - c_in, x.shape[1]), x.dtype)], axis=0)

    g = padded(g_ref[...])
    row = lax.broadcasted_iota(jnp.int32, (c, LANES), 0)
    valid = row < n_valid
    gi = jnp.where(valid, g[:, :LANES], NEG_INF)
    lf = jnp.where(valid, -_softplus(-g[:, LANES:]), 0.0)
    tri_f = (lax.broadcasted_iota(jnp.int32, (c, c), 1) <= lax.broadcasted_iota(jnp.int32, (c, c), 0))
    tri_b = tri_f.astype(BF16)
    bcum, rest = None, lf
    for _ in range(3):
        part = rest.astype(BF16)
        rest = rest - part.astype(F32)
        d = jnp.dot(tri_b, part, preferred_element_type=F32)
        bcum = d if bcum is None else bcum + d
    bcum_t = bcum.T
    gi_t = gi.T
    causal = tri_f

    for h in range(nh):
        sl = slice(h * DH, (h + 1) * DH)
        qh = padded(q_ref[:, sl])
        kh = padded(k_ref[:, sl])
        vh = padded(v_ref[:, sl])
        bc_col = bcum[:, h:h + 1]
        bc_row = bcum_t[h:h + 1, :]
        i_row = gi_t[h:h + 1, :]
        i_col = gi[:, h:h + 1]
        m_prev = m_ref[h:h + 1, 0:1]
        log_intra = jnp.where(causal, bc_col - bc_row + i_row, NEG_INF)
        log_inter = bc_col + m_prev
        m_t = jnp.maximum(log_inter, jnp.max(log_intra, axis=1, keepdims=True))
        qk = lax.dot_general(qh, kh, (((1,), (1,)), ((), ())), preferred_element_type=F32)
        w_intra = jnp.exp(log_intra - m_t) * qk
        w_inter = jnp.exp(log_inter - m_t)
        C_old = C_ref[h]
        n_old = n_ref[h:h + 1, :]
        num = jnp.dot(w_intra.astype(BF16), vh, preferred_element_type=F32)
        num = num + w_inter * jnp.dot(qh, C_old.astype(BF16), preferred_element_type=F32)
        n8 = jnp.broadcast_to(n_old, (SUBLANES, DH)).astype(BF16)
        qn = lax.dot_general(qh, n8, (((1,), (1,)), ((), ())), preferred_element_type=F32)[:, 0:1]
        den = jnp.sum(w_intra, axis=1, keepdims=True) + w_inter * qn
        hh = num * (1.0 / jnp.maximum(jnp.abs(den), jnp.exp(-m_t)))
        m_new = m_t[c - 1:c, :]
        bc_last = bc_col[c - 1:c, :]
        w_state = jnp.exp(bc_last - bc_col + i_col - m_new)
        decay = jnp.exp(bc_last + m_prev - m_new)
        kw = kh.astype(F32) * w_state
        C_ref[h] = decay * C_old + lax.dot_general(
            kw.astype(BF16), vh, (((0,), (0,)), ((), ())), preferred_element_type=F32)
        n_ref[h:h + 1, :] = decay * n_old + jnp.sum(kw, axis=0, keepdims=True)
        m_ref[h:h + 1, :] = jnp.broadcast_to(m_new, (1, LANES))
        hn = hh * lax.rsqrt(jnp.mean(hh * hh, axis=-1, keepdims=True) + EPS)
        hn = hn[:c_in] * nw_ref[:, sl]
        y = (hn + sk_ref[:, sl] * xc_ref[:, sl].astype(F32)) * gz_ref[:, sl].astype(F32)
        y_ref[:, sl] = y.astype(y_ref.dtype)


def _mlstm_core(q, k, v, xc, gz, gates, nw, sk, state, c, n_valid, nh):
    B, S, C = q.shape
    c_in = min(c, S)
    DH = C // nh
    zero_init = state is None
    act = pl.BlockSpec((None, c_in, C), lambda b, t: (b, t, 0))
    in_specs = [act, act, act, act, act,
                pl.BlockSpec((None, c_in, 2 * LANES), lambda b, t: (b, t, 0)),
                pl.BlockSpec((1, C), lambda b, t: (0, 0)), pl.BlockSpec((1, C), lambda b, t: (0, 0))]
    args = [q, k, v, xc, gz, gates, nw, sk]
    C_spec = pl.BlockSpec((None, nh, DH, DH), lambda b, t: (b, 0, 0, 0))
    n_spec = pl.BlockSpec((None, nh, DH), lambda b, t: (b, 0, 0))
    m_spec = pl.BlockSpec((None, nh, LANES), lambda b, t: (b, 0, 0))
    if not zero_init:
        in_specs += [C_spec, n_spec, m_spec]
        args += list(state)
    return pl.pallas_call(
        functools.partial(_mlstm_core_kernel, c=c, n_valid=n_valid, nh=nh, zero_init=zero_init),
        grid=(B, S // c_in),
        in_specs=in_specs,
        out_specs=[act, C_spec, n_spec, m_spec],
        out_shape=[jax.ShapeDtypeStruct((B, S, C), BF16),
                   jax.ShapeDtypeStruct((B, nh, DH, DH), F32),
                   jax.ShapeDtypeStruct((B, nh, DH), F32),
                   jax.ShapeDtypeStruct((B, nh, LANES), F32)],
        compiler_params=_cparams(("parallel", "arbitrary")),
        name="mlstm_core",
    )(*args)


def _rope_tables(pos, hd):
    rot = hd // 4
    half = rot // 2
    inv = ROPE_THETA ** (-np.arange(0, rot, 2, dtype=np.float64) / rot)
    ang = pos.astype(np.float64)[:, None] * inv[None, :]
    cos, sin = np.cos(ang), np.sin(ang)
    n = pos.shape[0]
    c = np.concatenate([cos, cos, np.ones((n, hd - rot))], axis=1)
    sa = np.concatenate([-sin, np.zeros((n, hd - half))], axis=1)
    sb = np.concatenate([np.zeros((n, half)), sin, np.zeros((n, hd - rot))], axis=1)
    return tuple(jnp.asarray(t, F32) for t in (c, sa, sb))


def _regroup_block_diag(w, group):
    nb, bi, bo = w.shape
    per = group // bi
    a = w.reshape(nb // per, per * bi, bo)
    cols = np.arange(per * bo)
    tile_cols = jnp.asarray(cols[None, :] % bo == np.arange(bo)[:, None], BF16)
    same_block = jnp.asarray((np.arange(per * bi)[:, None] // bi) == (cols[None, :] // bo), BF16)
    tiled = jnp.einsum("gro,oc->grc", a.astype(BF16), tile_cols, preferred_element_type=F32)
    return tiled.astype(BF16) * same_block


def _layer0(x, rope, p, prompt, state):
    Bg, L, D = x.shape
    M = Bg * L
    hd = D // H_B
    tm = _tile(M, 1024)
    tm_in = _tile(rope[0].shape[0], 512)
    x2 = x.reshape(M, D)
    xagq, k, v = _norm_proj(x2, p["g_mix_pre0"], p["w_in_ab"], ((0, 3, F32, None), (3, 1, F32, None), (4, 1, F32, None)),
                            tm_in, D, rope=rope, rope_q=2, rope_k=3, q_scale=hd ** -0.5)
    xagq3 = xagq.reshape(Bg, L, 3 * D)
    lru_w = (p["conv_a_w"], p["conv_a_b"], p["wa_bd"], p["lru_ba"], p["wx_bd"], p["lru_bx"], p["lam"])
    if prompt:
        ya, h_last = _rglru_prompt(xagq3, *lru_w, ts=_tile(L, 512))
        ya = ya.reshape(M, D)
        h_last = h_last.reshape(Bg, D)
        new_conv = xagq3[:, L - (CONV_W - 1):, :D]
        yb = _attn_prompt(xagq3, 2 * D // LANES, k.reshape(Bg, L, D), v.reshape(Bg, L, D),
                          sorted(DILATIONS, key=lambda wd: -wd[1]), QBLOCK).reshape(M, D)
        wb = min(MAX_WINDOW, L)
        new_k = k.reshape(Bg, L, H_B, hd)[:, L - wb:]
        new_v = v.reshape(Bg, L, H_B, hd)[:, L - wb:]
    else:
        conv0, h0, k_buf, v_buf = state
        xagq_t = xagq3.transpose(1, 0, 2)
        xa_t, ga_t = xagq_t[:, :, :D], xagq_t[:, :, D:2 * D]
        st_t = conv0.transpose(1, 0, 2)
        ya_t, h_last = _rglru_sample(xa_t, ga_t, st_t, h0, *lru_w)
        ya = ya_t.transpose(1, 0, 2).reshape(M, D)
        new_conv = jnp.concatenate([conv0, xagq3[:, :, :D]], axis=1)[:, -(CONV_W - 1):]
        wb = k_buf.shape[1]
        native = lambda a: a.reshape(Bg, -1, hd)
        yb = _attn_sample(native(xagq3[:, :, 2 * D:]), native(k), native(v), native(k_buf), native(v_buf),
                          L, H_B).reshape(M, D)
        new_k = k.reshape(Bg, L, H_B, hd)
        new_v = v.reshape(Bg, L, H_B, hd)
    x1, x1n = _out_proj([ya, yb], p["w_out_ab"], x2, p["g_mix_post0"], p["g_ffn_pre0"], tm)
    x2o = _ffn(x1, x1n, p["w_up"], p["w_dn"], 0, p["g_ffn_post0"], _tile(M, 1024), 1024)
    return x2o.reshape(Bg, L, D), (new_conv, h_last, new_k, new_v)


def _layer1(x, p, prompt, state):
    Bg, L, D = x.shape
    M = Bg * L
    x2 = x.reshape(M, D)
    tm = _tile(M, 1024)
    DC = p["w_in_c"].shape[1] // 2
    DH = DC // NH_C
    pre_w = (p["conv_c_w"], p["conv_c_b"], p["wq_bd"], p["wk_bd"], p["wv_bd"], p["wg"], p["bg"])
    if prompt:
        tm_in = _tile(L, 512)
        c = _tile(L, 256)
        *acts, tails = _mlstm_in(x2, p["g_mix_pre1"], p["w_in_c"], *pre_w, seq=L, tm=tm_in, k_scale=DH ** -0.5)
        q, k, v, xc, gz, gates = [a.reshape(Bg, L, -1) for a in acts]
        y, C, n, m = _mlstm_core(q, k, v, xc, gz, gates, p["mlstm_norm"], p["mlstm_skip"], None, c, c, NH_C)
        new_conv = tails.reshape(Bg, L // tm_in, SUBLANES, DC)[:, -1, SUBLANES - (CONV_W - 1):]
        y = y.reshape(M, DC)
    else:
        nt = DC // 1024
        xm, gz = _norm_proj(x2, p["g_mix_pre1"], p["w_in_c"], ((0, nt, F32, None), (nt, nt, BF16, _silu)),
                            _tile(M, 512), 1024)
        xm3 = xm.reshape(Bg, L, DC)
        gz3 = gz.reshape(Bg, L, DC)
        conv0, C0, n0, m0 = state
        lp = 16
        outs = _mlstm_pre_sample(xm3.transpose(1, 0, 2), conv0.transpose(1, 0, 2), *pre_w, k_scale=DH ** -0.5)
        batch_major = lambda a: jnp.pad(a.reshape(L, Bg, -1).transpose(1, 0, 2), ((0, 0), (0, lp - L), (0, 0)))
        q, k, v, xc, gates = [batch_major(a) for a in outs]
        gz_p = jnp.pad(gz3, ((0, 0), (0, lp - L), (0, 0)))
        m0b = jnp.broadcast_to(m0[:, :, None], m0.shape + (LANES,))
        y, C, n, m = _mlstm_core(q, k, v, xc, gz_p, gates, p["mlstm_norm"], p["mlstm_skip"],
                                 (C0, n0, m0b), 32, L, NH_C)
        new_conv = jnp.concatenate([conv0, xm3], axis=1)[:, -(CONV_W - 1):]
        y = y[:, :L].reshape(M, DC)
    x1, x1n = _out_proj([y], p["w_out_c"], x2, p["g_mix_post1"], p["g_ffn_pre1"], tm)
    x2o = _ffn(x1, x1n, p["w_up"], p["w_dn"], 1, p["g_ffn_post1"], _tile(M, 1024), 1024)
    return x2o.reshape(Bg, L, D), (new_conv, C, n, m[:, :, 0])


def kernel(x_prompt, x_sample, state_rglru_conv, state_rglru_h, cache_swa_k, cache_swa_v, state_mlstm_conv, state_mlstm_C, state_mlstm_n, state_mlstm_m, norm_mix_pre, norm_mix_post, norm_ffn_pre, norm_ffn_post, w_ffn_up, w_ffn_down, w_in_ab, conv_a_w, conv_a_b, lru_wa, lru_ba, lru_wx, lru_bx, lru_lambda, w_out_ab, w_in_c, conv_c_w, conv_c_b, mlstm_wq, mlstm_wk, mlstm_wv, mlstm_w_gate, mlstm_b_gate, mlstm_norm, mlstm_skip, w_out_c):
    B, S, D = x_prompt.shape
    Bs, Ts, _ = x_sample.shape
    DA = conv_a_w.shape[-1]
    DC = conv_c_w.shape[-1]
    hd = D // H_B
    row = lambda a: a.reshape(1, -1)
    lane_pad = lambda a: jnp.pad(a, ((0, 0), (0, LANES - NH_C)))
    wg = jnp.concatenate([lane_pad(mlstm_w_gate[0][:, :NH_C]), lane_pad(mlstm_w_gate[0][:, NH_C:])], axis=1)
    bg = jnp.concatenate([lane_pad(mlstm_b_gate[:, :NH_C]), lane_pad(mlstm_b_gate[:, NH_C:])], axis=1)
    p = {
        "g_mix_pre0": row(norm_mix_pre[0]), "g_mix_post0": row(norm_mix_post[0]),
        "g_ffn_pre0": row(norm_ffn_pre[0]), "g_ffn_post0": row(norm_ffn_post[0]),
        "g_mix_pre1": row(norm_mix_pre[1]), "g_mix_post1": row(norm_mix_post[1]),
        "g_ffn_pre1": row(norm_ffn_pre[1]), "g_ffn_post1": row(norm_ffn_post[1]),
        "w_up": w_ffn_up, "w_dn": w_ffn_down,
        "w_in_ab": w_in_ab[0].astype(BF16),
        "conv_a_w": conv_a_w[0], "conv_a_b": row(conv_a_b[0]),
        "wa_bd": _regroup_block_diag(lru_wa[0], MXU_DIM), "lru_ba": row(lru_ba[0]),
        "wx_bd": _regroup_block_diag(lru_wx[0], MXU_DIM), "lru_bx": row(lru_bx[0]),
        "lam": row(lru_lambda[0]),
        "w_out_ab": w_out_ab[0],
        "w_in_c": w_in_c[0].astype(BF16),
        "conv_c_w": conv_c_w[0], "conv_c_b": row(conv_c_b[0]),
        "wq_bd": _regroup_block_diag(mlstm_wq[0], MXU_DIM),
        "wk_bd": _regroup_block_diag(mlstm_wk[0], MXU_DIM),
        "wv_bd": _regroup_block_diag(mlstm_wv[0], MXU_DIM),
        "wg": wg.astype(BF16), "bg": bg,
        "mlstm_norm": row(mlstm_norm[0]), "mlstm_skip": row(mlstm_skip[0]),
        "w_out_c": w_out_c[0],
    }
    dt = state_rglru_conv.dtype

    xp, st0 = _layer0(x_prompt, _rope_tables(np.arange(S), hd), p, True, None)
    xp, st1 = _layer1(xp, p, True, None)
    pos_s = PAST_LEN + (np.arange(Bs * Ts) % Ts)
    xs, ss0 = _layer0(x_sample, _rope_tables(pos_s, hd), p, False,
                      (state_rglru_conv[0], state_rglru_h[0], cache_swa_k[0], cache_swa_v[0]))
    xs, ss1 = _layer1(xs, p, False, (state_mlstm_conv[0], state_mlstm_C[0], state_mlstm_n[0], state_mlstm_m[0]))

    lead = lambda a: a[None].astype(dt)
    return (xp, xs,
            lead(st0[0]), lead(ss0[0]), lead(st0[1]), lead(ss0[1]),
            lead(st0[2]), lead(ss0[2]), lead(st0[3]), lead(ss0[3]),
            lead(st1[0]), lead(ss1[0]), lead(st1[1]), lead(ss1[1]),
            lead(st1[2]), lead(ss1[2]), lead(st1[3]), lead(ss1[3]))
```

```python
import functools
import math

import numpy as np
import jax
import jax.numpy as jnp
from jax import lax
from jax.experimental import pallas as pl
from jax.experimental.pallas import tpu as pltpu

F32 = jnp.float32
BF16 = jnp.bfloat16

NA_BLOCKS = 16
CONV_W = 4
LRU_C = 8.0
H_B = 8
ROPE_THETA = 500000.0
DILATIONS = ((128, 1), (512, 4), (2048, 16))
MAX_WINDOW = 2048
QBLOCK = 128
NH_C = 4
QKV_BLOCK = 4
PAST_LEN = 16384
EPS = 1e-6

LANES = 128
SUBLANES = 8
MXU_DIM = 256
VMEM_LIMIT = 56 * 1024 * 1024

NEG_INF = float("-inf")


def _cparams(sem):
    return pltpu.CompilerParams(dimension_semantics=sem, vmem_limit_bytes=VMEM_LIMIT)


def _tile(n, pref):
    t = min(n, pref)
    while n % t:
        t -= 1
    return t


def _rms(x, g):
    return x * lax.rsqrt(jnp.mean(x * x, axis=-1, keepdims=True) + EPS) * g


def _softplus(z):
    return jnp.maximum(z, 0.0) + jnp.log1p(jnp.exp(-jnp.abs(z)))


def _gelu_tanh(x):
    c = math.sqrt(2.0 / math.pi)
    return x * (0.5 * (1.0 + jnp.tanh(c * (x + 0.044715 * (x * x * x)))))


def _silu(x):
    return x * jax.nn.sigmoid(x)


def _rope_tile(y, c, sa, sb):
    outs = []
    for h in range(y.shape[1] // LANES):
        yh = y[:, h * LANES:(h + 1) * LANES]
        outs.append(yh * c + pltpu.roll(yh, LANES - 16, 1) * sa + pltpu.roll(yh, 16, 1) * sb)
    return jnp.concatenate(outs, axis=1)


def _norm_proj_kernel(*refs, groups, tn, rope_q, rope_k, q_scale):
    n_out = len(groups)
    x_ref, g_ref, w_ref = refs[:3]
    pos = 3
    if rope_q is not None:
        c_ref, sa_ref, sb_ref = refs[3:6]
        pos = 6
    out_refs = refs[pos:pos + n_out]
    xn = _rms(x_ref[...], g_ref[...]).astype(BF16)
    for (start, count, _, act), o_ref in zip(groups, out_refs):
        for jj in range(start, start + count):
            y = jnp.dot(xn, w_ref[:, jj * tn:(jj + 1) * tn], preferred_element_type=F32)
            if rope_q is not None and jj == rope_q:
                y = _rope_tile(y, c_ref[...], sa_ref[...], sb_ref[...]) * q_scale
            elif rope_q is not None and jj == rope_k:
                y = _rope_tile(y, c_ref[...], sa_ref[...], sb_ref[...])
            if act is not None:
                y = act(y)
            o_ref[:, (jj - start) * tn:(jj - start + 1) * tn] = y.astype(o_ref.dtype)


def _norm_proj(x, g, w, groups, tm, tn, rope=None, rope_q=None, rope_k=None, q_scale=1.0):
    M, K = x.shape
    in_specs = [
        pl.BlockSpec((tm, K), lambda i: (i, 0)),
        pl.BlockSpec((1, K), lambda i: (0, 0)),
        pl.BlockSpec(w.shape, lambda i: (0, 0)),
    ]
    args = [x, g, w]
    if rope is not None:
        assert rope[0].shape[0] % tm == 0 and M % tm == 0
        pos_blocks = rope[0].shape[0] // tm
        for t in rope:
            in_specs.append(pl.BlockSpec((tm, LANES), lambda i, pb=pos_blocks: (i % pb, 0)))
            args.append(t)
    out_specs, out_shapes = [], []
    for (_, count, dtype, _) in groups:
        out_specs.append(pl.BlockSpec((tm, count * tn), lambda i: (i, 0)))
        out_shapes.append(jax.ShapeDtypeStruct((M, count * tn), dtype))
    kern = functools.partial(_norm_proj_kernel, groups=tuple(groups), tn=tn,
                             rope_q=rope_q if rope is not None else None, rope_k=rope_k, q_scale=q_scale)
    return pl.pallas_call(
        kern,
        grid=(M // tm,),
        in_specs=in_specs,
        out_specs=out_specs,
        out_shape=out_shapes,
        compiler_params=_cparams(("parallel",)),
        name="norm_proj",
    )(*args)


def _out_proj_kernel(*refs, n_in):
    a_refs = refs[:n_in]
    w_ref, x_ref, g_ref, gn_ref, o_ref, on_ref, wb_ref = refs[n_in:]

    @pl.when(pl.program_id(0) == 0)
    def _():
        wb_ref[...] = w_ref[...].astype(BF16)

    tm = x_ref.shape[0]
    half = min(tm, 2 * LANES)
    for r0 in range(0, tm, half):
        rs = slice(r0, r0 + half)
        y, row = None, 0
        for a_ref in a_refs:
            kk = a_ref.shape[1]
            d = jnp.dot(a_ref[rs, :].astype(BF16), wb_ref[row:row + kk, :], preferred_element_type=F32)
            y = d if y is None else y + d
            row += kk
        x1 = x_ref[rs, :] + _rms(y, g_ref[...])
        o_ref[rs, :] = x1
        on_ref[rs, :] = _rms(x1, gn_ref[...]).astype(on_ref.dtype)


def _out_proj(acts, w, x, g, g_next, tm):
    M, D = x.shape
    n_in = len(acts)
    assert sum(a.shape[1] for a in acts) == w.shape[0]
    row_tile = pl.BlockSpec((tm, D), lambda i: (i, 0))
    gain = pl.BlockSpec((1, D), lambda i: (0, 0))
    in_specs = [pl.BlockSpec((tm, a.shape[1]), lambda i: (i, 0)) for a in acts]
    in_specs += [pl.BlockSpec(w.shape, lambda i: (0, 0), pipeline_mode=pl.Buffered(1)), row_tile, gain, gain]
    return pl.pallas_call(
        functools.partial(_out_proj_kernel, n_in=n_in),
        grid=(M // tm,),
        in_specs=in_specs,
        out_specs=[row_tile, row_tile],
        out_shape=[jax.ShapeDtypeStruct((M, D), F32), jax.ShapeDtypeStruct((M, D), BF16)],
        scratch_shapes=[pltpu.VMEM(w.shape, BF16)],
        compiler_params=_cparams(("arbitrary",)),
        name="out_proj",
    )(*acts, w, x, g, g_next)


def _ffn_kernel(x_ref, xn_ref, wu_ref, wd_ref, g2_ref, o_ref, acc_ref):
    f = pl.program_id(1)

    @pl.when(f == 0)
    def _():
        acc_ref[...] = jnp.zeros_like(acc_ref)

    hid = jnp.dot(xn_ref[...], wu_ref[...].astype(BF16), preferred_element_type=F32)
    hid = jnp.square(jnp.maximum(hid, 0.0))
    acc_ref[...] += jnp.dot(hid.astype(BF16), wd_ref[...].astype(BF16), preferred_element_type=F32)

    @pl.when(f == pl.num_programs(1) - 1)
    def _():
        o_ref[...] = x_ref[...] + _rms(acc_ref[...], g2_ref[...])


def _ffn(x, xn, wu, wd, layer, g2, tm, tf):
    M, D = x.shape
    FF = wu.shape[2]
    return pl.pallas_call(
        _ffn_kernel,
        grid=(M // tm, FF // tf),
        in_specs=[
            pl.BlockSpec((tm, D), lambda i, f: (i, 0)),
            pl.BlockSpec((tm, D), lambda i, f: (i, 0)),
            pl.BlockSpec((None, D, tf), lambda i, f: (layer, 0, f)),
            pl.BlockSpec((None, tf, D), lambda i, f: (layer, f, 0)),
            pl.BlockSpec((1, D), lambda i, f: (0, 0)),
        ],
        out_specs=pl.BlockSpec((tm, D), lambda i, f: (i, 0)),
        out_shape=jax.ShapeDtypeStruct((M, D), F32),
        scratch_shapes=[pltpu.VMEM((tm, D), F32)],
        compiler_params=_cparams(("parallel", "arbitrary")),
        name="ffn",
    )(x, xn, wu, wd, g2)


def _lru_gates(xc, wa_ref, ba, wx_ref, bx, lam):
    xcb = xc.astype(BF16)
    ng = wa_ref.shape[0]
    gw = wa_ref.shape[1]
    ra, ri = [], []
    for g in range(ng):
        xs = xcb[:, g * gw:(g + 1) * gw]
        ra.append(jnp.dot(xs, wa_ref[g], preferred_element_type=F32))
        ri.append(jnp.dot(xs, wx_ref[g], preferred_element_type=F32))
    r = jax.nn.sigmoid(jnp.concatenate(ra, axis=1) + ba)
    i = jax.nn.sigmoid(jnp.concatenate(ri, axis=1) + bx)
    log_a = -LRU_C * r * _softplus(-lam)
    a = jnp.exp(log_a)
    om = 1.0 - a * a
    b = jnp.where(om > 0.0, om * lax.rsqrt(om), 0.0) * (i * xc)
    return a, b


def _conv4_rows(xe, halo, w, b):
    a = w[0:1, :] * xe
    for j in range(1, CONV_W):
        a = pltpu.roll(a, 1, 0) + w[j:j + 1, :] * xe
    return b + a[halo:]


def _conv4(tail_ref, x, first, w_ref, b):
    @pl.when(first)
    def _():
        tail_ref[...] = jnp.zeros_like(tail_ref)

    xe = jnp.concatenate([tail_ref[...], x], axis=0)
    tail_ref[...] = x[x.shape[0] - SUBLANES:, :]
    return _conv4_rows(xe, SUBLANES, w_ref[...], b)


def _rglru_prompt_kernel(xa_ref, ga_ref, cw_ref, cb_ref, wa_ref, ba_ref, wx_ref, bx_ref, lam_ref,
                         ya_ref, hl_ref, xe_ref, hc_ref, a_s, b_s, h_s):
    t = pl.program_id(1)
    T = xa_ref.shape[0]

    @pl.when(t == 0)
    def _():
        hc_ref[...] = jnp.zeros_like(hc_ref)

    xc = _conv4(xe_ref, xa_ref[...], t == 0, cw_ref, cb_ref[...])
    a, b = _lru_gates(xc, wa_ref, ba_ref[...], wx_ref, bx_ref[...], lam_ref[...])
    a_s[...] = a
    b_s[...] = b

    def body(s, h):
        h = a_s[pl.ds(s, 1), :] * h + b_s[pl.ds(s, 1), :]
        h_s[pl.ds(s, 1), :] = h
        return h

    h = lax.fori_loop(0, T, body, hc_ref[...], unroll=8)
    hc_ref[...] = h
    ya_ref[...] = (h_s[...] * _gelu_tanh(ga_ref[...])).astype(ya_ref.dtype)

    @pl.when(t == pl.num_programs(1) - 1)
    def _():
        hl_ref[...] = h


def _rglru_prompt(xaga, cw, cb, wa_bd, ba, wx_bd, bx, lam, ts):
    B, S, _ = xaga.shape
    C = cw.shape[-1]
    wspec = lambda a: pl.BlockSpec(a.shape, lambda b, t: (0,) * a.ndim)
    return pl.pallas_call(
        _rglru_prompt_kernel,
        grid=(B, S // ts),
        in_specs=[
            pl.BlockSpec((None, ts, C), lambda b, t: (b, t, 0)),
            pl.BlockSpec((None, ts, C), lambda b, t: (b, t, 1)),
            wspec(cw), wspec(cb), wspec(wa_bd), wspec(ba), wspec(wx_bd), wspec(bx), wspec(lam),
        ],
        out_specs=[
            pl.BlockSpec((None, ts, C), lambda b, t: (b, t, 0)),
            pl.BlockSpec((None, 1, C), lambda b, t: (b, 0, 0)),
        ],
        out_shape=[jax.ShapeDtypeStruct((B, S, C), BF16), jax.ShapeDtypeStruct((B, 1, C), F32)],
        scratch_shapes=[pltpu.VMEM((SUBLANES, C), F32), pltpu.VMEM((1, C), F32),
                        pltpu.VMEM((ts, C), F32), pltpu.VMEM((ts, C), F32), pltpu.VMEM((ts, C), F32)],
        compiler_params=_cparams(("parallel", "arbitrary")),
        name="rglru_prompt",
    )(xaga, xaga, cw, cb, wa_bd, ba, wx_bd, bx, lam)


def _rglru_sample_kernel(xa_ref, ga_ref, st_ref, h0_ref, cw_ref, cb_ref, wa_ref, ba_ref, wx_ref, bx_ref,
                         lam_ref, ya_ref, hl_ref):
    T = xa_ref.shape[0]
    rows = [st_ref[j] for j in range(CONV_W - 1)] + [xa_ref[s] for s in range(T)]
    h = h0_ref[...]
    for s in range(T):
        y = cb_ref[...]
        for j in range(CONV_W):
            y = y + cw_ref[j:j + 1, :] * rows[s + j]
        a, b = _lru_gates(y, wa_ref, ba_ref[...], wx_ref, bx_ref[...], lam_ref[...])
        h = a * h + b
        ya_ref[s] = (h * _gelu_tanh(ga_ref[s])).astype(ya_ref.dtype)
    hl_ref[...] = h


def _rglru_sample(xa_t, ga_t, st_t, h0, cw, cb, wa_bd, ba, wx_bd, bx, lam):
    T, B, C = xa_t.shape
    args = (xa_t, ga_t, st_t, h0, cw, cb, wa_bd, ba, wx_bd, bx, lam)
    return pl.pallas_call(
        _rglru_sample_kernel,
        grid=(1,),
        in_specs=[pl.BlockSpec(a.shape, lambda i, n=a.ndim: (0,) * n) for a in args],
        out_specs=[pl.BlockSpec((T, B, C), lambda i: (0, 0, 0)), pl.BlockSpec((B, C), lambda i: (0, 0))],
        out_shape=[jax.ShapeDtypeStruct((T, B, C), F32), jax.ShapeDtypeStruct((B, C), F32)],
        compiler_params=_cparams(("arbitrary",)),
        name="rglru_sample",
    )(*args)


def _attn_prompt_kernel(q_ref, k_ref, v_ref, o_ref, *scratch, groups, qb):
    S = q_ref.shape[0]
    E = q_ref.shape[1]
    dist2 = (lax.broadcasted_iota(jnp.int32, (qb, 2 * qb), 0) + qb
             - lax.broadcasted_iota(jnp.int32, (qb, 2 * qb), 1))
    dist1 = lax.broadcasted_iota(jnp.int32, (qb, qb), 0) - lax.broadcasted_iota(jnp.int32, (qb, qb), 1)

    def rows(start, d):
        return pl.ds(start, qb) if d == 1 else pl.ds(start, qb, stride=d)

    ng = len(groups)
    s_scr, p_scr = scratch[3 * ng], scratch[3 * ng + 1]
    dn_t = (((1,), (1,)), ((), ()))

    for gi, (w, d) in enumerate(groups):
        back = w // d
        nb = (S // d) // qb
        nblk = nb * d
        two = nb > 1
        kw = 2 * qb if two else qb
        dist = dist2 if two else dist1
        lim_all = jnp.full((qb, kw), back, jnp.int32)
        lim_first = jnp.minimum(lax.broadcasted_iota(jnp.int32, (qb, kw), 0), back)
        m_s, l_s, acc_s = scratch[3 * gi:3 * gi + 3]

        def locate(idx, d=d):
            if d == 1:
                n = idx
                start = pl.multiple_of(idx * qb, qb)
                prev = pl.multiple_of(jnp.maximum(idx - 1, 0) * qb, qb)
            else:
                n = idx // d
                start = n * (qb * d) + idx % d
                prev = jnp.where(n > 0, start - qb * d, start)
            return start, prev, n

        def gather(ref, start, prev, d=d, two=two):
            if two:
                return jnp.concatenate([ref[rows(prev, d), :], ref[rows(start, d), :]], axis=0).astype(BF16)
            return ref[rows(start, d), :].astype(BF16)

        def scores(idx, c, kw=kw, locate=locate, gather=gather, d=d):
            start, prev, _ = locate(idx)
            q = q_ref[rows(start, d), :].astype(BF16)
            s_scr[pl.ds(pl.multiple_of(idx * qb, qb), qb), 0:kw] = lax.dot_general(
                q, gather(k_ref, start, prev), dn_t, preferred_element_type=F32)
            return c

        def softmax(idx, c, kw=kw, locate=locate, d=d, two=two, dist=dist, lim_all=lim_all,
                    lim_first=lim_first, m_s=m_s):
            start, _, n = locate(idx)
            blk = pl.ds(pl.multiple_of(idx * qb, qb), qb)
            lim = jnp.where(n > 0, lim_all, lim_first) if two else lim_all
            valid = (dist >= 0) & (dist <= lim)
            s = jnp.where(valid, s_scr[blk, 0:kw], NEG_INF)
            m_b = jnp.max(s, axis=-1, keepdims=True)
            p_scr[blk, 0:kw] = jnp.exp(s - m_b).astype(BF16)
            m_s[rows(start, d), :] = jnp.broadcast_to(m_b, (qb, E))
            return c

        def values(idx, c, kw=kw, locate=locate, gather=gather, d=d, acc_s=acc_s, l_s=l_s):
            start, prev, _ = locate(idx)
            p = p_scr[pl.ds(pl.multiple_of(idx * qb, qb), qb), 0:kw]
            v1 = jnp.concatenate([gather(v_ref, start, prev), jnp.ones((kw, E), BF16)], axis=1)
            r = jnp.dot(p, v1, preferred_element_type=F32)
            acc_s[rows(start, d), :] = r[:, :E]
            l_s[rows(start, d), :] = r[:, E:]
            return c

        lax.fori_loop(0, nblk, scores, 0, unroll=True)
        lax.fori_loop(0, nblk, softmax, 0, unroll=True)
        lax.fori_loop(0, nblk, values, 0, unroll=True)

    ng = len(groups)

    def merge(i, c):
        rs = pl.ds(pl.multiple_of(i * qb, qb), qb)
        ms = [scratch[3 * g][rs, :] for g in range(ng)]
        m = functools.reduce(jnp.maximum, ms)
        num = den = None
        for g in range(ng):
            wg = jnp.exp(ms[g] - m)
            n_g = wg * scratch[3 * g + 2][rs, :]
            d_g = wg * scratch[3 * g + 1][rs, :]
            num = n_g if num is None else num + n_g
            den = d_g if den is None else den + d_g
        o_ref[rs, :] = (num / den).astype(o_ref.dtype)
        return c

    lax.fori_loop(0, S // qb, merge, 0, unroll=2)


def _attn_prompt(q, q_off, k, v, groups, qb):
    B, S, D = k.shape
    nh = D // LANES
    for (w, d) in groups:
        assert S % (d * qb) == 0 and w // d <= qb
    spec = pl.BlockSpec((None, S, LANES), lambda b, h: (b, 0, h))
    q_spec = pl.BlockSpec((None, S, LANES), lambda b, h: (b, 0, q_off + h))
    return pl.pallas_call(
        functools.partial(_attn_prompt_kernel, groups=tuple(groups), qb=qb),
        grid=(B, nh),
        in_specs=[q_spec, spec, spec],
        out_specs=spec,
        out_shape=jax.ShapeDtypeStruct((B, S, D), BF16),
        scratch_shapes=[pltpu.VMEM((S, LANES), F32)] * (3 * len(groups))
        + [pltpu.VMEM((S, 2 * qb), F32), pltpu.VMEM((S, 2 * qb), BF16)],
        compiler_params=_cparams(("parallel", "parallel")),
        name="attn_prompt",
    )(q, k, v)


def _attn_sample_kernel(q_ref, kn_ref, vn_ref, kt_ref, vt_ref, ks_ref, vs_ref, mt_ref, ms_ref, mn_ref, o_ref):
    nq, hd = q_ref.shape
    qb = jnp.concatenate([q_ref[...], jnp.zeros((LANES - nq, hd), F32)], axis=0).astype(BF16)
    nn = kn_ref.shape[0]
    zn = jnp.zeros((LANES - nn, hd), F32)
    key_sets = (
        (kt_ref[...], vt_ref[...], mt_ref[...]),
        (ks_ref[...].reshape(-1, hd), vs_ref[...].reshape(-1, hd), ms_ref[...]),
        (jnp.concatenate([kn_ref[...], zn], axis=0), jnp.concatenate([vn_ref[...], zn], axis=0), mn_ref[...]),
    )
    partial = []
    for k, v, bias in key_sets:
        s = lax.dot_general(k.astype(BF16), qb, (((1,), (1,)), ((), ())), preferred_element_type=F32) + bias
        m_set = jnp.max(s, axis=0, keepdims=True)
        p = jnp.exp(s - jnp.where(m_set == NEG_INF, 0.0, m_set)).astype(BF16)
        v1 = jnp.concatenate([v.astype(BF16), jnp.ones(v.shape, BF16)], axis=1)
        partial.append((m_set, lax.dot_general(p, v1, (((0,), (0,)), ((), ())), preferred_element_type=F32)))
    m = functools.reduce(jnp.maximum, [m_set for m_set, _ in partial])
    m = jnp.where(m == NEG_INF, 0.0, m)
    acc = None
    for m_set, r in partial:
        w = jnp.broadcast_to(jnp.exp(m_set - m), (LANES, LANES)).T
        r = jnp.concatenate([w, w], axis=1) * r
        acc = r if acc is None else acc + r
    o_ref[...] = acc[:nq, :hd] / acc[:nq, hd:]


def _attn_sample(q, kn, vn, kc, vc, T, nh):
    B, nq, hd = q.shape
    wb = kc.shape[1] // nh
    far = max(DILATIONS, key=lambda wd: wd[0])
    near = [g for g in DILATIONS if g != far]
    R = max(w for w, _ in near)
    d_far = far[1]
    assert near and wb % R == 0 and wb % d_far == 0 and T <= d_far and nq == T * nh and nq <= LANES
    A = wb // d_far

    def mult(delta, groups):
        m = np.zeros(delta.shape, np.float32)
        for (w, d) in groups:
            m = m + ((delta >= 0) & (delta % d == 0) & (delta <= (w // d) * d))
        return m

    def table(pos_of_row, head_of_row, groups):
        lane = np.arange(LANES)
        t, hq = lane // nh, lane % nh
        delta = (wb + t)[None, :] - pos_of_row[:, None]
        tab = mult(delta, groups) * (head_of_row[:, None] == hq[None, :]) * (t < T)[None, :]
        return jnp.asarray(np.where(tab > 0, np.log(np.maximum(tab, 1.0)), NEG_INF), F32)

    rt = np.arange(R * nh)
    mt = table(wb - R + rt // nh, rt % nh, near)
    rs = np.arange(A * T * nh)
    ms = table((rs // (T * nh)) * d_far + (rs % (T * nh)) // nh, rs % nh, [far])
    rn = np.arange(LANES)
    mn = table(np.where(rn < nq, wb + rn // nh, -10 ** 9), rn % nh, DILATIONS)

    small = pl.BlockSpec((None, nq, hd), lambda b: (b, 0, 0))
    tail = pl.BlockSpec((None, R * nh, hd), lambda b: (b, wb // R - 1, 0))
    strided = pl.BlockSpec((None, A, T * nh, hd), lambda b: (b, 0, 0, 0))
    const = lambda a: pl.BlockSpec(a.shape, lambda b: (0, 0))
    kc4 = kc.reshape(B, A, d_far * nh, hd)
    vc4 = vc.reshape(B, A, d_far * nh, hd)
    return pl.pallas_call(
        _attn_sample_kernel,
        grid=(B,),
        in_specs=[small, small, small, tail, tail, strided, strided, const(mt), const(ms), const(mn)],
        out_specs=small,
        out_shape=jax.ShapeDtypeStruct((B, nq, hd), F32),
        compiler_params=_cparams(("parallel",)),
        name="attn_sample",
    )(q, kn, vn, kc, vc, kc4, vc4, mt, ms, mn)


def _mlstm_qkv_gates(xc, xm, wq_ref, wk_ref, wv_ref, wg_ref, bg_ref, q_ref, k_ref, v_ref, xc_ref, g_ref, k_scale,
                     rs=slice(None)):
    xcb = xc.astype(BF16)
    xmb = xm.astype(BF16)
    ng, gw = wq_ref.shape[0], wq_ref.shape[1]
    qs, ks, vs = [], [], []
    for g in range(ng):
        sl = slice(g * gw, (g + 1) * gw)
        qs.append(jnp.dot(xcb[:, sl], wq_ref[g], preferred_element_type=F32))
        ks.append(jnp.dot(xcb[:, sl], wk_ref[g], preferred_element_type=F32))
        vs.append(jnp.dot(xmb[:, sl], wv_ref[g], preferred_element_type=F32))
    k = jnp.concatenate(ks, axis=1)
    qb = jnp.concatenate(qs, axis=1).astype(BF16)
    vb = jnp.concatenate(vs, axis=1).astype(BF16)
    qkv = jnp.concatenate([qb, k.astype(BF16), vb], axis=1)
    g_ref[rs, :] = jnp.dot(qkv, wg_ref[...], preferred_element_type=F32) + bg_ref[...]
    q_ref[rs, :] = qb
    k_ref[rs, :] = (k * k_scale).astype(k_ref.dtype)
    v_ref[rs, :] = vb
    xc_ref[rs, :] = xcb


def _mlstm_pre_sample_kernel(xm_ref, st_ref, cw_ref, cb_ref, wq_ref, wk_ref, wv_ref, wg_ref, bg_ref,
                             q_ref, k_ref, v_ref, xc_ref, g_ref, *, k_scale):
    T = xm_ref.shape[0]
    rows = [st_ref[j] for j in range(CONV_W - 1)] + [xm_ref[s] for s in range(T)]
    conv = []
    for s in range(T):
        y = cb_ref[...]
        for j in range(CONV_W):
            y = y + cw_ref[j:j + 1, :] * rows[s + j]
        conv.append(y)
    xc = _silu(jnp.concatenate(conv, axis=0))
    xm = jnp.concatenate(rows[CONV_W - 1:], axis=0)
    _mlstm_qkv_gates(xc, xm, wq_ref, wk_ref, wv_ref, wg_ref, bg_ref, q_ref, k_ref, v_ref, xc_ref, g_ref, k_scale)


def _mlstm_pre_sample(xm_t, st_t, cw, cb, wq_bd, wk_bd, wv_bd, wg, bg, k_scale):
    T, B, C = xm_t.shape
    args = (xm_t, st_t, cw, cb, wq_bd, wk_bd, wv_bd, wg, bg)
    full = lambda shape: pl.BlockSpec(shape, lambda i: (0,) * len(shape))
    return pl.pallas_call(
        functools.partial(_mlstm_pre_sample_kernel, k_scale=k_scale),
        grid=(1,),
        in_specs=[full(a.shape) for a in args],
        out_specs=[full((T * B, C))] * 4 + [full((T * B, 2 * LANES))],
        out_shape=[jax.ShapeDtypeStruct((T * B, C), BF16)] * 4 + [jax.ShapeDtypeStruct((T * B, 2 * LANES), F32)],
        compiler_params=_cparams(("arbitrary",)),
        name="mlstm_pre_sample",
    )(*args)


CONV_HALO = 16


def _mlstm_in_kernel(x_ref, xh_ref, g_ref, w_ref, cw_ref, cb_ref, wq_ref, wk_ref, wv_ref, wg_ref, bg_ref,
                     q_ref, k_ref, v_ref, xc_ref, gz_ref, gt_ref, tail_ref, *, tiles_per_seq, tn, k_scale):
    tm = x_ref.shape[0]
    C = cw_ref.shape[1]
    first = (pl.program_id(0) % tiles_per_seq) == 0
    g = g_ref[...]
    xn = _rms(x_ref[...], g)
    xe = jnp.concatenate([_rms(xh_ref[...], g), xn], axis=0).astype(BF16)
    xn = xn.astype(BF16)
    dead_halo = lax.broadcasted_iota(jnp.int32, (CONV_HALO + tm, tn), 0) < jnp.where(first, CONV_HALO, 0)
    xmes = []
    for jj in range(C // tn):
        cs = slice(jj * tn, (jj + 1) * tn)
        xm_e = jnp.dot(xe, w_ref[:, cs], preferred_element_type=F32)
        xmes.append(jnp.where(dead_halo, 0.0, xm_e))
        gz_ref[:, cs] = _silu(jnp.dot(xn, w_ref[:, C + jj * tn:C + (jj + 1) * tn],
                                      preferred_element_type=F32)).astype(gz_ref.dtype)
    half = LANES
    for r0 in range(0, tm, half):
        xc = jnp.concatenate(
            [_silu(_conv4_rows(xm_e[r0:r0 + CONV_HALO + half], CONV_HALO, cw_ref[:, jj * tn:(jj + 1) * tn],
                               cb_ref[:, jj * tn:(jj + 1) * tn])) for jj, xm_e in enumerate(xmes)], axis=1)
        xm = jnp.concatenate([xm_e[CONV_HALO + r0:CONV_HALO + r0 + half] for xm_e in xmes], axis=1)
        _mlstm_qkv_gates(xc, xm, wq_ref, wk_ref, wv_ref, wg_ref, bg_ref, q_ref, k_ref, v_ref, xc_ref, gt_ref,
                         k_scale, rs=slice(r0, r0 + half))
    tail_ref[...] = jnp.concatenate([xm_e[CONV_HALO + tm - SUBLANES:] for xm_e in xmes], axis=1)


def _mlstm_in(x, g, w, cw, cb, wq_bd, wk_bd, wv_bd, wg, bg, seq, tm, k_scale):
    M, K = x.shape
    C = cw.shape[1]
    assert seq % tm == 0 and tm % CONV_HALO == 0
    hb = tm // CONV_HALO
    const = lambda a: pl.BlockSpec(a.shape, lambda i: (0,) * a.ndim, pipeline_mode=pl.Buffered(1))
    act = pl.BlockSpec((tm, C), lambda i: (i, 0))
    return pl.pallas_call(
        functools.partial(_mlstm_in_kernel, tiles_per_seq=seq // tm, tn=1024, k_scale=k_scale),
        grid=(M // tm,),
        in_specs=[pl.BlockSpec((tm, K), lambda i: (i, 0)),
                  pl.BlockSpec((CONV_HALO, K), lambda i: (jnp.maximum(i * hb - 1, 0), 0)),
                  const(g), const(w), const(cw), const(cb), const(wq_bd), const(wk_bd), const(wv_bd),
                  const(wg), const(bg)],
        out_specs=[act, act, act, act, act, pl.BlockSpec((tm, 2 * LANES), lambda i: (i, 0)),
                   pl.BlockSpec((None, SUBLANES, C), lambda i: (i, 0, 0))],
        out_shape=[jax.ShapeDtypeStruct((M, C), BF16)] * 5 + [jax.ShapeDtypeStruct((M, 2 * LANES), F32),
                                                              jax.ShapeDtypeStruct((M // tm, SUBLANES, C), F32)],
        compiler_params=_cparams(("parallel",)),
        name="mlstm_in",
    )(x, x, g, w, cw, cb, wq_bd, wk_bd, wv_bd, wg, bg)


def _mlstm_core_kernel(*refs, c, n_valid, nh, zero_init):
    q_ref, k_ref, v_ref, xc_ref, gz_ref, g_ref, nw_ref, sk_ref = refs[:8]
    pos = 8
    if not zero_init:
        c0_ref, n0_ref, m0_ref = refs[8:11]
        pos = 11
    y_ref, C_ref, n_ref, m_ref = refs[pos:pos + 4]
    ch = pl.program_id(1)
    c_in = q_ref.shape[0]
    DH = q_ref.shape[1] // nh

    @pl.when(ch == 0)
    def _():
        if zero_init:
            C_ref[...] = jnp.zeros_like(C_ref)
            n_ref[...] = jnp.zeros_like(n_ref)
            m_ref[...] = jnp.zeros_like(m_ref)
        else:
            C_ref[...] = c0_ref[...]
            n_ref[...] = n0_ref[...]
            m_ref[...] = m0_ref[...]

    def padded(x):
        if c_in == c:
            return x
        return jnp.concatenate([x, jnp.zeros((c - c_in, x.shape[1]), x.dtype)], axis=0)

    g = padded(g_ref[...])
    row = lax.broadcasted_iota(jnp.int32, (c, LANES), 0)
    valid = row < n_valid
    gi = jnp.where(valid, g[:, :LANES], NEG_INF)
    lf = jnp.where(valid, -_softplus(-g[:, LANES:]), 0.0)
    tri_f = (lax.broadcasted_iota(jnp.int32, (c, c), 1) <= lax.broadcasted_iota(jnp.int32, (c, c), 0))
    tri_b = tri_f.astype(BF16)
    bcum, rest = None, lf
    for _ in range(3):
        part = rest.astype(BF16)
        rest = rest - part.astype(F32)
        d = jnp.dot(tri_b, part, preferred_element_type=F32)
        bcum = d if bcum is None else bcum + d
    bcum_t = bcum.T
    gi_t = gi.T
    causal = tri_f

    for h in range(nh):
        sl = slice(h * DH, (h + 1) * DH)
        qh = padded(q_ref[:, sl])
        kh = padded(k_ref[:, sl])
        vh = padded(v_ref[:, sl])
        bc_col = bcum[:, h:h + 1]
        bc_row = bcum_t[h:h + 1, :]
        i_row = gi_t[h:h + 1, :]
        i_col = gi[:, h:h + 1]
        m_prev = m_ref[h:h + 1, 0:1]
        log_intra = jnp.where(causal, bc_col - bc_row + i_row, NEG_INF)
        log_inter = bc_col + m_prev
        m_t = jnp.maximum(log_inter, jnp.max(log_intra, axis=1, keepdims=True))
        qk = lax.dot_general(qh, kh, (((1,), (1,)), ((), ())), preferred_element_type=F32)
        w_intra = jnp.exp(log_intra - m_t) * qk
        w_inter = jnp.exp(log_inter - m_t)
        C_old = C_ref[h]
        n_old = n_ref[h:h + 1, :]
        num = jnp.dot(w_intra.astype(BF16), vh, preferred_element_type=F32)
        num = num + w_inter * jnp.dot(qh, C_old.astype(BF16), preferred_element_type=F32)
        n8 = jnp.broadcast_to(n_old, (SUBLANES, DH)).astype(BF16)
        qn = lax.dot_general(qh, n8, (((1,), (1,)), ((), ())), preferred_element_type=F32)[:, 0:1]
        den = jnp.sum(w_intra, axis=1, keepdims=True) + w_inter * qn
        hh = num * (1.0 / jnp.maximum(jnp.abs(den), jnp.exp(-m_t)))
        m_new = m_t[c - 1:c, :]
        bc_last = bc_col[c - 1:c, :]
        w_state = jnp.exp(bc_last - bc_col + i_col - m_new)
        decay = jnp.exp(bc_last + m_prev - m_new)
        kw = kh.astype(F32) * w_state
        C_ref[h] = decay * C_old + lax.dot_general(
            kw.astype(BF16), vh, (((0,), (0,)), ((), ())), preferred_element_type=F32)
        n_ref[h:h + 1, :] = decay * n_old + jnp.sum(kw, axis=0, keepdims=True)
        m_ref[h:h + 1, :] = jnp.broadcast_to(m_new, (1, LANES))
        hn = hh * lax.rsqrt(jnp.mean(hh * hh, axis=-1, keepdims=True) + EPS)
        hn = hn[:c_in] * nw_ref[:, sl]
        y = (hn + sk_ref[:, sl] * xc_ref[:, sl].astype(F32)) * gz_ref[:, sl].astype(F32)
        y_ref[:, sl] = y.astype(y_ref.dtype)


def _mlstm_core(q, k, v, xc, gz, gates, nw, sk, state, c, n_valid, nh):
    B, S, C = q.shape
    c_in = min(c, S)
    DH = C // nh
    zero_init = state is None
    act = pl.BlockSpec((None, c_in, C), lambda b, t: (b, t, 0))
    in_specs = [act, act, act, act, act,
                pl.BlockSpec((None, c_in, 2 * LANES), lambda b, t: (b, t, 0)),
                pl.BlockSpec((1, C), lambda b, t: (0, 0)), pl.BlockSpec((1, C), lambda b, t: (0, 0))]
    args = [q, k, v, xc, gz, gates, nw, sk]
    C_spec = pl.BlockSpec((None, nh, DH, DH), lambda b, t: (b, 0, 0, 0))
    n_spec = pl.BlockSpec((None, nh, DH), lambda b, t: (b, 0, 0))
    m_spec = pl.BlockSpec((None, nh, LANES), lambda b, t: (b, 0, 0))
    if not zero_init:
        in_specs += [C_spec, n_spec, m_spec]
        args += list(state)
    return pl.pallas_call(
        functools.partial(_mlstm_core_kernel, c=c, n_valid=n_valid, nh=nh, zero_init=zero_init),
        grid=(B, S // c_in),
        in_specs=in_specs,
        out_specs=[act, C_spec, n_spec, m_spec],
        out_shape=[jax.ShapeDtypeStruct((B, S, C), BF16),
                   jax.ShapeDtypeStruct((B, nh, DH, DH), F32),
                   jax.ShapeDtypeStruct((B, nh, DH), F32),
                   jax.ShapeDtypeStruct((B, nh, LANES), F32)],
        compiler_params=_cparams(("parallel", "arbitrary")),
        name="mlstm_core",
    )(*args)


def _rope_tables(pos, hd):
    rot = hd // 4
    half = rot // 2
    inv = ROPE_THETA ** (-np.arange(0, rot, 2, dtype=np.float64) / rot)
    ang = pos.astype(np.float64)[:, None] * inv[None, :]
    cos, sin = np.cos(ang), np.sin(ang)
    n = pos.shape[0]
    c = np.concatenate([cos, cos, np.ones((n, hd - rot))], axis=1)
    sa = np.concatenate([-sin, np.zeros((n, hd - half))], axis=1)
    sb = np.concatenate([np.zeros((n, half)), sin, np.zeros((n, hd - rot))], axis=1)
    return tuple(jnp.asarray(t, F32) for t in (c, sa, sb))


def _regroup_block_diag(w, group):
    nb, bi, bo = w.shape
    per = group // bi
    a = w.reshape(nb // per, per * bi, bo)
    cols = np.arange(per * bo)
    tile_cols = jnp.asarray(cols[None, :] % bo == np.arange(bo)[:, None], BF16)
    same_block = jnp.asarray((np.arange(per * bi)[:, None] // bi) == (cols[None, :] // bo), BF16)
    tiled = jnp.einsum("gro,oc->grc", a.astype(BF16), tile_cols, preferred_element_type=F32)
    return tiled.astype(BF16) * same_block


def _layer0(x, rope, p, prompt, state):
    Bg, L, D = x.shape
    M = Bg * L
    hd = D // H_B
    tm = _tile(M, 1024)
    tm_in = _tile(rope[0].shape[0], 512)
    x2 = x.reshape(M, D)
    xagq, k, v = _norm_proj(x2, p["g_mix_pre0"], p["w_in_ab"], ((0, 3, F32, None), (3, 1, F32, None), (4, 1, F32, None)),
                            tm_in, D, rope=rope, rope_q=2, rope_k=3, q_scale=hd ** -0.5)
    xagq3 = xagq.reshape(Bg, L, 3 * D)
    lru_w = (p["conv_a_w"], p["conv_a_b"], p["wa_bd"], p["lru_ba"], p["wx_bd"], p["lru_bx"], p["lam"])
    if prompt:
        ya, h_last = _rglru_prompt(xagq3, *lru_w, ts=_tile(L, 512))
        ya = ya.reshape(M, D)
        h_last = h_last.reshape(Bg, D)
        new_conv = xagq3[:, L - (CONV_W - 1):, :D]
        yb = _attn_prompt(xagq3, 2 * D // LANES, k.reshape(Bg, L, D), v.reshape(Bg, L, D),
                          sorted(DILATIONS, key=lambda wd: -wd[1]), QBLOCK).reshape(M, D)
        wb = min(MAX_WINDOW, L)
        new_k = k.reshape(Bg, L, H_B, hd)[:, L - wb:]
        new_v = v.reshape(Bg, L, H_B, hd)[:, L - wb:]
    else:
        conv0, h0, k_buf, v_buf = state
        xagq_t = xagq3.transpose(1, 0, 2)
        xa_t, ga_t = xagq_t[:, :, :D], xagq_t[:, :, D:2 * D]
        st_t = conv0.transpose(1, 0, 2)
        ya_t, h_last = _rglru_sample(xa_t, ga_t, st_t, h0, *lru_w)
        ya = ya_t.transpose(1, 0, 2).reshape(M, D)
        new_conv = jnp.concatenate([conv0, xagq3[:, :, :D]], axis=1)[:, -(CONV_W - 1):]
        wb = k_buf.shape[1]
        native = lambda a: a.reshape(Bg, -1, hd)
        yb = _attn_sample(native(xagq3[:, :, 2 * D:]), native(k), native(v), native(k_buf), native(v_buf),
                          L, H_B).reshape(M, D)
        new_k = k.reshape(Bg, L, H_B, hd)
        new_v = v.reshape(Bg, L, H_B, hd)
    x1, x1n = _out_proj([ya, yb], p["w_out_ab"], x2, p["g_mix_post0"], p["g_ffn_pre0"], tm)
    x2o = _ffn(x1, x1n, p["w_up"], p["w_dn"], 0, p["g_ffn_post0"], _tile(M, 1024), 1024)
    return x2o.reshape(Bg, L, D), (new_conv, h_last, new_k, new_v)


def _layer1(x, p, prompt, state):
    Bg, L, D = x.shape
    M = Bg * L
    x2 = x.reshape(M, D)
    tm = _tile(M, 1024)
    DC = p["w_in_c"].shape[1] // 2
    DH = DC // NH_C
    pre_w = (p["conv_c_w"], p["conv_c_b"], p["wq_bd"], p["wk_bd"], p["wv_bd"], p["wg"], p["bg"])
    if prompt:
        tm_in = _tile(L, 512)
        c = _tile(L, 256)
        *acts, tails = _mlstm_in(x2, p["g_mix_pre1"], p["w_in_c"], *pre_w, seq=L, tm=tm_in, k_scale=DH ** -0.5)
        q, k, v, xc, gz, gates = [a.reshape(Bg, L, -1) for a in acts]
        y, C, n, m = _mlstm_core(q, k, v, xc, gz, gates, p["mlstm_norm"], p["mlstm_skip"], None, c, c, NH_C)
        new_conv = tails.reshape(Bg, L // tm_in, SUBLANES, DC)[:, -1, SUBLANES - (CONV_W - 1):]
        y = y.reshape(M, DC)
    else:
        nt = DC // 1024
        xm, gz = _norm_proj(x2, p["g_mix_pre1"], p["w_in_c"], ((0, nt, F32, None), (nt, nt, BF16, _silu)),
                            _tile(M, 512), 1024)
        xm3 = xm.reshape(Bg, L, DC)
        gz3 = gz.reshape(Bg, L, DC)
        conv0, C0, n0, m0 = state
        lp = 16
        outs = _mlstm_pre_sample(xm3.transpose(1, 0, 2), conv0.transpose(1, 0, 2), *pre_w, k_scale=DH ** -0.5)
        batch_major = lambda a: jnp.pad(a.reshape(L, Bg, -1).transpose(1, 0, 2), ((0, 0), (0, lp - L), (0, 0)))
        q, k, v, xc, gates = [batch_major(a) for a in outs]
        gz_p = jnp.pad(gz3, ((0, 0), (0, lp - L), (0, 0)))
        m0b = jnp.broadcast_to(m0[:, :, None], m0.shape + (LANES,))
        y, C, n, m = _mlstm_core(q, k, v, xc, gz_p, gates, p["mlstm_norm"], p["mlstm_skip"],
                                 (C0, n0, m0b), 32, L, NH_C)
        new_conv = jnp.concatenate([conv0, xm3], axis=1)[:, -(CONV_W - 1):]
        y = y[:, :L].reshape(M, DC)
    x1, x1n = _out_proj([y], p["w_out_c"], x2, p["g_mix_post1"], p["g_ffn_pre1"], tm)
    x2o = _ffn(x1, x1n, p["w_up"], p["w_dn"], 1, p["g_ffn_post1"], _tile(M, 1024), 1024)
    return x2o.reshape(Bg, L, D), (new_conv, C, n, m[:, :, 0])


def kernel(x_prompt, x_sample, state_rglru_conv, state_rglru_h, cache_swa_k, cache_swa_v, state_mlstm_conv, state_mlstm_C, state_mlstm_n, state_mlstm_m, norm_mix_pre, norm_mix_post, norm_ffn_pre, norm_ffn_post, w_ffn_up, w_ffn_down, w_in_ab, conv_a_w, conv_a_b, lru_wa, lru_ba, lru_wx, lru_bx, lru_lambda, w_out_ab, w_in_c, conv_c_w, conv_c_b, mlstm_wq, mlstm_wk, mlstm_wv, mlstm_w_gate, mlstm_b_gate, mlstm_norm, mlstm_skip, w_out_c):
    B, S, D = x_prompt.shape
    Bs, Ts, _ = x_sample.shape
    DA = conv_a_w.shape[-1]
    DC = conv_c_w.shape[-1]
    hd = D // H_B
    row = lambda a: a.reshape(1, -1)
    lane_pad = lambda a: jnp.pad(a, ((0, 0), (0, LANES - NH_C)))
    wg = jnp.concatenate([lane_pad(mlstm_w_gate[0][:, :NH_C]), lane_pad(mlstm_w_gate[0][:, NH_C:])], axis=1)
    bg = jnp.concatenate([lane_pad(mlstm_b_gate[:, :NH_C]), lane_pad(mlstm_b_gate[:, NH_C:])], axis=1)
    p = {
        "g_mix_pre0": row(norm_mix_pre[0]), "g_mix_post0": row(norm_mix_post[0]),
        "g_ffn_pre0": row(norm_ffn_pre[0]), "g_ffn_post0": row(norm_ffn_post[0]),
        "g_mix_pre1": row(norm_mix_pre[1]), "g_mix_post1": row(norm_mix_post[1]),
        "g_ffn_pre1": row(norm_ffn_pre[1]), "g_ffn_post1": row(norm_ffn_post[1]),
        "w_up": w_ffn_up, "w_dn": w_ffn_down,
        "w_in_ab": w_in_ab[0].astype(BF16),
        "conv_a_w": conv_a_w[0], "conv_a_b": row(conv_a_b[0]),
        "wa_bd": _regroup_block_diag(lru_wa[0], MXU_DIM), "lru_ba": row(lru_ba[0]),
        "wx_bd": _regroup_block_diag(lru_wx[0], MXU_DIM), "lru_bx": row(lru_bx[0]),
        "lam": row(lru_lambda[0]),
        "w_out_ab": w_out_ab[0],
        "w_in_c": w_in_c[0].astype(BF16),
        "conv_c_w": conv_c_w[0], "conv_c_b": row(conv_c_b[0]),
        "wq_bd": _regroup_block_diag(mlstm_wq[0], MXU_DIM),
        "wk_bd": _regroup_block_diag(mlstm_wk[0], MXU_DIM),
        "wv_bd": _regroup_block_diag(mlstm_wv[0], MXU_DIM),
        "wg": wg.astype(BF16), "bg": bg,
        "mlstm_norm": row(mlstm_norm[0]), "mlstm_skip": row(mlstm_skip[0]),
        "w_out_c": w_out_c[0],
    }
    dt = state_rglru_conv.dtype

    xp, st0 = _layer0(x_prompt, _rope_tables(np.arange(S), hd), p, True, None)
    xp, st1 = _layer1(xp, p, True, None)
    pos_s = PAST_LEN + (np.arange(Bs * Ts) % Ts)
    xs, ss0 = _layer0(x_sample, _rope_tables(pos_s, hd), p, False,
                      (state_rglru_conv[0], state_rglru_h[0], cache_swa_k[0], cache_swa_v[0]))
    xs, ss1 = _layer1(xs, p, False, (state_mlstm_conv[0], state_mlstm_C[0], state_mlstm_n[0], state_mlstm_m[0]))

    lead = lambda a: a[None].astype(dt)
    return (xp, xs,
            lead(st0[0]), lead(ss0[0]), lead(st0[1]), lead(ss0[1]),
            lead(st0[2]), lead(ss0[2]), lead(st0[3]), lead(ss0[3]),
            lead(st1[0]), lead(ss1[0]), lead(st1[1]), lead(ss1[1]),
            lead(st1[2]), lead(ss1[2]), lead(st1[3]), lead(ss1[3]))
```

```python
import functools
import math

import numpy as np
import jax
import jax.numpy as jnp
from jax import lax
from jax.experimental import pallas as pl
from jax.experimental.pallas import tpu as pltpu

F32 = jnp.float32
BF16 = jnp.bfloat16

CONV_W = 4
LRU_C = 8.0
H_B = 8
ROPE_THETA = 500000.0
DILATIONS = ((128, 1), (512, 4), (2048, 16))
MAX_WINDOW = 2048
QBLOCK = 128
NH_C = 4
PAST_LEN = 16384
EPS = 1e-6

LANES = 128
SUBLANES = 8
MXU_DIM = 256
BF16_ROWS = 16
V7X_VMEM_BYTES = 64 * 1024 * 1024
VMEM_LIMIT = V7X_VMEM_BYTES * 7 // 8

NEG_INF = float("-inf")


def _cparams(sem):
    return pltpu.CompilerParams(dimension_semantics=sem, vmem_limit_bytes=VMEM_LIMIT)


def _tile(n, pref):
    t = min(n, pref)
    while n % t:
        t -= 1
    return t


def _rms(x, g):
    return x * lax.rsqrt(jnp.mean(x * x, axis=-1, keepdims=True) + EPS) * g


def _softplus(z):
    return jnp.maximum(z, 0.0) + jnp.log1p(jnp.exp(-jnp.abs(z)))


def _gelu_tanh(x):
    c = math.sqrt(2.0 / math.pi)
    return x * (0.5 * (1.0 + jnp.tanh(c * (x + 0.044715 * (x * x * x)))))


def _silu(x):
    return x * jax.nn.sigmoid(x)


def _rope_tile(y, c, sa, sb):
    outs = []
    for h in range(y.shape[1] // LANES):
        yh = y[:, h * LANES:(h + 1) * LANES]
        outs.append(yh * c + pltpu.roll(yh, LANES - 16, 1) * sa + pltpu.roll(yh, 16, 1) * sb)
    return jnp.concatenate(outs, axis=1)


def _norm_proj_kernel(*refs, groups, tn, rope_q, rope_k, q_scale):
    n_out = len(groups)
    x_ref, g_ref, w_ref = refs[:3]
    pos = 3
    if rope_q is not None:
        c_ref, sa_ref, sb_ref = refs[3:6]
        pos = 6
    out_refs = refs[pos:pos + n_out]
    xn = _rms(x_ref[...], g_ref[...]).astype(BF16)
    for (start, count, _, act), o_ref in zip(groups, out_refs):
        for jj in range(start, start + count):
            y = jnp.dot(xn, w_ref[:, jj * tn:(jj + 1) * tn], preferred_element_type=F32)
            if rope_q is not None and jj == rope_q:
                y = _rope_tile(y, c_ref[...], sa_ref[...], sb_ref[...]) * q_scale
            elif rope_q is not None and jj == rope_k:
                y = _rope_tile(y, c_ref[...], sa_ref[...], sb_ref[...])
            if act is not None:
                y = act(y)
            o_ref[:, (jj - start) * tn:(jj - start + 1) * tn] = y.astype(o_ref.dtype)


def _norm_proj(x, g, w, groups, tm, tn, rope=None, rope_q=None, rope_k=None, q_scale=1.0):
    M, K = x.shape
    in_specs = [
        pl.BlockSpec((tm, K), lambda i: (i, 0)),
        pl.BlockSpec((1, K), lambda i: (0, 0)),
        pl.BlockSpec(w.shape, lambda i: (0, 0)),
    ]
    args = [x, g, w]
    if rope is not None:
        assert rope[0].shape[0] % tm == 0 and M % tm == 0
        pos_blocks = rope[0].shape[0] // tm
        for t in rope:
            in_specs.append(pl.BlockSpec((tm, LANES), lambda i, pb=pos_blocks: (i % pb, 0)))
            args.append(t)
    out_specs, out_shapes = [], []
    for (_, count, dtype, _) in groups:
        out_specs.append(pl.BlockSpec((tm, count * tn), lambda i: (i, 0)))
        out_shapes.append(jax.ShapeDtypeStruct((M, count * tn), dtype))
    kern = functools.partial(_norm_proj_kernel, groups=tuple(groups), tn=tn,
                             rope_q=rope_q if rope is not None else None, rope_k=rope_k, q_scale=q_scale)
    return pl.pallas_call(
        kern,
        grid=(M // tm,),
        in_specs=in_specs,
        out_specs=out_specs,
        out_shape=out_shapes,
        compiler_params=_cparams(("parallel",)),
        name="norm_proj",
    )(*args)


def _out_proj_kernel(*refs, n_in):
    a_refs = refs[:n_in]
    w_ref, x_ref, g_ref, gn_ref, o_ref, on_ref, wb_ref = refs[n_in:]

    @pl.when(pl.program_id(0) == 0)
    def _():
        wb_ref[...] = w_ref[...].astype(BF16)

    tm = x_ref.shape[0]
    half = min(tm, 2 * LANES)
    for r0 in range(0, tm, half):
        rs = slice(r0, r0 + half)
        y, row = None, 0
        for a_ref in a_refs:
            kk = a_ref.shape[1]
            d = jnp.dot(a_ref[rs, :].astype(BF16), wb_ref[row:row + kk, :], preferred_element_type=F32)
            y = d if y is None else y + d
            row += kk
        x1 = x_ref[rs, :] + _rms(y, g_ref[...])
        o_ref[rs, :] = x1
        on_ref[rs, :] = _rms(x1, gn_ref[...]).astype(on_ref.dtype)


def _out_proj(acts, w, x, g, g_next, tm):
    M, D = x.shape
    n_in = len(acts)
    assert sum(a.shape[1] for a in acts) == w.shape[0]
    row_tile = pl.BlockSpec((tm, D), lambda i: (i, 0))
    gain = pl.BlockSpec((1, D), lambda i: (0, 0))
    in_specs = [pl.BlockSpec((tm, a.shape[1]), lambda i: (i, 0)) for a in acts]
    in_specs += [pl.BlockSpec(w.shape, lambda i: (0, 0), pipeline_mode=pl.Buffered(1)), row_tile, gain, gain]
    return pl.pallas_call(
        functools.partial(_out_proj_kernel, n_in=n_in),
        grid=(M // tm,),
        in_specs=in_specs,
        out_specs=[row_tile, row_tile],
        out_shape=[jax.ShapeDtypeStruct((M, D), F32), jax.ShapeDtypeStruct((M, D), BF16)],
        scratch_shapes=[pltpu.VMEM(w.shape, BF16)],
        compiler_params=_cparams(("arbitrary",)),
        name="out_proj",
    )(*acts, w, x, g, g_next)


def _ffn_kernel(x_ref, xn_ref, wu_ref, wd_ref, g2_ref, o_ref, acc_ref):
    f = pl.program_id(1)

    @pl.when(f == 0)
    def _():
        acc_ref[...] = jnp.zeros_like(acc_ref)

    hid = jnp.dot(xn_ref[...], wu_ref[...].astype(BF16), preferred_element_type=F32)
    hid = jnp.square(jnp.maximum(hid, 0.0))
    acc_ref[...] += jnp.dot(hid.astype(BF16), wd_ref[...].astype(BF16), preferred_element_type=F32)

    @pl.when(f == pl.num_programs(1) - 1)
    def _():
        o_ref[...] = x_ref[...] + _rms(acc_ref[...], g2_ref[...])


def _ffn(x, xn, wu, wd, layer, g2, tm, tf):
    M, D = x.shape
    FF = wu.shape[2]
    return pl.pallas_call(
        _ffn_kernel,
        grid=(M // tm, FF // tf),
        in_specs=[
            pl.BlockSpec((tm, D), lambda i, f: (i, 0)),
            pl.BlockSpec((tm, D), lambda i, f: (i, 0)),
            pl.BlockSpec((None, D, tf), lambda i, f: (layer, 0, f)),
            pl.BlockSpec((None, tf, D), lambda i, f: (layer, f, 0)),
            pl.BlockSpec((1, D), lambda i, f: (0, 0)),
        ],
        out_specs=pl.BlockSpec((tm, D), lambda i, f: (i, 0)),
        out_shape=jax.ShapeDtypeStruct((M, D), F32),
        scratch_shapes=[pltpu.VMEM((tm, D), F32)],
        compiler_params=_cparams(("parallel", "arbitrary")),
        name="ffn",
    )(x, xn, wu, wd, g2)


def _lru_gates(xc, wa_ref, ba, wx_ref, bx, lam):
    xcb = xc.astype(BF16)
    ng = wa_ref.shape[0]
    gw = wa_ref.shape[1]
    ra, ri = [], []
    for g in range(ng):
        xs = xcb[:, g * gw:(g + 1) * gw]
        ra.append(jnp.dot(xs, wa_ref[g], preferred_element_type=F32))
        ri.append(jnp.dot(xs, wx_ref[g], preferred_element_type=F32))
    r = jax.nn.sigmoid(jnp.concatenate(ra, axis=1) + ba)
    i = jax.nn.sigmoid(jnp.concatenate(ri, axis=1) + bx)
    log_a = -LRU_C * r * _softplus(-lam)
    a = jnp.exp(log_a)
    om = 1.0 - a * a
    b = jnp.where(om > 0.0, om * lax.rsqrt(om), 0.0) * (i * xc)
    return a, b


def _conv4_rows(xe, halo, w, b):
    a = w[0:1, :] * xe
    for j in range(1, CONV_W):
        a = pltpu.roll(a, 1, 0) + w[j:j + 1, :] * xe
    return b + a[halo:]


def _conv4(tail_ref, x, first, w_ref, b):
    @pl.when(first)
    def _():
        tail_ref[...] = jnp.zeros_like(tail_ref)

    xe = jnp.concatenate([tail_ref[...], x], axis=0)
    tail_ref[...] = x[x.shape[0] - SUBLANES:, :]
    return _conv4_rows(xe, SUBLANES, w_ref[...], b)


def _rglru_prompt_kernel(xa_ref, ga_ref, cw_ref, cb_ref, wa_ref, ba_ref, wx_ref, bx_ref, lam_ref,
                         ya_ref, hl_ref, tail_ref, hc_ref, a_s, b_s, h_s):
    t = pl.program_id(1)
    T = xa_ref.shape[0]

    @pl.when(t == 0)
    def _():
        hc_ref[...] = jnp.zeros_like(hc_ref)

    xc = _conv4(tail_ref, xa_ref[...], t == 0, cw_ref, cb_ref[...])
    a, b = _lru_gates(xc, wa_ref, ba_ref[...], wx_ref, bx_ref[...], lam_ref[...])
    a_s[...] = a
    b_s[...] = b

    def body(s, h):
        h = a_s[pl.ds(s, 1), :] * h + b_s[pl.ds(s, 1), :]
        h_s[pl.ds(s, 1), :] = h
        return h

    h = lax.fori_loop(0, T, body, hc_ref[...], unroll=8)
    hc_ref[...] = h
    ya_ref[...] = (h_s[...] * _gelu_tanh(ga_ref[...])).astype(ya_ref.dtype)

    @pl.when(t == pl.num_programs(1) - 1)
    def _():
        hl_ref[...] = h


def _rglru_prompt(xaga, cw, cb, wa_bd, ba, wx_bd, bx, lam, ts):
    B, S, _ = xaga.shape
    C = cw.shape[-1]
    wspec = lambda a: pl.BlockSpec(a.shape, lambda b, t: (0,) * a.ndim)
    return pl.pallas_call(
        _rglru_prompt_kernel,
        grid=(B, S // ts),
        in_specs=[
            pl.BlockSpec((None, ts, C), lambda b, t: (b, t, 0)),
            pl.BlockSpec((None, ts, C), lambda b, t: (b, t, 1)),
            wspec(cw), wspec(cb), wspec(wa_bd), wspec(ba), wspec(wx_bd), wspec(bx), wspec(lam),
        ],
        out_specs=[
            pl.BlockSpec((None, ts, C), lambda b, t: (b, t, 0)),
            pl.BlockSpec((None, 1, C), lambda b, t: (b, 0, 0)),
        ],
        out_shape=[jax.ShapeDtypeStruct((B, S, C), BF16), jax.ShapeDtypeStruct((B, 1, C), F32)],
        scratch_shapes=[pltpu.VMEM((SUBLANES, C), F32), pltpu.VMEM((1, C), F32),
                        pltpu.VMEM((ts, C), F32), pltpu.VMEM((ts, C), F32), pltpu.VMEM((ts, C), F32)],
        compiler_params=_cparams(("parallel", "arbitrary")),
        name="rglru_prompt",
    )(xaga, xaga, cw, cb, wa_bd, ba, wx_bd, bx, lam)


def _rglru_sample_kernel(xa_ref, ga_ref, st_ref, h0_ref, cw_ref, cb_ref, wa_ref, ba_ref, wx_ref, bx_ref,
                         lam_ref, ya_ref, hl_ref):
    T = xa_ref.shape[0]
    rows = [st_ref[j] for j in range(CONV_W - 1)] + [xa_ref[s] for s in range(T)]
    h = h0_ref[...]
    for s in range(T):
        y = cb_ref[...]
        for j in range(CONV_W):
            y = y + cw_ref[j:j + 1, :] * rows[s + j]
        a, b = _lru_gates(y, wa_ref, ba_ref[...], wx_ref, bx_ref[...], lam_ref[...])
        h = a * h + b
        ya_ref[s] = (h * _gelu_tanh(ga_ref[s])).astype(ya_ref.dtype)
    hl_ref[...] = h


def _rglru_sample(xa_t, ga_t, st_t, h0, cw, cb, wa_bd, ba, wx_bd, bx, lam):
    T, B, C = xa_t.shape
    args = (xa_t, ga_t, st_t, h0, cw, cb, wa_bd, ba, wx_bd, bx, lam)
    return pl.pallas_call(
        _rglru_sample_kernel,
        grid=(1,),
        in_specs=[pl.BlockSpec(a.shape, lambda i, n=a.ndim: (0,) * n) for a in args],
        out_specs=[pl.BlockSpec((T, B, C), lambda i: (0, 0, 0)), pl.BlockSpec((B, C), lambda i: (0, 0))],
        out_shape=[jax.ShapeDtypeStruct((T, B, C), F32), jax.ShapeDtypeStruct((B, C), F32)],
        compiler_params=_cparams(("arbitrary",)),
        name="rglru_sample",
    )(*args)


def _attn_prompt_kernel(q_ref, k_ref, v_ref, o_ref, *scratch, groups, qb):
    S = q_ref.shape[0]
    E = q_ref.shape[1]
    dist2 = (lax.broadcasted_iota(jnp.int32, (qb, 2 * qb), 0) + qb
             - lax.broadcasted_iota(jnp.int32, (qb, 2 * qb), 1))
    dist1 = lax.broadcasted_iota(jnp.int32, (qb, qb), 0) - lax.broadcasted_iota(jnp.int32, (qb, qb), 1)

    def rows(start, d):
        return pl.ds(start, qb) if d == 1 else pl.ds(start, qb, stride=d)

    ng = len(groups)
    s_scr, p_scr = scratch[3 * ng], scratch[3 * ng + 1]
    dn_t = (((1,), (1,)), ((), ()))

    for gi, (w, d) in enumerate(groups):
        back = w // d
        nb = (S // d) // qb
        nblk = nb * d
        two = nb > 1
        kw = 2 * qb if two else qb
        dist = dist2 if two else dist1
        lim_all = jnp.full((qb, kw), back, jnp.int32)
        lim_first = jnp.minimum(lax.broadcasted_iota(jnp.int32, (qb, kw), 0), back)
        m_s, l_s, acc_s = scratch[3 * gi:3 * gi + 3]

        def locate(idx, d=d):
            if d == 1:
                n = idx
                start = pl.multiple_of(idx * qb, qb)
                prev = pl.multiple_of(jnp.maximum(idx - 1, 0) * qb, qb)
            else:
                n = idx // d
                start = n * (qb * d) + idx % d
                prev = jnp.where(n > 0, start - qb * d, start)
            return start, prev, n

        def gather(ref, start, prev, d=d, two=two):
            if two:
                return jnp.concatenate([ref[rows(prev, d), :], ref[rows(start, d), :]], axis=0).astype(BF16)
            return ref[rows(start, d), :].astype(BF16)

        def scores(idx, c, kw=kw, locate=locate, gather=gather, d=d):
            start, prev, _ = locate(idx)
            q = q_ref[rows(start, d), :].astype(BF16)
            s_scr[pl.ds(pl.multiple_of(idx * qb, qb), qb), 0:kw] = lax.dot_general(
                q, gather(k_ref, start, prev), dn_t, preferred_element_type=F32)
            return c

        def softmax(idx, c, kw=kw, locate=locate, d=d, two=two, dist=dist, lim_all=lim_all,
                    lim_first=lim_first, m_s=m_s):
            start, _, n = locate(idx)
            blk = pl.ds(pl.multiple_of(idx * qb, qb), qb)
            lim = jnp.where(n > 0, lim_all, lim_first) if two else lim_all
            valid = (dist >= 0) & (dist <= lim)
            s = jnp.where(valid, s_scr[blk, 0:kw], NEG_INF)
            m_b = jnp.max(s, axis=-1, keepdims=True)
            p_scr[blk, 0:kw] = jnp.exp(s - m_b).astype(BF16)
            m_s[rows(start, d), :] = jnp.broadcast_to(m_b, (qb, E))
            return c

        def values(idx, c, kw=kw, locate=locate, gather=gather, d=d, acc_s=acc_s, l_s=l_s):
            start, prev, _ = locate(idx)
            p = p_scr[pl.ds(pl.multiple_of(idx * qb, qb), qb), 0:kw]
            v1 = jnp.concatenate([gather(v_ref, start, prev), jnp.ones((kw, E), BF16)], axis=1)
            r = jnp.dot(p, v1, preferred_element_type=F32)
            acc_s[rows(start, d), :] = r[:, :E]
            l_s[rows(start, d), :] = r[:, E:]
            return c

        lax.fori_loop(0, nblk, scores, 0, unroll=True)
        lax.fori_loop(0, nblk, softmax, 0, unroll=True)
        lax.fori_loop(0, nblk, values, 0, unroll=True)

    ng = len(groups)

    def merge(i, c):
        rs = pl.ds(pl.multiple_of(i * qb, qb), qb)
        ms = [scratch[3 * g][rs, :] for g in range(ng)]
        m = functools.reduce(jnp.maximum, ms)
        num = den = None
        for g in range(ng):
            wg = jnp.exp(ms[g] - m)
            n_g = wg * scratch[3 * g + 2][rs, :]
            d_g = wg * scratch[3 * g + 1][rs, :]
            num = n_g if num is None else num + n_g
            den = d_g if den is None else den + d_g
        o_ref[rs, :] = (num / den).astype(o_ref.dtype)
        return c

    lax.fori_loop(0, S // qb, merge, 0, unroll=2)


def _attn_prompt(q, q_off, k, v, groups, qb):
    B, S, D = k.shape
    nh = D // LANES
    for (w, d) in groups:
        assert S % (d * qb) == 0 and w // d <= qb
    spec = pl.BlockSpec((None, S, LANES), lambda b, h: (b, 0, h))
    q_spec = pl.BlockSpec((None, S, LANES), lambda b, h: (b, 0, q_off + h))
    return pl.pallas_call(
        functools.partial(_attn_prompt_kernel, groups=tuple(groups), qb=qb),
        grid=(B, nh),
        in_specs=[q_spec, spec, spec],
        out_specs=spec,
        out_shape=jax.ShapeDtypeStruct((B, S, D), BF16),
        scratch_shapes=[pltpu.VMEM((S, LANES), F32)] * (3 * len(groups))
        + [pltpu.VMEM((S, 2 * qb), F32), pltpu.VMEM((S, 2 * qb), BF16)],
        compiler_params=_cparams(("parallel", "parallel")),
        name="attn_prompt",
    )(q, k, v)


def _attn_sample_kernel(q_ref, kn_ref, vn_ref, kt_ref, vt_ref, ks_ref, vs_ref, mt_ref, ms_ref, mn_ref, o_ref):
    nq, hd = q_ref.shape
    qb = jnp.concatenate([q_ref[...], jnp.zeros((LANES - nq, hd), F32)], axis=0).astype(BF16)
    nn = kn_ref.shape[0]
    zn = jnp.zeros((LANES - nn, hd), F32)
    key_sets = (
        (kt_ref[...], vt_ref[...], mt_ref[...]),
        (ks_ref[...].reshape(-1, hd), vs_ref[...].reshape(-1, hd), ms_ref[...]),
        (jnp.concatenate([kn_ref[...], zn], axis=0), jnp.concatenate([vn_ref[...], zn], axis=0), mn_ref[...]),
    )
    partial = []
    for k, v, bias in key_sets:
        s = lax.dot_general(k.astype(BF16), qb, (((1,), (1,)), ((), ())), preferred_element_type=F32) + bias
        m_set = jnp.max(s, axis=0, keepdims=True)
        p = jnp.exp(s - jnp.where(m_set == NEG_INF, 0.0, m_set)).astype(BF16)
        v1 = jnp.concatenate([v.astype(BF16), jnp.ones(v.shape, BF16)], axis=1)
        partial.append((m_set, lax.dot_general(p, v1, (((0,), (0,)), ((), ())), preferred_element_type=F32)))
    m = functools.reduce(jnp.maximum, [m_set for m_set, _ in partial])
    m = jnp.where(m == NEG_INF, 0.0, m)
    acc = None
    for m_set, r in partial:
        w = jnp.broadcast_to(jnp.exp(m_set - m), (LANES, LANES)).T
        r = jnp.concatenate([w, w], axis=1) * r
        acc = r if acc is None else acc + r
    o_ref[...] = acc[:nq, :hd] / acc[:nq, hd:]


def _attn_sample(q, kn, vn, kc, vc, T, nh):
    B, nq, hd = q.shape
    wb = kc.shape[1] // nh
    far = max(DILATIONS, key=lambda wd: wd[0])
    near = [g for g in DILATIONS if g != far]
    R = max(w for w, _ in near)
    d_far = far[1]
    assert near and wb % R == 0 and wb % d_far == 0 and T <= d_far and nq == T * nh and nq <= LANES
    A = wb // d_far

    def mult(delta, groups):
        m = np.zeros(delta.shape, np.float32)
        for (w, d) in groups:
            m = m + ((delta >= 0) & (delta % d == 0) & (delta <= (w // d) * d))
        return m

    def table(pos_of_row, head_of_row, groups):
        lane = np.arange(LANES)
        t, hq = lane // nh, lane % nh
        delta = (wb + t)[None, :] - pos_of_row[:, None]
        tab = mult(delta, groups) * (head_of_row[:, None] == hq[None, :]) * (t < T)[None, :]
        return jnp.asarray(np.where(tab > 0, np.log(np.maximum(tab, 1.0)), NEG_INF), F32)

    rt = np.arange(R * nh)
    mt = table(wb - R + rt // nh, rt % nh, near)
    rs = np.arange(A * T * nh)
    ms = table((rs // (T * nh)) * d_far + (rs % (T * nh)) // nh, rs % nh, [far])
    rn = np.arange(LANES)
    mn = table(np.where(rn < nq, wb + rn // nh, -10 ** 9), rn % nh, DILATIONS)

    small = pl.BlockSpec((None, nq, hd), lambda b: (b, 0, 0))
    tail = pl.BlockSpec((None, R * nh, hd), lambda b: (b, wb // R - 1, 0))
    strided = pl.BlockSpec((None, A, T * nh, hd), lambda b: (b, 0, 0, 0))
    const = lambda a: pl.BlockSpec(a.shape, lambda b: (0, 0))
    kc4 = kc.reshape(B, A, d_far * nh, hd)
    vc4 = vc.reshape(B, A, d_far * nh, hd)
    return pl.pallas_call(
        _attn_sample_kernel,
        grid=(B,),
        in_specs=[small, small, small, tail, tail, strided, strided, const(mt), const(ms), const(mn)],
        out_specs=small,
        out_shape=jax.ShapeDtypeStruct((B, nq, hd), F32),
        compiler_params=_cparams(("parallel",)),
        name="attn_sample",
    )(q, kn, vn, kc, vc, kc4, vc4, mt, ms, mn)


def _mlstm_qkv_gates(xc, xm, wq_ref, wk_ref, wv_ref, wg_ref, bg_ref, q_ref, k_ref, v_ref, xc_ref, g_ref, k_scale,
                     rs=slice(None)):
    xcb = xc.astype(BF16)
    xmb = xm.astype(BF16)
    ng, gw = wq_ref.shape[0], wq_ref.shape[1]
    qs, ks, vs = [], [], []
    for g in range(ng):
        sl = slice(g * gw, (g + 1) * gw)
        qs.append(jnp.dot(xcb[:, sl], wq_ref[g], preferred_element_type=F32))
        ks.append(jnp.dot(xcb[:, sl], wk_ref[g], preferred_element_type=F32))
        vs.append(jnp.dot(xmb[:, sl], wv_ref[g], preferred_element_type=F32))
    k = jnp.concatenate(ks, axis=1)
    qb = jnp.concatenate(qs, axis=1).astype(BF16)
    vb = jnp.concatenate(vs, axis=1).astype(BF16)
    qkv = jnp.concatenate([qb, k.astype(BF16), vb], axis=1)
    g_ref[rs, :] = jnp.dot(qkv, wg_ref[...], preferred_element_type=F32) + bg_ref[...]
    q_ref[rs, :] = qb
    k_ref[rs, :] = (k * k_scale).astype(k_ref.dtype)
    v_ref[rs, :] = vb
    xc_ref[rs, :] = xcb


def _mlstm_pre_sample_kernel(xm_ref, st_ref, cw_ref, cb_ref, wq_ref, wk_ref, wv_ref, wg_ref, bg_ref,
                             q_ref, k_ref, v_ref, xc_ref, g_ref, *, k_scale):
    T = xm_ref.shape[0]
    rows = [st_ref[j] for j in range(CONV_W - 1)] + [xm_ref[s] for s in range(T)]
    conv = []
    for s in range(T):
        y = cb_ref[...]
        for j in range(CONV_W):
            y = y + cw_ref[j:j + 1, :] * rows[s + j]
        conv.append(y)
    xc = _silu(jnp.concatenate(conv, axis=0))
    xm = jnp.concatenate(rows[CONV_W - 1:], axis=0)
    _mlstm_qkv_gates(xc, xm, wq_ref, wk_ref, wv_ref, wg_ref, bg_ref, q_ref, k_ref, v_ref, xc_ref, g_ref, k_scale)


def _mlstm_pre_sample(xm_t, st_t, cw, cb, wq_bd, wk_bd, wv_bd, wg, bg, k_scale):
    T, B, C = xm_t.shape
    args = (xm_t, st_t, cw, cb, wq_bd, wk_bd, wv_bd, wg, bg)
    full = lambda shape: pl.BlockSpec(shape, lambda i: (0,) * len(shape))
    return pl.pallas_call(
        functools.partial(_mlstm_pre_sample_kernel, k_scale=k_scale),
        grid=(1,),
        in_specs=[full(a.shape) for a in args],
        out_specs=[full((T * B, C))] * 4 + [full((T * B, 2 * LANES))],
        out_shape=[jax.ShapeDtypeStruct((T * B, C), BF16)] * 4 + [jax.ShapeDtypeStruct((T * B, 2 * LANES), F32)],
        compiler_params=_cparams(("arbitrary",)),
        name="mlstm_pre_sample",
    )(*args)


CONV_HALO = BF16_ROWS


def _mlstm_in_kernel(x_ref, xh_ref, g_ref, w_ref, cw_ref, cb_ref, wq_ref, wk_ref, wv_ref, wg_ref, bg_ref,
                     q_ref, k_ref, v_ref, xc_ref, gz_ref, gt_ref, tail_ref, *, tiles_per_seq, tn, k_scale):
    tm = x_ref.shape[0]
    C = cw_ref.shape[1]
    first = (pl.program_id(0) % tiles_per_seq) == 0
    g = g_ref[...]
    xn = _rms(x_ref[...], g)
    xe = jnp.concatenate([_rms(xh_ref[...], g), xn], axis=0).astype(BF16)
    xn = xn.astype(BF16)
    dead_halo = lax.broadcasted_iota(jnp.int32, (CONV_HALO + tm, tn), 0) < jnp.where(first, CONV_HALO, 0)
    xmes = []
    for jj in range(C // tn):
        cs = slice(jj * tn, (jj + 1) * tn)
        xm_e = jnp.dot(xe, w_ref[:, cs], preferred_element_type=F32)
        xmes.append(jnp.where(dead_halo, 0.0, xm_e))
        gz_ref[:, cs] = _silu(jnp.dot(xn, w_ref[:, C + jj * tn:C + (jj + 1) * tn],
                                      preferred_element_type=F32)).astype(gz_ref.dtype)
    half = LANES
    for r0 in range(0, tm, half):
        xc = jnp.concatenate(
            [_silu(_conv4_rows(xm_e[r0:r0 + CONV_HALO + half], CONV_HALO, cw_ref[:, jj * tn:(jj + 1) * tn],
                               cb_ref[:, jj * tn:(jj + 1) * tn])) for jj, xm_e in enumerate(xmes)], axis=1)
        xm = jnp.concatenate([xm_e[CONV_HALO + r0:CONV_HALO + r0 + half] for xm_e in xmes], axis=1)
        _mlstm_qkv_gates(xc, xm, wq_ref, wk_ref, wv_ref, wg_ref, bg_ref, q_ref, k_ref, v_ref, xc_ref, gt_ref,
                         k_scale, rs=slice(r0, r0 + half))
    tail_ref[...] = jnp.concatenate([xm_e[CONV_HALO + tm - SUBLANES:] for xm_e in xmes], axis=1)


def _mlstm_in(x, g, w, cw, cb, wq_bd, wk_bd, wv_bd, wg, bg, seq, tm, k_scale):
    M, K = x.shape
    C = cw.shape[1]
    assert seq % tm == 0 and tm % CONV_HALO == 0
    hb = tm // CONV_HALO
    const = lambda a: pl.BlockSpec(a.shape, lambda i: (0,) * a.ndim, pipeline_mode=pl.Buffered(1))
    act = pl.BlockSpec((tm, C), lambda i: (i, 0))
    return pl.pallas_call(
        functools.partial(_mlstm_in_kernel, tiles_per_seq=seq // tm, tn=1024, k_scale=k_scale),
        grid=(M // tm,),
        in_specs=[pl.BlockSpec((tm, K), lambda i: (i, 0)),
                  pl.BlockSpec((CONV_HALO, K), lambda i: (jnp.maximum(i * hb - 1, 0), 0)),
                  const(g), const(w), const(cw), const(cb), const(wq_bd), const(wk_bd), const(wv_bd),
                  const(wg), const(bg)],
        out_specs=[act, act, act, act, act, pl.BlockSpec((tm, 2 * LANES), lambda i: (i, 0)),
                   pl.BlockSpec((None, SUBLANES, C), lambda i: (i, 0, 0))],
        out_shape=[jax.ShapeDtypeStruct((M, C), BF16)] * 5 + [jax.ShapeDtypeStruct((M, 2 * LANES), F32),
                                                              jax.ShapeDtypeStruct((M // tm, SUBLANES, C), F32)],
        compiler_params=_cparams(("parallel",)),
        name="mlstm_in",
    )(x, x, g, w, cw, cb, wq_bd, wk_bd, wv_bd, wg, bg)


def _mlstm_core_kernel(*refs, c, n_valid, nh, zero_init):
    q_ref, k_ref, v_ref, xc_ref, gz_ref, g_ref, nw_ref, sk_ref = refs[:8]
    pos = 8
    if not zero_init:
        c0_ref, n0_ref, m0_ref = refs[8:11]
        pos = 11
    y_ref, C_ref, n_ref, m_ref = refs[pos:pos + 4]
    ch = pl.program_id(1)
    c_in = q_ref.shape[0]
    DH = q_ref.shape[1] // nh

    @pl.when(ch == 0)
    def _():
        if zero_init:
            C_ref[...] = jnp.zeros_like(C_ref)
            n_ref[...] = jnp.zeros_like(n_ref)
            m_ref[...] = jnp.zeros_like(m_ref)
        else:
            C_ref[...] = c0_ref[...]
            n_ref[...] = n0_ref[...]
            m_ref[...] = m0_ref[...]

    def padded(x):
        if c_in == c:
            return x
        return jnp.concatenate([x, jnp.zeros((c - c_in, x.shape[1]), x.dtype)], axis=0)

    g = padded(g_ref[...])
    row = lax.broadcasted_iota(jnp.int32, (c, LANES), 0)
    valid = row < n_valid
    gi = jnp.where(valid, g[:, :LANES], NEG_INF)
    lf = jnp.where(valid, -_softplus(-g[:, LANES:]), 0.0)
    tri_f = (lax.broadcasted_iota(jnp.int32, (c, c), 1) <= lax.broadcasted_iota(jnp.int32, (c, c), 0))
    tri_b = tri_f.astype(BF16)
    bcum, rest = None, lf
    for _ in range(3):
        part = rest.astype(BF16)
        rest = rest - part.astype(F32)
        d = jnp.dot(tri_b, part, preferred_element_type=F32)
        bcum = d if bcum is None else bcum + d
    bcum_t = bcum.T
    gi_t = gi.T
    causal = tri_f

    for h in range(nh):
        sl = slice(h * DH, (h + 1) * DH)
        qh = padded(q_ref[:, sl])
        kh = padded(k_ref[:, sl])
        vh = padded(v_ref[:, sl])
        bc_col = bcum[:, h:h + 1]
        bc_row = bcum_t[h:h + 1, :]
        i_row = gi_t[h:h + 1, :]
        i_col = gi[:, h:h + 1]
        m_prev = m_ref[h:h + 1, 0:1]
        log_intra = jnp.where(causal, bc_col - bc_row + i_row, NEG_INF)
        log_inter = bc_col + m_prev
        m_t = jnp.maximum(log_inter, jnp.max(log_intra, axis=1, keepdims=True))
        qk = lax.dot_general(qh, kh, (((1,), (1,)), ((), ())), preferred_element_type=F32)
        w_intra = jnp.exp(log_intra - m_t) * qk
        w_inter = jnp.exp(log_inter - m_t)
        C_old = C_ref[h]
        n_old = n_ref[h:h + 1, :]
        num = jnp.dot(w_intra.astype(BF16), vh, preferred_element_type=F32)
        num = num + w_inter * jnp.dot(qh, C_old.astype(BF16), preferred_element_type=F32)
        n8 = jnp.broadcast_to(n_old, (SUBLANES, DH)).astype(BF16)
        qn = lax.dot_general(qh, n8, (((1,), (1,)), ((), ())), preferred_element_type=F32)[:, 0:1]
        den = jnp.sum(w_intra, axis=1, keepdims=True) + w_inter * qn
        hh = num * (1.0 / jnp.maximum(jnp.abs(den), jnp.exp(-m_t)))
        m_new = m_t[c - 1:c, :]
        bc_last = bc_col[c - 1:c, :]
        w_state = jnp.exp(bc_last - bc_col + i_col - m_new)
        decay = jnp.exp(bc_last + m_prev - m_new)
        kw = kh.astype(F32) * w_state
        C_ref[h] = decay * C_old + lax.dot_general(
            kw.astype(BF16), vh, (((0,), (0,)), ((), ())), preferred_element_type=F32)
        n_ref[h:h + 1, :] = decay * n_old + jnp.sum(kw, axis=0, keepdims=True)
        m_ref[h:h + 1, :] = jnp.broadcast_to(m_new, (1, LANES))
        hn = hh * lax.rsqrt(jnp.mean(hh * hh, axis=-1, keepdims=True) + EPS)
        hn = hn[:c_in] * nw_ref[:, sl]
        y = (hn + sk_ref[:, sl] * xc_ref[:, sl].astype(F32)) * gz_ref[:, sl].astype(F32)
        y_ref[:, sl] = y.astype(y_ref.dtype)


def _mlstm_core(q, k, v, xc, gz, gates, nw, sk, state, c, n_valid, nh):
    B, S, C = q.shape
    c_in = min(c, S)
    DH = C // nh
    zero_init = state is None
    act = pl.BlockSpec((None, c_in, C), lambda b, t: (b, t, 0))
    in_specs = [act, act, act, act, act,
                pl.BlockSpec((None, c_in, 2 * LANES), lambda b, t: (b, t, 0)),
                pl.BlockSpec((1, C), lambda b, t: (0, 0)), pl.BlockSpec((1, C), lambda b, t: (0, 0))]
    args = [q, k, v, xc, gz, gates, nw, sk]
    C_spec = pl.BlockSpec((None, nh, DH, DH), lambda b, t: (b, 0, 0, 0))
    n_spec = pl.BlockSpec((None, nh, DH), lambda b, t: (b, 0, 0))
    m_spec = pl.BlockSpec((None, nh, LANES), lambda b, t: (b, 0, 0))
    if not zero_init:
        in_specs += [C_spec, n_spec, m_spec]
        args += list(state)
    return pl.pallas_call(
        functools.partial(_mlstm_core_kernel, c=c, n_valid=n_valid, nh=nh, zero_init=zero_init),
        grid=(B, S // c_in),
        in_specs=in_specs,
        out_specs=[act, C_spec, n_spec, m_spec],
        out_shape=[jax.ShapeDtypeStruct((B, S, C), BF16),
                   jax.ShapeDtypeStruct((B, nh, DH, DH), F32),
                   jax.ShapeDtypeStruct((B, nh, DH), F32),
                   jax.ShapeDtypeStruct((B, nh, LANES), F32)],
        compiler_params=_cparams(("parallel", "arbitrary")),
        name="mlstm_core",
    )(*args)


def _rope_tables(pos, hd):
    rot = hd // 4
    half = rot // 2
    inv = ROPE_THETA ** (-np.arange(0, rot, 2, dtype=np.float64) / rot)
    ang = pos.astype(np.float64)[:, None] * inv[None, :]
    cos, sin = np.cos(ang), np.sin(ang)
    n = pos.shape[0]
    c = np.concatenate([cos, cos, np.ones((n, hd - rot))], axis=1)
    sa = np.concatenate([-sin, np.zeros((n, hd - half))], axis=1)
    sb = np.concatenate([np.zeros((n, half)), sin, np.zeros((n, hd - rot))], axis=1)
    return tuple(jnp.asarray(t, F32) for t in (c, sa, sb))


def _regroup_block_diag(w, group):
    nb, bi, bo = w.shape
    per = group // bi
    a = w.reshape(nb // per, per * bi, bo)
    cols = np.arange(per * bo)
    tile_cols = jnp.asarray(cols[None, :] % bo == np.arange(bo)[:, None], BF16)
    same_block = jnp.asarray((np.arange(per * bi)[:, None] // bi) == (cols[None, :] // bo), BF16)
    tiled = jnp.einsum("gro,oc->grc", a.astype(BF16), tile_cols, preferred_element_type=F32)
    return tiled.astype(BF16) * same_block


def _layer0(x, rope, p, prompt, state):
    Bg, L, D = x.shape
    M = Bg * L
    hd = D // H_B
    tm = _tile(M, 1024)
    tm_in = _tile(rope[0].shape[0], 512)
    x2 = x.reshape(M, D)
    xagq, k, v = _norm_proj(x2, p["g_mix_pre0"], p["w_in_ab"], ((0, 3, F32, None), (3, 1, F32, None), (4, 1, F32, None)),
                            tm_in, D, rope=rope, rope_q=2, rope_k=3, q_scale=hd ** -0.5)
    xagq3 = xagq.reshape(Bg, L, 3 * D)
    lru_w = (p["conv_a_w"], p["conv_a_b"], p["wa_bd"], p["lru_ba"], p["wx_bd"], p["lru_bx"], p["lam"])
    if prompt:
        ya, h_last = _rglru_prompt(xagq3, *lru_w, ts=_tile(L, 512))
        ya = ya.reshape(M, D)
        h_last = h_last.reshape(Bg, D)
        new_conv = xagq3[:, L - (CONV_W - 1):, :D]
        yb = _attn_prompt(xagq3, 2 * D // LANES, k.reshape(Bg, L, D), v.reshape(Bg, L, D),
                          sorted(DILATIONS, key=lambda wd: -wd[1]), QBLOCK).reshape(M, D)
        wb = min(MAX_WINDOW, L)
        new_k = k.reshape(Bg, L, H_B, hd)[:, L - wb:]
        new_v = v.reshape(Bg, L, H_B, hd)[:, L - wb:]
    else:
        conv0, h0, k_buf, v_buf = state
        xagq_t = xagq3.transpose(1, 0, 2)
        xa_t, ga_t = xagq_t[:, :, :D], xagq_t[:, :, D:2 * D]
        st_t = conv0.transpose(1, 0, 2)
        ya_t, h_last = _rglru_sample(xa_t, ga_t, st_t, h0, *lru_w)
        ya = ya_t.transpose(1, 0, 2).reshape(M, D)
        new_conv = jnp.concatenate([conv0, xagq3[:, :, :D]], axis=1)[:, -(CONV_W - 1):]
        wb = k_buf.shape[1]
        native = lambda a: a.reshape(Bg, -1, hd)
        yb = _attn_sample(native(xagq3[:, :, 2 * D:]), native(k), native(v), native(k_buf), native(v_buf),
                          L, H_B).reshape(M, D)
        new_k = k.reshape(Bg, L, H_B, hd)
        new_v = v.reshape(Bg, L, H_B, hd)
    x1, x1n = _out_proj([ya, yb], p["w_out_ab"], x2, p["g_mix_post0"], p["g_ffn_pre0"], tm)
    x2o = _ffn(x1, x1n, p["w_up"], p["w_dn"], 0, p["g_ffn_post0"], _tile(M, 1024), 1024)
    return x2o.reshape(Bg, L, D), (new_conv, h_last, new_k, new_v)


def _layer1(x, p, prompt, state):
    Bg, L, D = x.shape
    M = Bg * L
    x2 = x.reshape(M, D)
    tm = _tile(M, 1024)
    DC = p["w_in_c"].shape[1] // 2
    DH = DC // NH_C
    pre_w = (p["conv_c_w"], p["conv_c_b"], p["wq_bd"], p["wk_bd"], p["wv_bd"], p["wg"], p["bg"])
    if prompt:
        tm_in = _tile(L, 512)
        c = _tile(L, 256)
        *acts, tails = _mlstm_in(x2, p["g_mix_pre1"], p["w_in_c"], *pre_w, seq=L, tm=tm_in, k_scale=DH ** -0.5)
        q, k, v, xc, gz, gates = [a.reshape(Bg, L, -1) for a in acts]
        y, C, n, m = _mlstm_core(q, k, v, xc, gz, gates, p["mlstm_norm"], p["mlstm_skip"], None, c, c, NH_C)
        new_conv = tails.reshape(Bg, L // tm_in, SUBLANES, DC)[:, -1, SUBLANES - (CONV_W - 1):]
        y = y.reshape(M, DC)
    else:
        nt = DC // 1024
        xm, gz = _norm_proj(x2, p["g_mix_pre1"], p["w_in_c"], ((0, nt, F32, None), (nt, nt, BF16, _silu)),
                            _tile(M, 512), 1024)
        xm3 = xm.reshape(Bg, L, DC)
        gz3 = gz.reshape(Bg, L, DC)
        conv0, C0, n0, m0 = state
        lp = BF16_ROWS * -(-L // BF16_ROWS)
        outs = _mlstm_pre_sample(xm3.transpose(1, 0, 2), conv0.transpose(1, 0, 2), *pre_w, k_scale=DH ** -0.5)
        batch_major = lambda a: jnp.pad(a.reshape(L, Bg, -1).transpose(1, 0, 2), ((0, 0), (0, lp - L), (0, 0)))
        q, k, v, xc, gates = [batch_major(a) for a in outs]
        gz_p = jnp.pad(gz3, ((0, 0), (0, lp - L), (0, 0)))
        m0b = jnp.broadcast_to(m0[:, :, None], m0.shape + (LANES,))
        y, C, n, m = _mlstm_core(q, k, v, xc, gz_p, gates, p["mlstm_norm"], p["mlstm_skip"],
                                 (C0, n0, m0b), 2 * lp, L, NH_C)
        new_conv = jnp.concatenate([conv0, xm3], axis=1)[:, -(CONV_W - 1):]
        y = y[:, :L].reshape(M, DC)
    x1, x1n = _out_proj([y], p["w_out_c"], x2, p["g_mix_post1"], p["g_ffn_pre1"], tm)
    x2o = _ffn(x1, x1n, p["w_up"], p["w_dn"], 1, p["g_ffn_post1"], _tile(M, 1024), 1024)
    return x2o.reshape(Bg, L, D), (new_conv, C, n, m[:, :, 0])


def kernel(x_prompt, x_sample, state_rglru_conv, state_rglru_h, cache_swa_k, cache_swa_v, state_mlstm_conv, state_mlstm_C, state_mlstm_n, state_mlstm_m, norm_mix_pre, norm_mix_post, norm_ffn_pre, norm_ffn_post, w_ffn_up, w_ffn_down, w_in_ab, conv_a_w, conv_a_b, lru_wa, lru_ba, lru_wx, lru_bx, lru_lambda, w_out_ab, w_in_c, conv_c_w, conv_c_b, mlstm_wq, mlstm_wk, mlstm_wv, mlstm_w_gate, mlstm_b_gate, mlstm_norm, mlstm_skip, w_out_c):
    B, S, D = x_prompt.shape
    Bs, Ts, _ = x_sample.shape
    DA = conv_a_w.shape[-1]
    DC = conv_c_w.shape[-1]
    hd = D // H_B
    row = lambda a: a.reshape(1, -1)
    lane_pad = lambda a: jnp.pad(a, ((0, 0), (0, LANES - NH_C)))
    wg = jnp.concatenate([lane_pad(mlstm_w_gate[0][:, :NH_C]), lane_pad(mlstm_w_gate[0][:, NH_C:])], axis=1)
    bg = jnp.concatenate([lane_pad(mlstm_b_gate[:, :NH_C]), lane_pad(mlstm_b_gate[:, NH_C:])], axis=1)
    p = {
        "g_mix_pre0": row(norm_mix_pre[0]), "g_mix_post0": row(norm_mix_post[0]),
        "g_ffn_pre0": row(norm_ffn_pre[0]), "g_ffn_post0": row(norm_ffn_post[0]),
        "g_mix_pre1": row(norm_mix_pre[1]), "g_mix_post1": row(norm_mix_post[1]),
        "g_ffn_pre1": row(norm_ffn_pre[1]), "g_ffn_post1": row(norm_ffn_post[1]),
        "w_up": w_ffn_up, "w_dn": w_ffn_down,
        "w_in_ab": w_in_ab[0].astype(BF16),
        "conv_a_w": conv_a_w[0], "conv_a_b": row(conv_a_b[0]),
        "wa_bd": _regroup_block_diag(lru_wa[0], MXU_DIM), "lru_ba": row(lru_ba[0]),
        "wx_bd": _regroup_block_diag(lru_wx[0], MXU_DIM), "lru_bx": row(lru_bx[0]),
        "lam": row(lru_lambda[0]),
        "w_out_ab": w_out_ab[0],
        "w_in_c": w_in_c[0].astype(BF16),
        "conv_c_w": conv_c_w[0], "conv_c_b": row(conv_c_b[0]),
        "wq_bd": _regroup_block_diag(mlstm_wq[0], MXU_DIM),
        "wk_bd": _regroup_block_diag(mlstm_wk[0], MXU_DIM),
        "wv_bd": _regroup_block_diag(mlstm_wv[0], MXU_DIM),
        "wg": wg.astype(BF16), "bg": bg,
        "mlstm_norm": row(mlstm_norm[0]), "mlstm_skip": row(mlstm_skip[0]),
        "w_out_c": w_out_c[0],
    }
    dt = state_rglru_conv.dtype

    xp, st0 = _layer0(x_prompt, _rope_tables(np.arange(S), hd), p, True, None)
    xp, st1 = _layer1(xp, p, True, None)
    pos_s = PAST_LEN + (np.arange(Bs * Ts) % Ts)
    xs, ss0 = _layer0(x_sample, _rope_tables(pos_s, hd), p, False,
                      (state_rglru_conv[0], state_rglru_h[0], cache_swa_k[0], cache_swa_v[0]))
    xs, ss1 = _layer1(xs, p, False, (state_mlstm_conv[0], state_mlstm_C[0], state_mlstm_n[0], state_mlstm_m[0]))

    lead = lambda a: a[None].astype(dt)
    return (xp, xs,
            lead(st0[0]), lead(ss0[0]), lead(st0[1]), lead(ss0[1]),
            lead(st0[2]), lead(ss0[2]), lead(st0[3]), lead(ss0[3]),
            lead(st1[0]), lead(ss1[0]), lead(st1[1]), lead(ss1[1]),
            lead(st1[2]), lead(ss1[2]), lead(st1[3]), lead(ss1[3]))
```

```python
import functools
import math

import numpy as np
import jax
import jax.numpy as jnp
from jax import lax
from jax.experimental import pallas as pl
from jax.experimental.pallas import tpu as pltpu

F32 = jnp.float32
BF16 = jnp.bfloat16

CONV_W = 4
LRU_C = 8.0
H_B = 8
ROPE_THETA = 500000.0
DILATIONS = ((128, 1), (512, 4), (2048, 16))
MAX_WINDOW = 2048
QBLOCK = 128
NH_C = 4
PAST_LEN = 16384
EPS = 1e-6

LANES = 128
SUBLANES = 8
MXU_DIM = 256
BF16_ROWS = 16
V7X_VMEM_BYTES = 64 * 1024 * 1024
VMEM_LIMIT = V7X_VMEM_BYTES * 7 // 8

NEG_INF = float("-inf")


def _cparams(sem):
    return pltpu.CompilerParams(dimension_semantics=sem, vmem_limit_bytes=VMEM_LIMIT)


def _tile(n, pref):
    t = min(n, pref)
    while n % t:
        t -= 1
    return t


def _rms(x, g):
    return x * lax.rsqrt(jnp.mean(x * x, axis=-1, keepdims=True) + EPS) * g


def _softplus(z):
    return jnp.maximum(z, 0.0) + jnp.log1p(jnp.exp(-jnp.abs(z)))


def _gelu_tanh(x):
    c = math.sqrt(2.0 / math.pi)
    return x * (0.5 * (1.0 + jnp.tanh(c * (x + 0.044715 * (x * x * x)))))


def _silu(x):
    return x * jax.nn.sigmoid(x)


def _rope_tile(y, c, sa, sb):
    outs = []
    for h in range(y.shape[1] // LANES):
        yh = y[:, h * LANES:(h + 1) * LANES]
        outs.append(yh * c + pltpu.roll(yh, LANES - 16, 1) * sa + pltpu.roll(yh, 16, 1) * sb)
    return jnp.concatenate(outs, axis=1)


def _norm_proj_kernel(*refs, groups, tn, rope_q, rope_k, q_scale):
    n_out = len(groups)
    x_ref, g_ref, w_ref = refs[:3]
    pos = 3
    if rope_q is not None:
        c_ref, sa_ref, sb_ref = refs[3:6]
        pos = 6
    out_refs = refs[pos:pos + n_out]
    xn = _rms(x_ref[...], g_ref[...]).astype(BF16)
    for (start, count, _, act), o_ref in zip(groups, out_refs):
        for jj in range(start, start + count):
            y = jnp.dot(xn, w_ref[:, jj * tn:(jj + 1) * tn], preferred_element_type=F32)
            if rope_q is not None and jj == rope_q:
                y = _rope_tile(y, c_ref[...], sa_ref[...], sb_ref[...]) * q_scale
            elif rope_q is not None and jj == rope_k:
                y = _rope_tile(y, c_ref[...], sa_ref[...], sb_ref[...])
            if act is not None:
                y = act(y)
            o_ref[:, (jj - start) * tn:(jj - start + 1) * tn] = y.astype(o_ref.dtype)


def _norm_proj(x, g, w, groups, tm, tn, rope=None, rope_q=None, rope_k=None, q_scale=1.0):
    M, K = x.shape
    in_specs = [
        pl.BlockSpec((tm, K), lambda i: (i, 0)),
        pl.BlockSpec((1, K), lambda i: (0, 0)),
        pl.BlockSpec(w.shape, lambda i: (0, 0)),
    ]
    args = [x, g, w]
    if rope is not None:
        assert rope[0].shape[0] % tm == 0 and M % tm == 0
        pos_blocks = rope[0].shape[0] // tm
        for t in rope:
            in_specs.append(pl.BlockSpec((tm, LANES), lambda i, pb=pos_blocks: (i % pb, 0)))
            args.append(t)
    out_specs, out_shapes = [], []
    for (_, count, dtype, _) in groups:
        out_specs.append(pl.BlockSpec((tm, count * tn), lambda i: (i, 0)))
        out_shapes.append(jax.ShapeDtypeStruct((M, count * tn), dtype))
    kern = functools.partial(_norm_proj_kernel, groups=tuple(groups), tn=tn,
                             rope_q=rope_q if rope is not None else None, rope_k=rope_k, q_scale=q_scale)
    return pl.pallas_call(
        kern,
        grid=(M // tm,),
        in_specs=in_specs,
        out_specs=out_specs,
        out_shape=out_shapes,
        compiler_params=_cparams(("parallel",)),
        name="norm_proj",
    )(*args)


def _out_proj_kernel(*refs, n_in):
    a_refs = refs[:n_in]
    w_ref, x_ref, g_ref, gn_ref, o_ref, on_ref, wb_ref = refs[n_in:]

    @pl.when(pl.program_id(0) == 0)
    def _():
        wb_ref[...] = w_ref[...].astype(BF16)

    tm = x_ref.shape[0]
    half = min(tm, 2 * LANES)
    for r0 in range(0, tm, half):
        rs = slice(r0, r0 + half)
        y, row = None, 0
        for a_ref in a_refs:
            kk = a_ref.shape[1]
            d = jnp.dot(a_ref[rs, :].astype(BF16), wb_ref[row:row + kk, :], preferred_element_type=F32)
            y = d if y is None else y + d
            row += kk
        x1 = x_ref[rs, :] + _rms(y, g_ref[...])
        o_ref[rs, :] = x1
        on_ref[rs, :] = _rms(x1, gn_ref[...]).astype(on_ref.dtype)


def _out_proj(acts, w, x, g, g_next, tm):
    M, D = x.shape
    n_in = len(acts)
    assert sum(a.shape[1] for a in acts) == w.shape[0]
    row_tile = pl.BlockSpec((tm, D), lambda i: (i, 0))
    gain = pl.BlockSpec((1, D), lambda i: (0, 0))
    in_specs = [pl.BlockSpec((tm, a.shape[1]), lambda i: (i, 0)) for a in acts]
    in_specs += [pl.BlockSpec(w.shape, lambda i: (0, 0), pipeline_mode=pl.Buffered(1)), row_tile, gain, gain]
    return pl.pallas_call(
        functools.partial(_out_proj_kernel, n_in=n_in),
        grid=(M // tm,),
        in_specs=in_specs,
        out_specs=[row_tile, row_tile],
        out_shape=[jax.ShapeDtypeStruct((M, D), F32), jax.ShapeDtypeStruct((M, D), BF16)],
        scratch_shapes=[pltpu.VMEM(w.shape, BF16)],
        compiler_params=_cparams(("arbitrary",)),
        name="out_proj",
    )(*acts, w, x, g, g_next)


def _ffn_kernel(x_ref, xn_ref, wu_ref, wd_ref, g2_ref, o_ref, acc_ref):
    f = pl.program_id(1)

    @pl.when(f == 0)
    def _():
        acc_ref[...] = jnp.zeros_like(acc_ref)

    hid = jnp.dot(xn_ref[...], wu_ref[...].astype(BF16), preferred_element_type=F32)
    hid = jnp.square(jnp.maximum(hid, 0.0))
    acc_ref[...] += jnp.dot(hid.astype(BF16), wd_ref[...].astype(BF16), preferred_element_type=F32)

    @pl.when(f == pl.num_programs(1) - 1)
    def _():
        o_ref[...] = x_ref[...] + _rms(acc_ref[...], g2_ref[...])


def _ffn(x, xn, wu, wd, layer, g2, tm, tf):
    M, D = x.shape
    FF = wu.shape[2]
    return pl.pallas_call(
        _ffn_kernel,
        grid=(M // tm, FF // tf),
        in_specs=[
            pl.BlockSpec((tm, D), lambda i, f: (i, 0)),
            pl.BlockSpec((tm, D), lambda i, f: (i, 0)),
            pl.BlockSpec((None, D, tf), lambda i, f: (layer, 0, f)),
            pl.BlockSpec((None, tf, D), lambda i, f: (layer, f, 0)),
            pl.BlockSpec((1, D), lambda i, f: (0, 0)),
        ],
        out_specs=pl.BlockSpec((tm, D), lambda i, f: (i, 0)),
        out_shape=jax.ShapeDtypeStruct((M, D), F32),
        scratch_shapes=[pltpu.VMEM((tm, D), F32)],
        compiler_params=_cparams(("parallel", "arbitrary")),
        name="ffn",
    )(x, xn, wu, wd, g2)


def _lru_gates(xc, wa_ref, ba, wx_ref, bx, lam):
    xcb = xc.astype(BF16)
    ng = wa_ref.shape[0]
    gw = wa_ref.shape[1]
    ra, ri = [], []
    for g in range(ng):
        xs = xcb[:, g * gw:(g + 1) * gw]
        ra.append(jnp.dot(xs, wa_ref[g], preferred_element_type=F32))
        ri.append(jnp.dot(xs, wx_ref[g], preferred_element_type=F32))
    r = jax.nn.sigmoid(jnp.concatenate(ra, axis=1) + ba)
    i = jax.nn.sigmoid(jnp.concatenate(ri, axis=1) + bx)
    log_a = -LRU_C * r * _softplus(-lam)
    a = jnp.exp(log_a)
    om = 1.0 - a * a
    b = jnp.where(om > 0.0, om * lax.rsqrt(om), 0.0) * (i * xc)
    return a, b


def _conv4_rows(xe, halo, w, b):
    a = w[0:1, :] * xe
    for j in range(1, CONV_W):
        a = pltpu.roll(a, 1, 0) + w[j:j + 1, :] * xe
    return b + a[halo:]


def _conv4(tail_ref, x, first, w_ref, b):
    @pl.when(first)
    def _():
        tail_ref[...] = jnp.zeros_like(tail_ref)

    xe = jnp.concatenate([tail_ref[...], x], axis=0)
    tail_ref[...] = x[x.shape[0] - SUBLANES:, :]
    return _conv4_rows(xe, SUBLANES, w_ref[...], b)


def _rglru_prompt_kernel(xa_ref, ga_ref, cw_ref, cb_ref, wa_ref, ba_ref, wx_ref, bx_ref, lam_ref,
                         ya_ref, hl_ref, tail_ref, hc_ref, a_s, b_s, h_s):
    t = pl.program_id(1)
    T = xa_ref.shape[0]

    @pl.when(t == 0)
    def _():
        hc_ref[...] = jnp.zeros_like(hc_ref)

    xc = _conv4(tail_ref, xa_ref[...], t == 0, cw_ref, cb_ref[...])
    a, b = _lru_gates(xc, wa_ref, ba_ref[...], wx_ref, bx_ref[...], lam_ref[...])
    a_s[...] = a
    b_s[...] = b

    def body(s, h):
        h = a_s[pl.ds(s, 1), :] * h + b_s[pl.ds(s, 1), :]
        h_s[pl.ds(s, 1), :] = h
        return h

    h = lax.fori_loop(0, T, body, hc_ref[...], unroll=8)
    hc_ref[...] = h
    ya_ref[...] = (h_s[...] * _gelu_tanh(ga_ref[...])).astype(ya_ref.dtype)

    @pl.when(t == pl.num_programs(1) - 1)
    def _():
        hl_ref[...] = h


def _rglru_prompt(xaga, cw, cb, wa_bd, ba, wx_bd, bx, lam, ts):
    B, S, _ = xaga.shape
    C = cw.shape[-1]
    wspec = lambda a: pl.BlockSpec(a.shape, lambda b, t: (0,) * a.ndim)
    return pl.pallas_call(
        _rglru_prompt_kernel,
        grid=(B, S // ts),
        in_specs=[
            pl.BlockSpec((None, ts, C), lambda b, t: (b, t, 0)),
            pl.BlockSpec((None, ts, C), lambda b, t: (b, t, 1)),
            wspec(cw), wspec(cb), wspec(wa_bd), wspec(ba), wspec(wx_bd), wspec(bx), wspec(lam),
        ],
        out_specs=[
            pl.BlockSpec((None, ts, C), lambda b, t: (b, t, 0)),
            pl.BlockSpec((None, 1, C), lambda b, t: (b, 0, 0)),
        ],
        out_shape=[jax.ShapeDtypeStruct((B, S, C), BF16), jax.ShapeDtypeStruct((B, 1, C), F32)],
        scratch_shapes=[pltpu.VMEM((SUBLANES, C), F32), pltpu.VMEM((1, C), F32),
                        pltpu.VMEM((ts, C), F32), pltpu.VMEM((ts, C), F32), pltpu.VMEM((ts, C), F32)],
        compiler_params=_cparams(("parallel", "arbitrary")),
        name="rglru_prompt",
    )(xaga, xaga, cw, cb, wa_bd, ba, wx_bd, bx, lam)


def _rglru_sample_kernel(xa_ref, ga_ref, st_ref, h0_ref, cw_ref, cb_ref, wa_ref, ba_ref, wx_ref, bx_ref,
                         lam_ref, ya_ref, hl_ref):
    T = xa_ref.shape[0]
    rows = [st_ref[j] for j in range(CONV_W - 1)] + [xa_ref[s] for s in range(T)]
    h = h0_ref[...]
    for s in range(T):
        y = cb_ref[...]
        for j in range(CONV_W):
            y = y + cw_ref[j:j + 1, :] * rows[s + j]
        a, b = _lru_gates(y, wa_ref, ba_ref[...], wx_ref, bx_ref[...], lam_ref[...])
        h = a * h + b
        ya_ref[s] = (h * _gelu_tanh(ga_ref[s])).astype(ya_ref.dtype)
    hl_ref[...] = h


def _rglru_sample(xa_t, ga_t, st_t, h0, cw, cb, wa_bd, ba, wx_bd, bx, lam):
    T, B, C = xa_t.shape
    args = (xa_t, ga_t, st_t, h0, cw, cb, wa_bd, ba, wx_bd, bx, lam)
    return pl.pallas_call(
        _rglru_sample_kernel,
        grid=(1,),
        in_specs=[pl.BlockSpec(a.shape, lambda i, n=a.ndim: (0,) * n) for a in args],
        out_specs=[pl.BlockSpec((T, B, C), lambda i: (0, 0, 0)), pl.BlockSpec((B, C), lambda i: (0, 0))],
        out_shape=[jax.ShapeDtypeStruct((T, B, C), F32), jax.ShapeDtypeStruct((B, C), F32)],
        compiler_params=_cparams(("arbitrary",)),
        name="rglru_sample",
    )(*args)


def _stat_pitch(d):
    return d + d // 2 if d % (2 * SUBLANES) == 0 else d


def _attn_prompt_kernel(q_ref, k_ref, v_ref, o_ref, *scratch, groups, qb):
    S = q_ref.shape[0]
    E = q_ref.shape[1]
    dist2 = (lax.broadcasted_iota(jnp.int32, (qb, 2 * qb), 0) + qb
             - lax.broadcasted_iota(jnp.int32, (qb, 2 * qb), 1))
    dist1 = lax.broadcasted_iota(jnp.int32, (qb, qb), 0) - lax.broadcasted_iota(jnp.int32, (qb, qb), 1)

    def rows(start, d):
        return pl.ds(start, qb) if d == 1 else pl.ds(start, qb, stride=d)

    ng = len(groups)
    s_scr, p_scr = scratch[3 * ng], scratch[3 * ng + 1]
    dn_t = (((1,), (1,)), ((), ()))

    for gi, (w, d) in enumerate(groups):
        back = w // d
        nb = (S // d) // qb
        nblk = nb * d
        two = nb > 1
        kw = 2 * qb if two else qb
        dist = dist2 if two else dist1
        lim_all = jnp.full((qb, kw), back, jnp.int32)
        lim_first = jnp.minimum(lax.broadcasted_iota(jnp.int32, (qb, kw), 0), back)
        m_s, l_s, acc_s = scratch[3 * gi:3 * gi + 3]

        def locate(idx, d=d):
            if d == 1:
                n = idx
                start = pl.multiple_of(idx * qb, qb)
                prev = pl.multiple_of(jnp.maximum(idx - 1, 0) * qb, qb)
            else:
                n = idx // d
                start = n * (qb * d) + idx % d
                prev = jnp.where(n > 0, start - qb * d, start)
            return start, prev, n

        def stat_rows(idx, d=d, pitch=_stat_pitch(d)):
            if d == 1:
                return pl.ds(pl.multiple_of(idx * qb, qb), qb)
            return pl.ds((idx // d) * (qb * pitch) + idx % d, qb, stride=pitch)

        def gather(ref, start, prev, d=d, two=two):
            if two:
                return jnp.concatenate([ref[rows(prev, d), :], ref[rows(start, d), :]], axis=0).astype(BF16)
            return ref[rows(start, d), :].astype(BF16)

        def scores(idx, c, kw=kw, locate=locate, gather=gather, d=d):
            start, prev, _ = locate(idx)
            q = q_ref[rows(start, d), :].astype(BF16)
            s_scr[pl.ds(pl.multiple_of(idx * qb, qb), qb), 0:kw] = lax.dot_general(
                q, gather(k_ref, start, prev), dn_t, preferred_element_type=F32)
            return c

        def softmax(idx, c, kw=kw, locate=locate, stat_rows=stat_rows, two=two, dist=dist, lim_all=lim_all,
                    lim_first=lim_first, m_s=m_s):
            _, _, n = locate(idx)
            blk = pl.ds(pl.multiple_of(idx * qb, qb), qb)
            lim = jnp.where(n > 0, lim_all, lim_first) if two else lim_all
            valid = (dist >= 0) & (dist <= lim)
            s = jnp.where(valid, s_scr[blk, 0:kw], NEG_INF)
            m_b = jnp.max(s, axis=-1, keepdims=True)
            p_scr[blk, 0:kw] = jnp.exp(s - m_b).astype(BF16)
            m_s[stat_rows(idx), :] = jnp.broadcast_to(m_b, (qb, E))
            return c

        def values(idx, c, kw=kw, locate=locate, gather=gather, stat_rows=stat_rows, acc_s=acc_s, l_s=l_s):
            start, prev, _ = locate(idx)
            p = p_scr[pl.ds(pl.multiple_of(idx * qb, qb), qb), 0:kw]
            v1 = jnp.concatenate([gather(v_ref, start, prev), jnp.ones((kw, E), BF16)], axis=1)
            r = jnp.dot(p, v1, preferred_element_type=F32)
            acc_s[stat_rows(idx), :] = r[:, :E]
            l_s[stat_rows(idx), :] = r[:, E:]
            return c

        lax.fori_loop(0, nblk, scores, 0, unroll=True)
        lax.fori_loop(0, nblk, softmax, 0, unroll=True)
        lax.fori_loop(0, nblk, values, 0, unroll=True)

    ng = len(groups)

    def merge(i, c):
        rs = pl.ds(pl.multiple_of(i * qb, qb), qb)

        def stats(ref, g):
            d = groups[g][1]
            pitch = _stat_pitch(d)
            if pitch == d:
                return ref[rs, :]
            runs = [ref[pl.ds(pl.multiple_of((i * (qb // d) + j) * pitch, SUBLANES), d), :]
                    for j in range(qb // d)]
            return jnp.concatenate(runs, axis=0)

        ms = [stats(scratch[3 * g], g) for g in range(ng)]
        m = functools.reduce(jnp.maximum, ms)
        num = den = None
        for g in range(ng):
            wg = jnp.exp(ms[g] - m)
            n_g = wg * stats(scratch[3 * g + 2], g)
            d_g = wg * stats(scratch[3 * g + 1], g)
            num = n_g if num is None else num + n_g
            den = d_g if den is None else den + d_g
        o_ref[rs, :] = (num / den).astype(o_ref.dtype)
        return c

    lax.fori_loop(0, S // qb, merge, 0, unroll=2)


def _attn_prompt(q, q_off, k, v, groups, qb):
    B, S, D = k.shape
    nh = D // LANES
    for (w, d) in groups:
        assert S % (d * qb) == 0 and w // d <= qb
    spec = pl.BlockSpec((None, S, LANES), lambda b, h: (b, 0, h))
    q_spec = pl.BlockSpec((None, S, LANES), lambda b, h: (b, 0, q_off + h))
    return pl.pallas_call(
        functools.partial(_attn_prompt_kernel, groups=tuple(groups), qb=qb),
        grid=(B, nh),
        in_specs=[q_spec, spec, spec],
        out_specs=spec,
        out_shape=jax.ShapeDtypeStruct((B, S, D), BF16),
        scratch_shapes=[pltpu.VMEM((S // d * _stat_pitch(d), LANES), F32) for (_, d) in groups for _ in range(3)]
        + [pltpu.VMEM((S, 2 * qb), F32), pltpu.VMEM((S, 2 * qb), BF16)],
        compiler_params=_cparams(("parallel", "parallel")),
        name="attn_prompt",
    )(q, k, v)


def _attn_sample_kernel(q_ref, kn_ref, vn_ref, kt_ref, vt_ref, ks_ref, vs_ref, mt_ref, ms_ref, mn_ref, o_ref):
    nq, hd = q_ref.shape
    qb = jnp.concatenate([q_ref[...], jnp.zeros((LANES - nq, hd), F32)], axis=0).astype(BF16)
    nn = kn_ref.shape[0]
    zn = jnp.zeros((LANES - nn, hd), F32)
    key_sets = (
        (kt_ref[...], vt_ref[...], mt_ref[...]),
        (ks_ref[...].reshape(-1, hd), vs_ref[...].reshape(-1, hd), ms_ref[...]),
        (jnp.concatenate([kn_ref[...], zn], axis=0), jnp.concatenate([vn_ref[...], zn], axis=0), mn_ref[...]),
    )
    partial = []
    for k, v, bias in key_sets:
        s = lax.dot_general(k.astype(BF16), qb, (((1,), (1,)), ((), ())), preferred_element_type=F32) + bias
        m_set = jnp.max(s, axis=0, keepdims=True)
        p = jnp.exp(s - jnp.where(m_set == NEG_INF, 0.0, m_set)).astype(BF16)
        v1 = jnp.concatenate([v.astype(BF16), jnp.ones(v.shape, BF16)], axis=1)
        partial.append((m_set, lax.dot_general(p, v1, (((0,), (0,)), ((), ())), preferred_element_type=F32)))
    m = functools.reduce(jnp.maximum, [m_set for m_set, _ in partial])
    m = jnp.where(m == NEG_INF, 0.0, m)
    acc = None
    for m_set, r in partial:
        w = jnp.broadcast_to(jnp.exp(m_set - m), (LANES, LANES)).T
        r = jnp.concatenate([w, w], axis=1) * r
        acc = r if acc is None else acc + r
    o_ref[...] = acc[:nq, :hd] / acc[:nq, hd:]


def _attn_sample(q, kn, vn, kc, vc, T, nh):
    B, nq, hd = q.shape
    wb = kc.shape[1] // nh
    far = max(DILATIONS, key=lambda wd: wd[0])
    near = [g for g in DILATIONS if g != far]
    R = max(w for w, _ in near)
    d_far = far[1]
    assert near and wb % R == 0 and wb % d_far == 0 and T <= d_far and nq == T * nh and nq <= LANES
    A = wb // d_far

    def mult(delta, groups):
        m = np.zeros(delta.shape, np.float32)
        for (w, d) in groups:
            m = m + ((delta >= 0) & (delta % d == 0) & (delta <= (w // d) * d))
        return m

    def table(pos_of_row, head_of_row, groups):
        lane = np.arange(LANES)
        t, hq = lane // nh, lane % nh
        delta = (wb + t)[None, :] - pos_of_row[:, None]
        tab = mult(delta, groups) * (head_of_row[:, None] == hq[None, :]) * (t < T)[None, :]
        return jnp.asarray(np.where(tab > 0, np.log(np.maximum(tab, 1.0)), NEG_INF), F32)

    rt = np.arange(R * nh)
    mt = table(wb - R + rt // nh, rt % nh, near)
    rs = np.arange(A * T * nh)
    ms = table((rs // (T * nh)) * d_far + (rs % (T * nh)) // nh, rs % nh, [far])
    rn = np.arange(LANES)
    mn = table(np.where(rn < nq, wb + rn // nh, -10 ** 9), rn % nh, DILATIONS)

    small = pl.BlockSpec((None, nq, hd), lambda b: (b, 0, 0))
    tail = pl.BlockSpec((None, R * nh, hd), lambda b: (b, wb // R - 1, 0))
    strided = pl.BlockSpec((None, A, T * nh, hd), lambda b: (b, 0, 0, 0))
    const = lambda a: pl.BlockSpec(a.shape, lambda b: (0, 0))
    kc4 = kc.reshape(B, A, d_far * nh, hd)
    vc4 = vc.reshape(B, A, d_far * nh, hd)
    return pl.pallas_call(
        _attn_sample_kernel,
        grid=(B,),
        in_specs=[small, small, small, tail, tail, strided, strided, const(mt), const(ms), const(mn)],
        out_specs=small,
        out_shape=jax.ShapeDtypeStruct((B, nq, hd), F32),
        compiler_params=_cparams(("parallel",)),
        name="attn_sample",
    )(q, kn, vn, kc, vc, kc4, vc4, mt, ms, mn)


def _mlstm_qkv_gates(xc, xm, wq_ref, wk_ref, wv_ref, wg_ref, bg_ref, q_ref, k_ref, v_ref, xc_ref, g_ref, k_scale,
                     rs=slice(None)):
    xcb = xc.astype(BF16)
    xmb = xm.astype(BF16)
    ng, gw = wq_ref.shape[0], wq_ref.shape[1]
    qs, ks, vs = [], [], []
    for g in range(ng):
        sl = slice(g * gw, (g + 1) * gw)
        qs.append(jnp.dot(xcb[:, sl], wq_ref[g], preferred_element_type=F32))
        ks.append(jnp.dot(xcb[:, sl], wk_ref[g], preferred_element_type=F32))
        vs.append(jnp.dot(xmb[:, sl], wv_ref[g], preferred_element_type=F32))
    k = jnp.concatenate(ks, axis=1)
    qb = jnp.concatenate(qs, axis=1).astype(BF16)
    vb = jnp.concatenate(vs, axis=1).astype(BF16)
    qkv = jnp.concatenate([qb, k.astype(BF16), vb], axis=1)
    g_ref[rs, :] = jnp.dot(qkv, wg_ref[...], preferred_element_type=F32) + bg_ref[...]
    q_ref[rs, :] = qb
    k_ref[rs, :] = (k * k_scale).astype(k_ref.dtype)
    v_ref[rs, :] = vb
    xc_ref[rs, :] = xcb


def _mlstm_pre_sample_kernel(xm_ref, st_ref, cw_ref, cb_ref, wq_ref, wk_ref, wv_ref, wg_ref, bg_ref,
                             q_ref, k_ref, v_ref, xc_ref, g_ref, *, k_scale):
    T = xm_ref.shape[0]
    rows = [st_ref[j] for j in range(CONV_W - 1)] + [xm_ref[s] for s in range(T)]
    conv = []
    for s in range(T):
        y = cb_ref[...]
        for j in range(CONV_W):
            y = y + cw_ref[j:j + 1, :] * rows[s + j]
        conv.append(y)
    xc = _silu(jnp.concatenate(conv, axis=0))
    xm = jnp.concatenate(rows[CONV_W - 1:], axis=0)
    _mlstm_qkv_gates(xc, xm, wq_ref, wk_ref, wv_ref, wg_ref, bg_ref, q_ref, k_ref, v_ref, xc_ref, g_ref, k_scale)


def _mlstm_pre_sample(xm_t, st_t, cw, cb, wq_bd, wk_bd, wv_bd, wg, bg, k_scale):
    T, B, C = xm_t.shape
    args = (xm_t, st_t, cw, cb, wq_bd, wk_bd, wv_bd, wg, bg)
    full = lambda shape: pl.BlockSpec(shape, lambda i: (0,) * len(shape))
    return pl.pallas_call(
        functools.partial(_mlstm_pre_sample_kernel, k_scale=k_scale),
        grid=(1,),
        in_specs=[full(a.shape) for a in args],
        out_specs=[full((T * B, C))] * 4 + [full((T * B, 2 * LANES))],
        out_shape=[jax.ShapeDtypeStruct((T * B, C), BF16)] * 4 + [jax.ShapeDtypeStruct((T * B, 2 * LANES), F32)],
        compiler_params=_cparams(("arbitrary",)),
        name="mlstm_pre_sample",
    )(*args)


CONV_HALO = BF16_ROWS


def _mlstm_in_kernel(x_ref, xh_ref, g_ref, w_ref, cw_ref, cb_ref, wq_ref, wk_ref, wv_ref, wg_ref, bg_ref,
                     q_ref, k_ref, v_ref, xc_ref, gz_ref, gt_ref, tail_ref, *, tiles_per_seq, tn, k_scale):
    tm = x_ref.shape[0]
    C = cw_ref.shape[1]
    first = (pl.program_id(0) % tiles_per_seq) == 0
    g = g_ref[...]
    xn = _rms(x_ref[...], g)
    xe = jnp.concatenate([_rms(xh_ref[...], g), xn], axis=0).astype(BF16)
    xn = xn.astype(BF16)
    dead_halo = lax.broadcasted_iota(jnp.int32, (CONV_HALO + tm, tn), 0) < jnp.where(first, CONV_HALO, 0)
    xmes = []
    for jj in range(C // tn):
        cs = slice(jj * tn, (jj + 1) * tn)
        xm_e = jnp.dot(xe, w_ref[:, cs], preferred_element_type=F32)
        xmes.append(jnp.where(dead_halo, 0.0, xm_e))
        gz_ref[:, cs] = _silu(jnp.dot(xn, w_ref[:, C + jj * tn:C + (jj + 1) * tn],
                                      preferred_element_type=F32)).astype(gz_ref.dtype)
    half = LANES
    for r0 in range(0, tm, half):
        xc = jnp.concatenate(
            [_silu(_conv4_rows(xm_e[r0:r0 + CONV_HALO + half], CONV_HALO, cw_ref[:, jj * tn:(jj + 1) * tn],
                               cb_ref[:, jj * tn:(jj + 1) * tn])) for jj, xm_e in enumerate(xmes)], axis=1)
        xm = jnp.concatenate([xm_e[CONV_HALO + r0:CONV_HALO + r0 + half] for xm_e in xmes], axis=1)
        _mlstm_qkv_gates(xc, xm, wq_ref, wk_ref, wv_ref, wg_ref, bg_ref, q_ref, k_ref, v_ref, xc_ref, gt_ref,
                         k_scale, rs=slice(r0, r0 + half))
    tail_ref[...] = jnp.concatenate([xm_e[CONV_HALO + tm - SUBLANES:] for xm_e in xmes], axis=1)


def _mlstm_in(x, g, w, cw, cb, wq_bd, wk_bd, wv_bd, wg, bg, seq, tm, k_scale):
    M, K = x.shape
    C = cw.shape[1]
    assert seq % tm == 0 and tm % CONV_HALO == 0
    hb = tm // CONV_HALO
    const = lambda a: pl.BlockSpec(a.shape, lambda i: (0,) * a.ndim, pipeline_mode=pl.Buffered(1))
    act = pl.BlockSpec((tm, C), lambda i: (i, 0))
    return pl.pallas_call(
        functools.partial(_mlstm_in_kernel, tiles_per_seq=seq // tm, tn=1024, k_scale=k_scale),
        grid=(M // tm,),
        in_specs=[pl.BlockSpec((tm, K), lambda i: (i, 0)),
                  pl.BlockSpec((CONV_HALO, K), lambda i: (jnp.maximum(i * hb - 1, 0), 0)),
                  const(g), const(w), const(cw), const(cb), const(wq_bd), const(wk_bd), const(wv_bd),
                  const(wg), const(bg)],
        out_specs=[act, act, act, act, act, pl.BlockSpec((tm, 2 * LANES), lambda i: (i, 0)),
                   pl.BlockSpec((None, SUBLANES, C), lambda i: (i, 0, 0))],
        out_shape=[jax.ShapeDtypeStruct((M, C), BF16)] * 5 + [jax.ShapeDtypeStruct((M, 2 * LANES), F32),
                                                              jax.ShapeDtypeStruct((M // tm, SUBLANES, C), F32)],
        compiler_params=_cparams(("parallel",)),
        name="mlstm_in",
    )(x, x, g, w, cw, cb, wq_bd, wk_bd, wv_bd, wg, bg)


def _mlstm_core_kernel(*refs, c, n_valid, nh, zero_init):
    q_ref, k_ref, v_ref, xc_ref, gz_ref, g_ref, nw_ref, sk_ref = refs[:8]
    pos = 8
    if not zero_init:
        c0_ref, n0_ref, m0_ref = refs[8:11]
        pos = 11
    y_ref, C_ref, n_ref, m_ref = refs[pos:pos + 4]
    ch = pl.program_id(1)
    c_in = q_ref.shape[0]
    DH = q_ref.shape[1] // nh

    @pl.when(ch == 0)
    def _():
        if zero_init:
            C_ref[...] = jnp.zeros_like(C_ref)
            n_ref[...] = jnp.zeros_like(n_ref)
            m_ref[...] = jnp.zeros_like(m_ref)
        else:
            C_ref[...] = c0_ref[...]
            n_ref[...] = n0_ref[...]
            m_ref[...] = m0_ref[...]

    def padded(x):
        if c_in == c:
            return x
        return jnp.concatenate([x, jnp.zeros((c - c_in, x.shape[1]), x.dtype)], axis=0)

    g = padded(g_ref[...])
    row = lax.broadcasted_iota(jnp.int32, (c, LANES), 0)
    valid = row < n_valid
    gi = jnp.where(valid, g[:, :LANES], NEG_INF)
    lf = jnp.where(valid, -_softplus(-g[:, LANES:]), 0.0)
    tri_f = (lax.broadcasted_iota(jnp.int32, (c, c), 1) <= lax.broadcasted_iota(jnp.int32, (c, c), 0))
    tri_b = tri_f.astype(BF16)
    bcum, rest = None, lf
    for _ in range(3):
        part = rest.astype(BF16)
        rest = rest - part.astype(F32)
        d = jnp.dot(tri_b, part, preferred_element_type=F32)
        bcum = d if bcum is None else bcum + d
    bcum_t = bcum.T
    gi_t = gi.T
    causal = tri_f

    for h in range(nh):
        sl = slice(h * DH, (h + 1) * DH)
        qh = padded(q_ref[:, sl])
        kh = padded(k_ref[:, sl])
        vh = padded(v_ref[:, sl])
        bc_col = bcum[:, h:h + 1]
        bc_row = bcum_t[h:h + 1, :]
        i_row = gi_t[h:h + 1, :]
        i_col = gi[:, h:h + 1]
        m_prev = m_ref[h:h + 1, 0:1]
        log_intra = jnp.where(causal, bc_col - bc_row + i_row, NEG_INF)
        log_inter = bc_col + m_prev
        m_t = jnp.maximum(log_inter, jnp.max(log_intra, axis=1, keepdims=True))
        qk = lax.dot_general(qh, kh, (((1,), (1,)), ((), ())), preferred_element_type=F32)
        w_intra = jnp.exp(log_intra - m_t) * qk
        w_inter = jnp.exp(log_inter - m_t)
        C_old = C_ref[h]
        n_old = n_ref[h:h + 1, :]
        num = jnp.dot(w_intra.astype(BF16), vh, preferred_element_type=F32)
        num = num + w_inter * jnp.dot(qh, C_old.astype(BF16), preferred_element_type=F32)
        n8 = jnp.broadcast_to(n_old, (SUBLANES, DH)).astype(BF16)
        qn = lax.dot_general(qh, n8, (((1,), (1,)), ((), ())), preferred_element_type=F32)[:, 0:1]
        den = jnp.sum(w_intra, axis=1, keepdims=True) + w_inter * qn
        hh = num * (1.0 / jnp.maximum(jnp.abs(den), jnp.exp(-m_t)))
        m_new = m_t[c - 1:c, :]
        bc_last = bc_col[c - 1:c, :]
        w_state = jnp.exp(bc_last - bc_col + i_col - m_new)
        decay = jnp.exp(bc_last + m_prev - m_new)
        kw = kh.astype(F32) * w_state
        C_ref[h] = decay * C_old + lax.dot_general(
            kw.astype(BF16), vh, (((0,), (0,)), ((), ())), preferred_element_type=F32)
        n_ref[h:h + 1, :] = decay * n_old + jnp.sum(kw, axis=0, keepdims=True)
        m_ref[h:h + 1, :] = jnp.broadcast_to(m_new, (1, LANES))
        hn = hh * lax.rsqrt(jnp.mean(hh * hh, axis=-1, keepdims=True) + EPS)
        hn = hn[:c_in] * nw_ref[:, sl]
        y = (hn + sk_ref[:, sl] * xc_ref[:, sl].astype(F32)) * gz_ref[:, sl].astype(F32)
        y_ref[:, sl] = y.astype(y_ref.dtype)


def _mlstm_core(q, k, v, xc, gz, gates, nw, sk, state, c, n_valid, nh):
    B, S, C = q.shape
    c_in = min(c, S)
    DH = C // nh
    zero_init = state is None
    act = pl.BlockSpec((None, c_in, C), lambda b, t: (b, t, 0))
    in_specs = [act, act, act, act, act,
                pl.BlockSpec((None, c_in, 2 * LANES), lambda b, t: (b, t, 0)),
                pl.BlockSpec((1, C), lambda b, t: (0, 0)), pl.BlockSpec((1, C), lambda b, t: (0, 0))]
    args = [q, k, v, xc, gz, gates, nw, sk]
    C_spec = pl.BlockSpec((None, nh, DH, DH), lambda b, t: (b, 0, 0, 0))
    n_spec = pl.BlockSpec((None, nh, DH), lambda b, t: (b, 0, 0))
    m_spec = pl.BlockSpec((None, nh, LANES), lambda b, t: (b, 0, 0))
    if not zero_init:
        in_specs += [C_spec, n_spec, m_spec]
        args += list(state)
    return pl.pallas_call(
        functools.partial(_mlstm_core_kernel, c=c, n_valid=n_valid, nh=nh, zero_init=zero_init),
        grid=(B, S // c_in),
        in_specs=in_specs,
        out_specs=[act, C_spec, n_spec, m_spec],
        out_shape=[jax.ShapeDtypeStruct((B, S, C), BF16),
                   jax.ShapeDtypeStruct((B, nh, DH, DH), F32),
                   jax.ShapeDtypeStruct((B, nh, DH), F32),
                   jax.ShapeDtypeStruct((B, nh, LANES), F32)],
        compiler_params=_cparams(("parallel", "arbitrary")),
        name="mlstm_core",
    )(*args)


def _rope_tables(pos, hd):
    rot = hd // 4
    half = rot // 2
    inv = ROPE_THETA ** (-np.arange(0, rot, 2, dtype=np.float64) / rot)
    ang = pos.astype(np.float64)[:, None] * inv[None, :]
    cos, sin = np.cos(ang), np.sin(ang)
    n = pos.shape[0]
    c = np.concatenate([cos, cos, np.ones((n, hd - rot))], axis=1)
    sa = np.concatenate([-sin, np.zeros((n, hd - half))], axis=1)
    sb = np.concatenate([np.zeros((n, half)), sin, np.zeros((n, hd - rot))], axis=1)
    return tuple(jnp.asarray(t, F32) for t in (c, sa, sb))


def _regroup_block_diag(w, group):
    nb, bi, bo = w.shape
    per = group // bi
    a = w.reshape(nb // per, per * bi, bo)
    cols = np.arange(per * bo)
    tile_cols = jnp.asarray(cols[None, :] % bo == np.arange(bo)[:, None], BF16)
    same_block = jnp.asarray((np.arange(per * bi)[:, None] // bi) == (cols[None, :] // bo), BF16)
    tiled = jnp.einsum("gro,oc->grc", a.astype(BF16), tile_cols, preferred_element_type=F32)
    return tiled.astype(BF16) * same_block


def _layer0(x, rope, p, prompt, state):
    Bg, L, D = x.shape
    M = Bg * L
    hd = D // H_B
    tm = _tile(M, 1024)
    tm_in = _tile(rope[0].shape[0], 512)
    x2 = x.reshape(M, D)
    xagq, k, v = _norm_proj(x2, p["g_mix_pre0"], p["w_in_ab"], ((0, 3, F32, None), (3, 1, F32, None), (4, 1, F32, None)),
                            tm_in, D, rope=rope, rope_q=2, rope_k=3, q_scale=hd ** -0.5)
    xagq3 = xagq.reshape(Bg, L, 3 * D)
    lru_w = (p["conv_a_w"], p["conv_a_b"], p["wa_bd"], p["lru_ba"], p["wx_bd"], p["lru_bx"], p["lam"])
    if prompt:
        ya, h_last = _rglru_prompt(xagq3, *lru_w, ts=_tile(L, 512))
        ya = ya.reshape(M, D)
        h_last = h_last.reshape(Bg, D)
        new_conv = xagq3[:, L - (CONV_W - 1):, :D]
        yb = _attn_prompt(xagq3, 2 * D // LANES, k.reshape(Bg, L, D), v.reshape(Bg, L, D),
                          sorted(DILATIONS, key=lambda wd: -wd[1]), QBLOCK).reshape(M, D)
        wb = min(MAX_WINDOW, L)
        new_k = k.reshape(Bg, L, H_B, hd)[:, L - wb:]
        new_v = v.reshape(Bg, L, H_B, hd)[:, L - wb:]
    else:
        conv0, h0, k_buf, v_buf = state
        xagq_t = xagq3.transpose(1, 0, 2)
        xa_t, ga_t = xagq_t[:, :, :D], xagq_t[:, :, D:2 * D]
        st_t = conv0.transpose(1, 0, 2)
        ya_t, h_last = _rglru_sample(xa_t, ga_t, st_t, h0, *lru_w)
        ya = ya_t.transpose(1, 0, 2).reshape(M, D)
        new_conv = jnp.concatenate([conv0, xagq3[:, :, :D]], axis=1)[:, -(CONV_W - 1):]
        wb = k_buf.shape[1]
        native = lambda a: a.reshape(Bg, -1, hd)
        yb = _attn_sample(native(xagq3[:, :, 2 * D:]), native(k), native(v), native(k_buf), native(v_buf),
                          L, H_B).reshape(M, D)
        new_k = k.reshape(Bg, L, H_B, hd)
        new_v = v.reshape(Bg, L, H_B, hd)
    x1, x1n = _out_proj([ya, yb], p["w_out_ab"], x2, p["g_mix_post0"], p["g_ffn_pre0"], tm)
    x2o = _ffn(x1, x1n, p["w_up"], p["w_dn"], 0, p["g_ffn_post0"], _tile(M, 1024), 1024)
    return x2o.reshape(Bg, L, D), (new_conv, h_last, new_k, new_v)


def _layer1(x, p, prompt, state):
    Bg, L, D = x.shape
    M = Bg * L
    x2 = x.reshape(M, D)
    tm = _tile(M, 1024)
    DC = p["w_in_c"].shape[1] // 2
    DH = DC // NH_C
    pre_w = (p["conv_c_w"], p["conv_c_b"], p["wq_bd"], p["wk_bd"], p["wv_bd"], p["wg"], p["bg"])
    if prompt:
        tm_in = _tile(L, 512)
        c = _tile(L, 256)
        *acts, tails = _mlstm_in(x2, p["g_mix_pre1"], p["w_in_c"], *pre_w, seq=L, tm=tm_in, k_scale=DH ** -0.5)
        q, k, v, xc, gz, gates = [a.reshape(Bg, L, -1) for a in acts]
        y, C, n, m = _mlstm_core(q, k, v, xc, gz, gates, p["mlstm_norm"], p["mlstm_skip"], None, c, c, NH_C)
        new_conv = tails.reshape(Bg, L // tm_in, SUBLANES, DC)[:, -1, SUBLANES - (CONV_W - 1):]
        y = y.reshape(M, DC)
    else:
        nt = DC // 1024
        xm, gz = _norm_proj(x2, p["g_mix_pre1"], p["w_in_c"], ((0, nt, F32, None), (nt, nt, BF16, _silu)),
                            _tile(M, 512), 1024)
        xm3 = xm.reshape(Bg, L, DC)
        gz3 = gz.reshape(Bg, L, DC)
        conv0, C0, n0, m0 = state
        lp = BF16_ROWS * -(-L // BF16_ROWS)
        outs = _mlstm_pre_sample(xm3.transpose(1, 0, 2), conv0.transpose(1, 0, 2), *pre_w, k_scale=DH ** -0.5)
        batch_major = lambda a: jnp.pad(a.reshape(L, Bg, -1).transpose(1, 0, 2), ((0, 0), (0, lp - L), (0, 0)))
        q, k, v, xc, gates = [batch_major(a) for a in outs]
        gz_p = jnp.pad(gz3, ((0, 0), (0, lp - L), (0, 0)))
        m0b = jnp.broadcast_to(m0[:, :, None], m0.shape + (LANES,))
        y, C, n, m = _mlstm_core(q, k, v, xc, gz_p, gates, p["mlstm_norm"], p["mlstm_skip"],
                                 (C0, n0, m0b), 2 * lp, L, NH_C)
        new_conv = jnp.concatenate([conv0, xm3], axis=1)[:, -(CONV_W - 1):]
        y = y[:, :L].reshape(M, DC)
    x1, x1n = _out_proj([y], p["w_out_c"], x2, p["g_mix_post1"], p["g_ffn_pre1"], tm)
    x2o = _ffn(x1, x1n, p["w_up"], p["w_dn"], 1, p["g_ffn_post1"], _tile(M, 1024), 1024)
    return x2o.reshape(Bg, L, D), (new_conv, C, n, m[:, :, 0])


def kernel(x_prompt, x_sample, state_rglru_conv, state_rglru_h, cache_swa_k, cache_swa_v, state_mlstm_conv, state_mlstm_C, state_mlstm_n, state_mlstm_m, norm_mix_pre, norm_mix_post, norm_ffn_pre, norm_ffn_post, w_ffn_up, w_ffn_down, w_in_ab, conv_a_w, conv_a_b, lru_wa, lru_ba, lru_wx, lru_bx, lru_lambda, w_out_ab, w_in_c, conv_c_w, conv_c_b, mlstm_wq, mlstm_wk, mlstm_wv, mlstm_w_gate, mlstm_b_gate, mlstm_norm, mlstm_skip, w_out_c):
    B, S, D = x_prompt.shape
    Bs, Ts, _ = x_sample.shape
    DA = conv_a_w.shape[-1]
    DC = conv_c_w.shape[-1]
    hd = D // H_B
    row = lambda a: a.reshape(1, -1)
    lane_pad = lambda a: jnp.pad(a, ((0, 0), (0, LANES - NH_C)))
    wg = jnp.concatenate([lane_pad(mlstm_w_gate[0][:, :NH_C]), lane_pad(mlstm_w_gate[0][:, NH_C:])], axis=1)
    bg = jnp.concatenate([lane_pad(mlstm_b_gate[:, :NH_C]), lane_pad(mlstm_b_gate[:, NH_C:])], axis=1)
    p = {
        "g_mix_pre0": row(norm_mix_pre[0]), "g_mix_post0": row(norm_mix_post[0]),
        "g_ffn_pre0": row(norm_ffn_pre[0]), "g_ffn_post0": row(norm_ffn_post[0]),
        "g_mix_pre1": row(norm_mix_pre[1]), "g_mix_post1": row(norm_mix_post[1]),
        "g_ffn_pre1": row(norm_ffn_pre[1]), "g_ffn_post1": row(norm_ffn_post[1]),
        "w_up": w_ffn_up, "w_dn": w_ffn_down,
        "w_in_ab": w_in_ab[0].astype(BF16),
        "conv_a_w": conv_a_w[0], "conv_a_b": row(conv_a_b[0]),
        "wa_bd": _regroup_block_diag(lru_wa[0], MXU_DIM), "lru_ba": row(lru_ba[0]),
        "wx_bd": _regroup_block_diag(lru_wx[0], MXU_DIM), "lru_bx": row(lru_bx[0]),
        "lam": row(lru_lambda[0]),
        "w_out_ab": w_out_ab[0],
        "w_in_c": w_in_c[0].astype(BF16),
        "conv_c_w": conv_c_w[0], "conv_c_b": row(conv_c_b[0]),
        "wq_bd": _regroup_block_diag(mlstm_wq[0], MXU_DIM),
        "wk_bd": _regroup_block_diag(mlstm_wk[0], MXU_DIM),
        "wv_bd": _regroup_block_diag(mlstm_wv[0], MXU_DIM),
        "wg": wg.astype(BF16), "bg": bg,
        "mlstm_norm": row(mlstm_norm[0]), "mlstm_skip": row(mlstm_skip[0]),
        "w_out_c": w_out_c[0],
    }
    dt = state_rglru_conv.dtype

    xp, st0 = _layer0(x_prompt, _rope_tables(np.arange(S), hd), p, True, None)
    xp, st1 = _layer1(xp, p, True, None)
    pos_s = PAST_LEN + (np.arange(Bs * Ts) % Ts)
    xs, ss0 = _layer0(x_sample, _rope_tables(pos_s, hd), p, False,
                      (state_rglru_conv[0], state_rglru_h[0], cache_swa_k[0], cache_swa_v[0]))
    xs, ss1 = _layer1(xs, p, False, (state_mlstm_conv[0], state_mlstm_C[0], state_mlstm_n[0], state_mlstm_m[0]))

    lead = lambda a: a[None].astype(dt)
    return (xp, xs,
            lead(st0[0]), lead(ss0[0]), lead(st0[1]), lead(ss0[1]),
            lead(st0[2]), lead(ss0[2]), lead(st0[3]), lead(ss0[3]),
            lead(st1[0]), lead(ss1[0]), lead(st1[1]), lead(ss1[1]),
            lead(st1[2]), lead(ss1[2]), lead(st1[3]), lead(ss1[3]))
```
